```python
import math
import jax
import jax.numpy as jnp
from jax import lax
import numpy as np

D_MODEL = 1024
BATCH = 8
SEQ = 4096
DEPTH = 1
DEC_BATCH = 2
DEC_SEQ = 8192
PAST_LEN = 128

D_RNN = 1280
N_LRU_BLOCKS = 8
LRU_BLOCK = D_RNN // N_LRU_BLOCKS
LRU_C = 8.0
LRU_CONV_W = 4
LRU_CONV_PAD = (2, 1)
HEAD_DIM = 128
N_HEADS = D_MODEL // HEAD_DIM
N_KV_HEADS = 2
GROUP = N_HEADS // N_KV_HEADS
WINDOW = 128
BLOCK = 128
N_BUCKETS = 32
MAX_DISTANCE = 128
Q_DIM = N_HEADS * HEAD_DIM
KV_DIM = N_KV_HEADS * HEAD_DIM
D_FF = 3 * D_MODEL
FFN_CONV_W = 3
FFN_CONV_PAD = (1, 1)
ALPHA = (2.0 * DEPTH) ** 0.25
BETA = (8.0 * DEPTH) ** -0.25
LN_EPS = 1e-5
NEG_INF = -1e30

SPLIT_SIZES = (D_RNN, D_RNN, Q_DIM, KV_DIM, KV_DIM, D_MODEL, D_MODEL)
SPLIT_IDX = tuple(sum(SPLIT_SIZES[:i + 1]) for i in range(len(SPLIT_SIZES) - 1))
D_IN = sum(SPLIT_SIZES)
V_OFF = 2 * D_RNN + Q_DIM + KV_DIM

kernel_name = "hybrid_rglru_swa_encoder"


def layer_norm(x, g, b):
    xf = x.astype(jnp.float32)
    mu = jnp.mean(xf, axis=-1, keepdims=True)
    var = jnp.mean(jnp.square(xf - mu), axis=-1, keepdims=True)
    y = (xf - mu) * lax.rsqrt(var + LN_EPS) * g.astype(jnp.float32) + b.astype(jnp.float32)
    return y.astype(x.dtype)


def depthwise_conv(x, w, b, pad):
    s = x.shape[1]
    xp = jnp.pad(x, ((0, 0), pad, (0, 0)))
    out = b
    for tap in range(w.shape[0]):
        out = out + w[tap] * xp[:, tap:tap + s]
    return out


def block_diag_linear(x, w, b):
    xb = x.reshape(x.shape[0], x.shape[1], N_LRU_BLOCKS, LRU_BLOCK)
    y = jnp.einsum('bsni,nij->bsnj', xb, w) + b
    return y.reshape(x.shape)


def _linear_recurrence_combine(left, right):
    a1, b1 = left
    a2, b2 = right
    return a1 * a2, a2 * b1 + b2


def rg_lru_scan(u, w_a, b_a, w_x, b_x, lam):
    r = jax.nn.sigmoid(block_diag_linear(u, w_a, b_a).astype(jnp.float32))
    i = jax.nn.sigmoid(block_diag_linear(u, w_x, b_x).astype(jnp.float32))
    log_a = LRU_C * r * jax.nn.log_sigmoid(lam.astype(jnp.float32))
    a = jnp.exp(log_a)
    bt = jnp.sqrt(-jnp.expm1(2.0 * log_a)) * (i * u.astype(jnp.float32))
    _, h = lax.associative_scan(_linear_recurrence_combine, (a, bt), axis=1)
    return h


def t5_bucket(rel):
    nb = N_BUCKETS // 2
    ret = jnp.where(rel > 0, nb, 0)
    n = jnp.abs(rel)
    max_exact = nb // 2
    nf = jnp.maximum(n, 1).astype(jnp.float32)
    large = max_exact + (jnp.log(nf / max_exact) / math.log(MAX_DISTANCE / max_exact)
                         * (nb - max_exact)).astype(jnp.int32)
    large = jnp.minimum(large, nb - 1)
    return ret + jnp.where(n < max_exact, n, large)


def band_bias_and_mask(rel_table, seq_len):
    nblk = seq_len // BLOCK
    q_off = jnp.arange(BLOCK)[:, None]
    c_off = jnp.arange(3 * BLOCK)[None, :]
    rel = c_off - BLOCK - q_off
    bias = rel_table.astype(jnp.float32)[t5_bucket(rel)]
    bias = bias.transpose(2, 0, 1).reshape(N_KV_HEADS, GROUP, BLOCK, 3 * BLOCK)
    key_pos = jnp.arange(nblk)[:, None] * BLOCK + jnp.arange(3 * BLOCK)[None, :] - BLOCK
    in_range = (key_pos >= 0) & (key_pos < seq_len)
    mask = (jnp.abs(rel) <= WINDOW)[None] & in_range[:, None, :]
    return bias, mask


def windowed_gqa(q, k, v, rel_table, sink):
    bsz, s = q.shape[0], q.shape[1]
    nblk = s // BLOCK
    qb = q.reshape(bsz, nblk, BLOCK, N_KV_HEADS, GROUP, HEAD_DIM)

    def band(t):
        tp = jnp.pad(t, ((0, 0), (BLOCK, BLOCK), (0, 0), (0, 0)))
        tp = tp.reshape(bsz, nblk + 2, BLOCK, N_KV_HEADS, HEAD_DIM)
        return jnp.concatenate([tp[:, :-2], tp[:, 1:-1], tp[:, 2:]], axis=2)

    kb, vb = band(k), band(v)
    bias, mask = band_bias_and_mask(rel_table, s)
    scores = jnp.einsum('bnqhgd,bnchd->bnhgqc', qb, kb,
                        preferred_element_type=jnp.float32) * (HEAD_DIM ** -0.5)
    scores = jnp.where(mask[None, :, None, None], scores + bias[None, None], NEG_INF)
    sink_l = sink.astype(jnp.float32).reshape(N_KV_HEADS, GROUP, 1, 1)
    m = jnp.maximum(jnp.max(scores, axis=-1, keepdims=True), sink_l)
    p = jnp.exp(scores - m)
    denom = jnp.sum(p, axis=-1, keepdims=True) + jnp.exp(sink_l - m)
    probs = (p / denom).astype(vb.dtype)
    out = jnp.einsum('bnhgqc,bnchd->bnqhgd', probs, vb)
    return out.reshape(bsz, s, Q_DIM)


def encoder_layer(x, rel_table, w_in, b_in, w_lru_conv, b_lru_conv, w_rg_a, b_rg_a, w_rg_x, b_rg_x,
                  lru_lambda, w_lru_out, attn_sink, w_attn_out, w_o, b_o, ln1_g, ln1_b,
                  w_up, b_up, w_ffn_conv, b_ffn_conv, w_down, b_down, ln2_g, ln2_b):
    bsz, s, _ = x.shape
    proj = x @ w_in + b_in
    u_lru, g_lru, q, k, v, gate_lru, gate_attn = jnp.split(proj, SPLIT_IDX, axis=-1)

    u = depthwise_conv(u_lru, w_lru_conv, b_lru_conv, LRU_CONV_PAD)
    h_fwd = rg_lru_scan(u, w_rg_a[0], b_rg_a[0], w_rg_x[0], b_rg_x[0], lru_lambda[0])
    h_bwd = jnp.flip(rg_lru_scan(jnp.flip(u, axis=1), w_rg_a[1], b_rg_a[1], w_rg_x[1], b_rg_x[1],
                                 lru_lambda[1]), axis=1)
    y_lru = ((h_fwd + h_bwd).astype(x.dtype) * jax.nn.gelu(g_lru)) @ w_lru_out

    attn = windowed_gqa(q.reshape(bsz, s, N_HEADS, HEAD_DIM),
                        k.reshape(bsz, s, N_KV_HEADS, HEAD_DIM),
                        v.reshape(bsz, s, N_KV_HEADS, HEAD_DIM), rel_table, attn_sink)
    y_attn = attn @ w_attn_out

    mix = (jax.nn.sigmoid(gate_lru) * y_lru + jax.nn.sigmoid(gate_attn) * y_attn) @ w_o + b_o
    x = layer_norm(ALPHA * x + mix, ln1_g, ln1_b)

    hu = depthwise_conv(x @ w_up + b_up, w_ffn_conv, b_ffn_conv, FFN_CONV_PAD)
    val, gt = jnp.split(hu, 2, axis=-1)
    ffn = (jax.nn.gelu(gt) * val) @ w_down + b_down
    return layer_norm(ALPHA * x + ffn, ln2_g, ln2_b)


def setup_inputs(seed: int = 0) -> dict:
    key = jax.random.key(seed)
    ks = jax.random.split(key, 32)
    f32 = jnp.float32

    def nrm(k, shape, scale):
        return jax.random.normal(k, shape, f32) * scale

    v_col_scale = jnp.ones((D_IN,), f32).at[V_OFF:V_OFF + KV_DIM].set(BETA)
    u = jax.random.uniform(ks[10], (DEPTH, 2, D_RNN), f32, 0.9, 0.999)
    a_base = u ** (1.0 / LRU_C)
    lru_lambda = jnp.log(a_base) - jnp.log1p(-a_base)
    return {
        'x_prompt': nrm(ks[0], (BATCH, SEQ, D_MODEL), 1.0),
        'x_sample': nrm(ks[1], (DEC_BATCH, DEC_SEQ, D_MODEL), 1.0),
        'rel_table': nrm(ks[2], (N_BUCKETS, N_HEADS), 0.5),
        'w_in': nrm(ks[3], (DEPTH, D_MODEL, D_IN), D_MODEL ** -0.5) * v_col_scale,
        'b_in': nrm(ks[4], (DEPTH, D_IN), 0.01),
        'w_lru_conv': nrm(ks[5], (DEPTH, LRU_CONV_W, D_RNN), LRU_CONV_W ** -0.5),
        'b_lru_conv': nrm(ks[6], (DEPTH, D_RNN), 0.01),
        'w_rg_a': nrm(ks[7], (DEPTH, 2, N_LRU_BLOCKS, LRU_BLOCK, LRU_BLOCK), LRU_BLOCK ** -0.5),
        'b_rg_a': nrm(ks[8], (DEPTH, 2, N_LRU_BLOCKS, LRU_BLOCK), 0.01),
        'w_rg_x': nrm(ks[9], (DEPTH, 2, N_LRU_BLOCKS, LRU_BLOCK, LRU_BLOCK), LRU_BLOCK ** -0.5),
        'b_rg_x': nrm(ks[11], (DEPTH, 2, N_LRU_BLOCKS, LRU_BLOCK), 0.01),
        'lru_lambda': lru_lambda,
        'w_lru_out': nrm(ks[12], (DEPTH, D_RNN, D_MODEL), BETA * D_RNN ** -0.5),
        'attn_sink': nrm(ks[13], (DEPTH, N_HEADS), 0.5),
        'w_attn_out': nrm(ks[14], (DEPTH, Q_DIM, D_MODEL), BETA * Q_DIM ** -0.5),
        'w_o': nrm(ks[15], (DEPTH, D_MODEL, D_MODEL), BETA * D_MODEL ** -0.5),
        'b_o': nrm(ks[16], (DEPTH, D_MODEL), 0.01),
        'ln1_g': 1.0 + nrm(ks[17], (DEPTH, D_MODEL), 0.02),
        'ln1_b': nrm(ks[18], (DEPTH, D_MODEL), 0.01),
        'w_up': nrm(ks[19], (DEPTH, D_MODEL, 2 * D_FF), D_MODEL ** -0.5),
        'b_up': nrm(ks[20], (DEPTH, 2 * D_FF), 0.01),
        'w_ffn_conv': nrm(ks[21], (DEPTH, FFN_CONV_W, 2 * D_FF), FFN_CONV_W ** -0.5),
        'b_ffn_conv': nrm(ks[22], (DEPTH, 2 * D_FF), 0.01),
        'w_down': nrm(ks[23], (DEPTH, D_FF, D_MODEL), BETA * D_FF ** -0.5),
        'b_down': nrm(ks[24], (DEPTH, D_MODEL), 0.01),
        'ln2_g': 1.0 + nrm(ks[25], (DEPTH, D_MODEL), 0.02),
        'ln2_b': nrm(ks[26], (DEPTH, D_MODEL), 0.01),
    }


def reference(x_prompt, x_sample, rel_table, w_in, b_in, w_lru_conv, b_lru_conv, w_rg_a, b_rg_a,
              w_rg_x, b_rg_x, lru_lambda, w_lru_out, attn_sink, w_attn_out, w_o, b_o, ln1_g, ln1_b,
              w_up, b_up, w_ffn_conv, b_ffn_conv, w_down, b_down, ln2_g, ln2_b):
    y_prompt = x_prompt
    y_sample = x_sample
    for l in range(DEPTH):
        p = (w_in[l], b_in[l], w_lru_conv[l], b_lru_conv[l], w_rg_a[l], b_rg_a[l], w_rg_x[l], b_rg_x[l],
             lru_lambda[l], w_lru_out[l], attn_sink[l], w_attn_out[l], w_o[l], b_o[l], ln1_g[l], ln1_b[l],
             w_up[l], b_up[l], w_ffn_conv[l], b_ffn_conv[l], w_down[l], b_down[l], ln2_g[l], ln2_b[l])
        y_prompt = encoder_layer(y_prompt, rel_table, *p)
        y_sample = encoder_layer(y_sample, rel_table, *p)
    return (y_prompt, y_sample)
```

```python
import functools
import math

import numpy as np
import jax
import jax.numpy as jnp
from jax import lax
from jax.experimental import pallas as pl
from jax.experimental.pallas import tpu as pltpu

F32 = jnp.float32
BF16 = jnp.bfloat16

D_MODEL = 1024
D_RNN = 1280
N_LRU_BLOCKS = 8
LRU_BLOCK = D_RNN // N_LRU_BLOCKS
LRU_C = 8.0
LRU_CONV_W = 4
HEAD_DIM = 128
N_HEADS = 8
N_KV_HEADS = 2
GROUP = N_HEADS // N_KV_HEADS
WINDOW = 128
BLOCK = 128
N_BUCKETS = 32
MAX_DISTANCE = 128
Q_DIM = N_HEADS * HEAD_DIM
KV_DIM = N_KV_HEADS * HEAD_DIM
D_FF = 3 * D_MODEL
LN_EPS = 1e-5
NEG_INF = -1e30

LANES = 128
SUBLANES = 8
MXU_DIM = 256

N_STREAMS = SUBLANES
T_SEG = 64
PERM_T = MXU_DIM // N_STREAMS
HALO_ROWS = 4 * N_STREAMS
ROWS_ATTN = 512
ROWS_FFN = 512
FFN_COLS = 512
FFN_HALO = 16
LRU_TILE = 64
VMEM_LIMIT = 56 * 1024 * 1024

GATE_OFF = tuple(LANES * ((LRU_BLOCK * i) // LANES) for i in range(N_LRU_BLOCKS))


def _sigmoid(x):
    return 0.5 * jnp.tanh(0.5 * x) + 0.5


def _gelu(x):
    c = math.sqrt(2.0 / math.pi)
    inner = x * (c + (c * 0.044715) * (x * x))
    return x * (0.5 * jnp.tanh(inner) + 0.5)


def _layer_norm(y, g, b):
    mu = jnp.mean(y, axis=-1, keepdims=True)
    yc = y - mu
    var = jnp.mean(yc * yc, axis=-1, keepdims=True)
    return yc * lax.rsqrt(var + LN_EPS) * g + b


def _dot(a, b):
    return jnp.dot(a, b, preferred_element_type=F32)


def _const_spec(shape):
    nd = len(shape)
    return pl.BlockSpec(shape, lambda *_: (0,) * nd, pipeline_mode=pl.Buffered(1))


def _lru_front(x4_ref, xh_ref, pm_ref, wu_ref, bu_ref, wc_ref, bc_ref, wg_ref, bg_ref, lam_ref,
               xb_scr, lhs_scr, uall_scr, u_scr, ub_scr, pre_scr, p_scr, carry_scr, h_dst,
               *, nb, ns, fwd, jj, nj):
    T = T_SEG
    R = N_STREAMS * T
    C = D_RNN

    for b in range(nb):
        for s in range(ns):
            i = b * ns + s
            xb_scr[i * T:(i + 1) * T, :] = x4_ref[b, s].astype(BF16)

    xh = xh_ref[...].astype(BF16)
    lhs_scr[0:2 * N_STREAMS, :] = xh[0:2 * N_STREAMS]
    lhs_scr[2 * N_STREAMS + R:, :] = xh[2 * N_STREAMS:]
    for tb in range(T // PERM_T):
        src = jnp.concatenate(
            [xb_scr[i * T + PERM_T * tb:i * T + PERM_T * (tb + 1), :] for i in range(N_STREAMS)], axis=0)
        lhs_scr[2 * N_STREAMS + MXU_DIM * tb:2 * N_STREAMS + MXU_DIM * (tb + 1), :] = (
            _dot(pm_ref[...], src).astype(BF16))

    uall_scr[...] = _dot(lhs_scr[...], wu_ref[...]) + bu_ref[...]

    sub = lax.broadcasted_iota(jnp.int32, (N_STREAMS, C), 0)
    seg = sub & (ns - 1)
    at_start = (jj == 0).astype(F32)
    at_end = (jj == nj - 1).astype(F32)
    keep_prev = 1.0 - (seg == 0).astype(F32) * at_start
    keep_next = 1.0 - (seg == ns - 1).astype(F32) * at_end
    uall_scr[0:N_STREAMS, :] = uall_scr[0:N_STREAMS, :] * keep_prev
    uall_scr[N_STREAMS:2 * N_STREAMS, :] = uall_scr[N_STREAMS:2 * N_STREAMS, :] * keep_prev
    uall_scr[2 * N_STREAMS + R:3 * N_STREAMS + R, :] = uall_scr[2 * N_STREAMS + R:3 * N_STREAMS + R, :] * keep_next

    wc = wc_ref[...]
    bc = bc_ref[...]

    def conv_tile(it, _):
        r0 = pl.multiple_of(it * LRU_TILE, LRU_TILE)
        acc = bc + wc[0:1] * uall_scr[pl.ds(r0, LRU_TILE), :]
        for k in range(1, LRU_CONV_W):
            acc = acc + wc[k:k + 1] * uall_scr[pl.ds(r0 + N_STREAMS * k, LRU_TILE), :]
        u_scr[pl.ds(r0, LRU_TILE), :] = acc
        ub_scr[pl.ds(r0, LRU_TILE), :] = acc.astype(BF16)
        return 0

    lax.fori_loop(0, R // LRU_TILE, conv_tile, 0)

    bg = bg_ref[...]
    touched = set()
    for i in range(N_LRU_BLOCKS):
        off = GATE_OFF[i]
        y = _dot(ub_scr[:, off:off + MXU_DIM], wg_ref[i])
        for half in range(2):
            for part in range(MXU_DIM // LANES):
                col = half * C + off + part * LANES
                val = y[:, half * MXU_DIM + part * LANES:half * MXU_DIM + (part + 1) * LANES]
                if col in touched:
                    pre_scr[:, col:col + LANES] = pre_scr[:, col:col + LANES] + val
                else:
                    pre_scr[:, col:col + LANES] = val + bg[:, col:col + LANES]
                    touched.add(col)

    lam = lam_ref[...]
    log_sig = jnp.minimum(lam, 0.0) - jnp.log1p(jnp.exp(-jnp.abs(lam)))
    cl = jnp.broadcast_to(LRU_C * log_sig, (N_STREAMS, C))

    first = (jj == 0) if fwd else (jj == nj - 1)

    @pl.when(first)
    def _():
        carry_scr[...] = jnp.zeros_like(carry_scr)

    def scan_step(tt, carry):
        h, p = carry
        t = tt if fwd else T - 1 - tt
        r0 = pl.multiple_of(t * N_STREAMS, N_STREAMS)
        rg = _sigmoid(pre_scr[pl.ds(r0, N_STREAMS), 0:C])
        ig = _sigmoid(pre_scr[pl.ds(r0, N_STREAMS), C:2 * C])
        log_a = rg * cl
        a = jnp.exp(log_a)
        gain = -jnp.tanh(log_a) * (a * a + 1.0)
        bt = jnp.sqrt(gain) * (ig * u_scr[pl.ds(r0, N_STREAMS), :])
        h = a * h + bt
        h_dst[pl.ds(r0, N_STREAMS), :] = h
        if ns > 1:
            p = p * a
            p_scr[pl.ds(r0, N_STREAMS), :] = p
        return h, p

    if ns == 1:
        h0 = carry_scr[...]
    else:
        h0 = jnp.zeros((N_STREAMS, C), F32)
    h_end, p_end = lax.fori_loop(0, T, scan_step, (h0, jnp.ones((N_STREAMS, C), F32)))

    if ns == 1:
        carry_scr[...] = h_end
        return None

    carry = carry_scr[...]
    head = (seg == 0) if fwd else (seg == ns - 1)
    from_prev = 1 if fwd else N_STREAMS - 1
    final = jnp.zeros((N_STREAMS, C), F32)
    cin = final
    for _ in range(ns):
        cin = jnp.where(head, carry, pltpu.roll(final, from_prev, 0))
        final = h_end + p_end * cin
    to_head = (N_STREAMS - (ns - 1)) % N_STREAMS if fwd else ns - 1
    carry_scr[...] = pltpu.roll(final, to_head, 0)
    return cin


def _lru_bwd_kernel(x4_ref, xh_ref, pm_ref, wu_ref, bu_ref, wc_ref, bc_ref, wg_ref, bg_ref, lam_ref,
                    wkv_ref, bkv_ref,
                    h_ref, k_ref, v_ref,
                    xb_scr, lhs_scr, uall_scr, u_scr, ub_scr, pre_scr, p_scr, carry_scr,
                    *, nb, ns, nj):
    T = T_SEG
    R = N_STREAMS * T
    jj = nj - 1 - pl.program_id(1)
    cin = _lru_front(x4_ref, xh_ref, pm_ref, wu_ref, bu_ref, wc_ref, bc_ref, wg_ref, bg_ref, lam_ref,
                     xb_scr, lhs_scr, uall_scr, u_scr, ub_scr, pre_scr, p_scr, carry_scr, h_ref,
                     nb=nb, ns=ns, fwd=False, jj=jj, nj=nj)
    if cin is not None:
        def fix_tile(it, _):
            r0 = pl.multiple_of(it * LRU_TILE, LRU_TILE)
            nt = LRU_TILE // N_STREAMS
            p = p_scr[pl.ds(r0, LRU_TILE), :].reshape(nt, N_STREAMS, D_RNN)
            h = h_ref[pl.ds(r0, LRU_TILE), :].reshape(nt, N_STREAMS, D_RNN)
            h_ref[pl.ds(r0, LRU_TILE), :] = (h + p * cin[None]).reshape(LRU_TILE, D_RNN)
            return 0

        lax.fori_loop(0, R // LRU_TILE, fix_tile, 0)

    kv = _dot(xb_scr[...], wkv_ref[...]) + bkv_ref[...]
    for b in range(nb):
        for s in range(ns):
            i = b * ns + s
            k_ref[b, s] = kv[i * T:(i + 1) * T, 0:KV_DIM].astype(BF16)
            v_ref[b, s] = kv[i * T:(i + 1) * T, KV_DIM:2 * KV_DIM].astype(BF16)


def _lru_fwd_kernel(x4_ref, xh_ref, hb_ref, pm_ref, pmt_ref, wu_ref, bu_ref, wc_ref, bc_ref, wg_ref, bg_ref,
                    lam_ref, wgl_ref, bgl_ref, wmg_ref, bmg_ref, wout_ref,
                    m_ref,
                    xb_scr, lhs_scr, uall_scr, u_scr, ub_scr, pre_scr, p_scr, carry_scr,
                    h_scr, g_scr, hg_scr, hgn_scr,
                    *, nb, ns, nj):
    T = T_SEG
    R = N_STREAMS * T
    jj = pl.program_id(1)
    cin = _lru_front(x4_ref, xh_ref, pm_ref, wu_ref, bu_ref, wc_ref, bc_ref, wg_ref, bg_ref, lam_ref,
                     xb_scr, lhs_scr, uall_scr, u_scr, ub_scr, pre_scr, p_scr, carry_scr, h_scr,
                     nb=nb, ns=ns, fwd=True, jj=jj, nj=nj)

    g_scr[...] = _dot(lhs_scr[2 * N_STREAMS:2 * N_STREAMS + R, :], wgl_ref[...]) + bgl_ref[...]

    def gate_tile(it, _):
        r0 = pl.multiple_of(it * LRU_TILE, LRU_TILE)
        h = h_scr[pl.ds(r0, LRU_TILE), :]
        if cin is not None:
            nt = LRU_TILE // N_STREAMS
            p = p_scr[pl.ds(r0, LRU_TILE), :].reshape(nt, N_STREAMS, D_RNN)
            h = (h.reshape(nt, N_STREAMS, D_RNN) + p * cin[None]).reshape(LRU_TILE, D_RNN)
        h = h + hb_ref[pl.ds(r0, LRU_TILE), :]
        hg_scr[pl.ds(r0, LRU_TILE), :] = (h * _gelu(g_scr[pl.ds(r0, LRU_TILE), :])).astype(BF16)
        return 0

    lax.fori_loop(0, R // LRU_TILE, gate_tile, 0)

    for tb in range(T // PERM_T):
        nat = _dot(pmt_ref[...], hg_scr[MXU_DIM * tb:MXU_DIM * (tb + 1), :]).astype(BF16)
        for i in range(N_STREAMS):
            hgn_scr[i * T + PERM_T * tb:i * T + PERM_T * (tb + 1), :] = nat[i * PERM_T:(i + 1) * PERM_T]

    y_lru = _dot(hgn_scr[...], wout_ref[...])
    gate = _sigmoid(_dot(xb_scr[...], wmg_ref[...]) + bmg_ref[...])
    m = (gate * y_lru).astype(BF16)
    for b in range(nb):
        for s in range(ns):
            i = b * ns + s
            m_ref[b, s] = m[i * T:(i + 1) * T]


def _stream_split(batch):
    nb = math.gcd(batch, N_STREAMS)
    return nb, N_STREAMS // nb


def _halo_rows(x4, nb, ns):
    B, Q, T, D = x4.shape
    last2 = x4[:, :, T - 2:, :]
    first = x4[:, :, 0, :]
    prev2 = jnp.pad(last2, ((0, 0), (1, 0), (0, 0), (0, 0)))[:, :Q]
    next1 = jnp.pad(first, ((0, 0), (0, 1), (0, 0)))[:, 1:]
    slots = jnp.stack([prev2[:, :, 0], prev2[:, :, 1], next1, jnp.zeros_like(next1)], axis=2)
    nj = Q // ns
    slots = slots.reshape(B // nb, nb, nj, ns, 4, D).transpose(0, 2, 4, 1, 3, 5)
    return slots.reshape(B // nb, nj, HALO_ROWS, D)


def _perm_matrix():
    pm = np.zeros((MXU_DIM, MXU_DIM), np.float32)
    for i in range(N_STREAMS):
        for t in range(PERM_T):
            pm[t * N_STREAMS + i, i * PERM_T + t] = 1.0
    return pm


def _lru_scratch(R):
    C = D_RNN
    return [
        pltpu.VMEM((R, D_MODEL), BF16),
        pltpu.VMEM((R + HALO_ROWS, D_MODEL), BF16),
        pltpu.VMEM((R + HALO_ROWS, C), F32),
        pltpu.VMEM((R, C), F32),
        pltpu.VMEM((R, C), BF16),
        pltpu.VMEM((R, 2 * C), F32),
        pltpu.VMEM((R, C), F32),
        pltpu.VMEM((N_STREAMS, C), F32),
    ]


def _lru_bwd_call(x, p):
    B, S, D = x.shape
    nb, ns = _stream_split(B)
    T = T_SEG
    R = N_STREAMS * T
    Q = S // T
    nj = Q // ns
    x4 = x.reshape(B, Q, T, D)
    xh = _halo_rows(x4, nb, ns)
    C = D_RNN
    rev = lambda bb, j: (bb, nj - 1 - j, 0, 0)
    in_specs = [
        pl.BlockSpec((nb, ns, T, D), rev),
        pl.BlockSpec((None, None, HALO_ROWS, D), rev),
        _const_spec((MXU_DIM, MXU_DIM)),
        _const_spec((D, C)), _const_spec((1, C)),
        _const_spec((LRU_CONV_W, C)), _const_spec((1, C)),
        _const_spec((N_LRU_BLOCKS, MXU_DIM, 2 * MXU_DIM)), _const_spec((1, 2 * C)),
        _const_spec((1, C)),
        _const_spec((D, 2 * KV_DIM)), _const_spec((1, 2 * KV_DIM)),
    ]
    out_shape = [
        jax.ShapeDtypeStruct((B // nb, nj, R, C), F32),
        jax.ShapeDtypeStruct((B, Q, T, KV_DIM), BF16),
        jax.ShapeDtypeStruct((B, Q, T, KV_DIM), BF16),
    ]
    out_specs = [
        pl.BlockSpec((None, None, R, C), rev),
        pl.BlockSpec((nb, ns, T, KV_DIM), rev),
        pl.BlockSpec((nb, ns, T, KV_DIM), rev),
    ]
    h_b, k, v = pl.pallas_call(
        functools.partial(_lru_bwd_kernel, nb=nb, ns=ns, nj=nj),
        grid=(B // nb, nj),
        in_specs=in_specs, out_specs=out_specs, out_shape=out_shape,
        scratch_shapes=_lru_scratch(R),
        compiler_params=pltpu.CompilerParams(
            dimension_semantics=("arbitrary", "arbitrary"), vmem_limit_bytes=VMEM_LIMIT),
        name="lru_bwd",
    )(x4, xh, p["pm"], p["w_u"], p["b_u"], p["w_conv"], p["b_conv"], p["wg_b"], p["bg_b"], p["lam_b"],
      p["w_kv"], p["b_kv"])
    return h_b, k.reshape(B, S, KV_DIM), v.reshape(B, S, KV_DIM)


def _lru_fwd_call(x, h_b, p):
    B, S, D = x.shape
    nb, ns = _stream_split(B)
    T = T_SEG
    R = N_STREAMS * T
    Q = S // T
    nj = Q // ns
    x4 = x.reshape(B, Q, T, D)
    xh = _halo_rows(x4, nb, ns)
    C = D_RNN
    fw = lambda bb, j: (bb, j, 0, 0)
    in_specs = [
        pl.BlockSpec((nb, ns, T, D), fw),
        pl.BlockSpec((None, None, HALO_ROWS, D), fw),
        pl.BlockSpec((None, None, R, C), fw),
        _const_spec((MXU_DIM, MXU_DIM)), _const_spec((MXU_DIM, MXU_DIM)),
        _const_spec((D, C)), _const_spec((1, C)),
        _const_spec((LRU_CONV_W, C)), _const_spec((1, C)),
        _const_spec((N_LRU_BLOCKS, MXU_DIM, 2 * MXU_DIM)), _const_spec((1, 2 * C)),
        _const_spec((1, C)),
        _const_spec((D, C)), _const_spec((1, C)),
        _const_spec((D, D)), _const_spec((1, D)),
        _const_spec((C, D)),
    ]
    scratch = _lru_scratch(R) + [
        pltpu.VMEM((R, C), F32),
        pltpu.VMEM((R, C), F32),
        pltpu.VMEM((R, C), BF16),
        pltpu.VMEM((R, C), BF16),
    ]
    m = pl.pallas_call(
        functools.partial(_lru_fwd_kernel, nb=nb, ns=ns, nj=nj),
        grid=(B // nb, nj),
        in_specs=in_specs,
        out_specs=pl.BlockSpec((nb, ns, T, D), fw),
        out_shape=jax.ShapeDtypeStruct((B, Q, T, D), BF16),
        scratch_shapes=scratch,
        compiler_params=pltpu.CompilerParams(
            dimension_semantics=("arbitrary", "arbitrary"), vmem_limit_bytes=VMEM_LIMIT),
        name="lru_fwd",
    )(x4, xh, h_b, p["pm"], p["pmt"], p["w_u"], p["b_u"], p["w_conv"], p["b_conv"], p["wg_f"], p["bg_f"],
      p["lam_f"], p["w_gl"], p["b_gl"], p["w_mg_lru"], p["b_mg_lru"], p["w_lru_out"])
    return m.reshape(B, S, D)


def _attn_kernel(x_ref, m_ref, kp_ref, km_ref, kn_ref, vp_ref, vm_ref, vn_ref, idx_ref, rel_ref, sink_ref,
                 wq_ref, bq_ref, wga_ref, bga_ref, wao_ref, wo_ref, bo_ref, g1_ref, b1_ref,
                 o_ref, q_scr, attn_scr, bias_scr, *, nr, alpha):
    R = ROWS_ATTN
    nblk = R // BLOCK
    r = pl.program_id(1)

    @pl.when((pl.program_id(0) == 0) & (r == 0))
    def _():
        idx = idx_ref[...]
        for h in range(N_HEADS):
            def pick(bkt, acc):
                return jnp.where(idx == bkt, rel_ref[bkt, h], acc)
            tab = lax.fori_loop(0, N_BUCKETS, pick, jnp.full((BLOCK, 3 * BLOCK), NEG_INF, F32))
            kvh, g = divmod(h, GROUP)
            bias_scr[kvh, g * BLOCK:(g + 1) * BLOCK, :] = tab

    xb = x_ref[...].astype(BF16)
    q_scr[...] = (_dot(xb, wq_ref[...]) + bq_ref[...]).astype(BF16)

    pen_prev = (r == 0).astype(F32) * NEG_INF
    pen_next = (r == nr - 1).astype(F32) * NEG_INF
    row_head = lax.broadcasted_iota(jnp.int32, (GROUP * BLOCK, 1), 0) // BLOCK

    for kvh in range(N_KV_HEADS):
        cols = slice(kvh * HEAD_DIM, (kvh + 1) * HEAD_DIM)
        sink = jnp.zeros((GROUP * BLOCK, 1), F32)
        for g in range(GROUP):
            sink = jnp.where(row_head == g, sink_ref[kvh * GROUP + g], sink)
        for n in range(nblk):
            q4 = jnp.concatenate(
                [q_scr[n * BLOCK:(n + 1) * BLOCK, (kvh * GROUP + g) * HEAD_DIM:(kvh * GROUP + g + 1) * HEAD_DIM]
                 for g in range(GROUP)], axis=0)
            own = slice(n * BLOCK, (n + 1) * BLOCK)
            prv = slice((n - 1) * BLOCK, n * BLOCK)
            nxt = slice((n + 1) * BLOCK, (n + 2) * BLOCK)
            k_prev = km_ref[prv, cols] if n > 0 else kp_ref[:, cols]
            v_prev = vm_ref[prv, cols] if n > 0 else vp_ref[:, cols]
            k_next = km_ref[nxt, cols] if n < nblk - 1 else kn_ref[:, cols]
            v_next = vm_ref[nxt, cols] if n < nblk - 1 else vn_ref[:, cols]
            k_ext = jnp.concatenate([k_prev, km_ref[own, cols], k_next], axis=0)
            v_ext = jnp.concatenate([v_prev, vm_ref[own, cols], v_next], axis=0)
            s = lax.dot_general(q4, k_ext, (((1,), (1,)), ((), ())), preferred_element_type=F32)
            s = s + bias_scr[kvh]
            if n == 0:
                s = jnp.concatenate([s[:, 0:BLOCK] + pen_prev, s[:, BLOCK:]], axis=1)
            if n == nblk - 1:
                s = jnp.concatenate([s[:, 0:2 * BLOCK], s[:, 2 * BLOCK:] + pen_next], axis=1)
            mx = jnp.maximum(jnp.max(s, axis=1, keepdims=True), sink)
            pr = jnp.exp(s - mx)
            denom = jnp.sum(pr, axis=1, keepdims=True) + jnp.exp(sink - mx)
            o = _dot(pr.astype(BF16), v_ext) * (1.0 / denom)
            for g in range(GROUP):
                h = kvh * GROUP + g
                attn_scr[own, h * HEAD_DIM:(h + 1) * HEAD_DIM] = o[g * BLOCK:(g + 1) * BLOCK].astype(BF16)

    y_attn = _dot(attn_scr[...], wao_ref[...])
    gate = _sigmoid(_dot(xb, wga_ref[...]) + bga_ref[...])
    merged = (gate * y_attn + m_ref[...].astype(F32)).astype(BF16)
    mix = _dot(merged, wo_ref[...]) + bo_ref[...]
    o_ref[...] = _layer_norm(alpha * x_ref[...] + mix, g1_ref[...], b1_ref[...])


def _attn_call(x, m_lru, k, v, p, alpha):
    B, S, D = x.shape
    R = ROWS_ATTN
    nr = S // R
    bpr = R // BLOCK
    nkb = S // BLOCK
    main = lambda b, r: (b, r, 0)
    prev = lambda b, r: (b, jnp.maximum(r * bpr - 1, 0), 0)
    nxt = lambda b, r: (b, jnp.minimum((r + 1) * bpr, nkb - 1), 0)
    kv_main = pl.BlockSpec((None, R, KV_DIM), main)
    kv_prev = pl.BlockSpec((None, BLOCK, KV_DIM), prev)
    kv_next = pl.BlockSpec((None, BLOCK, KV_DIM), nxt)
    in_specs = [
        pl.BlockSpec((None, R, D), main),
        pl.BlockSpec((None, R, D), main),
        kv_prev, kv_main, kv_next,
        kv_prev, kv_main, kv_next,
        _const_spec((BLOCK, 3 * BLOCK)),
        pl.BlockSpec(memory_space=pltpu.SMEM),
        pl.BlockSpec(memory_space=pltpu.SMEM),
        _const_spec((D, Q_DIM)), _const_spec((1, Q_DIM)),
        _const_spec((D, D)), _const_spec((1, D)),
        _const_spec((Q_DIM, D)),
        _const_spec((D, D)), _const_spec((1, D)),
        _const_spec((1, D)), _const_spec((1, D)),
    ]
    scratch = [
        pltpu.VMEM((R, Q_DIM), BF16),
        pltpu.VMEM((R, Q_DIM), BF16),
        pltpu.VMEM((N_KV_HEADS, GROUP * BLOCK, 3 * BLOCK), F32),
    ]
    return pl.pallas_call(
        functools.partial(_attn_kernel, nr=nr, alpha=alpha),
        grid=(B, nr),
        in_specs=in_specs,
        out_specs=pl.BlockSpec((None, R, D), main),
        out_shape=jax.ShapeDtypeStruct((B, S, D), F32),
        scratch_shapes=scratch,
        compiler_params=pltpu.CompilerParams(
            dimension_semantics=("arbitrary", "arbitrary"), vmem_limit_bytes=VMEM_LIMIT),
        name="attn_merge",
    )(x, m_lru, k, k, k, v, v, v, p["bias_idx"], p["rel_table"], p["sink"],
      p["w_q"], p["b_q"], p["w_mg_attn"], p["b_mg_attn"], p["w_attn_out"], p["w_o"], p["b_o"],
      p["ln1_g"], p["ln1_b"])


def _ffn_kernel(x_ref, xp_ref, xn_ref, wup_ref, bup_ref, wcv_ref, bcv_ref, wdn_ref, bdn_ref, g2_ref, b2_ref,
                o_ref, xe_scr, acc_scr, *, nr, alpha):
    R = ROWS_FFN
    H = FFN_HALO
    r = pl.program_id(1)
    keep_prev = (r > 0).astype(F32)
    keep_next = (r < nr - 1).astype(F32)
    zeros = jnp.zeros((H - SUBLANES, D_MODEL), F32)
    xe_scr[...] = jnp.concatenate(
        [zeros, xp_ref[...] * keep_prev, x_ref[...], xn_ref[...] * keep_next, zeros], axis=0).astype(BF16)

    row = lax.broadcasted_iota(jnp.int32, (SUBLANES, FFN_COLS), 0)
    first_row = (row == 0).astype(F32) * (1.0 - keep_prev)
    last_row = (row == SUBLANES - 1).astype(F32) * (1.0 - keep_next)

    def conv(hx, half, j):
        w = wcv_ref[half, j]
        b_up = bup_ref[half, j]
        up = pltpu.roll(hx, 1, 0)
        dn = pltpu.roll(hx, R + 2 * H - 1, 0)
        full = w[0:1] * up + w[1:2] * hx + w[2:3] * dn + (bcv_ref[half, j] + b_up * (w[0:1] + w[1:2] + w[2:3]))
        c = full[H:H + R]
        head = c[0:SUBLANES] - first_row * (b_up * w[0:1])
        tail = c[R - SUBLANES:] - last_row * (b_up * w[2:3])
        return jnp.concatenate([head, c[SUBLANES:R - SUBLANES], tail], axis=0)

    def col_step(j, _):
        xe = xe_scr[...]
        val = conv(_dot(xe, wup_ref[0, j]), 0, j)
        gt = conv(_dot(xe, wup_ref[1, j]), 1, j)
        act = (_gelu(gt) * val).astype(BF16)
        contrib = _dot(act, wdn_ref[j])

        @pl.when(j == 0)
        def _():
            acc_scr[...] = contrib

        @pl.when(j > 0)
        def _():
            acc_scr[...] = acc_scr[...] + contrib
        return 0

    lax.fori_loop(0, D_FF // FFN_COLS, col_step, 0)
    y = alpha * x_ref[...] + acc_scr[...] + bdn_ref[...]
    o_ref[...] = _layer_norm(y, g2_ref[...], b2_ref[...])


def _ffn_call(x, p, alpha):
    B, S, D = x.shape
    R = ROWS_FFN
    nr = S // R
    rpb = R // SUBLANES
    nsb = S // SUBLANES
    nt = D_FF // FFN_COLS
    main = lambda b, r: (b, r, 0)
    prev = lambda b, r: (b, jnp.maximum(r * rpb - 1, 0), 0)
    nxt = lambda b, r: (b, jnp.minimum((r + 1) * rpb, nsb - 1), 0)
    in_specs = [
        pl.BlockSpec((None, R, D), main),
        pl.BlockSpec((None, SUBLANES, D), prev),
        pl.BlockSpec((None, SUBLANES, D), nxt),
        _const_spec((2, nt, D, FFN_COLS)), _const_spec((2, nt, 1, FFN_COLS)),
        _const_spec((2, nt, 3, FFN_COLS)), _const_spec((2, nt, 1, FFN_COLS)),
        _const_spec((nt, FFN_COLS, D)), _const_spec((1, D)),
        _const_spec((1, D)), _const_spec((1, D)),
    ]
    scratch = [
        pltpu.VMEM((R + 2 * FFN_HALO, D), BF16),
        pltpu.VMEM((R, D), F32),
    ]
    return pl.pallas_call(
        functools.partial(_ffn_kernel, nr=nr, alpha=alpha),
        grid=(B, nr),
        in_specs=in_specs,
        out_specs=pl.BlockSpec((None, R, D), main),
        out_shape=jax.ShapeDtypeStruct((B, S, D), F32),
        scratch_shapes=scratch,
        compiler_params=pltpu.CompilerParams(
            dimension_semantics=("arbitrary", "arbitrary"), vmem_limit_bytes=VMEM_LIMIT),
        name="ffn",
    )(x, x, x, p["w_up"], p["b_up"], p["w_ffn_conv"], p["b_ffn_conv"], p["w_down"], p["b_down"],
      p["ln2_g"], p["ln2_b"])


def _t5_bucket(rel):
    nb = N_BUCKETS // 2
    ret = jnp.where(rel > 0, nb, 0)
    n = jnp.abs(rel)
    max_exact = nb // 2
    nf = jnp.maximum(n, 1).astype(jnp.float32)
    large = max_exact + (jnp.log(nf / max_exact) / math.log(MAX_DISTANCE / max_exact)
                         * (nb - max_exact)).astype(jnp.int32)
    large = jnp.minimum(large, nb - 1)
    return ret + jnp.where(n < max_exact, n, large)


def _pack_gate_weights(w_a, w_x):
    out = jnp.zeros((N_LRU_BLOCKS, MXU_DIM, 2 * MXU_DIM), F32)
    for i in range(N_LRU_BLOCKS):
        lo = LRU_BLOCK * i - GATE_OFF[i]
        out = out.at[i, lo:lo + LRU_BLOCK, lo:lo + LRU_BLOCK].set(w_a[i])
        out = out.at[i, lo:lo + LRU_BLOCK, MXU_DIM + lo:MXU_DIM + lo + LRU_BLOCK].set(w_x[i])
    return out.astype(BF16)


def _prepare(rel_table, w_in, b_in, w_lru_conv, b_lru_conv, w_rg_a, b_rg_a, w_rg_x, b_rg_x, lru_lambda,
             w_lru_out, attn_sink, w_attn_out, w_o, b_o, ln1_g, ln1_b, w_up, b_up, w_ffn_conv, b_ffn_conv,
             w_down, b_down, ln2_g, ln2_b):
    C = D_RNN
    o_g, o_q, o_k, o_gl, o_ga = C, 2 * C, 2 * C + Q_DIM, 2 * C + Q_DIM + 2 * KV_DIM, 2 * C + Q_DIM + 2 * KV_DIM + D_MODEL
    row = lambda a: a.reshape(1, -1).astype(F32)
    scale = HEAD_DIM ** -0.5
    nt = D_FF // FFN_COLS
    pm = _perm_matrix()

    q_off = jnp.arange(BLOCK)[:, None]
    c_off = jnp.arange(3 * BLOCK)[None, :]
    rel = c_off - BLOCK - q_off
    bias_idx = jnp.where(jnp.abs(rel) <= WINDOW, _t5_bucket(rel), -1).astype(jnp.int32)

    def ffn_cols(a):
        lead = a.shape[:-1]
        a = a.reshape(lead + (2, nt, FFN_COLS))
        return jnp.moveaxis(a, (-3, -2), (0, 1))

    return {
        "pm": jnp.asarray(pm, BF16), "pmt": jnp.asarray(pm.T, BF16),
        "w_u": w_in[:, 0:o_g].astype(BF16), "b_u": row(b_in[0:o_g]),
        "w_gl": w_in[:, o_g:o_q].astype(BF16), "b_gl": row(b_in[o_g:o_q]),
        "w_q": (w_in[:, o_q:o_k] * scale).astype(BF16), "b_q": row(b_in[o_q:o_k] * scale),
        "w_kv": w_in[:, o_k:o_gl].astype(BF16), "b_kv": row(b_in[o_k:o_gl]),
        "w_mg_lru": w_in[:, o_gl:o_ga].astype(BF16), "b_mg_lru": row(b_in[o_gl:o_ga]),
        "w_mg_attn": w_in[:, o_ga:].astype(BF16), "b_mg_attn": row(b_in[o_ga:]),
        "w_conv": w_lru_conv.astype(F32), "b_conv": row(b_lru_conv),
        "wg_f": _pack_gate_weights(w_rg_a[0], w_rg_x[0]),
        "bg_f": row(jnp.concatenate([b_rg_a[0].reshape(-1), b_rg_x[0].reshape(-1)])),
        "wg_b": _pack_gate_weights(w_rg_a[1], w_rg_x[1]),
        "bg_b": row(jnp.concatenate([b_rg_a[1].reshape(-1), b_rg_x[1].reshape(-1)])),
        "lam_f": row(lru_lambda[0]), "lam_b": row(lru_lambda[1]),
        "w_lru_out": w_lru_out.astype(BF16),
        "bias_idx": bias_idx, "rel_table": rel_table.astype(F32), "sink": attn_sink.astype(F32),
        "w_attn_out": w_attn_out.astype(BF16),
        "w_o": w_o.astype(BF16), "b_o": row(b_o),
        "ln1_g": row(ln1_g), "ln1_b": row(ln1_b),
        "w_up": ffn_cols(w_up).astype(BF16), "b_up": ffn_cols(b_up.reshape(1, -1)).astype(F32),
        "w_ffn_conv": ffn_cols(w_ffn_conv).astype(F32), "b_ffn_conv": ffn_cols(b_ffn_conv.reshape(1, -1)).astype(F32),
        "w_down": w_down.reshape(nt, FFN_COLS, D_MODEL).astype(BF16), "b_down": row(b_down),
        "ln2_g": row(ln2_g), "ln2_b": row(ln2_b),
    }


def _encoder_layer(x, p, alpha):
    h_b, k, v = _lru_bwd_call(x, p)
    m_lru = _lru_fwd_call(x, h_b, p)
    x1 = _attn_call(x, m_lru, k, v, p, alpha)
    return _ffn_call(x1, p, alpha)


def kernel(x_prompt, x_sample, rel_table, w_in, b_in, w_lru_conv, b_lru_conv, w_rg_a, b_rg_a, w_rg_x, b_rg_x, lru_lambda, w_lru_out, attn_sink, w_attn_out, w_o, b_o, ln1_g, ln1_b, w_up, b_up, w_ffn_conv, b_ffn_conv, w_down, b_down, ln2_g, ln2_b):
    depth = w_in.shape[0]
    alpha = (2.0 * depth) ** 0.25
    y_prompt, y_sample = x_prompt, x_sample
    for l in range(depth):
        p = _prepare(rel_table, w_in[l], b_in[l], w_lru_conv[l], b_lru_conv[l], w_rg_a[l], b_rg_a[l],
                     w_rg_x[l], b_rg_x[l], lru_lambda[l], w_lru_out[l], attn_sink[l], w_attn_out[l], w_o[l],
                     b_o[l], ln1_g[l], ln1_b[l], w_up[l], b_up[l], w_ffn_conv[l], b_ffn_conv[l], w_down[l],
                     b_down[l], ln2_g[l], ln2_b[l])
        y_prompt = _encoder_layer(y_prompt, p, alpha)
        y_sample = _encoder_layer(y_sample, p, alpha)
    return (y_prompt, y_sample)
```

```python
import functools
import math

import numpy as np
import jax
import jax.numpy as jnp
from jax import lax
from jax.experimental import pallas as pl
from jax.experimental.pallas import tpu as pltpu

F32 = jnp.float32
BF16 = jnp.bfloat16

D_MODEL = 1024
D_RNN = 1280
N_LRU_BLOCKS = 8
LRU_BLOCK = D_RNN // N_LRU_BLOCKS
LRU_C = 8.0
LRU_CONV_W = 4
HEAD_DIM = 128
N_HEADS = 8
N_KV_HEADS = 2
GROUP = N_HEADS // N_KV_HEADS
WINDOW = 128
BLOCK = 128
N_BUCKETS = 32
MAX_DISTANCE = 128
Q_DIM = N_HEADS * HEAD_DIM
KV_DIM = N_KV_HEADS * HEAD_DIM
D_FF = 3 * D_MODEL
LN_EPS = 1e-5
NEG_INF = -1e30
F32_TINY = float(np.finfo(np.float32).tiny)

LANES = 128
SUBLANES = 8
MXU_DIM = 256

N_STREAMS = SUBLANES
T_SEG = 64
PERM_T = MXU_DIM // N_STREAMS
HALO_ROWS = 4 * N_STREAMS
ROWS_ATTN = 512
ROWS_FFN = 512
FFN_COLS = 512
FFN_HALO = 16
LRU_TILE = 64
SCAN_UNROLL = 8
VMEM_LIMIT = 56 * 1024 * 1024

GATE_OFF = tuple(LANES * ((LRU_BLOCK * i) // LANES) for i in range(N_LRU_BLOCKS))


def _sigmoid_of_twice(z):
    return 0.5 * jnp.tanh(z) + 0.5


def _gelu(x):
    c = math.sqrt(2.0 / math.pi)
    inner = x * (c + (c * 0.044715) * (x * x))
    return x * (0.5 * jnp.tanh(inner) + 0.5)


def _layer_norm(y, g, b):
    mu = jnp.mean(y, axis=-1, keepdims=True)
    yc = y - mu
    var = jnp.mean(yc * yc, axis=-1, keepdims=True)
    return yc * lax.rsqrt(var + LN_EPS) * g + b


def _dot(a, b):
    return jnp.dot(a, b, preferred_element_type=F32)


def _const_spec(shape):
    nd = len(shape)
    return pl.BlockSpec(shape, lambda *_: (0,) * nd, pipeline_mode=pl.Buffered(1))


def _lru_front(x4_ref, xh_ref, pm_ref, wu_ref, bu_ref, wc_ref, bc_ref, wg_ref, bg_ref, lam_ref,
               xb_scr, lhs_scr, uall_scr, u_scr, ub_scr, pre_scr, p_scr, carry_scr, h_dst,
               *, nb, ns, fwd, jj, nj):
    T = T_SEG
    R = N_STREAMS * T
    C = D_RNN

    for b in range(nb):
        for s in range(ns):
            i = b * ns + s
            xb_scr[i * T:(i + 1) * T, :] = x4_ref[b, s].astype(BF16)

    xh = xh_ref[...].astype(BF16)
    lhs_scr[0:2 * N_STREAMS, :] = xh[0:2 * N_STREAMS]
    lhs_scr[2 * N_STREAMS + R:, :] = xh[2 * N_STREAMS:]
    for tb in range(T // PERM_T):
        src = jnp.concatenate(
            [xb_scr[i * T + PERM_T * tb:i * T + PERM_T * (tb + 1), :] for i in range(N_STREAMS)], axis=0)
        lhs_scr[2 * N_STREAMS + MXU_DIM * tb:2 * N_STREAMS + MXU_DIM * (tb + 1), :] = (
            _dot(pm_ref[...], src).astype(BF16))

    uall_scr[...] = _dot(lhs_scr[...], wu_ref[...]) + bu_ref[...]

    sub = lax.broadcasted_iota(jnp.int32, (N_STREAMS, C), 0)
    seg = sub & (ns - 1)
    at_start = (jj == 0).astype(F32)
    at_end = (jj == nj - 1).astype(F32)
    keep_prev = 1.0 - (seg == 0).astype(F32) * at_start
    keep_next = 1.0 - (seg == ns - 1).astype(F32) * at_end
    uall_scr[0:N_STREAMS, :] = uall_scr[0:N_STREAMS, :] * keep_prev
    uall_scr[N_STREAMS:2 * N_STREAMS, :] = uall_scr[N_STREAMS:2 * N_STREAMS, :] * keep_prev
    uall_scr[2 * N_STREAMS + R:3 * N_STREAMS + R, :] = uall_scr[2 * N_STREAMS + R:3 * N_STREAMS + R, :] * keep_next

    wc = wc_ref[...]
    bc = bc_ref[...]

    def conv_tile(it, _):
        r0 = pl.multiple_of(it * LRU_TILE, LRU_TILE)
        acc = bc + wc[0:1] * uall_scr[pl.ds(r0, LRU_TILE), :]
        for k in range(1, LRU_CONV_W):
            acc = acc + wc[k:k + 1] * uall_scr[pl.ds(r0 + N_STREAMS * k, LRU_TILE), :]
        u_scr[pl.ds(r0, LRU_TILE), :] = acc
        ub_scr[pl.ds(r0, LRU_TILE), :] = acc.astype(BF16)
        return 0

    lax.fori_loop(0, R // LRU_TILE, conv_tile, 0)

    bg = bg_ref[...]
    touched = set()
    for i in range(N_LRU_BLOCKS):
        off = GATE_OFF[i]
        y = _dot(ub_scr[:, off:off + MXU_DIM], wg_ref[i])
        for half in range(2):
            for part in range(MXU_DIM // LANES):
                col = half * C + off + part * LANES
                val = y[:, half * MXU_DIM + part * LANES:half * MXU_DIM + (part + 1) * LANES]
                if col in touched:
                    pre_scr[:, col:col + LANES] = pre_scr[:, col:col + LANES] + val
                else:
                    pre_scr[:, col:col + LANES] = val + bg[:, col:col + LANES]
                    touched.add(col)

    lam = lam_ref[...]
    log_sig = jnp.minimum(lam, 0.0) - jnp.log1p(jnp.exp(-jnp.abs(lam)))
    half_cl = (0.5 * LRU_C) * log_sig

    def coeff_tile(it, _):
        r0 = pl.multiple_of(it * LRU_TILE, LRU_TILE)
        log_a = half_cl * jnp.tanh(pre_scr[pl.ds(r0, LRU_TILE), 0:C]) + half_cl
        ig = 0.5 * jnp.tanh(pre_scr[pl.ds(r0, LRU_TILE), C:2 * C]) + 0.5
        a = jnp.exp(log_a)
        gain = jnp.tanh(log_a) * (-1.0 - a * a)
        root = gain * lax.rsqrt(jnp.maximum(gain, F32_TINY))
        pre_scr[pl.ds(r0, LRU_TILE), 0:C] = a
        pre_scr[pl.ds(r0, LRU_TILE), C:2 * C] = root * (ig * u_scr[pl.ds(r0, LRU_TILE), :])
        return 0

    lax.fori_loop(0, R // LRU_TILE, coeff_tile, 0)

    first = (jj == 0) if fwd else (jj == nj - 1)

    @pl.when(first)
    def _():
        carry_scr[...] = jnp.zeros_like(carry_scr)

    def scan_step(tt, carry):
        h, p = carry
        t = tt if fwd else T - 1 - tt
        r0 = pl.multiple_of(t * N_STREAMS, N_STREAMS)
        a = pre_scr[pl.ds(r0, N_STREAMS), 0:C]
        h = a * h + pre_scr[pl.ds(r0, N_STREAMS), C:2 * C]
        h_dst[pl.ds(r0, N_STREAMS), :] = h
        if ns > 1:
            p = p * a
            p_scr[pl.ds(r0, N_STREAMS), :] = p
        return h, p

    if ns == 1:
        h0 = carry_scr[...]
    else:
        h0 = jnp.zeros((N_STREAMS, C), F32)
    h_end, p_end = lax.fori_loop(0, T, scan_step, (h0, jnp.ones((N_STREAMS, C), F32)), unroll=SCAN_UNROLL)

    if ns == 1:
        carry_scr[...] = h_end
        return None

    carry = carry_scr[...]
    head = (seg == 0) if fwd else (seg == ns - 1)
    from_prev = 1 if fwd else N_STREAMS - 1
    final = jnp.zeros((N_STREAMS, C), F32)
    cin = final
    for _ in range(ns):
        cin = jnp.where(head, carry, pltpu.roll(final, from_prev, 0))
        final = h_end + p_end * cin
    to_head = (N_STREAMS - (ns - 1)) % N_STREAMS if fwd else ns - 1
    carry_scr[...] = pltpu.roll(final, to_head, 0)
    return cin


def _lru_bwd_kernel(x4_ref, xh_ref, pm_ref, wu_ref, bu_ref, wc_ref, bc_ref, wg_ref, bg_ref, lam_ref,
                    wkv_ref, bkv_ref,
                    h_ref, k_ref, v_ref,
                    xb_scr, lhs_scr, uall_scr, u_scr, ub_scr, pre_scr, p_scr, carry_scr,
                    *, nb, ns, nj):
    T = T_SEG
    R = N_STREAMS * T
    jj = nj - 1 - pl.program_id(1)
    cin = _lru_front(x4_ref, xh_ref, pm_ref, wu_ref, bu_ref, wc_ref, bc_ref, wg_ref, bg_ref, lam_ref,
                     xb_scr, lhs_scr, uall_scr, u_scr, ub_scr, pre_scr, p_scr, carry_scr, h_ref,
                     nb=nb, ns=ns, fwd=False, jj=jj, nj=nj)
    if cin is not None:
        def fix_tile(it, _):
            r0 = pl.multiple_of(it * LRU_TILE, LRU_TILE)
            nt = LRU_TILE // N_STREAMS
            p = p_scr[pl.ds(r0, LRU_TILE), :].reshape(nt, N_STREAMS, D_RNN)
            h = h_ref[pl.ds(r0, LRU_TILE), :].reshape(nt, N_STREAMS, D_RNN)
            h_ref[pl.ds(r0, LRU_TILE), :] = (h + p * cin[None]).reshape(LRU_TILE, D_RNN)
            return 0

        lax.fori_loop(0, R // LRU_TILE, fix_tile, 0)

    kv = _dot(xb_scr[...], wkv_ref[...]) + bkv_ref[...]
    for b in range(nb):
        for s in range(ns):
            i = b * ns + s
            k_ref[b, s] = kv[i * T:(i + 1) * T, 0:KV_DIM].astype(BF16)
            v_ref[b, s] = kv[i * T:(i + 1) * T, KV_DIM:2 * KV_DIM].astype(BF16)


def _lru_fwd_kernel(x4_ref, xh_ref, hb_ref, pm_ref, pmt_ref, wu_ref, bu_ref, wc_ref, bc_ref, wg_ref, bg_ref,
                    lam_ref, wgl_ref, bgl_ref, wmg_ref, bmg_ref, wout_ref,
                    m_ref,
                    xb_scr, lhs_scr, uall_scr, u_scr, ub_scr, pre_scr, p_scr, carry_scr,
                    h_scr, g_scr, hg_scr, hgn_scr,
                    *, nb, ns, nj):
    T = T_SEG
    R = N_STREAMS * T
    jj = pl.program_id(1)
    cin = _lru_front(x4_ref, xh_ref, pm_ref, wu_ref, bu_ref, wc_ref, bc_ref, wg_ref, bg_ref, lam_ref,
                     xb_scr, lhs_scr, uall_scr, u_scr, ub_scr, pre_scr, p_scr, carry_scr, h_scr,
                     nb=nb, ns=ns, fwd=True, jj=jj, nj=nj)

    g_scr[...] = _dot(lhs_scr[2 * N_STREAMS:2 * N_STREAMS + R, :], wgl_ref[...]) + bgl_ref[...]

    def gate_tile(it, _):
        r0 = pl.multiple_of(it * LRU_TILE, LRU_TILE)
        h = h_scr[pl.ds(r0, LRU_TILE), :]
        if cin is not None:
            nt = LRU_TILE // N_STREAMS
            p = p_scr[pl.ds(r0, LRU_TILE), :].reshape(nt, N_STREAMS, D_RNN)
            h = (h.reshape(nt, N_STREAMS, D_RNN) + p * cin[None]).reshape(LRU_TILE, D_RNN)
        h = h + hb_ref[pl.ds(r0, LRU_TILE), :]
        hg_scr[pl.ds(r0, LRU_TILE), :] = (h * _gelu(g_scr[pl.ds(r0, LRU_TILE), :])).astype(BF16)
        return 0

    lax.fori_loop(0, R // LRU_TILE, gate_tile, 0)

    for tb in range(T // PERM_T):
        nat = _dot(pmt_ref[...], hg_scr[MXU_DIM * tb:MXU_DIM * (tb + 1), :]).astype(BF16)
        for i in range(N_STREAMS):
            hgn_scr[i * T + PERM_T * tb:i * T + PERM_T * (tb + 1), :] = nat[i * PERM_T:(i + 1) * PERM_T]

    y_lru = _dot(hgn_scr[...], wout_ref[...])
    gate = _sigmoid_of_twice(_dot(xb_scr[...], wmg_ref[...]) + bmg_ref[...])
    m = (gate * y_lru).astype(BF16)
    for b in range(nb):
        for s in range(ns):
            i = b * ns + s
            m_ref[b, s] = m[i * T:(i + 1) * T]


def _stream_split(batch):
    nb = math.gcd(batch, N_STREAMS)
    return nb, N_STREAMS // nb


def _halo_rows(x4, nb, ns):
    B, Q, T, D = x4.shape
    last2 = x4[:, :, T - 2:, :]
    first = x4[:, :, 0, :]
    prev2 = jnp.pad(last2, ((0, 0), (1, 0), (0, 0), (0, 0)))[:, :Q]
    next1 = jnp.pad(first, ((0, 0), (0, 1), (0, 0)))[:, 1:]
    slots = jnp.stack([prev2[:, :, 0], prev2[:, :, 1], next1, jnp.zeros_like(next1)], axis=2)
    nj = Q // ns
    slots = slots.reshape(B // nb, nb, nj, ns, 4, D).transpose(0, 2, 4, 1, 3, 5)
    return slots.reshape(B // nb, nj, HALO_ROWS, D)


def _perm_matrix():
    pm = np.zeros((MXU_DIM, MXU_DIM), np.float32)
    for i in range(N_STREAMS):
        for t in range(PERM_T):
            pm[t * N_STREAMS + i, i * PERM_T + t] = 1.0
    return pm


def _lru_scratch(R):
    C = D_RNN
    return [
        pltpu.VMEM((R, D_MODEL), BF16),
        pltpu.VMEM((R + HALO_ROWS, D_MODEL), BF16),
        pltpu.VMEM((R + HALO_ROWS, C), F32),
        pltpu.VMEM((R, C), F32),
        pltpu.VMEM((R, C), BF16),
        pltpu.VMEM((R, 2 * C), F32),
        pltpu.VMEM((R, C), F32),
        pltpu.VMEM((N_STREAMS, C), F32),
    ]


def _lru_bwd_call(x, p):
    B, S, D = x.shape
    nb, ns = _stream_split(B)
    T = T_SEG
    R = N_STREAMS * T
    Q = S // T
    nj = Q // ns
    x4 = x.reshape(B, Q, T, D)
    xh = _halo_rows(x4, nb, ns)
    C = D_RNN
    rev = lambda bb, j: (bb, nj - 1 - j, 0, 0)
    in_specs = [
        pl.BlockSpec((nb, ns, T, D), rev),
        pl.BlockSpec((None, None, HALO_ROWS, D), rev),
        _const_spec((MXU_DIM, MXU_DIM)),
        _const_spec((D, C)), _const_spec((1, C)),
        _const_spec((LRU_CONV_W, C)), _const_spec((1, C)),
        _const_spec((N_LRU_BLOCKS, MXU_DIM, 2 * MXU_DIM)), _const_spec((1, 2 * C)),
        _const_spec((1, C)),
        _const_spec((D, 2 * KV_DIM)), _const_spec((1, 2 * KV_DIM)),
    ]
    out_shape = [
        jax.ShapeDtypeStruct((B // nb, nj, R, C), F32),
        jax.ShapeDtypeStruct((B, Q, T, KV_DIM), BF16),
        jax.ShapeDtypeStruct((B, Q, T, KV_DIM), BF16),
    ]
    out_specs = [
        pl.BlockSpec((None, None, R, C), rev),
        pl.BlockSpec((nb, ns, T, KV_DIM), rev),
        pl.BlockSpec((nb, ns, T, KV_DIM), rev),
    ]
    h_b, k, v = pl.pallas_call(
        functools.partial(_lru_bwd_kernel, nb=nb, ns=ns, nj=nj),
        grid=(B // nb, nj),
        in_specs=in_specs, out_specs=out_specs, out_shape=out_shape,
        scratch_shapes=_lru_scratch(R),
        compiler_params=pltpu.CompilerParams(
            dimension_semantics=("arbitrary", "arbitrary"), vmem_limit_bytes=VMEM_LIMIT),
        name="lru_bwd",
    )(x4, xh, p["pm"], p["w_u"], p["b_u"], p["w_conv"], p["b_conv"], p["wg_b"], p["bg_b"], p["lam_b"],
      p["w_kv"], p["b_kv"])
    return h_b, k.reshape(B, S, KV_DIM), v.reshape(B, S, KV_DIM)


def _lru_fwd_call(x, h_b, p):
    B, S, D = x.shape
    nb, ns = _stream_split(B)
    T = T_SEG
    R = N_STREAMS * T
    Q = S // T
    nj = Q // ns
    x4 = x.reshape(B, Q, T, D)
    xh = _halo_rows(x4, nb, ns)
    C = D_RNN
    fw = lambda bb, j: (bb, j, 0, 0)
    in_specs = [
        pl.BlockSpec((nb, ns, T, D), fw),
        pl.BlockSpec((None, None, HALO_ROWS, D), fw),
        pl.BlockSpec((None, None, R, C), fw),
        _const_spec((MXU_DIM, MXU_DIM)), _const_spec((MXU_DIM, MXU_DIM)),
        _const_spec((D, C)), _const_spec((1, C)),
        _const_spec((LRU_CONV_W, C)), _const_spec((1, C)),
        _const_spec((N_LRU_BLOCKS, MXU_DIM, 2 * MXU_DIM)), _const_spec((1, 2 * C)),
        _const_spec((1, C)),
        _const_spec((D, C)), _const_spec((1, C)),
        _const_spec((D, D)), _const_spec((1, D)),
        _const_spec((C, D)),
    ]
    scratch = _lru_scratch(R) + [
        pltpu.VMEM((R, C), F32),
        pltpu.VMEM((R, C), F32),
        pltpu.VMEM((R, C), BF16),
        pltpu.VMEM((R, C), BF16),
    ]
    m = pl.pallas_call(
        functools.partial(_lru_fwd_kernel, nb=nb, ns=ns, nj=nj),
        grid=(B // nb, nj),
        in_specs=in_specs,
        out_specs=pl.BlockSpec((nb, ns, T, D), fw),
        out_shape=jax.ShapeDtypeStruct((B, Q, T, D), BF16),
        scratch_shapes=scratch,
        compiler_params=pltpu.CompilerParams(
            dimension_semantics=("arbitrary", "arbitrary"), vmem_limit_bytes=VMEM_LIMIT),
        name="lru_fwd",
    )(x4, xh, h_b, p["pm"], p["pmt"], p["w_u"], p["b_u"], p["w_conv"], p["b_conv"], p["wg_f"], p["bg_f"],
      p["lam_f"], p["w_gl"], p["b_gl"], p["w_mg_lru"], p["b_mg_lru"], p["w_lru_out"])
    return m.reshape(B, S, D)


def _attn_kernel(x_ref, m_ref, kp_ref, km_ref, kn_ref, vp_ref, vm_ref, vn_ref, idx_ref, rel_ref, sink_ref,
                 wq_ref, bq_ref, wga_ref, bga_ref, wao_ref, wo_ref, bo_ref, g1_ref, b1_ref,
                 o_ref, q_scr, attn_scr, bias_scr, *, nr, alpha):
    R = ROWS_ATTN
    nblk = R // BLOCK
    r = pl.program_id(1)

    @pl.when((pl.program_id(0) == 0) & (r == 0))
    def _():
        idx = idx_ref[...]
        for h in range(N_HEADS):
            def pick(bkt, acc):
                return jnp.where(idx == bkt, rel_ref[bkt, h], acc)
            tab = lax.fori_loop(0, N_BUCKETS, pick, jnp.full((BLOCK, 3 * BLOCK), NEG_INF, F32))
            kvh, g = divmod(h, GROUP)
            bias_scr[kvh, g * BLOCK:(g + 1) * BLOCK, :] = tab

    xb = x_ref[...].astype(BF16)
    q_scr[...] = (_dot(xb, wq_ref[...]) + bq_ref[...]).astype(BF16)

    pen_prev = (r == 0).astype(F32) * NEG_INF
    pen_next = (r == nr - 1).astype(F32) * NEG_INF
    row_head = lax.broadcasted_iota(jnp.int32, (GROUP * BLOCK, 1), 0) // BLOCK

    for kvh in range(N_KV_HEADS):
        cols = slice(kvh * HEAD_DIM, (kvh + 1) * HEAD_DIM)
        sink = jnp.zeros((GROUP * BLOCK, 1), F32)
        for g in range(GROUP):
            sink = jnp.where(row_head == g, sink_ref[kvh * GROUP + g], sink)
        for n in range(nblk):
            q4 = jnp.concatenate(
                [q_scr[n * BLOCK:(n + 1) * BLOCK, (kvh * GROUP + g) * HEAD_DIM:(kvh * GROUP + g + 1) * HEAD_DIM]
                 for g in range(GROUP)], axis=0)
            own = slice(n * BLOCK, (n + 1) * BLOCK)
            prv = slice((n - 1) * BLOCK, n * BLOCK)
            nxt = slice((n + 1) * BLOCK, (n + 2) * BLOCK)
            k_prev = km_ref[prv, cols] if n > 0 else kp_ref[:, cols]
            v_prev = vm_ref[prv, cols] if n > 0 else vp_ref[:, cols]
            k_next = km_ref[nxt, cols] if n < nblk - 1 else kn_ref[:, cols]
            v_next = vm_ref[nxt, cols] if n < nblk - 1 else vn_ref[:, cols]
            k_ext = jnp.concatenate([k_prev, km_ref[own, cols], k_next], axis=0)
            v_ext = jnp.concatenate([v_prev, vm_ref[own, cols], v_next], axis=0)
            s = lax.dot_general(q4, k_ext, (((1,), (1,)), ((), ())), preferred_element_type=F32)
            s = s + bias_scr[kvh]
            if n == 0:
                s = jnp.concatenate([s[:, 0:BLOCK] + pen_prev, s[:, BLOCK:]], axis=1)
            if n == nblk - 1:
                s = jnp.concatenate([s[:, 0:2 * BLOCK], s[:, 2 * BLOCK:] + pen_next], axis=1)
            mx = jnp.maximum(jnp.max(s, axis=1, keepdims=True), sink)
            pr = jnp.exp(s - mx)
            denom = jnp.sum(pr, axis=1, keepdims=True) + jnp.exp(sink - mx)
            o = _dot(pr.astype(BF16), v_ext) * (1.0 / denom)
            for g in range(GROUP):
                h = kvh * GROUP + g
                attn_scr[own, h * HEAD_DIM:(h + 1) * HEAD_DIM] = o[g * BLOCK:(g + 1) * BLOCK].astype(BF16)

    y_attn = _dot(attn_scr[...], wao_ref[...])
    gate = _sigmoid_of_twice(_dot(xb, wga_ref[...]) + bga_ref[...])
    merged = (gate * y_attn + m_ref[...].astype(F32)).astype(BF16)
    mix = _dot(merged, wo_ref[...]) + bo_ref[...]
    o_ref[...] = _layer_norm(alpha * x_ref[...] + mix, g1_ref[...], b1_ref[...])


def _attn_call(x, m_lru, k, v, p, alpha):
    B, S, D = x.shape
    R = ROWS_ATTN
    nr = S // R
    bpr = R // BLOCK
    nkb = S // BLOCK
    main = lambda b, r: (b, r, 0)
    prev = lambda b, r: (b, jnp.maximum(r * bpr - 1, 0), 0)
    nxt = lambda b, r: (b, jnp.minimum((r + 1) * bpr, nkb - 1), 0)
    kv_main = pl.BlockSpec((None, R, KV_DIM), main)
    kv_prev = pl.BlockSpec((None, BLOCK, KV_DIM), prev)
    kv_next = pl.BlockSpec((None, BLOCK, KV_DIM), nxt)
    in_specs = [
        pl.BlockSpec((None, R, D), main),
        pl.BlockSpec((None, R, D), main),
        kv_prev, kv_main, kv_next,
        kv_prev, kv_main, kv_next,
        _const_spec((BLOCK, 3 * BLOCK)),
        pl.BlockSpec(memory_space=pltpu.SMEM),
        pl.BlockSpec(memory_space=pltpu.SMEM),
        _const_spec((D, Q_DIM)), _const_spec((1, Q_DIM)),
        _const_spec((D, D)), _const_spec((1, D)),
        _const_spec((Q_DIM, D)),
        _const_spec((D, D)), _const_spec((1, D)),
        _const_spec((1, D)), _const_spec((1, D)),
    ]
    scratch = [
        pltpu.VMEM((R, Q_DIM), BF16),
        pltpu.VMEM((R, Q_DIM), BF16),
        pltpu.VMEM((N_KV_HEADS, GROUP * BLOCK, 3 * BLOCK), F32),
    ]
    return pl.pallas_call(
        functools.partial(_attn_kernel, nr=nr, alpha=alpha),
        grid=(B, nr),
        in_specs=in_specs,
        out_specs=pl.BlockSpec((None, R, D), main),
        out_shape=jax.ShapeDtypeStruct((B, S, D), F32),
        scratch_shapes=scratch,
        compiler_params=pltpu.CompilerParams(
            dimension_semantics=("arbitrary", "arbitrary"), vmem_limit_bytes=VMEM_LIMIT),
        name="attn_merge",
    )(x, m_lru, k, k, k, v, v, v, p["bias_idx"], p["rel_table"], p["sink"],
      p["w_q"], p["b_q"], p["w_mg_attn"], p["b_mg_attn"], p["w_attn_out"], p["w_o"], p["b_o"],
      p["ln1_g"], p["ln1_b"])


def _ffn_kernel(x_ref, xp_ref, xn_ref, wup_ref, bup_ref, wcv_ref, bcv_ref, wdn_ref, bdn_ref, g2_ref, b2_ref,
                o_ref, xe_scr, hs_scr, act_scr, *, nr, alpha):
    R = ROWS_FFN
    H = FFN_HALO
    nt = D_FF // FFN_COLS
    nslab = FFN_COLS // LANES
    r = pl.program_id(1)
    keep_prev = (r > 0).astype(F32)
    keep_next = (r < nr - 1).astype(F32)
    zeros = jnp.zeros((H - SUBLANES, D_MODEL), F32)
    xe_scr[...] = jnp.concatenate(
        [zeros, xp_ref[...] * keep_prev, x_ref[...], xn_ref[...] * keep_next, zeros], axis=0).astype(BF16)

    row = lax.broadcasted_iota(jnp.int32, (SUBLANES, FFN_COLS), 0)
    pad_prev = (row == SUBLANES - 1).astype(F32) * (1.0 - keep_prev)
    pad_next = (row == 0).astype(F32) * (1.0 - keep_next)

    def up_proj(j):
        xe = xe_scr[...]
        for half in range(2):
            y = _dot(xe, wup_ref[half, j])
            b_up = bup_ref[half, j]
            y = jnp.concatenate(
                [y[0:H - SUBLANES], y[H - SUBLANES:H] - pad_prev * b_up, y[H:H + R],
                 y[H + R:H + R + SUBLANES] - pad_next * b_up, y[H + R + SUBLANES:]], axis=0)
            for c in range(nslab):
                hs_scr[j % 2, half, c] = y[:, c * LANES:(c + 1) * LANES]

    def conv(j, half, c):
        lanes = slice(c * LANES, (c + 1) * LANES)
        w = wcv_ref[half, j][:, lanes]
        bias = bcv_ref[half, j][:, lanes] + bup_ref[half, j][:, lanes] * (w[0:1] + w[1:2] + w[2:3])
        return (w[0:1] * hs_scr[j % 2, half, c, H - 1:H - 1 + R, :]
                + w[1:2] * hs_scr[j % 2, half, c, H:H + R, :]
                + w[2:3] * hs_scr[j % 2, half, c, H + 1:H + 1 + R, :] + bias)

    def gate(j):
        for c in range(nslab):
            act = _gelu(conv(j, 1, c)) * conv(j, 0, c)
            act_scr[:, j * FFN_COLS + c * LANES:j * FFN_COLS + (c + 1) * LANES] = act.astype(BF16)

    up_proj(0)
    for j in range(nt):
        if j + 1 < nt:
            up_proj(j + 1)
        gate(j)

    y = alpha * x_ref[...] + _dot(act_scr[...], wdn_ref[...]) + bdn_ref[...]
    o_ref[...] = _layer_norm(y, g2_ref[...], b2_ref[...])


def _ffn_call(x, p, alpha):
    B, S, D = x.shape
    R = ROWS_FFN
    nr = S // R
    rpb = R // SUBLANES
    nsb = S // SUBLANES
    nt = D_FF // FFN_COLS
    main = lambda b, r: (b, r, 0)
    prev = lambda b, r: (b, jnp.maximum(r * rpb - 1, 0), 0)
    nxt = lambda b, r: (b, jnp.minimum((r + 1) * rpb, nsb - 1), 0)
    in_specs = [
        pl.BlockSpec((None, R, D), main),
        pl.BlockSpec((None, SUBLANES, D), prev),
        pl.BlockSpec((None, SUBLANES, D), nxt),
        _const_spec((2, nt, D, FFN_COLS)), _const_spec((2, nt, 1, FFN_COLS)),
        _const_spec((2, nt, 3, FFN_COLS)), _const_spec((2, nt, 1, FFN_COLS)),
        _const_spec((D_FF, D)), _const_spec((1, D)),
        _const_spec((1, D)), _const_spec((1, D)),
    ]
    scratch = [
        pltpu.VMEM((R + 2 * FFN_HALO, D), BF16),
        pltpu.VMEM((2, 2, FFN_COLS // LANES, R + 2 * FFN_HALO, LANES), F32),
        pltpu.VMEM((R, D_FF), BF16),
    ]
    return pl.pallas_call(
        functools.partial(_ffn_kernel, nr=nr, alpha=alpha),
        grid=(B, nr),
        in_specs=in_specs,
        out_specs=pl.BlockSpec((None, R, D), main),
        out_shape=jax.ShapeDtypeStruct((B, S, D), F32),
        scratch_shapes=scratch,
        compiler_params=pltpu.CompilerParams(
            dimension_semantics=("arbitrary", "arbitrary"), vmem_limit_bytes=VMEM_LIMIT),
        name="ffn",
    )(x, x, x, p["w_up"], p["b_up"], p["w_ffn_conv"], p["b_ffn_conv"], p["w_down"], p["b_down"],
      p["ln2_g"], p["ln2_b"])


def _t5_bucket(rel):
    nb = N_BUCKETS // 2
    ret = jnp.where(rel > 0, nb, 0)
    n = jnp.abs(rel)
    max_exact = nb // 2
    nf = jnp.maximum(n, 1).astype(jnp.float32)
    large = max_exact + (jnp.log(nf / max_exact) / math.log(MAX_DISTANCE / max_exact)
                         * (nb - max_exact)).astype(jnp.int32)
    large = jnp.minimum(large, nb - 1)
    return ret + jnp.where(n < max_exact, n, large)


def _pack_gate_weights(w_a, w_x):
    out = jnp.zeros((N_LRU_BLOCKS, MXU_DIM, 2 * MXU_DIM), F32)
    for i in range(N_LRU_BLOCKS):
        lo = LRU_BLOCK * i - GATE_OFF[i]
        out = out.at[i, lo:lo + LRU_BLOCK, lo:lo + LRU_BLOCK].set(w_a[i])
        out = out.at[i, lo:lo + LRU_BLOCK, MXU_DIM + lo:MXU_DIM + lo + LRU_BLOCK].set(w_x[i])
    return out.astype(BF16)


def _prepare(rel_table, w_in, b_in, w_lru_conv, b_lru_conv, w_rg_a, b_rg_a, w_rg_x, b_rg_x, lru_lambda,
             w_lru_out, attn_sink, w_attn_out, w_o, b_o, ln1_g, ln1_b, w_up, b_up, w_ffn_conv, b_ffn_conv,
             w_down, b_down, ln2_g, ln2_b):
    C = D_RNN
    o_g, o_q, o_k, o_gl, o_ga = C, 2 * C, 2 * C + Q_DIM, 2 * C + Q_DIM + 2 * KV_DIM, 2 * C + Q_DIM + 2 * KV_DIM + D_MODEL
    row = lambda a: a.reshape(1, -1).astype(F32)
    scale = HEAD_DIM ** -0.5
    nt = D_FF // FFN_COLS
    pm = _perm_matrix()

    q_off = jnp.arange(BLOCK)[:, None]
    c_off = jnp.arange(3 * BLOCK)[None, :]
    rel = c_off - BLOCK - q_off
    bias_idx = jnp.where(jnp.abs(rel) <= WINDOW, _t5_bucket(rel), -1).astype(jnp.int32)

    def ffn_cols(a):
        lead = a.shape[:-1]
        a = a.reshape(lead + (2, nt, FFN_COLS))
        return jnp.moveaxis(a, (-3, -2), (0, 1))

    return {
        "pm": jnp.asarray(pm, BF16), "pmt": jnp.asarray(pm.T, BF16),
        "w_u": w_in[:, 0:o_g].astype(BF16), "b_u": row(b_in[0:o_g]),
        "w_gl": w_in[:, o_g:o_q].astype(BF16), "b_gl": row(b_in[o_g:o_q]),
        "w_q": (w_in[:, o_q:o_k] * scale).astype(BF16), "b_q": row(b_in[o_q:o_k] * scale),
        "w_kv": w_in[:, o_k:o_gl].astype(BF16), "b_kv": row(b_in[o_k:o_gl]),
        "w_mg_lru": (0.5 * w_in[:, o_gl:o_ga]).astype(BF16), "b_mg_lru": row(0.5 * b_in[o_gl:o_ga]),
        "w_mg_attn": (0.5 * w_in[:, o_ga:]).astype(BF16), "b_mg_attn": row(0.5 * b_in[o_ga:]),
        "w_conv": w_lru_conv.astype(F32), "b_conv": row(b_lru_conv),
        "wg_f": _pack_gate_weights(0.5 * w_rg_a[0], 0.5 * w_rg_x[0]),
        "bg_f": row(0.5 * jnp.concatenate([b_rg_a[0].reshape(-1), b_rg_x[0].reshape(-1)])),
        "wg_b": _pack_gate_weights(0.5 * w_rg_a[1], 0.5 * w_rg_x[1]),
        "bg_b": row(0.5 * jnp.concatenate([b_rg_a[1].reshape(-1), b_rg_x[1].reshape(-1)])),
        "lam_f": row(lru_lambda[0]), "lam_b": row(lru_lambda[1]),
        "w_lru_out": w_lru_out.astype(BF16),
        "bias_idx": bias_idx, "rel_table": rel_table.astype(F32), "sink": attn_sink.astype(F32),
        "w_attn_out": w_attn_out.astype(BF16),
        "w_o": w_o.astype(BF16), "b_o": row(b_o),
        "ln1_g": row(ln1_g), "ln1_b": row(ln1_b),
        "w_up": ffn_cols(w_up).astype(BF16), "b_up": ffn_cols(b_up.reshape(1, -1)).astype(F32),
        "w_ffn_conv": ffn_cols(w_ffn_conv).astype(F32), "b_ffn_conv": ffn_cols(b_ffn_conv.reshape(1, -1)).astype(F32),
        "w_down": w_down.astype(BF16), "b_down": row(b_down),
        "ln2_g": row(ln2_g), "ln2_b": row(ln2_b),
    }


def _encoder_layer(x, p, alpha):
    h_b, k, v = _lru_bwd_call(x, p)
    m_lru = _lru_fwd_call(x, h_b, p)
    x1 = _attn_call(x, m_lru, k, v, p, alpha)
    return _ffn_call(x1, p, alpha)


def kernel(x_prompt, x_sample, rel_table, w_in, b_in, w_lru_conv, b_lru_conv, w_rg_a, b_rg_a, w_rg_x, b_rg_x, lru_lambda, w_lru_out, attn_sink, w_attn_out, w_o, b_o, ln1_g, ln1_b, w_up, b_up, w_ffn_conv, b_ffn_conv, w_down, b_down, ln2_g, ln2_b):
    depth = w_in.shape[0]
    alpha = (2.0 * depth) ** 0.25
    y_prompt, y_sample = x_prompt, x_sample
    for l in range(depth):
        p = _prepare(rel_table, w_in[l], b_in[l], w_lru_conv[l], b_lru_conv[l], w_rg_a[l], b_rg_a[l],
                     w_rg_x[l], b_rg_x[l], lru_lambda[l], w_lru_out[l], attn_sink[l], w_attn_out[l], w_o[l],
                     b_o[l], ln1_g[l], ln1_b[l], w_up[l], b_up[l], w_ffn_conv[l], b_ffn_conv[l], w_down[l],
                     b_down[l], ln2_g[l], ln2_b[l])
        y_prompt = _encoder_layer(y_prompt, p, alpha)
        y_sample = _encoder_layer(y_sample, p, alpha)
    return (y_prompt, y_sample)
```

```python
import functools
import math

import numpy as np
import jax
import jax.numpy as jnp
from jax import lax
from jax.experimental import pallas as pl
from jax.experimental.pallas import tpu as pltpu

F32 = jnp.float32
BF16 = jnp.bfloat16

D_MODEL = 1024
D_RNN = 1280
N_LRU_BLOCKS = 8
LRU_BLOCK = D_RNN // N_LRU_BLOCKS
LRU_C = 8.0
LRU_CONV_W = 4
HEAD_DIM = 128
N_HEADS = 8
N_KV_HEADS = 2
GROUP = N_HEADS // N_KV_HEADS
WINDOW = 128
BLOCK = 128
N_BUCKETS = 32
MAX_DISTANCE = 128
Q_DIM = N_HEADS * HEAD_DIM
KV_DIM = N_KV_HEADS * HEAD_DIM
D_FF = 3 * D_MODEL
LN_EPS = 1e-5
NEG_INF = -1e30
F32_TINY = float(np.finfo(np.float32).tiny)

LANES = 128
SUBLANES = 8
MXU_DIM = 256

N_STREAMS = SUBLANES
T_SEG = 64
PERM_T = MXU_DIM // N_STREAMS
HALO_ROWS = 4 * N_STREAMS
ROWS_ATTN = 512
ROWS_FFN = 512
FFN_COLS = 512
FFN_HALO = 16
LRU_TILE = 64
LRU_COLS = 256
SCAN_UNROLL = 8
VMEM_LIMIT = 56 * 1024 * 1024

GATE_OFF = tuple(LANES * ((LRU_BLOCK * i) // LANES) for i in range(N_LRU_BLOCKS))


def _sigmoid_of_twice(z):
    return 0.5 * jnp.tanh(z) + 0.5


def _gelu(x):
    c = math.sqrt(2.0 / math.pi)
    inner = x * (c + (c * 0.044715) * (x * x))
    return x * (0.5 * jnp.tanh(inner) + 0.5)


def _layer_norm(y, g, b):
    mu = jnp.mean(y, axis=-1, keepdims=True)
    yc = y - mu
    var = jnp.mean(yc * yc, axis=-1, keepdims=True)
    return yc * lax.rsqrt(var + LN_EPS) * g + b


def _dot(a, b):
    return jnp.dot(a, b, preferred_element_type=F32)


def _const_spec(shape):
    nd = len(shape)
    return pl.BlockSpec(shape, lambda *_: (0,) * nd, pipeline_mode=pl.Buffered(1))


def _mxu_cycles(m, k, n):
    return (m // SUBLANES) * (n // LANES) * (-(-k // MXU_DIM))


def _vpu_cycles(rows, cols, ops_per_vreg):
    return rows * cols // (SUBLANES * LANES) * ops_per_vreg // 4


def _interleave(*streams):
    queues = [list(q) for q in streams]
    mxu = vpu = 0
    while any(queues):
        q = min((q for q in queues if q), key=lambda q: abs((mxu + q[0][1]) - (vpu + q[0][2])))
        fn, m, v = q.pop(0)
        fn()
        mxu += m
        vpu += v


def _input_items(x4_ref, xh_ref, pm_ref, wu_ref, bu_ref, wc_ref, bc_ref,
                 xb_dst, lhs_dst, uall_scr, u_dst, ub_dst, *, nb, ns, jj, nj):
    T = T_SEG
    R = N_STREAMS * T
    C = D_RNN
    ncol = C // LRU_COLS

    def prep():
        for b in range(nb):
            for sg in range(ns):
                i = b * ns + sg
                xb_dst[i * T:(i + 1) * T, :] = x4_ref[b, sg].astype(BF16)
        xh = xh_ref[...].astype(BF16)
        lhs_dst[0:2 * N_STREAMS, :] = xh[0:2 * N_STREAMS]
        lhs_dst[2 * N_STREAMS + R:, :] = xh[2 * N_STREAMS:]

    def permute(tb):
        src = jnp.concatenate(
            [xb_dst[i * T + PERM_T * tb:i * T + PERM_T * (tb + 1), :] for i in range(N_STREAMS)], axis=0)
        lhs_dst[2 * N_STREAMS + MXU_DIM * tb:2 * N_STREAMS + MXU_DIM * (tb + 1), :] = (
            _dot(pm_ref[...], src).astype(BF16))

    def project(c):
        cs = slice(c * LRU_COLS, (c + 1) * LRU_COLS)
        uall_scr[:, cs] = _dot(lhs_dst[...], wu_ref[:, cs]) + bu_ref[:, cs]
        seg = lax.broadcasted_iota(jnp.int32, (N_STREAMS, LRU_COLS), 0) & (ns - 1)
        keep_prev = 1.0 - (seg == 0).astype(F32) * (jj == 0).astype(F32)
        keep_next = 1.0 - (seg == ns - 1).astype(F32) * (jj == nj - 1).astype(F32)
        prev_rows = slice(0, 2 * N_STREAMS)
        next_rows = slice(2 * N_STREAMS + R, 3 * N_STREAMS + R)
        uall_scr[prev_rows, cs] = uall_scr[prev_rows, cs] * jnp.concatenate([keep_prev, keep_prev], axis=0)
        uall_scr[next_rows, cs] = uall_scr[next_rows, cs] * keep_next

    def conv(c):
        cs = slice(c * LRU_COLS, (c + 1) * LRU_COLS)
        acc = bc_ref[:, cs] + wc_ref[0:1, cs] * uall_scr[0:R, cs]
        for k in range(1, LRU_CONV_W):
            acc = acc + wc_ref[k:k + 1, cs] * uall_scr[N_STREAMS * k:N_STREAMS * k + R, cs]
        u_dst[:, cs] = acc
        ub_dst[:, cs] = acc.astype(BF16)

    items = [(prep, 0, _vpu_cycles(R, D_MODEL, 2))]
    for tb in range(T // PERM_T):
        items.append((functools.partial(permute, tb), _mxu_cycles(MXU_DIM, MXU_DIM, D_MODEL),
                      _vpu_cycles(MXU_DIM, D_MODEL, 1)))
    project_cost = _mxu_cycles(R + HALO_ROWS, D_MODEL, LRU_COLS)
    conv_cost = _vpu_cycles(R, LRU_COLS, 2 * LRU_CONV_W + 1)
    items.append((functools.partial(project, 0), project_cost, 0))
    for c in range(ncol):
        if c + 1 < ncol:
            items.append((functools.partial(project, c + 1), project_cost, 0))
        items.append((functools.partial(conv, c), 0, conv_cost))
    return items


def _coeff_items(ub_src, u_src, wg_ref, bg_ref, lam_ref, pre_scr):
    R = N_STREAMS * T_SEG
    C = D_RNN

    def coeff(t):
        cols = slice(t * LANES, (t + 1) * LANES)
        xcols = slice(C + t * LANES, C + (t + 1) * LANES)
        lam = lam_ref[:, cols]
        log_sig = jnp.minimum(lam, 0.0) - jnp.log1p(jnp.exp(-jnp.abs(lam)))
        hc = (0.5 * LRU_C) * log_sig
        log_a = hc * jnp.tanh(pre_scr[:, cols]) + hc
        ig = 0.5 * jnp.tanh(pre_scr[:, xcols]) + 0.5
        a = jnp.exp(log_a)
        gain = jnp.tanh(log_a) * (-1.0 - a * a)
        root = gain * lax.rsqrt(jnp.maximum(gain, F32_TINY))
        pre_scr[:, cols] = a
        pre_scr[:, xcols] = root * (ig * u_src[:, cols])

    first_touch = {}
    last_touch = {}
    for i in range(N_LRU_BLOCKS):
        for part in range(MXU_DIM // LANES):
            tile = GATE_OFF[i] // LANES + part
            first_touch.setdefault(tile, i)
            last_touch[tile] = i

    def gate(i):
        off = GATE_OFF[i]
        y = _dot(ub_src[:, off:off + MXU_DIM], wg_ref[i])
        for half in range(2):
            for part in range(MXU_DIM // LANES):
                col = half * C + off + part * LANES
                val = y[:, half * MXU_DIM + part * LANES:half * MXU_DIM + (part + 1) * LANES]
                if first_touch[off // LANES + part] == i:
                    pre_scr[:, col:col + LANES] = val + bg_ref[:, col:col + LANES]
                else:
                    pre_scr[:, col:col + LANES] = pre_scr[:, col:col + LANES] + val

    gate_cost = (_mxu_cycles(R, MXU_DIM, 2 * MXU_DIM), _vpu_cycles(R, 2 * MXU_DIM, 1))
    coeff_cost = _vpu_cycles(R, LANES, 20)
    items = [(functools.partial(gate, 0),) + gate_cost]
    for i in range(N_LRU_BLOCKS):
        if i + 1 < N_LRU_BLOCKS:
            items.append((functools.partial(gate, i + 1),) + gate_cost)
        for t in sorted(t for t, last in last_touch.items() if last == i):
            items.append((functools.partial(coeff, t), 0, coeff_cost))
    return items


def _stage_scan(pre_scr, p_scr, carry_scr, h_dst, *, ns, fwd, first):
    T = T_SEG
    C = D_RNN

    @pl.when(first)
    def _():
        carry_scr[...] = jnp.zeros_like(carry_scr)

    def scan_step(tt, carry):
        h, p = carry
        t = tt if fwd else T - 1 - tt
        r0 = pl.multiple_of(t * N_STREAMS, N_STREAMS)
        a = pre_scr[pl.ds(r0, N_STREAMS), 0:C]
        h = a * h + pre_scr[pl.ds(r0, N_STREAMS), C:2 * C]
        h_dst[pl.ds(r0, N_STREAMS), :] = h
        if ns > 1:
            p = p * a
            p_scr[pl.ds(r0, N_STREAMS), :] = p
        return h, p

    if ns == 1:
        h0 = carry_scr[...]
    else:
        h0 = jnp.zeros((N_STREAMS, C), F32)
    h_end, p_end = lax.fori_loop(0, T, scan_step, (h0, jnp.ones((N_STREAMS, C), F32)), unroll=SCAN_UNROLL)

    if ns == 1:
        carry_scr[...] = h_end
        return None

    sub = lax.broadcasted_iota(jnp.int32, (N_STREAMS, C), 0)
    seg = sub & (ns - 1)
    carry = carry_scr[...]
    head = (seg == 0) if fwd else (seg == ns - 1)
    from_prev = 1 if fwd else N_STREAMS - 1
    final = jnp.zeros((N_STREAMS, C), F32)
    cin = final
    for _ in range(ns):
        cin = jnp.where(head, carry, pltpu.roll(final, from_prev, 0))
        final = h_end + p_end * cin
    to_head = (N_STREAMS - (ns - 1)) % N_STREAMS if fwd else ns - 1
    carry_scr[...] = pltpu.roll(final, to_head, 0)
    return cin


def _add_carry(h, p, cin):
    shape3 = (h.shape[0] // N_STREAMS, N_STREAMS, h.shape[1])
    return (h.reshape(shape3) + p.reshape(shape3) * cin[None]).reshape(h.shape)


def _lru_bwd_kernel(x4_ref, xh_ref, pm_ref, wu_ref, bu_ref, wc_ref, bc_ref, wg_ref, bg_ref, lam_ref,
                    wkv_ref, bkv_ref,
                    h_ref, k_ref, v_ref,
                    xb_scr, lhs_scr, uall_scr, u_scr, ub_scr, pre_scr, p_scr, carry_scr,
                    *, nb, ns, nj):
    T = T_SEG
    R = N_STREAMS * T
    s = pl.program_id(1)
    c_in = jnp.minimum(s, nj - 1)
    c_scan = jnp.clip(s - 2, 0, nj - 1)
    slot = s % 2

    @pl.when(s == 0)
    def _():
        pre_scr[...] = jnp.zeros_like(pre_scr)
        u_scr[1] = jnp.zeros((R, D_RNN), F32)
        ub_scr[1] = jnp.zeros((R, D_RNN), BF16)

    cin = _stage_scan(pre_scr, p_scr, carry_scr, h_ref, ns=ns, fwd=False, first=(c_scan == 0))
    if cin is not None:
        for it in range(R // LRU_TILE):
            rows = slice(it * LRU_TILE, (it + 1) * LRU_TILE)
            h_ref[rows, :] = _add_carry(h_ref[rows, :], p_scr[rows, :], cin)

    def kv_project():
        kv = _dot(xb_scr[...], wkv_ref[...]) + bkv_ref[...]
        for b in range(nb):
            for sg in range(ns):
                i = b * ns + sg
                k_ref[b, sg] = kv[i * T:(i + 1) * T, 0:KV_DIM].astype(BF16)
                v_ref[b, sg] = kv[i * T:(i + 1) * T, KV_DIM:2 * KV_DIM].astype(BF16)

    input_items = _input_items(x4_ref, xh_ref, pm_ref, wu_ref, bu_ref, wc_ref, bc_ref,
                               xb_scr, lhs_scr, uall_scr, u_scr.at[slot], ub_scr.at[slot],
                               nb=nb, ns=ns, jj=nj - 1 - c_in, nj=nj)
    input_items.append((kv_project, _mxu_cycles(R, D_MODEL, 2 * KV_DIM), _vpu_cycles(R, 2 * KV_DIM, 2)))
    _interleave(_coeff_items(ub_scr.at[1 - slot], u_scr.at[1 - slot], wg_ref, bg_ref, lam_ref, pre_scr),
                input_items)


def _lru_fwd_kernel(x4_ref, xh_ref, hb_ref, pm_ref, pmt_ref, wu_ref, bu_ref, wc_ref, bc_ref, wg_ref, bg_ref,
                    lam_ref, wgl_ref, bgl_ref, wmg_ref, bmg_ref, wout_ref,
                    m_ref,
                    xb_scr, lhs_scr, uall_scr, u_scr, ub_scr, pre_scr, p_scr, carry_scr,
                    h_scr, g_scr, hg_scr, hgn_scr,
                    *, nb, ns, nj):
    T = T_SEG
    R = N_STREAMS * T
    C = D_RNN
    ncol = C // LRU_COLS
    s = pl.program_id(1)
    c_in = jnp.minimum(s, nj - 1)
    c_out = jnp.clip(s - 1, 0, nj - 1)
    slot = s % 2

    @pl.when(s == 0)
    def _():
        pre_scr[...] = jnp.zeros_like(pre_scr)
        xb_scr[1] = jnp.zeros((R, D_MODEL), BF16)
        lhs_scr[1] = jnp.zeros((R + HALO_ROWS, D_MODEL), BF16)

    cin = _stage_scan(pre_scr, p_scr, carry_scr, h_scr, ns=ns, fwd=True, first=(c_out == 0))

    def gate_project(c):
        cs = slice(c * LRU_COLS, (c + 1) * LRU_COLS)
        lhs_old = lhs_scr[1 - slot, 2 * N_STREAMS:2 * N_STREAMS + R, :]
        g_scr[c % 2] = _dot(lhs_old, wgl_ref[:, cs]) + bgl_ref[:, cs]

    def gate_apply(c):
        cs = slice(c * LRU_COLS, (c + 1) * LRU_COLS)
        h = h_scr[:, cs]
        if cin is not None:
            h = _add_carry(h, p_scr[:, cs], cin[:, cs])
        h = h + hb_ref[:, cs]
        hg_scr[:, cs] = (h * _gelu(g_scr[c % 2])).astype(BF16)

    def unpermute(c):
        cs = slice(c * LRU_COLS, (c + 1) * LRU_COLS)
        for tb in range(T // PERM_T):
            nat = _dot(pmt_ref[...], hg_scr[MXU_DIM * tb:MXU_DIM * (tb + 1), cs]).astype(BF16)
            for i in range(N_STREAMS):
                hgn_scr[i * T + PERM_T * tb:i * T + PERM_T * (tb + 1), cs] = nat[i * PERM_T:(i + 1) * PERM_T]

    def out_project(c):
        cs = slice(c * LRU_COLS, (c + 1) * LRU_COLS)
        y_lru = _dot(hgn_scr[...], wout_ref[:, cs])
        gate = _sigmoid_of_twice(_dot(xb_scr[1 - slot], wmg_ref[:, cs]) + bmg_ref[:, cs])
        m = (gate * y_lru).astype(BF16)
        for b in range(nb):
            for sg in range(ns):
                i = b * ns + sg
                m_ref[b, sg, :, cs] = m[i * T:(i + 1) * T]

    gp_cost = _mxu_cycles(R, D_MODEL, LRU_COLS)
    out_items = [(functools.partial(gate_project, 0), gp_cost, 0)]
    for c in range(ncol):
        if c + 1 < ncol:
            out_items.append((functools.partial(gate_project, c + 1), gp_cost, 0))
        out_items.append((functools.partial(gate_apply, c), 0, _vpu_cycles(R, LRU_COLS, 14)))
        out_items.append((functools.partial(unpermute, c), (T // PERM_T) * _mxu_cycles(MXU_DIM, MXU_DIM, LRU_COLS),
                          _vpu_cycles(R, LRU_COLS, 1)))
    for c in range(D_MODEL // LRU_COLS):
        out_items.append((functools.partial(out_project, c),
                          _mxu_cycles(R, C, LRU_COLS) + _mxu_cycles(R, D_MODEL, LRU_COLS),
                          _vpu_cycles(R, LRU_COLS, 5)))

    in_items = _input_items(x4_ref, xh_ref, pm_ref, wu_ref, bu_ref, wc_ref, bc_ref,
                            xb_scr.at[slot], lhs_scr.at[slot], uall_scr, u_scr, ub_scr,
                            nb=nb, ns=ns, jj=c_in, nj=nj)
    in_items += _coeff_items(ub_scr, u_scr, wg_ref, bg_ref, lam_ref, pre_scr)
    _interleave(out_items, in_items)


def _stream_split(batch):
    nb = math.gcd(batch, N_STREAMS)
    return nb, N_STREAMS // nb


def _halo_rows(x4, nb, ns):
    B, Q, T, D = x4.shape
    last2 = x4[:, :, T - 2:, :]
    first = x4[:, :, 0, :]
    prev2 = jnp.pad(last2, ((0, 0), (1, 0), (0, 0), (0, 0)))[:, :Q]
    next1 = jnp.pad(first, ((0, 0), (0, 1), (0, 0)))[:, 1:]
    slots = jnp.stack([prev2[:, :, 0], prev2[:, :, 1], next1, jnp.zeros_like(next1)], axis=2)
    nj = Q // ns
    slots = slots.reshape(B // nb, nb, nj, ns, 4, D).transpose(0, 2, 4, 1, 3, 5)
    return slots.reshape(B // nb, nj, HALO_ROWS, D)


def _perm_matrix():
    pm = np.zeros((MXU_DIM, MXU_DIM), np.float32)
    for i in range(N_STREAMS):
        for t in range(PERM_T):
            pm[t * N_STREAMS + i, i * PERM_T + t] = 1.0
    return pm


def _lru_bwd_call(x, p):
    B, S, D = x.shape
    nb, ns = _stream_split(B)
    T = T_SEG
    R = N_STREAMS * T
    Q = S // T
    nj = Q // ns
    x4 = x.reshape(B, Q, T, D)
    xh = _halo_rows(x4, nb, ns)
    C = D_RNN
    chunk_in = lambda bb, s: (bb, nj - 1 - jnp.minimum(s, nj - 1), 0, 0)
    chunk_scan = lambda bb, s: (bb, nj - 1 - jnp.clip(s - 2, 0, nj - 1), 0, 0)
    in_specs = [
        pl.BlockSpec((nb, ns, T, D), chunk_in),
        pl.BlockSpec((None, None, HALO_ROWS, D), chunk_in),
        _const_spec((MXU_DIM, MXU_DIM)),
        _const_spec((D, C)), _const_spec((1, C)),
        _const_spec((LRU_CONV_W, C)), _const_spec((1, C)),
        _const_spec((N_LRU_BLOCKS, MXU_DIM, 2 * MXU_DIM)), _const_spec((1, 2 * C)),
        _const_spec((1, C)),
        _const_spec((D, 2 * KV_DIM)), _const_spec((1, 2 * KV_DIM)),
    ]
    out_shape = [
        jax.ShapeDtypeStruct((B // nb, nj, R, C), F32),
        jax.ShapeDtypeStruct((B, Q, T, KV_DIM), BF16),
        jax.ShapeDtypeStruct((B, Q, T, KV_DIM), BF16),
    ]
    out_specs = [
        pl.BlockSpec((None, None, R, C), chunk_scan),
        pl.BlockSpec((nb, ns, T, KV_DIM), chunk_in),
        pl.BlockSpec((nb, ns, T, KV_DIM), chunk_in),
    ]
    scratch = [
        pltpu.VMEM((R, D), BF16),
        pltpu.VMEM((R + HALO_ROWS, D), BF16),
        pltpu.VMEM((R + HALO_ROWS, C), F32),
        pltpu.VMEM((2, R, C), F32),
        pltpu.VMEM((2, R, C), BF16),
        pltpu.VMEM((R, 2 * C), F32),
        pltpu.VMEM((R, C), F32),
        pltpu.VMEM((N_STREAMS, C), F32),
    ]
    h_b, k, v = pl.pallas_call(
        functools.partial(_lru_bwd_kernel, nb=nb, ns=ns, nj=nj),
        grid=(B // nb, nj + 2),
        in_specs=in_specs, out_specs=out_specs, out_shape=out_shape,
        scratch_shapes=scratch,
        compiler_params=pltpu.CompilerParams(
            dimension_semantics=("arbitrary", "arbitrary"), vmem_limit_bytes=VMEM_LIMIT),
        name="lru_bwd",
    )(x4, xh, p["pm"], p["w_u"], p["b_u"], p["w_conv"], p["b_conv"], p["wg_b"], p["bg_b"], p["lam_b"],
      p["w_kv"], p["b_kv"])
    return h_b, k.reshape(B, S, KV_DIM), v.reshape(B, S, KV_DIM)


def _lru_fwd_call(x, h_b, p):
    B, S, D = x.shape
    nb, ns = _stream_split(B)
    T = T_SEG
    R = N_STREAMS * T
    Q = S // T
    nj = Q // ns
    x4 = x.reshape(B, Q, T, D)
    xh = _halo_rows(x4, nb, ns)
    C = D_RNN
    chunk_in = lambda bb, s: (bb, jnp.minimum(s, nj - 1), 0, 0)
    chunk_out = lambda bb, s: (bb, jnp.clip(s - 1, 0, nj - 1), 0, 0)
    in_specs = [
        pl.BlockSpec((nb, ns, T, D), chunk_in),
        pl.BlockSpec((None, None, HALO_ROWS, D), chunk_in),
        pl.BlockSpec((None, None, R, C), chunk_out),
        _const_spec((MXU_DIM, MXU_DIM)), _const_spec((MXU_DIM, MXU_DIM)),
        _const_spec((D, C)), _const_spec((1, C)),
        _const_spec((LRU_CONV_W, C)), _const_spec((1, C)),
        _const_spec((N_LRU_BLOCKS, MXU_DIM, 2 * MXU_DIM)), _const_spec((1, 2 * C)),
        _const_spec((1, C)),
        _const_spec((D, C)), _const_spec((1, C)),
        _const_spec((D, D)), _const_spec((1, D)),
        _const_spec((C, D)),
    ]
    scratch = [
        pltpu.VMEM((2, R, D), BF16),
        pltpu.VMEM((2, R + HALO_ROWS, D), BF16),
        pltpu.VMEM((R + HALO_ROWS, C), F32),
        pltpu.VMEM((R, C), F32),
        pltpu.VMEM((R, C), BF16),
        pltpu.VMEM((R, 2 * C), F32),
        pltpu.VMEM((R, C), F32),
        pltpu.VMEM((N_STREAMS, C), F32),
        pltpu.VMEM((R, C), F32),
        pltpu.VMEM((2, R, LRU_COLS), F32),
        pltpu.VMEM((R, C), BF16),
        pltpu.VMEM((R, C), BF16),
    ]
    m = pl.pallas_call(
        functools.partial(_lru_fwd_kernel, nb=nb, ns=ns, nj=nj),
        grid=(B // nb, nj + 1),
        in_specs=in_specs,
        out_specs=pl.BlockSpec((nb, ns, T, D), chunk_out),
        out_shape=jax.ShapeDtypeStruct((B, Q, T, D), BF16),
        scratch_shapes=scratch,
        compiler_params=pltpu.CompilerParams(
            dimension_semantics=("arbitrary", "arbitrary"), vmem_limit_bytes=VMEM_LIMIT),
        name="lru_fwd",
    )(x4, xh, h_b, p["pm"], p["pmt"], p["w_u"], p["b_u"], p["w_conv"], p["b_conv"], p["wg_f"], p["bg_f"],
      p["lam_f"], p["w_gl"], p["b_gl"], p["w_mg_lru"], p["b_mg_lru"], p["w_lru_out"])
    return m.reshape(B, S, D)


def _attn_kernel(x_ref, m_ref, kp_ref, km_ref, kn_ref, vp_ref, vm_ref, vn_ref, idx_ref, rel_ref, sink_ref,
                 wq_ref, bq_ref, wga_ref, bga_ref, wao_ref, wo_ref, bo_ref, g1_ref, b1_ref,
                 o_ref, q_scr, attn_scr, bias_scr, *, nr, alpha):
    R = ROWS_ATTN
    nblk = R // BLOCK
    r = pl.program_id(1)

    @pl.when((pl.program_id(0) == 0) & (r == 0))
    def _():
        idx = idx_ref[...]
        for h in range(N_HEADS):
            def pick(bkt, acc):
                return jnp.where(idx == bkt, rel_ref[bkt, h], acc)
            tab = lax.fori_loop(0, N_BUCKETS, pick, jnp.full((BLOCK, 3 * BLOCK), NEG_INF, F32))
            kvh, g = divmod(h, GROUP)
            bias_scr[kvh, g * BLOCK:(g + 1) * BLOCK, :] = tab

    xb = x_ref[...].astype(BF16)
    q_scr[...] = (_dot(xb, wq_ref[...]) + bq_ref[...]).astype(BF16)

    pen_prev = (r == 0).astype(F32) * NEG_INF
    pen_next = (r == nr - 1).astype(F32) * NEG_INF
    row_head = lax.broadcasted_iota(jnp.int32, (GROUP * BLOCK, 1), 0) // BLOCK

    for kvh in range(N_KV_HEADS):
        cols = slice(kvh * HEAD_DIM, (kvh + 1) * HEAD_DIM)
        sink = jnp.zeros((GROUP * BLOCK, 1), F32)
        for g in range(GROUP):
            sink = jnp.where(row_head == g, sink_ref[kvh * GROUP + g], sink)
        for n in range(nblk):
            q4 = jnp.concatenate(
                [q_scr[n * BLOCK:(n + 1) * BLOCK, (kvh * GROUP + g) * HEAD_DIM:(kvh * GROUP + g + 1) * HEAD_DIM]
                 for g in range(GROUP)], axis=0)
            own = slice(n * BLOCK, (n + 1) * BLOCK)
            prv = slice((n - 1) * BLOCK, n * BLOCK)
            nxt = slice((n + 1) * BLOCK, (n + 2) * BLOCK)
            k_prev = km_ref[prv, cols] if n > 0 else kp_ref[:, cols]
            v_prev = vm_ref[prv, cols] if n > 0 else vp_ref[:, cols]
            k_next = km_ref[nxt, cols] if n < nblk - 1 else kn_ref[:, cols]
            v_next = vm_ref[nxt, cols] if n < nblk - 1 else vn_ref[:, cols]
            k_ext = jnp.concatenate([k_prev, km_ref[own, cols], k_next], axis=0)
            v_ext = jnp.concatenate([v_prev, vm_ref[own, cols], v_next], axis=0)
            s = lax.dot_general(q4, k_ext, (((1,), (1,)), ((), ())), preferred_element_type=F32)
            s = s + bias_scr[kvh]
            if n == 0:
                s = jnp.concatenate([s[:, 0:BLOCK] + pen_prev, s[:, BLOCK:]], axis=1)
            if n == nblk - 1:
                s = jnp.concatenate([s[:, 0:2 * BLOCK], s[:, 2 * BLOCK:] + pen_next], axis=1)
            mx = jnp.maximum(jnp.max(s, axis=1, keepdims=True), sink)
            pr = jnp.exp(s - mx)
            denom = jnp.sum(pr, axis=1, keepdims=True) + jnp.exp(sink - mx)
            o = _dot(pr.astype(BF16), v_ext) * (1.0 / denom)
            for g in range(GROUP):
                h = kvh * GROUP + g
                attn_scr[own, h * HEAD_DIM:(h + 1) * HEAD_DIM] = o[g * BLOCK:(g + 1) * BLOCK].astype(BF16)

    y_attn = _dot(attn_scr[...], wao_ref[...])
    gate = _sigmoid_of_twice(_dot(xb, wga_ref[...]) + bga_ref[...])
    merged = (gate * y_attn + m_ref[...].astype(F32)).astype(BF16)
    mix = _dot(merged, wo_ref[...]) + bo_ref[...]
    o_ref[...] = _layer_norm(alpha * x_ref[...] + mix, g1_ref[...], b1_ref[...])


def _attn_call(x, m_lru, k, v, p, alpha):
    B, S, D = x.shape
    R = ROWS_ATTN
    nr = S // R
    bpr = R // BLOCK
    nkb = S // BLOCK
    main = lambda b, r: (b, r, 0)
    prev = lambda b, r: (b, jnp.maximum(r * bpr - 1, 0), 0)
    nxt = lambda b, r: (b, jnp.minimum((r + 1) * bpr, nkb - 1), 0)
    kv_main = pl.BlockSpec((None, R, KV_DIM), main)
    kv_prev = pl.BlockSpec((None, BLOCK, KV_DIM), prev)
    kv_next = pl.BlockSpec((None, BLOCK, KV_DIM), nxt)
    in_specs = [
        pl.BlockSpec((None, R, D), main),
        pl.BlockSpec((None, R, D), main),
        kv_prev, kv_main, kv_next,
        kv_prev, kv_main, kv_next,
        _const_spec((BLOCK, 3 * BLOCK)),
        pl.BlockSpec(memory_space=pltpu.SMEM),
        pl.BlockSpec(memory_space=pltpu.SMEM),
        _const_spec((D, Q_DIM)), _const_spec((1, Q_DIM)),
        _const_spec((D, D)), _const_spec((1, D)),
        _const_spec((Q_DIM, D)),
        _const_spec((D, D)), _const_spec((1, D)),
        _const_spec((1, D)), _const_spec((1, D)),
    ]
    scratch = [
        pltpu.VMEM((R, Q_DIM), BF16),
        pltpu.VMEM((R, Q_DIM), BF16),
        pltpu.VMEM((N_KV_HEADS, GROUP * BLOCK, 3 * BLOCK), F32),
    ]
    return pl.pallas_call(
        functools.partial(_attn_kernel, nr=nr, alpha=alpha),
        grid=(B, nr),
        in_specs=in_specs,
        out_specs=pl.BlockSpec((None, R, D), main),
        out_shape=jax.ShapeDtypeStruct((B, S, D), F32),
        scratch_shapes=scratch,
        compiler_params=pltpu.CompilerParams(
            dimension_semantics=("arbitrary", "arbitrary"), vmem_limit_bytes=VMEM_LIMIT),
        name="attn_merge",
    )(x, m_lru, k, k, k, v, v, v, p["bias_idx"], p["rel_table"], p["sink"],
      p["w_q"], p["b_q"], p["w_mg_attn"], p["b_mg_attn"], p["w_attn_out"], p["w_o"], p["b_o"],
      p["ln1_g"], p["ln1_b"])


def _ffn_kernel(x_ref, xp_ref, xn_ref, wup_ref, bup_ref, wcv_ref, bcv_ref, wdn_ref, bdn_ref, g2_ref, b2_ref,
                o_ref, xe_scr, hs_scr, act_scr, *, nr, alpha):
    R = ROWS_FFN
    H = FFN_HALO
    nt = D_FF // FFN_COLS
    nslab = FFN_COLS // LANES
    r = pl.program_id(1)
    keep_prev = (r > 0).astype(F32)
    keep_next = (r < nr - 1).astype(F32)
    zeros = jnp.zeros((H - SUBLANES, D_MODEL), F32)
    xe_scr[...] = jnp.concatenate(
        [zeros, xp_ref[...] * keep_prev, x_ref[...], xn_ref[...] * keep_next, zeros], axis=0).astype(BF16)

    row = lax.broadcasted_iota(jnp.int32, (SUBLANES, FFN_COLS), 0)
    pad_prev = (row == SUBLANES - 1).astype(F32) * (1.0 - keep_prev)
    pad_next = (row == 0).astype(F32) * (1.0 - keep_next)

    def up_proj(j):
        xe = xe_scr[...]
        for half in range(2):
            y = _dot(xe, wup_ref[half, j])
            b_up = bup_ref[half, j]
            y = jnp.concatenate(
                [y[0:H - SUBLANES], y[H - SUBLANES:H] - pad_prev * b_up, y[H:H + R],
                 y[H + R:H + R + SUBLANES] - pad_next * b_up, y[H + R + SUBLANES:]], axis=0)
            for c in range(nslab):
                hs_scr[j % 2, half, c] = y[:, c * LANES:(c + 1) * LANES]

    def conv(j, half, c):
        lanes = slice(c * LANES, (c + 1) * LANES)
        w = wcv_ref[half, j][:, lanes]
        bias = bcv_ref[half, j][:, lanes] + bup_ref[half, j][:, lanes] * (w[0:1] + w[1:2] + w[2:3])
        return (w[0:1] * hs_scr[j % 2, half, c, H - 1:H - 1 + R, :]
                + w[1:2] * hs_scr[j % 2, half, c, H:H + R, :]
                + w[2:3] * hs_scr[j % 2, half, c, H + 1:H + 1 + R, :] + bias)

    def gate(j):
        for c in range(nslab):
            act = _gelu(conv(j, 1, c)) * conv(j, 0, c)
            act_scr[:, j * FFN_COLS + c * LANES:j * FFN_COLS + (c + 1) * LANES] = act.astype(BF16)

    up_proj(0)
    for j in range(nt):
        if j + 1 < nt:
            up_proj(j + 1)
        gate(j)

    y = alpha * x_ref[...] + _dot(act_scr[...], wdn_ref[...]) + bdn_ref[...]
    o_ref[...] = _layer_norm(y, g2_ref[...], b2_ref[...])


def _ffn_call(x, p, alpha):
    B, S, D = x.shape
    R = ROWS_FFN
    nr = S // R
    rpb = R // SUBLANES
    nsb = S // SUBLANES
    nt = D_FF // FFN_COLS
    main = lambda b, r: (b, r, 0)
    prev = lambda b, r: (b, jnp.maximum(r * rpb - 1, 0), 0)
    nxt = lambda b, r: (b, jnp.minimum((r + 1) * rpb, nsb - 1), 0)
    in_specs = [
        pl.BlockSpec((None, R, D), main),
        pl.BlockSpec((None, SUBLANES, D), prev),
        pl.BlockSpec((None, SUBLANES, D), nxt),
        _const_spec((2, nt, D, FFN_COLS)), _const_spec((2, nt, 1, FFN_COLS)),
        _const_spec((2, nt, 3, FFN_COLS)), _const_spec((2, nt, 1, FFN_COLS)),
        _const_spec((D_FF, D)), _const_spec((1, D)),
        _const_spec((1, D)), _const_spec((1, D)),
    ]
    scratch = [
        pltpu.VMEM((R + 2 * FFN_HALO, D), BF16),
        pltpu.VMEM((2, 2, FFN_COLS // LANES, R + 2 * FFN_HALO, LANES), F32),
        pltpu.VMEM((R, D_FF), BF16),
    ]
    return pl.pallas_call(
        functools.partial(_ffn_kernel, nr=nr, alpha=alpha),
        grid=(B, nr),
        in_specs=in_specs,
        out_specs=pl.BlockSpec((None, R, D), main),
        out_shape=jax.ShapeDtypeStruct((B, S, D), F32),
        scratch_shapes=scratch,
        compiler_params=pltpu.CompilerParams(
            dimension_semantics=("arbitrary", "arbitrary"), vmem_limit_bytes=VMEM_LIMIT),
        name="ffn",
    )(x, x, x, p["w_up"], p["b_up"], p["w_ffn_conv"], p["b_ffn_conv"], p["w_down"], p["b_down"],
      p["ln2_g"], p["ln2_b"])


def _t5_bucket(rel):
    nb = N_BUCKETS // 2
    ret = jnp.where(rel > 0, nb, 0)
    n = jnp.abs(rel)
    max_exact = nb // 2
    nf = jnp.maximum(n, 1).astype(jnp.float32)
    large = max_exact + (jnp.log(nf / max_exact) / math.log(MAX_DISTANCE / max_exact)
                         * (nb - max_exact)).astype(jnp.int32)
    large = jnp.minimum(large, nb - 1)
    return ret + jnp.where(n < max_exact, n, large)


def _pack_gate_weights(w_a, w_x):
    out = jnp.zeros((N_LRU_BLOCKS, MXU_DIM, 2 * MXU_DIM), F32)
    for i in range(N_LRU_BLOCKS):
        lo = LRU_BLOCK * i - GATE_OFF[i]
        out = out.at[i, lo:lo + LRU_BLOCK, lo:lo + LRU_BLOCK].set(w_a[i])
        out = out.at[i, lo:lo + LRU_BLOCK, MXU_DIM + lo:MXU_DIM + lo + LRU_BLOCK].set(w_x[i])
    return out.astype(BF16)


def _prepare(rel_table, w_in, b_in, w_lru_conv, b_lru_conv, w_rg_a, b_rg_a, w_rg_x, b_rg_x, lru_lambda,
             w_lru_out, attn_sink, w_attn_out, w_o, b_o, ln1_g, ln1_b, w_up, b_up, w_ffn_conv, b_ffn_conv,
             w_down, b_down, ln2_g, ln2_b):
    C = D_RNN
    o_g, o_q, o_k, o_gl, o_ga = C, 2 * C, 2 * C + Q_DIM, 2 * C + Q_DIM + 2 * KV_DIM, 2 * C + Q_DIM + 2 * KV_DIM + D_MODEL
    row = lambda a: a.reshape(1, -1).astype(F32)
    scale = HEAD_DIM ** -0.5
    nt = D_FF // FFN_COLS
    pm = _perm_matrix()

    q_off = jnp.arange(BLOCK)[:, None]
    c_off = jnp.arange(3 * BLOCK)[None, :]
    rel = c_off - BLOCK - q_off
    bias_idx = jnp.where(jnp.abs(rel) <= WINDOW, _t5_bucket(rel), -1).astype(jnp.int32)

    def ffn_cols(a):
        lead = a.shape[:-1]
        a = a.reshape(lead + (2, nt, FFN_COLS))
        return jnp.moveaxis(a, (-3, -2), (0, 1))

    return {
        "pm": jnp.asarray(pm, BF16), "pmt": jnp.asarray(pm.T, BF16),
        "w_u": w_in[:, 0:o_g].astype(BF16), "b_u": row(b_in[0:o_g]),
        "w_gl": w_in[:, o_g:o_q].astype(BF16), "b_gl": row(b_in[o_g:o_q]),
        "w_q": (w_in[:, o_q:o_k] * scale).astype(BF16), "b_q": row(b_in[o_q:o_k] * scale),
        "w_kv": w_in[:, o_k:o_gl].astype(BF16), "b_kv": row(b_in[o_k:o_gl]),
        "w_mg_lru": (0.5 * w_in[:, o_gl:o_ga]).astype(BF16), "b_mg_lru": row(0.5 * b_in[o_gl:o_ga]),
        "w_mg_attn": (0.5 * w_in[:, o_ga:]).astype(BF16), "b_mg_attn": row(0.5 * b_in[o_ga:]),
        "w_conv": w_lru_conv.astype(F32), "b_conv": row(b_lru_conv),
        "wg_f": _pack_gate_weights(0.5 * w_rg_a[0], 0.5 * w_rg_x[0]),
        "bg_f": row(0.5 * jnp.concatenate([b_rg_a[0].reshape(-1), b_rg_x[0].reshape(-1)])),
        "wg_b": _pack_gate_weights(0.5 * w_rg_a[1], 0.5 * w_rg_x[1]),
        "bg_b": row(0.5 * jnp.concatenate([b_rg_a[1].reshape(-1), b_rg_x[1].reshape(-1)])),
        "lam_f": row(lru_lambda[0]), "lam_b": row(lru_lambda[1]),
        "w_lru_out": w_lru_out.astype(BF16),
        "bias_idx": bias_idx, "rel_table": rel_table.astype(F32), "sink": attn_sink.astype(F32),
        "w_attn_out": w_attn_out.astype(BF16),
        "w_o": w_o.astype(BF16), "b_o": row(b_o),
        "ln1_g": row(ln1_g), "ln1_b": row(ln1_b),
        "w_up": ffn_cols(w_up).astype(BF16), "b_up": ffn_cols(b_up.reshape(1, -1)).astype(F32),
        "w_ffn_conv": ffn_cols(w_ffn_conv).astype(F32), "b_ffn_conv": ffn_cols(b_ffn_conv.reshape(1, -1)).astype(F32),
        "w_down": w_down.astype(BF16), "b_down": row(b_down),
        "ln2_g": row(ln2_g), "ln2_b": row(ln2_b),
    }


def _encoder_layer(x, p, alpha):
    h_b, k, v = _lru_bwd_call(x, p)
    m_lru = _lru_fwd_call(x, h_b, p)
    x1 = _attn_call(x, m_lru, k, v, p, alpha)
    return _ffn_call(x1, p, alpha)


def kernel(x_prompt, x_sample, rel_table, w_in, b_in, w_lru_conv, b_lru_conv, w_rg_a, b_rg_a, w_rg_x, b_rg_x, lru_lambda, w_lru_out, attn_sink, w_attn_out, w_o, b_o, ln1_g, ln1_b, w_up, b_up, w_ffn_conv, b_ffn_conv, w_down, b_down, ln2_g, ln2_b):
    depth = w_in.shape[0]
    alpha = (2.0 * depth) ** 0.25
    y_prompt, y_sample = x_prompt, x_sample
    for l in range(depth):
        p = _prepare(rel_table, w_in[l], b_in[l], w_lru_conv[l], b_lru_conv[l], w_rg_a[l], b_rg_a[l],
                     w_rg_x[l], b_rg_x[l], lru_lambda[l], w_lru_out[l], attn_sink[l], w_attn_out[l], w_o[l],
                     b_o[l], ln1_g[l], ln1_b[l], w_up[l], b_up[l], w_ffn_conv[l], b_ffn_conv[l], w_down[l],
                     b_down[l], ln2_g[l], ln2_b[l])
        y_prompt = _encoder_layer(y_prompt, p, alpha)
        y_sample = _encoder_layer(y_sample, p, alpha)
    return (y_prompt, y_sample)
```

```python
import functools
import math

import numpy as np
import jax
import jax.numpy as jnp
from jax import lax
from jax.experimental import pallas as pl
from jax.experimental.pallas import tpu as pltpu

F32 = jnp.float32
BF16 = jnp.bfloat16

D_MODEL = 1024
D_RNN = 1280
N_LRU_BLOCKS = 8
LRU_BLOCK = D_RNN // N_LRU_BLOCKS
LRU_C = 8.0
LRU_CONV_W = 4
HEAD_DIM = 128
N_HEADS = 8
N_KV_HEADS = 2
GROUP = N_HEADS // N_KV_HEADS
WINDOW = 128
BLOCK = 128
N_BUCKETS = 32
MAX_DISTANCE = 128
Q_DIM = N_HEADS * HEAD_DIM
KV_DIM = N_KV_HEADS * HEAD_DIM
D_FF = 3 * D_MODEL
LN_EPS = 1e-5
NEG_INF = -1e30
F32_TINY = float(np.finfo(np.float32).tiny)

LANES = 128
SUBLANES = 8
MXU_DIM = 256

N_STREAMS = SUBLANES
T_SEG = 64
PERM_T = MXU_DIM // N_STREAMS
HALO_ROWS = 4 * N_STREAMS
ROWS_ATTN = 512
ROWS_FFN = 512
FFN_COLS = 512
FFN_HALO = 16
LRU_TILE = 64
LRU_COLS = 256
SCAN_UNROLL = 8
VMEM_LIMIT = 56 * 1024 * 1024

GATE_OFF = tuple(LANES * ((LRU_BLOCK * i) // LANES) for i in range(N_LRU_BLOCKS))


def _sigmoid_of_twice(z):
    return 0.5 * jnp.tanh(z) + 0.5


def _gelu(x):
    c = math.sqrt(2.0 / math.pi)
    inner = x * (c + (c * 0.044715) * (x * x))
    return x * (0.5 * jnp.tanh(inner) + 0.5)


def _layer_norm(y, g, b):
    mu = jnp.mean(y, axis=-1, keepdims=True)
    yc = y - mu
    var = jnp.mean(yc * yc, axis=-1, keepdims=True)
    return yc * lax.rsqrt(var + LN_EPS) * g + b


def _dot(a, b):
    return jnp.dot(a, b, preferred_element_type=F32)


def _const_spec(shape):
    nd = len(shape)
    return pl.BlockSpec(shape, lambda *_: (0,) * nd, pipeline_mode=pl.Buffered(1))


def _mxu_cycles(m, k, n):
    return (m // SUBLANES) * (n // LANES) * (-(-k // MXU_DIM))


def _vpu_cycles(rows, cols, ops_per_vreg):
    return rows * cols // (SUBLANES * LANES) * ops_per_vreg // 4


def _interleave(*streams):
    queues = [list(q) for q in streams]
    mxu = vpu = 0
    while any(queues):
        q = min((q for q in queues if q), key=lambda q: abs((mxu + q[0][1]) - (vpu + q[0][2])))
        fn, m, v = q.pop(0)
        fn()
        mxu += m
        vpu += v


def _permute_items(x4_ref, pm_ref, xb_dst, lhs_dst, row0, *, nb, ns):
    T = T_SEG
    R = N_STREAMS * T

    def copy():
        for b in range(nb):
            for sg in range(ns):
                i = b * ns + sg
                xb_dst[i * T:(i + 1) * T, :] = x4_ref[b, sg].astype(BF16)

    def permute(tb):
        src = jnp.concatenate(
            [xb_dst[i * T + PERM_T * tb:i * T + PERM_T * (tb + 1), :] for i in range(N_STREAMS)], axis=0)
        lhs_dst[row0 + MXU_DIM * tb:row0 + MXU_DIM * (tb + 1), :] = _dot(pm_ref[...], src).astype(BF16)

    items = [(copy, 0, _vpu_cycles(R, D_MODEL, 2))]
    for tb in range(T // PERM_T):
        items.append((functools.partial(permute, tb), _mxu_cycles(MXU_DIM, MXU_DIM, D_MODEL),
                      _vpu_cycles(MXU_DIM, D_MODEL, 1)))
    return items


def _input_items(x4_ref, xh_ref, pm_ref, wu_ref, bu_ref, wc_ref, bc_ref,
                 xb_dst, lhs_dst, uall_scr, u_dst, ub_dst, u_out, *, nb, ns, jj, nj):
    T = T_SEG
    R = N_STREAMS * T
    C = D_RNN
    ncol = C // LRU_COLS

    def halo():
        xh = xh_ref[...].astype(BF16)
        lhs_dst[0:2 * N_STREAMS, :] = xh[0:2 * N_STREAMS]
        lhs_dst[2 * N_STREAMS + R:, :] = xh[2 * N_STREAMS:]

    def project(c):
        cs = slice(c * LRU_COLS, (c + 1) * LRU_COLS)
        uall_scr[:, cs] = _dot(lhs_dst[...], wu_ref[:, cs]) + bu_ref[:, cs]
        seg = lax.broadcasted_iota(jnp.int32, (N_STREAMS, LRU_COLS), 0) & (ns - 1)
        keep_prev = 1.0 - (seg == 0).astype(F32) * (jj == 0).astype(F32)
        keep_next = 1.0 - (seg == ns - 1).astype(F32) * (jj == nj - 1).astype(F32)
        prev_rows = slice(0, 2 * N_STREAMS)
        next_rows = slice(2 * N_STREAMS + R, 3 * N_STREAMS + R)
        uall_scr[prev_rows, cs] = uall_scr[prev_rows, cs] * jnp.concatenate([keep_prev, keep_prev], axis=0)
        uall_scr[next_rows, cs] = uall_scr[next_rows, cs] * keep_next

    def conv(c):
        cs = slice(c * LRU_COLS, (c + 1) * LRU_COLS)
        acc = bc_ref[:, cs] + wc_ref[0:1, cs] * uall_scr[0:R, cs]
        for k in range(1, LRU_CONV_W):
            acc = acc + wc_ref[k:k + 1, cs] * uall_scr[N_STREAMS * k:N_STREAMS * k + R, cs]
        u_dst[:, cs] = acc
        ub_dst[:, cs] = acc.astype(BF16)
        u_out[:, cs] = acc

    items = _permute_items(x4_ref, pm_ref, xb_dst, lhs_dst, 2 * N_STREAMS, nb=nb, ns=ns)
    items.append((halo, 0, 0))
    project_cost = _mxu_cycles(R + HALO_ROWS, D_MODEL, LRU_COLS)
    conv_cost = _vpu_cycles(R, LRU_COLS, 2 * LRU_CONV_W + 1)
    items.append((functools.partial(project, 0), project_cost, 0))
    for c in range(ncol):
        if c + 1 < ncol:
            items.append((functools.partial(project, c + 1), project_cost, 0))
        items.append((functools.partial(conv, c), 0, conv_cost))
    return items


def _coeff_items(ub_src, u_src, wg_ref, bg_ref, lam_ref, pre_scr):
    R = N_STREAMS * T_SEG
    C = D_RNN

    def coeff(t):
        cols = slice(t * LANES, (t + 1) * LANES)
        xcols = slice(C + t * LANES, C + (t + 1) * LANES)
        lam = lam_ref[:, cols]
        log_sig = jnp.minimum(lam, 0.0) - jnp.log1p(jnp.exp(-jnp.abs(lam)))
        hc = (0.5 * LRU_C) * log_sig
        log_a = hc * jnp.tanh(pre_scr[:, cols]) + hc
        ig = 0.5 * jnp.tanh(pre_scr[:, xcols]) + 0.5
        a = jnp.exp(log_a)
        gain = jnp.tanh(log_a) * (-1.0 - a * a)
        root = gain * lax.rsqrt(jnp.maximum(gain, F32_TINY))
        pre_scr[:, cols] = a
        pre_scr[:, xcols] = root * (ig * u_src[:, cols])

    first_touch = {}
    last_touch = {}
    for i in range(N_LRU_BLOCKS):
        for part in range(MXU_DIM // LANES):
            tile = GATE_OFF[i] // LANES + part
            first_touch.setdefault(tile, i)
            last_touch[tile] = i

    def gate(i):
        off = GATE_OFF[i]
        y = _dot(ub_src[:, off:off + MXU_DIM], wg_ref[i])
        for half in range(2):
            for part in range(MXU_DIM // LANES):
                col = half * C + off + part * LANES
                val = y[:, half * MXU_DIM + part * LANES:half * MXU_DIM + (part + 1) * LANES]
                if first_touch[off // LANES + part] == i:
                    pre_scr[:, col:col + LANES] = val + bg_ref[:, col:col + LANES]
                else:
                    pre_scr[:, col:col + LANES] = pre_scr[:, col:col + LANES] + val

    gate_cost = (_mxu_cycles(R, MXU_DIM, 2 * MXU_DIM), _vpu_cycles(R, 2 * MXU_DIM, 1))
    coeff_cost = _vpu_cycles(R, LANES, 20)
    items = [(functools.partial(gate, 0),) + gate_cost]
    for i in range(N_LRU_BLOCKS):
        if i + 1 < N_LRU_BLOCKS:
            items.append((functools.partial(gate, i + 1),) + gate_cost)
        for t in sorted(t for t, last in last_touch.items() if last == i):
            items.append((functools.partial(coeff, t), 0, coeff_cost))
    return items


def _stage_scan(pre_scr, p_scr, carry_scr, h_dst, *, ns, fwd, first):
    T = T_SEG
    C = D_RNN

    @pl.when(first)
    def _():
        carry_scr[...] = jnp.zeros_like(carry_scr)

    def scan_step(tt, carry):
        h, p = carry
        t = tt if fwd else T - 1 - tt
        r0 = pl.multiple_of(t * N_STREAMS, N_STREAMS)
        a = pre_scr[pl.ds(r0, N_STREAMS), 0:C]
        h = a * h + pre_scr[pl.ds(r0, N_STREAMS), C:2 * C]
        h_dst[pl.ds(r0, N_STREAMS), :] = h
        if ns > 1:
            p = p * a
            p_scr[pl.ds(r0, N_STREAMS), :] = p
        return h, p

    if ns == 1:
        h0 = carry_scr[...]
    else:
        h0 = jnp.zeros((N_STREAMS, C), F32)
    h_end, p_end = lax.fori_loop(0, T, scan_step, (h0, jnp.ones((N_STREAMS, C), F32)), unroll=SCAN_UNROLL)

    if ns == 1:
        carry_scr[...] = h_end
        return None

    sub = lax.broadcasted_iota(jnp.int32, (N_STREAMS, C), 0)
    seg = sub & (ns - 1)
    carry = carry_scr[...]
    head = (seg == 0) if fwd else (seg == ns - 1)
    from_prev = 1 if fwd else N_STREAMS - 1
    final = jnp.zeros((N_STREAMS, C), F32)
    cin = final
    for _ in range(ns):
        cin = jnp.where(head, carry, pltpu.roll(final, from_prev, 0))
        final = h_end + p_end * cin
    to_head = (N_STREAMS - (ns - 1)) % N_STREAMS if fwd else ns - 1
    carry_scr[...] = pltpu.roll(final, to_head, 0)
    return cin


def _add_carry(h, p, cin):
    shape3 = (h.shape[0] // N_STREAMS, N_STREAMS, h.shape[1])
    return (h.reshape(shape3) + p.reshape(shape3) * cin[None]).reshape(h.shape)


def _lru_bwd_kernel(x4_ref, xh_ref, pm_ref, wu_ref, bu_ref, wc_ref, bc_ref, wg_ref, bg_ref, lam_ref,
                    wkv_ref, bkv_ref,
                    h_ref, u_ref, k_ref, v_ref,
                    xb_scr, lhs_scr, uall_scr, u_scr, ub_scr, pre_scr, p_scr, carry_scr,
                    *, nb, ns, nj):
    T = T_SEG
    R = N_STREAMS * T
    s = pl.program_id(1)
    c_in = jnp.minimum(s, nj - 1)
    c_scan = jnp.clip(s - 2, 0, nj - 1)
    slot = s % 2

    @pl.when(s == 0)
    def _():
        pre_scr[...] = jnp.zeros_like(pre_scr)
        u_scr[1] = jnp.zeros((R, D_RNN), F32)
        ub_scr[1] = jnp.zeros((R, D_RNN), BF16)

    cin = _stage_scan(pre_scr, p_scr, carry_scr, h_ref, ns=ns, fwd=False, first=(c_scan == 0))
    if cin is not None:
        for it in range(R // LRU_TILE):
            rows = slice(it * LRU_TILE, (it + 1) * LRU_TILE)
            h_ref[rows, :] = _add_carry(h_ref[rows, :], p_scr[rows, :], cin)

    def kv_project():
        kv = _dot(xb_scr[...], wkv_ref[...]) + bkv_ref[...]
        for b in range(nb):
            for sg in range(ns):
                i = b * ns + sg
                k_ref[b, sg] = kv[i * T:(i + 1) * T, 0:KV_DIM].astype(BF16)
                v_ref[b, sg] = kv[i * T:(i + 1) * T, KV_DIM:2 * KV_DIM].astype(BF16)

    input_items = _input_items(x4_ref, xh_ref, pm_ref, wu_ref, bu_ref, wc_ref, bc_ref,
                               xb_scr, lhs_scr, uall_scr, u_scr.at[slot], ub_scr.at[slot], u_ref,
                               nb=nb, ns=ns, jj=nj - 1 - c_in, nj=nj)
    input_items.append((kv_project, _mxu_cycles(R, D_MODEL, 2 * KV_DIM), _vpu_cycles(R, 2 * KV_DIM, 2)))
    _interleave(_coeff_items(ub_scr.at[1 - slot], u_scr.at[1 - slot], wg_ref, bg_ref, lam_ref, pre_scr),
                input_items)


def _lru_fwd_kernel(x4_ref, u_ref, hb_ref, pm_ref, pmt_ref, wg_ref, bg_ref,
                    lam_ref, wgl_ref, bgl_ref, wmg_ref, bmg_ref, wout_ref,
                    m_ref,
                    xb_scr, lhs_scr, ub_scr, pre_scr, p_scr, carry_scr,
                    h_scr, g_scr, hg_scr, hgn_scr,
                    *, nb, ns, nj):
    T = T_SEG
    R = N_STREAMS * T
    C = D_RNN
    ncol = C // LRU_COLS
    s = pl.program_id(1)
    c_out = jnp.clip(s - 1, 0, nj - 1)
    slot = s % 2

    @pl.when(s == 0)
    def _():
        pre_scr[...] = jnp.zeros_like(pre_scr)
        xb_scr[1] = jnp.zeros((R, D_MODEL), BF16)
        lhs_scr[1] = jnp.zeros((R, D_MODEL), BF16)

    cin = _stage_scan(pre_scr, p_scr, carry_scr, h_scr, ns=ns, fwd=True, first=(c_out == 0))

    def gate_project(c):
        cs = slice(c * LRU_COLS, (c + 1) * LRU_COLS)
        g_scr[c % 2] = _dot(lhs_scr[1 - slot], wgl_ref[:, cs]) + bgl_ref[:, cs]

    def gate_apply(c):
        cs = slice(c * LRU_COLS, (c + 1) * LRU_COLS)
        h = h_scr[:, cs]
        if cin is not None:
            h = _add_carry(h, p_scr[:, cs], cin[:, cs])
        h = h + hb_ref[:, cs]
        hg_scr[:, cs] = (h * _gelu(g_scr[c % 2])).astype(BF16)

    def unpermute(c):
        cs = slice(c * LRU_COLS, (c + 1) * LRU_COLS)
        for tb in range(T // PERM_T):
            nat = _dot(pmt_ref[...], hg_scr[MXU_DIM * tb:MXU_DIM * (tb + 1), cs]).astype(BF16)
            for i in range(N_STREAMS):
                hgn_scr[i * T + PERM_T * tb:i * T + PERM_T * (tb + 1), cs] = nat[i * PERM_T:(i + 1) * PERM_T]

    def out_project(c):
        cs = slice(c * LRU_COLS, (c + 1) * LRU_COLS)
        y_lru = _dot(hgn_scr[...], wout_ref[:, cs])
        gate = _sigmoid_of_twice(_dot(xb_scr[1 - slot], wmg_ref[:, cs]) + bmg_ref[:, cs])
        m = (gate * y_lru).astype(BF16)
        for b in range(nb):
            for sg in range(ns):
                i = b * ns + sg
                m_ref[b, sg, :, cs] = m[i * T:(i + 1) * T]

    gp_cost = _mxu_cycles(R, D_MODEL, LRU_COLS)
    out_items = [(functools.partial(gate_project, 0), gp_cost, 0)]
    for c in range(ncol):
        if c + 1 < ncol:
            out_items.append((functools.partial(gate_project, c + 1), gp_cost, 0))
        out_items.append((functools.partial(gate_apply, c), 0, _vpu_cycles(R, LRU_COLS, 14)))
        out_items.append((functools.partial(unpermute, c), (T // PERM_T) * _mxu_cycles(MXU_DIM, MXU_DIM, LRU_COLS),
                          _vpu_cycles(R, LRU_COLS, 1)))
    for c in range(D_MODEL // LRU_COLS):
        out_items.append((functools.partial(out_project, c),
                          _mxu_cycles(R, C, LRU_COLS) + _mxu_cycles(R, D_MODEL, LRU_COLS),
                          _vpu_cycles(R, LRU_COLS, 5)))

    def cast_u(c):
        cs = slice(c * LRU_COLS, (c + 1) * LRU_COLS)
        ub_scr[:, cs] = u_ref[:, cs].astype(BF16)

    in_items = _permute_items(x4_ref, pm_ref, xb_scr.at[slot], lhs_scr.at[slot], 0, nb=nb, ns=ns)
    in_items += [(functools.partial(cast_u, c), 0, _vpu_cycles(R, LRU_COLS, 1)) for c in range(ncol)]
    in_items += _coeff_items(ub_scr, u_ref, wg_ref, bg_ref, lam_ref, pre_scr)
    _interleave(out_items, in_items)


def _stream_split(batch):
    nb = math.gcd(batch, N_STREAMS)
    return nb, N_STREAMS // nb


def _halo_rows(x4, nb, ns):
    B, Q, T, D = x4.shape
    last2 = x4[:, :, T - 2:, :]
    first = x4[:, :, 0, :]
    prev2 = jnp.pad(last2, ((0, 0), (1, 0), (0, 0), (0, 0)))[:, :Q]
    next1 = jnp.pad(first, ((0, 0), (0, 1), (0, 0)))[:, 1:]
    slots = jnp.stack([prev2[:, :, 0], prev2[:, :, 1], next1, jnp.zeros_like(next1)], axis=2)
    nj = Q // ns
    slots = slots.reshape(B // nb, nb, nj, ns, 4, D).transpose(0, 2, 4, 1, 3, 5)
    return slots.reshape(B // nb, nj, HALO_ROWS, D)


def _perm_matrix():
    pm = np.zeros((MXU_DIM, MXU_DIM), np.float32)
    for i in range(N_STREAMS):
        for t in range(PERM_T):
            pm[t * N_STREAMS + i, i * PERM_T + t] = 1.0
    return pm


def _lru_bwd_call(x, p):
    B, S, D = x.shape
    nb, ns = _stream_split(B)
    T = T_SEG
    R = N_STREAMS * T
    Q = S // T
    nj = Q // ns
    x4 = x.reshape(B, Q, T, D)
    xh = _halo_rows(x4, nb, ns)
    C = D_RNN
    chunk_in = lambda bb, s: (bb, nj - 1 - jnp.minimum(s, nj - 1), 0, 0)
    chunk_scan = lambda bb, s: (bb, nj - 1 - jnp.clip(s - 2, 0, nj - 1), 0, 0)
    in_specs = [
        pl.BlockSpec((nb, ns, T, D), chunk_in),
        pl.BlockSpec((None, None, HALO_ROWS, D), chunk_in),
        _const_spec((MXU_DIM, MXU_DIM)),
        _const_spec((D, C)), _const_spec((1, C)),
        _const_spec((LRU_CONV_W, C)), _const_spec((1, C)),
        _const_spec((N_LRU_BLOCKS, MXU_DIM, 2 * MXU_DIM)), _const_spec((1, 2 * C)),
        _const_spec((1, C)),
        _const_spec((D, 2 * KV_DIM)), _const_spec((1, 2 * KV_DIM)),
    ]
    out_shape = [
        jax.ShapeDtypeStruct((B // nb, nj, R, C), F32),
        jax.ShapeDtypeStruct((B // nb, nj, R, C), F32),
        jax.ShapeDtypeStruct((B, Q, T, KV_DIM), BF16),
        jax.ShapeDtypeStruct((B, Q, T, KV_DIM), BF16),
    ]
    out_specs = [
        pl.BlockSpec((None, None, R, C), chunk_scan),
        pl.BlockSpec((None, None, R, C), chunk_in),
        pl.BlockSpec((nb, ns, T, KV_DIM), chunk_in),
        pl.BlockSpec((nb, ns, T, KV_DIM), chunk_in),
    ]
    scratch = [
        pltpu.VMEM((R, D), BF16),
        pltpu.VMEM((R + HALO_ROWS, D), BF16),
        pltpu.VMEM((R + HALO_ROWS, C), F32),
        pltpu.VMEM((2, R, C), F32),
        pltpu.VMEM((2, R, C), BF16),
        pltpu.VMEM((R, 2 * C), F32),
        pltpu.VMEM((R, C), F32),
        pltpu.VMEM((N_STREAMS, C), F32),
    ]
    h_b, u, k, v = pl.pallas_call(
        functools.partial(_lru_bwd_kernel, nb=nb, ns=ns, nj=nj),
        grid=(B // nb, nj + 2),
        in_specs=in_specs, out_specs=out_specs, out_shape=out_shape,
        scratch_shapes=scratch,
        compiler_params=pltpu.CompilerParams(
            dimension_semantics=("arbitrary", "arbitrary"), vmem_limit_bytes=VMEM_LIMIT),
        name="lru_bwd",
    )(x4, xh, p["pm"], p["w_u"], p["b_u"], p["w_conv"], p["b_conv"], p["wg_b"], p["bg_b"], p["lam_b"],
      p["w_kv"], p["b_kv"])
    return h_b, u, k.reshape(B, S, KV_DIM), v.reshape(B, S, KV_DIM)


def _lru_fwd_call(x, u, h_b, p):
    B, S, D = x.shape
    nb, ns = _stream_split(B)
    T = T_SEG
    R = N_STREAMS * T
    Q = S // T
    nj = Q // ns
    x4 = x.reshape(B, Q, T, D)
    C = D_RNN
    chunk_in = lambda bb, s: (bb, jnp.minimum(s, nj - 1), 0, 0)
    chunk_out = lambda bb, s: (bb, jnp.clip(s - 1, 0, nj - 1), 0, 0)
    in_specs = [
        pl.BlockSpec((nb, ns, T, D), chunk_in),
        pl.BlockSpec((None, None, R, C), chunk_in),
        pl.BlockSpec((None, None, R, C), chunk_out),
        _const_spec((MXU_DIM, MXU_DIM)), _const_spec((MXU_DIM, MXU_DIM)),
        _const_spec((N_LRU_BLOCKS, MXU_DIM, 2 * MXU_DIM)), _const_spec((1, 2 * C)),
        _const_spec((1, C)),
        _const_spec((D, C)), _const_spec((1, C)),
        _const_spec((D, D)), _const_spec((1, D)),
        _const_spec((C, D)),
    ]
    scratch = [
        pltpu.VMEM((2, R, D), BF16),
        pltpu.VMEM((2, R, D), BF16),
        pltpu.VMEM((R, C), BF16),
        pltpu.VMEM((R, 2 * C), F32),
        pltpu.VMEM((R, C), F32),
        pltpu.VMEM((N_STREAMS, C), F32),
        pltpu.VMEM((R, C), F32),
        pltpu.VMEM((2, R, LRU_COLS), F32),
        pltpu.VMEM((R, C), BF16),
        pltpu.VMEM((R, C), BF16),
    ]
    m = pl.pallas_call(
        functools.partial(_lru_fwd_kernel, nb=nb, ns=ns, nj=nj),
        grid=(B // nb, nj + 1),
        in_specs=in_specs,
        out_specs=pl.BlockSpec((nb, ns, T, D), chunk_out),
        out_shape=jax.ShapeDtypeStruct((B, Q, T, D), BF16),
        scratch_shapes=scratch,
        compiler_params=pltpu.CompilerParams(
            dimension_semantics=("arbitrary", "arbitrary"), vmem_limit_bytes=VMEM_LIMIT),
        name="lru_fwd",
    )(x4, u, h_b, p["pm"], p["pmt"], p["wg_f"], p["bg_f"],
      p["lam_f"], p["w_gl"], p["b_gl"], p["w_mg_lru"], p["b_mg_lru"], p["w_lru_out"])
    return m.reshape(B, S, D)


def _attn_kernel(x_ref, m_ref, kp_ref, km_ref, kn_ref, vp_ref, vm_ref, vn_ref, idx_ref, rel_ref, sink_ref,
                 wq_ref, bq_ref, wga_ref, bga_ref, wao_ref, wo_ref, bo_ref, g1_ref, b1_ref,
                 o_ref, q_scr, attn_scr, bias_scr, *, nr, alpha):
    R = ROWS_ATTN
    nblk = R // BLOCK
    r = pl.program_id(1)

    @pl.when((pl.program_id(0) == 0) & (r == 0))
    def _():
        idx = idx_ref[...]
        for h in range(N_HEADS):
            def pick(bkt, acc):
                return jnp.where(idx == bkt, rel_ref[bkt, h], acc)
            tab = lax.fori_loop(0, N_BUCKETS, pick, jnp.full((BLOCK, 3 * BLOCK), NEG_INF, F32))
            kvh, g = divmod(h, GROUP)
            bias_scr[kvh, g * BLOCK:(g + 1) * BLOCK, :] = tab

    xb = x_ref[...].astype(BF16)
    q_scr[...] = (_dot(xb, wq_ref[...]) + bq_ref[...]).astype(BF16)

    pen_prev = (r == 0).astype(F32) * NEG_INF
    pen_next = (r == nr - 1).astype(F32) * NEG_INF
    row_head = lax.broadcasted_iota(jnp.int32, (GROUP * BLOCK, 1), 0) // BLOCK

    for kvh in range(N_KV_HEADS):
        cols = slice(kvh * HEAD_DIM, (kvh + 1) * HEAD_DIM)
        sink = jnp.zeros((GROUP * BLOCK, 1), F32)
        for g in range(GROUP):
            sink = jnp.where(row_head == g, sink_ref[kvh * GROUP + g], sink)
        for n in range(nblk):
            q4 = jnp.concatenate(
                [q_scr[n * BLOCK:(n + 1) * BLOCK, (kvh * GROUP + g) * HEAD_DIM:(kvh * GROUP + g + 1) * HEAD_DIM]
                 for g in range(GROUP)], axis=0)
            own = slice(n * BLOCK, (n + 1) * BLOCK)
            prv = slice((n - 1) * BLOCK, n * BLOCK)
            nxt = slice((n + 1) * BLOCK, (n + 2) * BLOCK)
            k_prev = km_ref[prv, cols] if n > 0 else kp_ref[:, cols]
            v_prev = vm_ref[prv, cols] if n > 0 else vp_ref[:, cols]
            k_next = km_ref[nxt, cols] if n < nblk - 1 else kn_ref[:, cols]
            v_next = vm_ref[nxt, cols] if n < nblk - 1 else vn_ref[:, cols]
            k_ext = jnp.concatenate([k_prev, km_ref[own, cols], k_next], axis=0)
            v_ext = jnp.concatenate([v_prev, vm_ref[own, cols], v_next], axis=0)
            s = lax.dot_general(q4, k_ext, (((1,), (1,)), ((), ())), preferred_element_type=F32)
            s = s + bias_scr[kvh]
            if n == 0:
                s = jnp.concatenate([s[:, 0:BLOCK] + pen_prev, s[:, BLOCK:]], axis=1)
            if n == nblk - 1:
                s = jnp.concatenate([s[:, 0:2 * BLOCK], s[:, 2 * BLOCK:] + pen_next], axis=1)
            mx = jnp.maximum(jnp.max(s, axis=1, keepdims=True), sink)
            pr = jnp.exp(s - mx)
            denom = jnp.sum(pr, axis=1, keepdims=True) + jnp.exp(sink - mx)
            o = _dot(pr.astype(BF16), v_ext) * (1.0 / denom)
            for g in range(GROUP):
                h = kvh * GROUP + g
                attn_scr[own, h * HEAD_DIM:(h + 1) * HEAD_DIM] = o[g * BLOCK:(g + 1) * BLOCK].astype(BF16)

    y_attn = _dot(attn_scr[...], wao_ref[...])
    gate = _sigmoid_of_twice(_dot(xb, wga_ref[...]) + bga_ref[...])
    merged = (gate * y_attn + m_ref[...].astype(F32)).astype(BF16)
    mix = _dot(merged, wo_ref[...]) + bo_ref[...]
    o_ref[...] = _layer_norm(alpha * x_ref[...] + mix, g1_ref[...], b1_ref[...])


def _attn_call(x, m_lru, k, v, p, alpha):
    B, S, D = x.shape
    R = ROWS_ATTN
    nr = S // R
    bpr = R // BLOCK
    nkb = S // BLOCK
    main = lambda b, r: (b, r, 0)
    prev = lambda b, r: (b, jnp.maximum(r * bpr - 1, 0), 0)
    nxt = lambda b, r: (b, jnp.minimum((r + 1) * bpr, nkb - 1), 0)
    kv_main = pl.BlockSpec((None, R, KV_DIM), main)
    kv_prev = pl.BlockSpec((None, BLOCK, KV_DIM), prev)
    kv_next = pl.BlockSpec((None, BLOCK, KV_DIM), nxt)
    in_specs = [
        pl.BlockSpec((None, R, D), main),
        pl.BlockSpec((None, R, D), main),
        kv_prev, kv_main, kv_next,
        kv_prev, kv_main, kv_next,
        _const_spec((BLOCK, 3 * BLOCK)),
        pl.BlockSpec(memory_space=pltpu.SMEM),
        pl.BlockSpec(memory_space=pltpu.SMEM),
        _const_spec((D, Q_DIM)), _const_spec((1, Q_DIM)),
        _const_spec((D, D)), _const_spec((1, D)),
        _const_spec((Q_DIM, D)),
        _const_spec((D, D)), _const_spec((1, D)),
        _const_spec((1, D)), _const_spec((1, D)),
    ]
    scratch = [
        pltpu.VMEM((R, Q_DIM), BF16),
        pltpu.VMEM((R, Q_DIM), BF16),
        pltpu.VMEM((N_KV_HEADS, GROUP * BLOCK, 3 * BLOCK), F32),
    ]
    return pl.pallas_call(
        functools.partial(_attn_kernel, nr=nr, alpha=alpha),
        grid=(B, nr),
        in_specs=in_specs,
        out_specs=pl.BlockSpec((None, R, D), main),
        out_shape=jax.ShapeDtypeStruct((B, S, D), F32),
        scratch_shapes=scratch,
        compiler_params=pltpu.CompilerParams(
            dimension_semantics=("arbitrary", "arbitrary"), vmem_limit_bytes=VMEM_LIMIT),
        name="attn_merge",
    )(x, m_lru, k, k, k, v, v, v, p["bias_idx"], p["rel_table"], p["sink"],
      p["w_q"], p["b_q"], p["w_mg_attn"], p["b_mg_attn"], p["w_attn_out"], p["w_o"], p["b_o"],
      p["ln1_g"], p["ln1_b"])


def _ffn_kernel(x_ref, xp_ref, xn_ref, wup_ref, bup_ref, wcv_ref, bcv_ref, wdn_ref, bdn_ref, g2_ref, b2_ref,
                o_ref, xe_scr, hs_scr, act_scr, *, nr, alpha):
    R = ROWS_FFN
    H = FFN_HALO
    nt = D_FF // FFN_COLS
    nslab = FFN_COLS // LANES
    r = pl.program_id(1)
    keep_prev = (r > 0).astype(F32)
    keep_next = (r < nr - 1).astype(F32)
    zeros = jnp.zeros((H - SUBLANES, D_MODEL), F32)
    xe_scr[...] = jnp.concatenate(
        [zeros, xp_ref[...] * keep_prev, x_ref[...], xn_ref[...] * keep_next, zeros], axis=0).astype(BF16)

    row = lax.broadcasted_iota(jnp.int32, (SUBLANES, FFN_COLS), 0)
    pad_prev = (row == SUBLANES - 1).astype(F32) * (1.0 - keep_prev)
    pad_next = (row == 0).astype(F32) * (1.0 - keep_next)

    def up_proj(j):
        xe = xe_scr[...]
        for half in range(2):
            y = _dot(xe, wup_ref[half, j])
            b_up = bup_ref[half, j]
            y = jnp.concatenate(
                [y[0:H - SUBLANES], y[H - SUBLANES:H] - pad_prev * b_up, y[H:H + R],
                 y[H + R:H + R + SUBLANES] - pad_next * b_up, y[H + R + SUBLANES:]], axis=0)
            for c in range(nslab):
                hs_scr[j % 2, half, c] = y[:, c * LANES:(c + 1) * LANES]

    def conv(j, half, c):
        lanes = slice(c * LANES, (c + 1) * LANES)
        w = wcv_ref[half, j][:, lanes]
        bias = bcv_ref[half, j][:, lanes] + bup_ref[half, j][:, lanes] * (w[0:1] + w[1:2] + w[2:3])
        return (w[0:1] * hs_scr[j % 2, half, c, H - 1:H - 1 + R, :]
                + w[1:2] * hs_scr[j % 2, half, c, H:H + R, :]
                + w[2:3] * hs_scr[j % 2, half, c, H + 1:H + 1 + R, :] + bias)

    def gate(j):
        for c in range(nslab):
            act = _gelu(conv(j, 1, c)) * conv(j, 0, c)
            act_scr[:, j * FFN_COLS + c * LANES:j * FFN_COLS + (c + 1) * LANES] = act.astype(BF16)

    up_proj(0)
    for j in range(nt):
        if j + 1 < nt:
            up_proj(j + 1)
        gate(j)

    y = alpha * x_ref[...] + _dot(act_scr[...], wdn_ref[...]) + bdn_ref[...]
    o_ref[...] = _layer_norm(y, g2_ref[...], b2_ref[...])


def _ffn_call(x, p, alpha):
    B, S, D = x.shape
    R = ROWS_FFN
    nr = S // R
    rpb = R // SUBLANES
    nsb = S // SUBLANES
    nt = D_FF // FFN_COLS
    main = lambda b, r: (b, r, 0)
    prev = lambda b, r: (b, jnp.maximum(r * rpb - 1, 0), 0)
    nxt = lambda b, r: (b, jnp.minimum((r + 1) * rpb, nsb - 1), 0)
    in_specs = [
        pl.BlockSpec((None, R, D), main),
        pl.BlockSpec((None, SUBLANES, D), prev),
        pl.BlockSpec((None, SUBLANES, D), nxt),
        _const_spec((2, nt, D, FFN_COLS)), _const_spec((2, nt, 1, FFN_COLS)),
        _const_spec((2, nt, 3, FFN_COLS)), _const_spec((2, nt, 1, FFN_COLS)),
        _const_spec((D_FF, D)), _const_spec((1, D)),
        _const_spec((1, D)), _const_spec((1, D)),
    ]
    scratch = [
        pltpu.VMEM((R + 2 * FFN_HALO, D), BF16),
        pltpu.VMEM((2, 2, FFN_COLS // LANES, R + 2 * FFN_HALO, LANES), F32),
        pltpu.VMEM((R, D_FF), BF16),
    ]
    return pl.pallas_call(
        functools.partial(_ffn_kernel, nr=nr, alpha=alpha),
        grid=(B, nr),
        in_specs=in_specs,
        out_specs=pl.BlockSpec((None, R, D), main),
        out_shape=jax.ShapeDtypeStruct((B, S, D), F32),
        scratch_shapes=scratch,
        compiler_params=pltpu.CompilerParams(
            dimension_semantics=("arbitrary", "arbitrary"), vmem_limit_bytes=VMEM_LIMIT),
        name="ffn",
    )(x, x, x, p["w_up"], p["b_up"], p["w_ffn_conv"], p["b_ffn_conv"], p["w_down"], p["b_down"],
      p["ln2_g"], p["ln2_b"])


def _t5_bucket(rel):
    nb = N_BUCKETS // 2
    ret = jnp.where(rel > 0, nb, 0)
    n = jnp.abs(rel)
    max_exact = nb // 2
    nf = jnp.maximum(n, 1).astype(jnp.float32)
    large = max_exact + (jnp.log(nf / max_exact) / math.log(MAX_DISTANCE / max_exact)
                         * (nb - max_exact)).astype(jnp.int32)
    large = jnp.minimum(large, nb - 1)
    return ret + jnp.where(n < max_exact, n, large)


def _pack_gate_weights(w_a, w_x):
    out = jnp.zeros((N_LRU_BLOCKS, MXU_DIM, 2 * MXU_DIM), F32)
    for i in range(N_LRU_BLOCKS):
        lo = LRU_BLOCK * i - GATE_OFF[i]
        out = out.at[i, lo:lo + LRU_BLOCK, lo:lo + LRU_BLOCK].set(w_a[i])
        out = out.at[i, lo:lo + LRU_BLOCK, MXU_DIM + lo:MXU_DIM + lo + LRU_BLOCK].set(w_x[i])
    return out.astype(BF16)


def _prepare(rel_table, w_in, b_in, w_lru_conv, b_lru_conv, w_rg_a, b_rg_a, w_rg_x, b_rg_x, lru_lambda,
             w_lru_out, attn_sink, w_attn_out, w_o, b_o, ln1_g, ln1_b, w_up, b_up, w_ffn_conv, b_ffn_conv,
             w_down, b_down, ln2_g, ln2_b):
    C = D_RNN
    o_g, o_q, o_k, o_gl, o_ga = C, 2 * C, 2 * C + Q_DIM, 2 * C + Q_DIM + 2 * KV_DIM, 2 * C + Q_DIM + 2 * KV_DIM + D_MODEL
    row = lambda a: a.reshape(1, -1).astype(F32)
    scale = HEAD_DIM ** -0.5
    nt = D_FF // FFN_COLS
    pm = _perm_matrix()

    q_off = jnp.arange(BLOCK)[:, None]
    c_off = jnp.arange(3 * BLOCK)[None, :]
    rel = c_off - BLOCK - q_off
    bias_idx = jnp.where(jnp.abs(rel) <= WINDOW, _t5_bucket(rel), -1).astype(jnp.int32)

    def ffn_cols(a):
        lead = a.shape[:-1]
        a = a.reshape(lead + (2, nt, FFN_COLS))
        return jnp.moveaxis(a, (-3, -2), (0, 1))

    return {
        "pm": jnp.asarray(pm, BF16), "pmt": jnp.asarray(pm.T, BF16),
        "w_u": w_in[:, 0:o_g].astype(BF16), "b_u": row(b_in[0:o_g]),
        "w_gl": w_in[:, o_g:o_q].astype(BF16), "b_gl": row(b_in[o_g:o_q]),
        "w_q": (w_in[:, o_q:o_k] * scale).astype(BF16), "b_q": row(b_in[o_q:o_k] * scale),
        "w_kv": w_in[:, o_k:o_gl].astype(BF16), "b_kv": row(b_in[o_k:o_gl]),
        "w_mg_lru": (0.5 * w_in[:, o_gl:o_ga]).astype(BF16), "b_mg_lru": row(0.5 * b_in[o_gl:o_ga]),
        "w_mg_attn": (0.5 * w_in[:, o_ga:]).astype(BF16), "b_mg_attn": row(0.5 * b_in[o_ga:]),
        "w_conv": w_lru_conv.astype(F32), "b_conv": row(b_lru_conv),
        "wg_f": _pack_gate_weights(0.5 * w_rg_a[0], 0.5 * w_rg_x[0]),
        "bg_f": row(0.5 * jnp.concatenate([b_rg_a[0].reshape(-1), b_rg_x[0].reshape(-1)])),
        "wg_b": _pack_gate_weights(0.5 * w_rg_a[1], 0.5 * w_rg_x[1]),
        "bg_b": row(0.5 * jnp.concatenate([b_rg_a[1].reshape(-1), b_rg_x[1].reshape(-1)])),
        "lam_f": row(lru_lambda[0]), "lam_b": row(lru_lambda[1]),
        "w_lru_out": w_lru_out.astype(BF16),
        "bias_idx": bias_idx, "rel_table": rel_table.astype(F32), "sink": attn_sink.astype(F32),
        "w_attn_out": w_attn_out.astype(BF16),
        "w_o": w_o.astype(BF16), "b_o": row(b_o),
        "ln1_g": row(ln1_g), "ln1_b": row(ln1_b),
        "w_up": ffn_cols(w_up).astype(BF16), "b_up": ffn_cols(b_up.reshape(1, -1)).astype(F32),
        "w_ffn_conv": ffn_cols(w_ffn_conv).astype(F32), "b_ffn_conv": ffn_cols(b_ffn_conv.reshape(1, -1)).astype(F32),
        "w_down": w_down.astype(BF16), "b_down": row(b_down),
        "ln2_g": row(ln2_g), "ln2_b": row(ln2_b),
    }


def _encoder_layer(x, p, alpha):
    h_b, u, k, v = _lru_bwd_call(x, p)
    m_lru = _lru_fwd_call(x, u, h_b, p)
    x1 = _attn_call(x, m_lru, k, v, p, alpha)
    return _ffn_call(x1, p, alpha)


def kernel(x_prompt, x_sample, rel_table, w_in, b_in, w_lru_conv, b_lru_conv, w_rg_a, b_rg_a, w_rg_x, b_rg_x, lru_lambda, w_lru_out, attn_sink, w_attn_out, w_o, b_o, ln1_g, ln1_b, w_up, b_up, w_ffn_conv, b_ffn_conv, w_down, b_down, ln2_g, ln2_b):
    depth = w_in.shape[0]
    alpha = (2.0 * depth) ** 0.25
    y_prompt, y_sample = x_prompt, x_sample
    for l in range(depth):
        p = _prepare(rel_table, w_in[l], b_in[l], w_lru_conv[l], b_lru_conv[l], w_rg_a[l], b_rg_a[l],
                     w_rg_x[l], b_rg_x[l], lru_lambda[l], w_lru_out[l], attn_sink[l], w_attn_out[l], w_o[l],
                     b_o[l], ln1_g[l], ln1_b[l], w_up[l], b_up[l], w_ffn_conv[l], b_ffn_conv[l], w_down[l],
                     b_down[l], ln2_g[l], ln2_b[l])
        y_prompt = _encoder_layer(y_prompt, p, alpha)
        y_sample = _encoder_layer(y_sample, p, alpha)
    return (y_prompt, y_sample)
```

```python
import functools
import math

import numpy as np
import jax
import jax.numpy as jnp
from jax import lax
from jax.experimental import pallas as pl
from jax.experimental.pallas import tpu as pltpu

F32 = jnp.float32
BF16 = jnp.bfloat16

D_MODEL = 1024
D_RNN = 1280
N_LRU_BLOCKS = 8
LRU_BLOCK = D_RNN // N_LRU_BLOCKS
LRU_C = 8.0
LRU_CONV_W = 4
HEAD_DIM = 128
N_HEADS = 8
N_KV_HEADS = 2
GROUP = N_HEADS // N_KV_HEADS
WINDOW = 128
BLOCK = 128
N_BUCKETS = 32
MAX_DISTANCE = 128
Q_DIM = N_HEADS * HEAD_DIM
KV_DIM = N_KV_HEADS * HEAD_DIM
D_FF = 3 * D_MODEL
LN_EPS = 1e-5
NEG_INF = -1e30
F32_TINY = float(np.finfo(np.float32).tiny)

LANES = 128
SUBLANES = 8
MXU_DIM = 256

N_STREAMS = SUBLANES
T_SEG = 64
PERM_T = MXU_DIM // N_STREAMS
HALO_ROWS = 4 * N_STREAMS
ROWS_ATTN = 512
ATTN_COLS = 256
ROWS_FFN = 512
FFN_COLS = 512
FFN_DOWN_SPLIT = 3
FFN_HALO = 16
LRU_TILE = 64
LRU_COLS = 256
SCAN_UNROLL = 8
VMEM_LIMIT = 56 * 1024 * 1024

GATE_OFF = tuple(LANES * ((LRU_BLOCK * i) // LANES) for i in range(N_LRU_BLOCKS))


def _sigmoid_of_twice(z):
    return 0.5 * jnp.tanh(z) + 0.5


def _gelu(x):
    c = math.sqrt(2.0 / math.pi)
    inner = x * (c + (c * 0.044715) * (x * x))
    return x * (0.5 * jnp.tanh(inner) + 0.5)


def _layer_norm(y, g, b):
    mu = jnp.mean(y, axis=-1, keepdims=True)
    yc = y - mu
    var = jnp.mean(yc * yc, axis=-1, keepdims=True)
    return yc * lax.rsqrt(var + LN_EPS) * g + b


def _dot(a, b):
    return jnp.dot(a, b, preferred_element_type=F32)


def _const_spec(shape):
    nd = len(shape)
    return pl.BlockSpec(shape, lambda *_: (0,) * nd, pipeline_mode=pl.Buffered(1))


def _mxu_cycles(m, k, n):
    return (m // SUBLANES) * (n // LANES) * (-(-k // MXU_DIM))


def _vpu_cycles(rows, cols, ops_per_vreg):
    return rows * cols // (SUBLANES * LANES) * ops_per_vreg // 4


def _interleave(*streams):
    queues = [list(q) for q in streams]
    mxu = vpu = 0
    while any(queues):
        q = min((q for q in queues if q), key=lambda q: abs((mxu + q[0][1]) - (vpu + q[0][2])))
        fn, m, v = q.pop(0)
        fn()
        mxu += m
        vpu += v


def _permute_items(x4_ref, pm_ref, xb_dst, lhs_dst, row0, *, nb, ns):
    T = T_SEG
    R = N_STREAMS * T

    def copy():
        for b in range(nb):
            for sg in range(ns):
                i = b * ns + sg
                xb_dst[i * T:(i + 1) * T, :] = x4_ref[b, sg].astype(BF16)

    def permute(tb):
        src = jnp.concatenate(
            [xb_dst[i * T + PERM_T * tb:i * T + PERM_T * (tb + 1), :] for i in range(N_STREAMS)], axis=0)
        lhs_dst[row0 + MXU_DIM * tb:row0 + MXU_DIM * (tb + 1), :] = _dot(pm_ref[...], src).astype(BF16)

    items = [(copy, 0, _vpu_cycles(R, D_MODEL, 2))]
    for tb in range(T // PERM_T):
        items.append((functools.partial(permute, tb), _mxu_cycles(MXU_DIM, MXU_DIM, D_MODEL),
                      _vpu_cycles(MXU_DIM, D_MODEL, 1)))
    return items


def _input_items(x4_ref, xh_ref, pm_ref, wu_ref, bu_ref, wc_ref, bc_ref,
                 xb_dst, lhs_dst, uall_scr, u_dst, ub_dst, u_out, *, nb, ns, jj, nj):
    T = T_SEG
    R = N_STREAMS * T
    C = D_RNN
    ncol = C // LRU_COLS

    def halo():
        xh = xh_ref[...].astype(BF16)
        lhs_dst[0:2 * N_STREAMS, :] = xh[0:2 * N_STREAMS]
        lhs_dst[2 * N_STREAMS + R:, :] = xh[2 * N_STREAMS:]

    def project(c):
        cs = slice(c * LRU_COLS, (c + 1) * LRU_COLS)
        uall_scr[:, cs] = _dot(lhs_dst[...], wu_ref[:, cs]) + bu_ref[:, cs]
        seg = lax.broadcasted_iota(jnp.int32, (N_STREAMS, LRU_COLS), 0) & (ns - 1)
        keep_prev = 1.0 - (seg == 0).astype(F32) * (jj == 0).astype(F32)
        keep_next = 1.0 - (seg == ns - 1).astype(F32) * (jj == nj - 1).astype(F32)
        prev_rows = slice(0, 2 * N_STREAMS)
        next_rows = slice(2 * N_STREAMS + R, 3 * N_STREAMS + R)
        uall_scr[prev_rows, cs] = uall_scr[prev_rows, cs] * jnp.concatenate([keep_prev, keep_prev], axis=0)
        uall_scr[next_rows, cs] = uall_scr[next_rows, cs] * keep_next

    def conv(c):
        cs = slice(c * LRU_COLS, (c + 1) * LRU_COLS)
        acc = bc_ref[:, cs] + wc_ref[0:1, cs] * uall_scr[0:R, cs]
        for k in range(1, LRU_CONV_W):
            acc = acc + wc_ref[k:k + 1, cs] * uall_scr[N_STREAMS * k:N_STREAMS * k + R, cs]
        u_dst[:, cs] = acc
        ub_dst[:, cs] = acc.astype(BF16)
        u_out[:, cs] = acc.astype(BF16)

    items = _permute_items(x4_ref, pm_ref, xb_dst, lhs_dst, 2 * N_STREAMS, nb=nb, ns=ns)
    items.append((halo, 0, 0))
    project_cost = _mxu_cycles(R + HALO_ROWS, D_MODEL, LRU_COLS)
    conv_cost = _vpu_cycles(R, LRU_COLS, 2 * LRU_CONV_W + 1)
    items.append((functools.partial(project, 0), project_cost, 0))
    for c in range(ncol):
        if c + 1 < ncol:
            items.append((functools.partial(project, c + 1), project_cost, 0))
        items.append((functools.partial(conv, c), 0, conv_cost))
    return items


def _coeff_items(ub_src, u_src, wg_ref, bg_ref, lam_ref, pre_scr):
    R = N_STREAMS * T_SEG
    C = D_RNN

    def coeff(t):
        cols = slice(t * LANES, (t + 1) * LANES)
        xcols = slice(C + t * LANES, C + (t + 1) * LANES)
        lam = lam_ref[:, cols]
        log_sig = jnp.minimum(lam, 0.0) - jnp.log1p(jnp.exp(-jnp.abs(lam)))
        hc = (0.5 * LRU_C) * log_sig
        log_a = hc * jnp.tanh(pre_scr[:, cols]) + hc
        ig = 0.5 * jnp.tanh(pre_scr[:, xcols]) + 0.5
        a = jnp.exp(log_a)
        gain = jnp.tanh(log_a) * (-1.0 - a * a)
        root = gain * lax.rsqrt(jnp.maximum(gain, F32_TINY))
        pre_scr[:, cols] = a
        pre_scr[:, xcols] = root * (ig * u_src[:, cols])

    first_touch = {}
    last_touch = {}
    for i in range(N_LRU_BLOCKS):
        for part in range(MXU_DIM // LANES):
            tile = GATE_OFF[i] // LANES + part
            first_touch.setdefault(tile, i)
            last_touch[tile] = i

    def gate(i):
        off = GATE_OFF[i]
        y = _dot(ub_src[:, off:off + MXU_DIM], wg_ref[i])
        for half in range(2):
            for part in range(MXU_DIM // LANES):
                col = half * C + off + part * LANES
                val = y[:, half * MXU_DIM + part * LANES:half * MXU_DIM + (part + 1) * LANES]
                if first_touch[off // LANES + part] == i:
                    pre_scr[:, col:col + LANES] = val + bg_ref[:, col:col + LANES]
                else:
                    pre_scr[:, col:col + LANES] = pre_scr[:, col:col + LANES] + val

    gate_cost = (_mxu_cycles(R, MXU_DIM, 2 * MXU_DIM), _vpu_cycles(R, 2 * MXU_DIM, 1))
    coeff_cost = _vpu_cycles(R, LANES, 20)
    items = [(functools.partial(gate, 0),) + gate_cost]
    for i in range(N_LRU_BLOCKS):
        if i + 1 < N_LRU_BLOCKS:
            items.append((functools.partial(gate, i + 1),) + gate_cost)
        for t in sorted(t for t, last in last_touch.items() if last == i):
            items.append((functools.partial(coeff, t), 0, coeff_cost))
    return items


def _stage_scan(pre_scr, p_scr, carry_scr, h_dst, *, ns, fwd, first):
    T = T_SEG
    C = D_RNN

    @pl.when(first)
    def _():
        carry_scr[...] = jnp.zeros_like(carry_scr)

    def scan_step(tt, carry):
        h, p = carry
        t = tt if fwd else T - 1 - tt
        r0 = pl.multiple_of(t * N_STREAMS, N_STREAMS)
        a = pre_scr[pl.ds(r0, N_STREAMS), 0:C]
        h = a * h + pre_scr[pl.ds(r0, N_STREAMS), C:2 * C]
        h_dst[pl.ds(r0, N_STREAMS), :] = h
        if ns > 1:
            p = p * a
            p_scr[pl.ds(r0, N_STREAMS), :] = p
        return h, p

    if ns == 1:
        h0 = carry_scr[...]
    else:
        h0 = jnp.zeros((N_STREAMS, C), F32)
    h_end, p_end = lax.fori_loop(0, T, scan_step, (h0, jnp.ones((N_STREAMS, C), F32)), unroll=SCAN_UNROLL)

    if ns == 1:
        carry_scr[...] = h_end
        return None

    sub = lax.broadcasted_iota(jnp.int32, (N_STREAMS, C), 0)
    seg = sub & (ns - 1)
    carry = carry_scr[...]
    head = (seg == 0) if fwd else (seg == ns - 1)
    from_prev = 1 if fwd else N_STREAMS - 1
    final = jnp.zeros((N_STREAMS, C), F32)
    cin = final
    for _ in range(ns):
        cin = jnp.where(head, carry, pltpu.roll(final, from_prev, 0))
        final = h_end + p_end * cin
    to_head = (N_STREAMS - (ns - 1)) % N_STREAMS if fwd else ns - 1
    carry_scr[...] = pltpu.roll(final, to_head, 0)
    return cin


def _add_carry(h, p, cin):
    shape3 = (h.shape[0] // N_STREAMS, N_STREAMS, h.shape[1])
    return (h.reshape(shape3) + p.reshape(shape3) * cin[None]).reshape(h.shape)


def _lru_bwd_kernel(x4_ref, xh_ref, pm_ref, wu_ref, bu_ref, wc_ref, bc_ref, wg_ref, bg_ref, lam_ref,
                    wkv_ref, bkv_ref,
                    h_ref, u_ref, k_ref, v_ref,
                    xb_scr, lhs_scr, uall_scr, u_scr, ub_scr, pre_scr, p_scr, carry_scr,
                    *, nb, ns, nj):
    T = T_SEG
    R = N_STREAMS * T
    s = pl.program_id(1)
    c_in = jnp.minimum(s, nj - 1)
    c_scan = jnp.clip(s - 2, 0, nj - 1)
    slot = s % 2

    @pl.when(s == 0)
    def _():
        pre_scr[...] = jnp.zeros_like(pre_scr)
        u_scr[1] = jnp.zeros((R, D_RNN), F32)
        ub_scr[1] = jnp.zeros((R, D_RNN), BF16)

    cin = _stage_scan(pre_scr, p_scr, carry_scr, h_ref, ns=ns, fwd=False, first=(c_scan == 0))
    if cin is not None:
        for it in range(R // LRU_TILE):
            rows = slice(it * LRU_TILE, (it + 1) * LRU_TILE)
            h_ref[rows, :] = _add_carry(h_ref[rows, :], p_scr[rows, :], cin)

    def kv_project():
        kv = _dot(xb_scr[...], wkv_ref[...]) + bkv_ref[...]
        for b in range(nb):
            for sg in range(ns):
                i = b * ns + sg
                k_ref[b, sg] = kv[i * T:(i + 1) * T, 0:KV_DIM].astype(BF16)
                v_ref[b, sg] = kv[i * T:(i + 1) * T, KV_DIM:2 * KV_DIM].astype(BF16)

    input_items = _input_items(x4_ref, xh_ref, pm_ref, wu_ref, bu_ref, wc_ref, bc_ref,
                               xb_scr, lhs_scr, uall_scr, u_scr.at[slot], ub_scr.at[slot], u_ref,
                               nb=nb, ns=ns, jj=nj - 1 - c_in, nj=nj)
    input_items.append((kv_project, _mxu_cycles(R, D_MODEL, 2 * KV_DIM), _vpu_cycles(R, 2 * KV_DIM, 2)))
    _interleave(_coeff_items(ub_scr.at[1 - slot], u_scr.at[1 - slot], wg_ref, bg_ref, lam_ref, pre_scr),
                input_items)


def _lru_fwd_kernel(x4_ref, u_ref, hb_ref, pm_ref, pmt_ref, wg_ref, bg_ref,
                    lam_ref, wgl_ref, bgl_ref, wmg_ref, bmg_ref, wout_ref,
                    m_ref,
                    xb_scr, lhs_scr, pre_scr, p_scr, carry_scr,
                    h_scr, g_scr, hg_scr, hgn_scr,
                    *, nb, ns, nj):
    T = T_SEG
    R = N_STREAMS * T
    C = D_RNN
    ncol = C // LRU_COLS
    s = pl.program_id(1)
    c_out = jnp.clip(s - 1, 0, nj - 1)
    slot = s % 2

    @pl.when(s == 0)
    def _():
        pre_scr[...] = jnp.zeros_like(pre_scr)
        xb_scr[1] = jnp.zeros((R, D_MODEL), BF16)
        lhs_scr[1] = jnp.zeros((R, D_MODEL), BF16)

    cin = _stage_scan(pre_scr, p_scr, carry_scr, h_scr, ns=ns, fwd=True, first=(c_out == 0))

    def gate_project(c):
        cs = slice(c * LRU_COLS, (c + 1) * LRU_COLS)
        g_scr[c % 2] = _dot(lhs_scr[1 - slot], wgl_ref[:, cs]) + bgl_ref[:, cs]

    def gate_apply(c):
        cs = slice(c * LRU_COLS, (c + 1) * LRU_COLS)
        h = h_scr[:, cs]
        if cin is not None:
            h = _add_carry(h, p_scr[:, cs], cin[:, cs])
        h = h + hb_ref[:, cs]
        hg_scr[:, cs] = (h * _gelu(g_scr[c % 2])).astype(BF16)

    def unpermute(c):
        cs = slice(c * LRU_COLS, (c + 1) * LRU_COLS)
        for tb in range(T // PERM_T):
            nat = _dot(pmt_ref[...], hg_scr[MXU_DIM * tb:MXU_DIM * (tb + 1), cs]).astype(BF16)
            for i in range(N_STREAMS):
                hgn_scr[i * T + PERM_T * tb:i * T + PERM_T * (tb + 1), cs] = nat[i * PERM_T:(i + 1) * PERM_T]

    def out_project(c):
        cs = slice(c * LRU_COLS, (c + 1) * LRU_COLS)
        y_lru = _dot(hgn_scr[...], wout_ref[:, cs])
        gate = _sigmoid_of_twice(_dot(xb_scr[1 - slot], wmg_ref[:, cs]) + bmg_ref[:, cs])
        m = (gate * y_lru).astype(BF16)
        for b in range(nb):
            for sg in range(ns):
                i = b * ns + sg
                m_ref[b, sg, :, cs] = m[i * T:(i + 1) * T]

    gp_cost = _mxu_cycles(R, D_MODEL, LRU_COLS)
    out_items = [(functools.partial(gate_project, 0), gp_cost, 0)]
    for c in range(ncol):
        if c + 1 < ncol:
            out_items.append((functools.partial(gate_project, c + 1), gp_cost, 0))
        out_items.append((functools.partial(gate_apply, c), 0, _vpu_cycles(R, LRU_COLS, 14)))
        out_items.append((functools.partial(unpermute, c), (T // PERM_T) * _mxu_cycles(MXU_DIM, MXU_DIM, LRU_COLS),
                          _vpu_cycles(R, LRU_COLS, 1)))
    for c in range(D_MODEL // LRU_COLS):
        out_items.append((functools.partial(out_project, c),
                          _mxu_cycles(R, C, LRU_COLS) + _mxu_cycles(R, D_MODEL, LRU_COLS),
                          _vpu_cycles(R, LRU_COLS, 5)))

    in_items = _permute_items(x4_ref, pm_ref, xb_scr.at[slot], lhs_scr.at[slot], 0, nb=nb, ns=ns)
    in_items += _coeff_items(u_ref, u_ref, wg_ref, bg_ref, lam_ref, pre_scr)
    _interleave(out_items, in_items)


def _stream_split(batch):
    nb = math.gcd(batch, N_STREAMS)
    return nb, N_STREAMS // nb


def _halo_rows(x4, nb, ns):
    B, Q, T, D = x4.shape
    last2 = x4[:, :, T - 2:, :]
    first = x4[:, :, 0, :]
    prev2 = jnp.pad(last2, ((0, 0), (1, 0), (0, 0), (0, 0)))[:, :Q]
    next1 = jnp.pad(first, ((0, 0), (0, 1), (0, 0)))[:, 1:]
    slots = jnp.stack([prev2[:, :, 0], prev2[:, :, 1], next1, jnp.zeros_like(next1)], axis=2)
    nj = Q // ns
    slots = slots.reshape(B // nb, nb, nj, ns, 4, D).transpose(0, 2, 4, 1, 3, 5)
    return slots.reshape(B // nb, nj, HALO_ROWS, D)


def _perm_matrix():
    pm = np.zeros((MXU_DIM, MXU_DIM), np.float32)
    for i in range(N_STREAMS):
        for t in range(PERM_T):
            pm[t * N_STREAMS + i, i * PERM_T + t] = 1.0
    return pm


def _lru_bwd_call(x, p):
    B, S, D = x.shape
    nb, ns = _stream_split(B)
    T = T_SEG
    R = N_STREAMS * T
    Q = S // T
    nj = Q // ns
    x4 = x.reshape(B, Q, T, D)
    xh = _halo_rows(x4, nb, ns)
    C = D_RNN
    chunk_in = lambda bb, s: (bb, nj - 1 - jnp.minimum(s, nj - 1), 0, 0)
    chunk_scan = lambda bb, s: (bb, nj - 1 - jnp.clip(s - 2, 0, nj - 1), 0, 0)
    in_specs = [
        pl.BlockSpec((nb, ns, T, D), chunk_in),
        pl.BlockSpec((None, None, HALO_ROWS, D), chunk_in),
        _const_spec((MXU_DIM, MXU_DIM)),
        _const_spec((D, C)), _const_spec((1, C)),
        _const_spec((LRU_CONV_W, C)), _const_spec((1, C)),
        _const_spec((N_LRU_BLOCKS, MXU_DIM, 2 * MXU_DIM)), _const_spec((1, 2 * C)),
        _const_spec((1, C)),
        _const_spec((D, 2 * KV_DIM)), _const_spec((1, 2 * KV_DIM)),
    ]
    out_shape = [
        jax.ShapeDtypeStruct((B // nb, nj, R, C), F32),
        jax.ShapeDtypeStruct((B // nb, nj, R, C), BF16),
        jax.ShapeDtypeStruct((B, Q, T, KV_DIM), BF16),
        jax.ShapeDtypeStruct((B, Q, T, KV_DIM), BF16),
    ]
    out_specs = [
        pl.BlockSpec((None, None, R, C), chunk_scan),
        pl.BlockSpec((None, None, R, C), chunk_in),
        pl.BlockSpec((nb, ns, T, KV_DIM), chunk_in),
        pl.BlockSpec((nb, ns, T, KV_DIM), chunk_in),
    ]
    scratch = [
        pltpu.VMEM((R, D), BF16),
        pltpu.VMEM((R + HALO_ROWS, D), BF16),
        pltpu.VMEM((R + HALO_ROWS, C), F32),
        pltpu.VMEM((2, R, C), F32),
        pltpu.VMEM((2, R, C), BF16),
        pltpu.VMEM((R, 2 * C), F32),
        pltpu.VMEM((R, C), F32),
        pltpu.VMEM((N_STREAMS, C), F32),
    ]
    h_b, u, k, v = pl.pallas_call(
        functools.partial(_lru_bwd_kernel, nb=nb, ns=ns, nj=nj),
        grid=(B // nb, nj + 2),
        in_specs=in_specs, out_specs=out_specs, out_shape=out_shape,
        scratch_shapes=scratch,
        compiler_params=pltpu.CompilerParams(
            dimension_semantics=("arbitrary", "arbitrary"), vmem_limit_bytes=VMEM_LIMIT),
        name="lru_bwd",
    )(x4, xh, p["pm"], p["w_u"], p["b_u"], p["w_conv"], p["b_conv"], p["wg_b"], p["bg_b"], p["lam_b"],
      p["w_kv"], p["b_kv"])
    return h_b, u, k.reshape(B, S, KV_DIM), v.reshape(B, S, KV_DIM)


def _lru_fwd_call(x, u, h_b, p):
    B, S, D = x.shape
    nb, ns = _stream_split(B)
    T = T_SEG
    R = N_STREAMS * T
    Q = S // T
    nj = Q // ns
    x4 = x.reshape(B, Q, T, D)
    C = D_RNN
    chunk_in = lambda bb, s: (bb, jnp.minimum(s, nj - 1), 0, 0)
    chunk_out = lambda bb, s: (bb, jnp.clip(s - 1, 0, nj - 1), 0, 0)
    in_specs = [
        pl.BlockSpec((nb, ns, T, D), chunk_in),
        pl.BlockSpec((None, None, R, C), chunk_in),
        pl.BlockSpec((None, None, R, C), chunk_out),
        _const_spec((MXU_DIM, MXU_DIM)), _const_spec((MXU_DIM, MXU_DIM)),
        _const_spec((N_LRU_BLOCKS, MXU_DIM, 2 * MXU_DIM)), _const_spec((1, 2 * C)),
        _const_spec((1, C)),
        _const_spec((D, C)), _const_spec((1, C)),
        _const_spec((D, D)), _const_spec((1, D)),
        _const_spec((C, D)),
    ]
    scratch = [
        pltpu.VMEM((2, R, D), BF16),
        pltpu.VMEM((2, R, D), BF16),
        pltpu.VMEM((R, 2 * C), F32),
        pltpu.VMEM((R, C), F32),
        pltpu.VMEM((N_STREAMS, C), F32),
        pltpu.VMEM((R, C), F32),
        pltpu.VMEM((2, R, LRU_COLS), F32),
        pltpu.VMEM((R, C), BF16),
        pltpu.VMEM((R, C), BF16),
    ]
    m = pl.pallas_call(
        functools.partial(_lru_fwd_kernel, nb=nb, ns=ns, nj=nj),
        grid=(B // nb, nj + 1),
        in_specs=in_specs,
        out_specs=pl.BlockSpec((nb, ns, T, D), chunk_out),
        out_shape=jax.ShapeDtypeStruct((B, Q, T, D), BF16),
        scratch_shapes=scratch,
        compiler_params=pltpu.CompilerParams(
            dimension_semantics=("arbitrary", "arbitrary"), vmem_limit_bytes=VMEM_LIMIT),
        name="lru_fwd",
    )(x4, u, h_b, p["pm"], p["pmt"], p["wg_f"], p["bg_f"],
      p["lam_f"], p["w_gl"], p["b_gl"], p["w_mg_lru"], p["b_mg_lru"], p["w_lru_out"])
    return m.reshape(B, S, D)


def _attn_kernel(xa_ref, xb_ref, m_ref, kp_ref, km_ref, kn_ref, vp_ref, vm_ref, vn_ref, idx_ref, rel_ref, sink_ref,
                 wq_ref, bq_ref, wga_ref, bga_ref, wao_ref, wo_ref, bo_ref, g1_ref, b1_ref,
                 o_ref, xc_scr, q_scr, s_scr, attn_scr, merged_scr, y_scr, bias_scr, *, nr, alpha):
    R = ROWS_ATTN
    nblk = R // BLOCK
    i = pl.program_id(0)
    n_chunks = pl.num_programs(0) - 1
    r = lax.rem(jnp.minimum(i, n_chunks - 1), nr)
    slot = i % 2

    @pl.when(i == 0)
    def _():
        idx = idx_ref[...]
        for h in range(N_HEADS):
            def pick(bkt, acc):
                return jnp.where(idx == bkt, rel_ref[bkt, h], acc)
            tab = lax.fori_loop(0, N_BUCKETS, pick, jnp.full((BLOCK, 3 * BLOCK), NEG_INF, F32))
            kvh, g = divmod(h, GROUP)
            bias_scr[kvh, g * BLOCK:(g + 1) * BLOCK, :] = tab
        xc_scr[1] = jnp.zeros((R, D_MODEL), BF16)
        attn_scr[1] = jnp.zeros((R, Q_DIM), BF16)

    pen_prev = (r == 0).astype(F32) * NEG_INF
    pen_next = (r == nr - 1).astype(F32) * NEG_INF
    row_head = lax.broadcasted_iota(jnp.int32, (GROUP * BLOCK, 1), 0) // BLOCK

    def cast_x():
        xc_scr[slot] = xa_ref[...].astype(BF16)

    def q_project(kvh):
        cols = slice(kvh * GROUP * HEAD_DIM, (kvh + 1) * GROUP * HEAD_DIM)
        q_scr[:, cols] = (_dot(xc_scr[slot], wq_ref[:, cols]) + bq_ref[:, cols]).astype(BF16)

    def kv_window(ref_prev, ref_main, ref_next, kvh, n):
        cols = slice(kvh * HEAD_DIM, (kvh + 1) * HEAD_DIM)
        prev = ref_main[(n - 1) * BLOCK:n * BLOCK, cols] if n > 0 else ref_prev[:, cols]
        nxt = ref_main[(n + 1) * BLOCK:(n + 2) * BLOCK, cols] if n < nblk - 1 else ref_next[:, cols]
        return jnp.concatenate([prev, ref_main[n * BLOCK:(n + 1) * BLOCK, cols], nxt], axis=0)

    def scores(kvh, n):
        q4 = jnp.concatenate(
            [q_scr[n * BLOCK:(n + 1) * BLOCK, (kvh * GROUP + g) * HEAD_DIM:(kvh * GROUP + g + 1) * HEAD_DIM]
             for g in range(GROUP)], axis=0)
        k_ext = kv_window(kp_ref, km_ref, kn_ref, kvh, n)
        s = lax.dot_general(q4, k_ext, (((1,), (1,)), ((), ())), preferred_element_type=F32)
        s = s + bias_scr[kvh]
        if n == 0:
            s = jnp.concatenate([s[:, 0:BLOCK] + pen_prev, s[:, BLOCK:]], axis=1)
        if n == nblk - 1:
            s = jnp.concatenate([s[:, 0:2 * BLOCK], s[:, 2 * BLOCK:] + pen_next], axis=1)
        s_scr[(kvh * nblk + n) % 2] = s

    def softmax_pv(kvh, n):
        sink = jnp.zeros((GROUP * BLOCK, 1), F32)
        for g in range(GROUP):
            sink = jnp.where(row_head == g, sink_ref[kvh * GROUP + g], sink)
        s = s_scr[(kvh * nblk + n) % 2]
        mx = jnp.maximum(jnp.max(s, axis=1, keepdims=True), sink)
        pr = jnp.exp(s - mx)
        denom = jnp.sum(pr, axis=1, keepdims=True) + jnp.exp(sink - mx)
        v_ext = kv_window(vp_ref, vm_ref, vn_ref, kvh, n)
        o = _dot(pr.astype(BF16), v_ext) * (1.0 / denom)
        for g in range(GROUP):
            h = kvh * GROUP + g
            attn_scr[slot, n * BLOCK:(n + 1) * BLOCK, h * HEAD_DIM:(h + 1) * HEAD_DIM] = (
                o[g * BLOCK:(g + 1) * BLOCK].astype(BF16))

    rows4 = GROUP * BLOCK
    attn_items = [(cast_x, 0, _vpu_cycles(R, D_MODEL, 1))]
    for kvh in range(N_KV_HEADS):
        attn_items.append((functools.partial(q_project, kvh), _mxu_cycles(R, D_MODEL, GROUP * HEAD_DIM),
                           _vpu_cycles(R, GROUP * HEAD_DIM, 2)))
    blocks = [(kvh, n) for kvh in range(N_KV_HEADS) for n in range(nblk)]
    score_cost = (_mxu_cycles(rows4, HEAD_DIM, 3 * BLOCK), _vpu_cycles(rows4, 3 * BLOCK, 1))
    pv_cost = (2 * _mxu_cycles(rows4, 3 * BLOCK, HEAD_DIM), _vpu_cycles(rows4, 3 * BLOCK, 8))
    attn_items.append((functools.partial(scores, *blocks[0]),) + score_cost)
    for j, blk in enumerate(blocks):
        if j + 1 < len(blocks):
            attn_items.append((functools.partial(scores, *blocks[j + 1]),) + score_cost)
        attn_items.append((functools.partial(softmax_pv, *blk),) + pv_cost)

    def merge(c):
        cs = slice(c * ATTN_COLS, (c + 1) * ATTN_COLS)
        y_attn = _dot(attn_scr[1 - slot], wao_ref[:, cs])
        gate = _sigmoid_of_twice(_dot(xc_scr[1 - slot], wga_ref[:, cs]) + bga_ref[:, cs])
        merged_scr[:, cs] = (gate * y_attn + m_ref[:, cs].astype(F32)).astype(BF16)

    def out_project(c):
        cs = slice(c * ATTN_COLS, (c + 1) * ATTN_COLS)
        y_scr[:, cs] = alpha * xb_ref[:, cs] + (_dot(merged_scr[...], wo_ref[:, cs]) + bo_ref[:, cs])

    def norm():
        o_ref[...] = _layer_norm(y_scr[...], g1_ref[...], b1_ref[...])

    ncol = D_MODEL // ATTN_COLS
    merge_items = [(functools.partial(merge, c), _mxu_cycles(R, Q_DIM, ATTN_COLS) + _mxu_cycles(R, D_MODEL, ATTN_COLS),
                    _vpu_cycles(R, ATTN_COLS, 6)) for c in range(ncol)]
    merge_items += [(functools.partial(out_project, c), _mxu_cycles(R, D_MODEL, ATTN_COLS),
                     _vpu_cycles(R, ATTN_COLS, 3)) for c in range(ncol)]
    merge_items.append((norm, 0, _vpu_cycles(R, D_MODEL, 12)))
    _interleave(attn_items, merge_items)


def _attn_call(x, m_lru, k, v, p, alpha):
    B, S, D = x.shape
    R = ROWS_ATTN
    nr = S // R
    bpr = R // BLOCK
    nkb = S // BLOCK
    n_chunks = B * nr
    cur = lambda i: jnp.minimum(i, n_chunks - 1)
    old = lambda i: jnp.maximum(i - 1, 0)
    main_cur = lambda i: (cur(i) // nr, cur(i) % nr, 0)
    main_old = lambda i: (old(i) // nr, old(i) % nr, 0)
    prev = lambda i: (cur(i) // nr, jnp.maximum((cur(i) % nr) * bpr - 1, 0), 0)
    nxt = lambda i: (cur(i) // nr, jnp.minimum((cur(i) % nr + 1) * bpr, nkb - 1), 0)
    kv_main = pl.BlockSpec((None, R, KV_DIM), main_cur)
    kv_prev = pl.BlockSpec((None, BLOCK, KV_DIM), prev)
    kv_next = pl.BlockSpec((None, BLOCK, KV_DIM), nxt)
    in_specs = [
        pl.BlockSpec((None, R, D), main_cur),
        pl.BlockSpec((None, R, D), main_old),
        pl.BlockSpec((None, R, D), main_old),
        kv_prev, kv_main, kv_next,
        kv_prev, kv_main, kv_next,
        _const_spec((BLOCK, 3 * BLOCK)),
        pl.BlockSpec(memory_space=pltpu.SMEM),
        pl.BlockSpec(memory_space=pltpu.SMEM),
        _const_spec((D, Q_DIM)), _const_spec((1, Q_DIM)),
        _const_spec((D, D)), _const_spec((1, D)),
        _const_spec((Q_DIM, D)),
        _const_spec((D, D)), _const_spec((1, D)),
        _const_spec((1, D)), _const_spec((1, D)),
    ]
    scratch = [
        pltpu.VMEM((2, R, D), BF16),
        pltpu.VMEM((R, Q_DIM), BF16),
        pltpu.VMEM((2, GROUP * BLOCK, 3 * BLOCK), F32),
        pltpu.VMEM((2, R, Q_DIM), BF16),
        pltpu.VMEM((R, D), BF16),
        pltpu.VMEM((R, D), F32),
        pltpu.VMEM((N_KV_HEADS, GROUP * BLOCK, 3 * BLOCK), F32),
    ]
    return pl.pallas_call(
        functools.partial(_attn_kernel, nr=nr, alpha=alpha),
        grid=(n_chunks + 1,),
        in_specs=in_specs,
        out_specs=pl.BlockSpec((None, R, D), main_old),
        out_shape=jax.ShapeDtypeStruct((B, S, D), F32),
        scratch_shapes=scratch,
        compiler_params=pltpu.CompilerParams(
            dimension_semantics=("arbitrary",), vmem_limit_bytes=VMEM_LIMIT),
        name="attn_merge",
    )(x, x, m_lru, k, k, k, v, v, v, p["bias_idx"], p["rel_table"], p["sink"],
      p["w_q"], p["b_q"], p["w_mg_attn"], p["b_mg_attn"], p["w_attn_out"], p["w_o"], p["b_o"],
      p["ln1_g"], p["ln1_b"])


def _ffn_kernel(x_ref, xp_ref, xn_ref, wup_ref, bup_ref, wcv_ref, bcv_ref, wdn_ref, bdn_ref, g2_ref, b2_ref,
                o_ref, xe_scr, hs_scr, act_scr, acc_scr, *, nr, alpha):
    R = ROWS_FFN
    H = FFN_HALO
    nt = D_FF // FFN_COLS
    nslab = FFN_COLS // LANES
    r = pl.program_id(1)
    keep_prev = (r > 0).astype(F32)
    keep_next = (r < nr - 1).astype(F32)
    zeros = jnp.zeros((H - SUBLANES, D_MODEL), F32)
    xe_scr[...] = jnp.concatenate(
        [zeros, xp_ref[...] * keep_prev, x_ref[...], xn_ref[...] * keep_next, zeros], axis=0).astype(BF16)

    row = lax.broadcasted_iota(jnp.int32, (SUBLANES, FFN_COLS), 0)
    pad_prev = (row == SUBLANES - 1).astype(F32) * (1.0 - keep_prev)
    pad_next = (row == 0).astype(F32) * (1.0 - keep_next)

    def up_proj(j):
        xe = xe_scr[...]
        for half in range(2):
            y = _dot(xe, wup_ref[half, j])
            b_up = bup_ref[half, j]
            y = jnp.concatenate(
                [y[0:H - SUBLANES], y[H - SUBLANES:H] - pad_prev * b_up, y[H:H + R],
                 y[H + R:H + R + SUBLANES] - pad_next * b_up, y[H + R + SUBLANES:]], axis=0)
            for c in range(nslab):
                hs_scr[j % 2, half, c] = y[:, c * LANES:(c + 1) * LANES]

    def conv(j, half, c):
        lanes = slice(c * LANES, (c + 1) * LANES)
        w = wcv_ref[half, j][:, lanes]
        bias = bcv_ref[half, j][:, lanes] + bup_ref[half, j][:, lanes] * (w[0:1] + w[1:2] + w[2:3])
        return (w[0:1] * hs_scr[j % 2, half, c, H - 1:H - 1 + R, :]
                + w[1:2] * hs_scr[j % 2, half, c, H:H + R, :]
                + w[2:3] * hs_scr[j % 2, half, c, H + 1:H + 1 + R, :] + bias)

    def gate(j):
        for c in range(nslab):
            act = _gelu(conv(j, 1, c)) * conv(j, 0, c)
            act_scr[:, j * FFN_COLS + c * LANES:j * FFN_COLS + (c + 1) * LANES] = act.astype(BF16)

    per = nt // FFN_DOWN_SPLIT
    up_proj(0)
    for j in range(nt):
        if j + 1 < nt:
            up_proj(j + 1)
        gate(j)
        if (j + 1) % per == 0:
            part = (j + 1) // per - 1
            ks = slice(part * per * FFN_COLS, (part + 1) * per * FFN_COLS)
            contrib = _dot(act_scr[:, ks], wdn_ref[ks, :])
            if part == 0:
                acc_scr[...] = contrib + (alpha * x_ref[...] + bdn_ref[...])
            elif part < FFN_DOWN_SPLIT - 1:
                acc_scr[...] = acc_scr[...] + contrib
            else:
                o_ref[...] = _layer_norm(acc_scr[...] + contrib, g2_ref[...], b2_ref[...])


def _ffn_call(x, p, alpha):
    B, S, D = x.shape
    R = ROWS_FFN
    nr = S // R
    rpb = R // SUBLANES
    nsb = S // SUBLANES
    nt = D_FF // FFN_COLS
    main = lambda b, r: (b, r, 0)
    prev = lambda b, r: (b, jnp.maximum(r * rpb - 1, 0), 0)
    nxt = lambda b, r: (b, jnp.minimum((r + 1) * rpb, nsb - 1), 0)
    in_specs = [
        pl.BlockSpec((None, R, D), main),
        pl.BlockSpec((None, SUBLANES, D), prev),
        pl.BlockSpec((None, SUBLANES, D), nxt),
        _const_spec((2, nt, D, FFN_COLS)), _const_spec((2, nt, 1, FFN_COLS)),
        _const_spec((2, nt, 3, FFN_COLS)), _const_spec((2, nt, 1, FFN_COLS)),
        _const_spec((D_FF, D)), _const_spec((1, D)),
        _const_spec((1, D)), _const_spec((1, D)),
    ]
    scratch = [
        pltpu.VMEM((R + 2 * FFN_HALO, D), BF16),
        pltpu.VMEM((2, 2, FFN_COLS // LANES, R + 2 * FFN_HALO, LANES), F32),
        pltpu.VMEM((R, D_FF), BF16),
        pltpu.VMEM((R, D), F32),
    ]
    return pl.pallas_call(
        functools.partial(_ffn_kernel, nr=nr, alpha=alpha),
        grid=(B, nr),
        in_specs=in_specs,
        out_specs=pl.BlockSpec((None, R, D), main),
        out_shape=jax.ShapeDtypeStruct((B, S, D), F32),
        scratch_shapes=scratch,
        compiler_params=pltpu.CompilerParams(
            dimension_semantics=("arbitrary", "arbitrary"), vmem_limit_bytes=VMEM_LIMIT),
        name="ffn",
    )(x, x, x, p["w_up"], p["b_up"], p["w_ffn_conv"], p["b_ffn_conv"], p["w_down"], p["b_down"],
      p["ln2_g"], p["ln2_b"])


def _t5_bucket(rel):
    nb = N_BUCKETS // 2
    ret = jnp.where(rel > 0, nb, 0)
    n = jnp.abs(rel)
    max_exact = nb // 2
    nf = jnp.maximum(n, 1).astype(jnp.float32)
    large = max_exact + (jnp.log(nf / max_exact) / math.log(MAX_DISTANCE / max_exact)
                         * (nb - max_exact)).astype(jnp.int32)
    large = jnp.minimum(large, nb - 1)
    return ret + jnp.where(n < max_exact, n, large)


def _pack_gate_weights(w_a, w_x):
    out = jnp.zeros((N_LRU_BLOCKS, MXU_DIM, 2 * MXU_DIM), F32)
    for i in range(N_LRU_BLOCKS):
        lo = LRU_BLOCK * i - GATE_OFF[i]
        out = out.at[i, lo:lo + LRU_BLOCK, lo:lo + LRU_BLOCK].set(w_a[i])
        out = out.at[i, lo:lo + LRU_BLOCK, MXU_DIM + lo:MXU_DIM + lo + LRU_BLOCK].set(w_x[i])
    return out.astype(BF16)


def _prepare(rel_table, w_in, b_in, w_lru_conv, b_lru_conv, w_rg_a, b_rg_a, w_rg_x, b_rg_x, lru_lambda,
             w_lru_out, attn_sink, w_attn_out, w_o, b_o, ln1_g, ln1_b, w_up, b_up, w_ffn_conv, b_ffn_conv,
             w_down, b_down, ln2_g, ln2_b):
    C = D_RNN
    o_g, o_q, o_k, o_gl, o_ga = C, 2 * C, 2 * C + Q_DIM, 2 * C + Q_DIM + 2 * KV_DIM, 2 * C + Q_DIM + 2 * KV_DIM + D_MODEL
    row = lambda a: a.reshape(1, -1).astype(F32)
    scale = HEAD_DIM ** -0.5
    nt = D_FF // FFN_COLS
    pm = _perm_matrix()

    q_off = jnp.arange(BLOCK)[:, None]
    c_off = jnp.arange(3 * BLOCK)[None, :]
    rel = c_off - BLOCK - q_off
    bias_idx = jnp.where(jnp.abs(rel) <= WINDOW, _t5_bucket(rel), -1).astype(jnp.int32)

    def ffn_cols(a):
        lead = a.shape[:-1]
        a = a.reshape(lead + (2, nt, FFN_COLS))
        return jnp.moveaxis(a, (-3, -2), (0, 1))

    return {
        "pm": jnp.asarray(pm, BF16), "pmt": jnp.asarray(pm.T, BF16),
        "w_u": w_in[:, 0:o_g].astype(BF16), "b_u": row(b_in[0:o_g]),
        "w_gl": w_in[:, o_g:o_q].astype(BF16), "b_gl": row(b_in[o_g:o_q]),
        "w_q": (w_in[:, o_q:o_k] * scale).astype(BF16), "b_q": row(b_in[o_q:o_k] * scale),
        "w_kv": w_in[:, o_k:o_gl].astype(BF16), "b_kv": row(b_in[o_k:o_gl]),
        "w_mg_lru": (0.5 * w_in[:, o_gl:o_ga]).astype(BF16), "b_mg_lru": row(0.5 * b_in[o_gl:o_ga]),
        "w_mg_attn": (0.5 * w_in[:, o_ga:]).astype(BF16), "b_mg_attn": row(0.5 * b_in[o_ga:]),
        "w_conv": w_lru_conv.astype(F32), "b_conv": row(b_lru_conv),
        "wg_f": _pack_gate_weights(0.5 * w_rg_a[0], 0.5 * w_rg_x[0]),
        "bg_f": row(0.5 * jnp.concatenate([b_rg_a[0].reshape(-1), b_rg_x[0].reshape(-1)])),
        "wg_b": _pack_gate_weights(0.5 * w_rg_a[1], 0.5 * w_rg_x[1]),
        "bg_b": row(0.5 * jnp.concatenate([b_rg_a[1].reshape(-1), b_rg_x[1].reshape(-1)])),
        "lam_f": row(lru_lambda[0]), "lam_b": row(lru_lambda[1]),
        "w_lru_out": w_lru_out.astype(BF16),
        "bias_idx": bias_idx, "rel_table": rel_table.astype(F32), "sink": attn_sink.astype(F32),
        "w_attn_out": w_attn_out.astype(BF16),
        "w_o": w_o.astype(BF16), "b_o": row(b_o),
        "ln1_g": row(ln1_g), "ln1_b": row(ln1_b),
        "w_up": ffn_cols(w_up).astype(BF16), "b_up": ffn_cols(b_up.reshape(1, -1)).astype(F32),
        "w_ffn_conv": ffn_cols(w_ffn_conv).astype(F32), "b_ffn_conv": ffn_cols(b_ffn_conv.reshape(1, -1)).astype(F32),
        "w_down": w_down.astype(BF16), "b_down": row(b_down),
        "ln2_g": row(ln2_g), "ln2_b": row(ln2_b),
    }


def _encoder_layer(x, p, alpha):
    h_b, u, k, v = _lru_bwd_call(x, p)
    m_lru = _lru_fwd_call(x, u, h_b, p)
    x1 = _attn_call(x, m_lru, k, v, p, alpha)
    return _ffn_call(x1, p, alpha)


def kernel(x_prompt, x_sample, rel_table, w_in, b_in, w_lru_conv, b_lru_conv, w_rg_a, b_rg_a, w_rg_x, b_rg_x, lru_lambda, w_lru_out, attn_sink, w_attn_out, w_o, b_o, ln1_g, ln1_b, w_up, b_up, w_ffn_conv, b_ffn_conv, w_down, b_down, ln2_g, ln2_b):
    depth = w_in.shape[0]
    alpha = (2.0 * depth) ** 0.25
    y_prompt, y_sample = x_prompt, x_sample
    for l in range(depth):
        p = _prepare(rel_table, w_in[l], b_in[l], w_lru_conv[l], b_lru_conv[l], w_rg_a[l], b_rg_a[l],
                     w_rg_x[l], b_rg_x[l], lru_lambda[l], w_lru_out[l], attn_sink[l], w_attn_out[l], w_o[l],
                     b_o[l], ln1_g[l], ln1_b[l], w_up[l], b_up[l], w_ffn_conv[l], b_ffn_conv[l], w_down[l],
                     b_down[l], ln2_g[l], ln2_b[l])
        y_prompt = _encoder_layer(y_prompt, p, alpha)
        y_sample = _encoder_layer(y_sample, p, alpha)
    return (y_prompt, y_sample)
```

```python
import functools
import math

import numpy as np
import jax
import jax.numpy as jnp
from jax import lax
from jax.experimental import pallas as pl
from jax.experimental.pallas import tpu as pltpu

F32 = jnp.float32
BF16 = jnp.bfloat16

D_MODEL = 1024
D_RNN = 1280
N_LRU_BLOCKS = 8
LRU_BLOCK = D_RNN // N_LRU_BLOCKS
LRU_C = 8.0
LRU_CONV_W = 4
HEAD_DIM = 128
N_HEADS = 8
N_KV_HEADS = 2
GROUP = N_HEADS // N_KV_HEADS
WINDOW = 128
BLOCK = 128
N_BUCKETS = 32
MAX_DISTANCE = 128
Q_DIM = N_HEADS * HEAD_DIM
KV_DIM = N_KV_HEADS * HEAD_DIM
D_FF = 3 * D_MODEL
LN_EPS = 1e-5
NEG_INF = -1e30
F32_TINY = float(np.finfo(np.float32).tiny)

LANES = 128
SUBLANES = 8
MXU_DIM = 256

N_STREAMS = SUBLANES
T_SEG = 64
PERM_T = MXU_DIM // N_STREAMS
HALO_ROWS = 4 * N_STREAMS
ROWS_ATTN = 512
ATTN_COLS = 256
ROWS_FFN = 1024
FFN_COLS = 256
FFN_DOWN_SPLIT = 3
FFN_HALO = 16
LRU_TILE = 64
LRU_COLS = 256
SCAN_UNROLL = 8
VMEM_LIMIT = 56 * 1024 * 1024

GATE_OFF = tuple(LANES * ((LRU_BLOCK * i) // LANES) for i in range(N_LRU_BLOCKS))


def _sigmoid_of_twice(z):
    return 0.5 * jnp.tanh(z) + 0.5


def _gelu(x):
    c = math.sqrt(2.0 / math.pi)
    inner = x * (c + (c * 0.044715) * (x * x))
    return x * (0.5 * jnp.tanh(inner) + 0.5)


def _layer_norm(y, g, b):
    mu = jnp.mean(y, axis=-1, keepdims=True)
    yc = y - mu
    var = jnp.mean(yc * yc, axis=-1, keepdims=True)
    return yc * lax.rsqrt(var + LN_EPS) * g + b


def _dot(a, b):
    return jnp.dot(a, b, preferred_element_type=F32)


def _const_spec(shape):
    nd = len(shape)
    return pl.BlockSpec(shape, lambda *_: (0,) * nd, pipeline_mode=pl.Buffered(1))


def _mxu_cycles(m, k, n):
    return (m // SUBLANES) * (n // LANES) * (-(-k // MXU_DIM))


def _vpu_cycles(rows, cols, ops_per_vreg):
    return rows * cols // (SUBLANES * LANES) * ops_per_vreg // 4


def _interleave(*streams):
    queues = [list(q) for q in streams]
    mxu = vpu = 0
    while any(queues):
        q = min((q for q in queues if q), key=lambda q: abs((mxu + q[0][1]) - (vpu + q[0][2])))
        fn, m, v = q.pop(0)
        fn()
        mxu += m
        vpu += v


def _permute_items(x4_ref, pm_ref, xb_dst, lhs_dst, row0, *, nb, ns):
    T = T_SEG
    R = N_STREAMS * T

    def copy():
        for b in range(nb):
            for sg in range(ns):
                i = b * ns + sg
                xb_dst[i * T:(i + 1) * T, :] = x4_ref[b, sg].astype(BF16)

    def permute(tb):
        src = jnp.concatenate(
            [xb_dst[i * T + PERM_T * tb:i * T + PERM_T * (tb + 1), :] for i in range(N_STREAMS)], axis=0)
        lhs_dst[row0 + MXU_DIM * tb:row0 + MXU_DIM * (tb + 1), :] = _dot(pm_ref[...], src).astype(BF16)

    items = [(copy, 0, _vpu_cycles(R, D_MODEL, 2))]
    for tb in range(T // PERM_T):
        items.append((functools.partial(permute, tb), _mxu_cycles(MXU_DIM, MXU_DIM, D_MODEL),
                      _vpu_cycles(MXU_DIM, D_MODEL, 1)))
    return items


def _input_items(x4_ref, xh_ref, pm_ref, wu_ref, bu_ref, wc_ref, bc_ref,
                 xb_dst, lhs_dst, uall_scr, u_dst, ub_dst, u_out, *, nb, ns, jj, nj):
    T = T_SEG
    R = N_STREAMS * T
    C = D_RNN
    ncol = C // LRU_COLS

    def halo():
        xh = xh_ref[...].astype(BF16)
        lhs_dst[0:2 * N_STREAMS, :] = xh[0:2 * N_STREAMS]
        lhs_dst[2 * N_STREAMS + R:, :] = xh[2 * N_STREAMS:]

    def project(c):
        cs = slice(c * LRU_COLS, (c + 1) * LRU_COLS)
        uall_scr[:, cs] = _dot(lhs_dst[...], wu_ref[:, cs]) + bu_ref[:, cs]
        seg = lax.broadcasted_iota(jnp.int32, (N_STREAMS, LRU_COLS), 0) & (ns - 1)
        keep_prev = 1.0 - (seg == 0).astype(F32) * (jj == 0).astype(F32)
        keep_next = 1.0 - (seg == ns - 1).astype(F32) * (jj == nj - 1).astype(F32)
        prev_rows = slice(0, 2 * N_STREAMS)
        next_rows = slice(2 * N_STREAMS + R, 3 * N_STREAMS + R)
        uall_scr[prev_rows, cs] = uall_scr[prev_rows, cs] * jnp.concatenate([keep_prev, keep_prev], axis=0)
        uall_scr[next_rows, cs] = uall_scr[next_rows, cs] * keep_next

    def conv(c):
        cs = slice(c * LRU_COLS, (c + 1) * LRU_COLS)
        acc = bc_ref[:, cs] + wc_ref[0:1, cs] * uall_scr[0:R, cs]
        for k in range(1, LRU_CONV_W):
            acc = acc + wc_ref[k:k + 1, cs] * uall_scr[N_STREAMS * k:N_STREAMS * k + R, cs]
        u_dst[:, cs] = acc
        ub_dst[:, cs] = acc.astype(BF16)
        u_out[:, cs] = acc.astype(BF16)

    items = _permute_items(x4_ref, pm_ref, xb_dst, lhs_dst, 2 * N_STREAMS, nb=nb, ns=ns)
    items.append((halo, 0, 0))
    project_cost = _mxu_cycles(R + HALO_ROWS, D_MODEL, LRU_COLS)
    conv_cost = _vpu_cycles(R, LRU_COLS, 2 * LRU_CONV_W + 1)
    items.append((functools.partial(project, 0), project_cost, 0))
    for c in range(ncol):
        if c + 1 < ncol:
            items.append((functools.partial(project, c + 1), project_cost, 0))
        items.append((functools.partial(conv, c), 0, conv_cost))
    return items


def _coeff_items(ub_src, u_src, wg_ref, bg_ref, lam_ref, pre_scr):
    R = N_STREAMS * T_SEG
    C = D_RNN

    def coeff(t):
        cols = slice(t * LANES, (t + 1) * LANES)
        xcols = slice(C + t * LANES, C + (t + 1) * LANES)
        lam = lam_ref[:, cols]
        log_sig = jnp.minimum(lam, 0.0) - jnp.log1p(jnp.exp(-jnp.abs(lam)))
        hc = (0.5 * LRU_C) * log_sig
        log_a = hc * jnp.tanh(pre_scr[:, cols]) + hc
        ig = 0.5 * jnp.tanh(pre_scr[:, xcols]) + 0.5
        a = jnp.exp(log_a)
        gain = jnp.tanh(log_a) * (-1.0 - a * a)
        root = gain * lax.rsqrt(jnp.maximum(gain, F32_TINY))
        pre_scr[:, cols] = a
        pre_scr[:, xcols] = root * (ig * u_src[:, cols])

    first_touch = {}
    last_touch = {}
    for i in range(N_LRU_BLOCKS):
        for part in range(MXU_DIM // LANES):
            tile = GATE_OFF[i] // LANES + part
            first_touch.setdefault(tile, i)
            last_touch[tile] = i

    def gate(i):
        off = GATE_OFF[i]
        y = _dot(ub_src[:, off:off + MXU_DIM], wg_ref[i])
        for half in range(2):
            for part in range(MXU_DIM // LANES):
                col = half * C + off + part * LANES
                val = y[:, half * MXU_DIM + part * LANES:half * MXU_DIM + (part + 1) * LANES]
                if first_touch[off // LANES + part] == i:
                    pre_scr[:, col:col + LANES] = val + bg_ref[:, col:col + LANES]
                else:
                    pre_scr[:, col:col + LANES] = pre_scr[:, col:col + LANES] + val

    gate_cost = (_mxu_cycles(R, MXU_DIM, 2 * MXU_DIM), _vpu_cycles(R, 2 * MXU_DIM, 1))
    coeff_cost = _vpu_cycles(R, LANES, 20)
    items = [(functools.partial(gate, 0),) + gate_cost]
    for i in range(N_LRU_BLOCKS):
        if i + 1 < N_LRU_BLOCKS:
            items.append((functools.partial(gate, i + 1),) + gate_cost)
        for t in sorted(t for t, last in last_touch.items() if last == i):
            items.append((functools.partial(coeff, t), 0, coeff_cost))
    return items


def _stage_scan(pre_scr, p_scr, carry_scr, h_dst, *, ns, fwd, first):
    T = T_SEG
    C = D_RNN

    @pl.when(first)
    def _():
        carry_scr[...] = jnp.zeros_like(carry_scr)

    def scan_step(tt, carry):
        h, p = carry
        t = tt if fwd else T - 1 - tt
        r0 = pl.multiple_of(t * N_STREAMS, N_STREAMS)
        a = pre_scr[pl.ds(r0, N_STREAMS), 0:C]
        h = a * h + pre_scr[pl.ds(r0, N_STREAMS), C:2 * C]
        h_dst[pl.ds(r0, N_STREAMS), :] = h
        if ns > 1:
            p = p * a
            p_scr[pl.ds(r0, N_STREAMS), :] = p
        return h, p

    if ns == 1:
        h0 = carry_scr[...]
    else:
        h0 = jnp.zeros((N_STREAMS, C), F32)
    h_end, p_end = lax.fori_loop(0, T, scan_step, (h0, jnp.ones((N_STREAMS, C), F32)), unroll=SCAN_UNROLL)

    if ns == 1:
        carry_scr[...] = h_end
        return None

    sub = lax.broadcasted_iota(jnp.int32, (N_STREAMS, C), 0)
    seg = sub & (ns - 1)
    carry = carry_scr[...]
    head = (seg == 0) if fwd else (seg == ns - 1)
    from_prev = 1 if fwd else N_STREAMS - 1
    final = jnp.zeros((N_STREAMS, C), F32)
    cin = final
    for _ in range(ns):
        cin = jnp.where(head, carry, pltpu.roll(final, from_prev, 0))
        final = h_end + p_end * cin
    to_head = (N_STREAMS - (ns - 1)) % N_STREAMS if fwd else ns - 1
    carry_scr[...] = pltpu.roll(final, to_head, 0)
    return cin


def _add_carry(h, p, cin):
    shape3 = (h.shape[0] // N_STREAMS, N_STREAMS, h.shape[1])
    return (h.reshape(shape3) + p.reshape(shape3) * cin[None]).reshape(h.shape)


def _lru_bwd_kernel(x4_ref, xh_ref, pm_ref, wu_ref, bu_ref, wc_ref, bc_ref, wg_ref, bg_ref, lam_ref,
                    wkv_ref, bkv_ref,
                    h_ref, u_ref, k_ref, v_ref,
                    xb_scr, lhs_scr, uall_scr, u_scr, ub_scr, pre_scr, p_scr, carry_scr,
                    *, nb, ns, nj):
    T = T_SEG
    R = N_STREAMS * T
    s = pl.program_id(1)
    c_in = jnp.minimum(s, nj - 1)
    c_scan = jnp.clip(s - 2, 0, nj - 1)
    slot = s % 2

    @pl.when(s == 0)
    def _():
        pre_scr[...] = jnp.zeros_like(pre_scr)
        u_scr[1] = jnp.zeros((R, D_RNN), F32)
        ub_scr[1] = jnp.zeros((R, D_RNN), BF16)

    cin = _stage_scan(pre_scr, p_scr, carry_scr, h_ref, ns=ns, fwd=False, first=(c_scan == 0))
    if cin is not None:
        for it in range(R // LRU_TILE):
            rows = slice(it * LRU_TILE, (it + 1) * LRU_TILE)
            h_ref[rows, :] = _add_carry(h_ref[rows, :], p_scr[rows, :], cin)

    def kv_project():
        kv = _dot(xb_scr[...], wkv_ref[...]) + bkv_ref[...]
        for b in range(nb):
            for sg in range(ns):
                i = b * ns + sg
                k_ref[b, sg] = kv[i * T:(i + 1) * T, 0:KV_DIM].astype(BF16)
                v_ref[b, sg] = kv[i * T:(i + 1) * T, KV_DIM:2 * KV_DIM].astype(BF16)

    input_items = _input_items(x4_ref, xh_ref, pm_ref, wu_ref, bu_ref, wc_ref, bc_ref,
                               xb_scr, lhs_scr, uall_scr, u_scr.at[slot], ub_scr.at[slot], u_ref,
                               nb=nb, ns=ns, jj=nj - 1 - c_in, nj=nj)
    input_items.append((kv_project, _mxu_cycles(R, D_MODEL, 2 * KV_DIM), _vpu_cycles(R, 2 * KV_DIM, 2)))
    _interleave(_coeff_items(ub_scr.at[1 - slot], u_scr.at[1 - slot], wg_ref, bg_ref, lam_ref, pre_scr),
                input_items)


def _lru_fwd_kernel(x4_ref, u_ref, hb_ref, pm_ref, pmt_ref, wg_ref, bg_ref,
                    lam_ref, wgl_ref, bgl_ref, wmg_ref, bmg_ref, wout_ref,
                    m_ref,
                    xb_scr, lhs_scr, pre_scr, p_scr, carry_scr,
                    h_scr, g_scr, hg_scr, hgn_scr,
                    *, nb, ns, nj):
    T = T_SEG
    R = N_STREAMS * T
    C = D_RNN
    ncol = C // LRU_COLS
    s = pl.program_id(1)
    c_out = jnp.clip(s - 1, 0, nj - 1)
    slot = s % 2

    @pl.when(s == 0)
    def _():
        pre_scr[...] = jnp.zeros_like(pre_scr)
        xb_scr[1] = jnp.zeros((R, D_MODEL), BF16)
        lhs_scr[1] = jnp.zeros((R, D_MODEL), BF16)

    cin = _stage_scan(pre_scr, p_scr, carry_scr, h_scr, ns=ns, fwd=True, first=(c_out == 0))

    def gate_project(c):
        cs = slice(c * LRU_COLS, (c + 1) * LRU_COLS)
        g_scr[c % 2] = _dot(lhs_scr[1 - slot], wgl_ref[:, cs]) + bgl_ref[:, cs]

    def gate_apply(c):
        cs = slice(c * LRU_COLS, (c + 1) * LRU_COLS)
        h = h_scr[:, cs]
        if cin is not None:
            h = _add_carry(h, p_scr[:, cs], cin[:, cs])
        h = h + hb_ref[:, cs]
        hg_scr[:, cs] = (h * _gelu(g_scr[c % 2])).astype(BF16)

    def unpermute(c):
        cs = slice(c * LRU_COLS, (c + 1) * LRU_COLS)
        for tb in range(T // PERM_T):
            nat = _dot(pmt_ref[...], hg_scr[MXU_DIM * tb:MXU_DIM * (tb + 1), cs]).astype(BF16)
            for i in range(N_STREAMS):
                hgn_scr[i * T + PERM_T * tb:i * T + PERM_T * (tb + 1), cs] = nat[i * PERM_T:(i + 1) * PERM_T]

    def out_project(c):
        cs = slice(c * LRU_COLS, (c + 1) * LRU_COLS)
        y_lru = _dot(hgn_scr[...], wout_ref[:, cs])
        gate = _sigmoid_of_twice(_dot(xb_scr[1 - slot], wmg_ref[:, cs]) + bmg_ref[:, cs])
        m = (gate * y_lru).astype(BF16)
        for b in range(nb):
            for sg in range(ns):
                i = b * ns + sg
                m_ref[b, sg, :, cs] = m[i * T:(i + 1) * T]

    gp_cost = _mxu_cycles(R, D_MODEL, LRU_COLS)
    out_items = [(functools.partial(gate_project, 0), gp_cost, 0)]
    for c in range(ncol):
        if c + 1 < ncol:
            out_items.append((functools.partial(gate_project, c + 1), gp_cost, 0))
        out_items.append((functools.partial(gate_apply, c), 0, _vpu_cycles(R, LRU_COLS, 14)))
        out_items.append((functools.partial(unpermute, c), (T // PERM_T) * _mxu_cycles(MXU_DIM, MXU_DIM, LRU_COLS),
                          _vpu_cycles(R, LRU_COLS, 1)))
    for c in range(D_MODEL // LRU_COLS):
        out_items.append((functools.partial(out_project, c),
                          _mxu_cycles(R, C, LRU_COLS) + _mxu_cycles(R, D_MODEL, LRU_COLS),
                          _vpu_cycles(R, LRU_COLS, 5)))

    in_items = _permute_items(x4_ref, pm_ref, xb_scr.at[slot], lhs_scr.at[slot], 0, nb=nb, ns=ns)
    in_items += _coeff_items(u_ref, u_ref, wg_ref, bg_ref, lam_ref, pre_scr)
    _interleave(out_items, in_items)


def _stream_split(batch):
    nb = math.gcd(batch, N_STREAMS)
    return nb, N_STREAMS // nb


def _halo_rows(x4, nb, ns):
    B, Q, T, D = x4.shape
    last2 = x4[:, :, T - 2:, :]
    first = x4[:, :, 0, :]
    prev2 = jnp.pad(last2, ((0, 0), (1, 0), (0, 0), (0, 0)))[:, :Q]
    next1 = jnp.pad(first, ((0, 0), (0, 1), (0, 0)))[:, 1:]
    slots = jnp.stack([prev2[:, :, 0], prev2[:, :, 1], next1, jnp.zeros_like(next1)], axis=2)
    nj = Q // ns
    slots = slots.reshape(B // nb, nb, nj, ns, 4, D).transpose(0, 2, 4, 1, 3, 5)
    return slots.reshape(B // nb, nj, HALO_ROWS, D)


def _perm_matrix():
    pm = np.zeros((MXU_DIM, MXU_DIM), np.float32)
    for i in range(N_STREAMS):
        for t in range(PERM_T):
            pm[t * N_STREAMS + i, i * PERM_T + t] = 1.0
    return pm


def _lru_bwd_call(x, p):
    B, S, D = x.shape
    nb, ns = _stream_split(B)
    T = T_SEG
    R = N_STREAMS * T
    Q = S // T
    nj = Q // ns
    x4 = x.reshape(B, Q, T, D)
    xh = _halo_rows(x4, nb, ns)
    C = D_RNN
    chunk_in = lambda bb, s: (bb, nj - 1 - jnp.minimum(s, nj - 1), 0, 0)
    chunk_scan = lambda bb, s: (bb, nj - 1 - jnp.clip(s - 2, 0, nj - 1), 0, 0)
    in_specs = [
        pl.BlockSpec((nb, ns, T, D), chunk_in),
        pl.BlockSpec((None, None, HALO_ROWS, D), chunk_in),
        _const_spec((MXU_DIM, MXU_DIM)),
        _const_spec((D, C)), _const_spec((1, C)),
        _const_spec((LRU_CONV_W, C)), _const_spec((1, C)),
        _const_spec((N_LRU_BLOCKS, MXU_DIM, 2 * MXU_DIM)), _const_spec((1, 2 * C)),
        _const_spec((1, C)),
        _const_spec((D, 2 * KV_DIM)), _const_spec((1, 2 * KV_DIM)),
    ]
    out_shape = [
        jax.ShapeDtypeStruct((B // nb, nj, R, C), F32),
        jax.ShapeDtypeStruct((B // nb, nj, R, C), BF16),
        jax.ShapeDtypeStruct((B, Q, T, KV_DIM), BF16),
        jax.ShapeDtypeStruct((B, Q, T, KV_DIM), BF16),
    ]
    out_specs = [
        pl.BlockSpec((None, None, R, C), chunk_scan),
        pl.BlockSpec((None, None, R, C), chunk_in),
        pl.BlockSpec((nb, ns, T, KV_DIM), chunk_in),
        pl.BlockSpec((nb, ns, T, KV_DIM), chunk_in),
    ]
    scratch = [
        pltpu.VMEM((R, D), BF16),
        pltpu.VMEM((R + HALO_ROWS, D), BF16),
        pltpu.VMEM((R + HALO_ROWS, C), F32),
        pltpu.VMEM((2, R, C), F32),
        pltpu.VMEM((2, R, C), BF16),
        pltpu.VMEM((R, 2 * C), F32),
        pltpu.VMEM((R, C), F32),
        pltpu.VMEM((N_STREAMS, C), F32),
    ]
    h_b, u, k, v = pl.pallas_call(
        functools.partial(_lru_bwd_kernel, nb=nb, ns=ns, nj=nj),
        grid=(B // nb, nj + 2),
        in_specs=in_specs, out_specs=out_specs, out_shape=out_shape,
        scratch_shapes=scratch,
        compiler_params=pltpu.CompilerParams(
            dimension_semantics=("arbitrary", "arbitrary"), vmem_limit_bytes=VMEM_LIMIT),
        name="lru_bwd",
    )(x4, xh, p["pm"], p["w_u"], p["b_u"], p["w_conv"], p["b_conv"], p["wg_b"], p["bg_b"], p["lam_b"],
      p["w_kv"], p["b_kv"])
    return h_b, u, k.reshape(B, S, KV_DIM), v.reshape(B, S, KV_DIM)


def _lru_fwd_call(x, u, h_b, p):
    B, S, D = x.shape
    nb, ns = _stream_split(B)
    T = T_SEG
    R = N_STREAMS * T
    Q = S // T
    nj = Q // ns
    x4 = x.reshape(B, Q, T, D)
    C = D_RNN
    chunk_in = lambda bb, s: (bb, jnp.minimum(s, nj - 1), 0, 0)
    chunk_out = lambda bb, s: (bb, jnp.clip(s - 1, 0, nj - 1), 0, 0)
    in_specs = [
        pl.BlockSpec((nb, ns, T, D), chunk_in),
        pl.BlockSpec((None, None, R, C), chunk_in),
        pl.BlockSpec((None, None, R, C), chunk_out),
        _const_spec((MXU_DIM, MXU_DIM)), _const_spec((MXU_DIM, MXU_DIM)),
        _const_spec((N_LRU_BLOCKS, MXU_DIM, 2 * MXU_DIM)), _const_spec((1, 2 * C)),
        _const_spec((1, C)),
        _const_spec((D, C)), _const_spec((1, C)),
        _const_spec((D, D)), _const_spec((1, D)),
        _const_spec((C, D)),
    ]
    scratch = [
        pltpu.VMEM((2, R, D), BF16),
        pltpu.VMEM((2, R, D), BF16),
        pltpu.VMEM((R, 2 * C), F32),
        pltpu.VMEM((R, C), F32),
        pltpu.VMEM((N_STREAMS, C), F32),
        pltpu.VMEM((R, C), F32),
        pltpu.VMEM((2, R, LRU_COLS), F32),
        pltpu.VMEM((R, C), BF16),
        pltpu.VMEM((R, C), BF16),
    ]
    m = pl.pallas_call(
        functools.partial(_lru_fwd_kernel, nb=nb, ns=ns, nj=nj),
        grid=(B // nb, nj + 1),
        in_specs=in_specs,
        out_specs=pl.BlockSpec((nb, ns, T, D), chunk_out),
        out_shape=jax.ShapeDtypeStruct((B, Q, T, D), BF16),
        scratch_shapes=scratch,
        compiler_params=pltpu.CompilerParams(
            dimension_semantics=("arbitrary", "arbitrary"), vmem_limit_bytes=VMEM_LIMIT),
        name="lru_fwd",
    )(x4, u, h_b, p["pm"], p["pmt"], p["wg_f"], p["bg_f"],
      p["lam_f"], p["w_gl"], p["b_gl"], p["w_mg_lru"], p["b_mg_lru"], p["w_lru_out"])
    return m.reshape(B, S, D)


def _attn_kernel(xa_ref, xb_ref, m_ref, kp_ref, km_ref, kn_ref, vp_ref, vm_ref, vn_ref, idx_ref, rel_ref, sink_ref,
                 wq_ref, bq_ref, wga_ref, bga_ref, wao_ref, wo_ref, bo_ref, g1_ref, b1_ref,
                 o_ref, xc_scr, q_scr, s_scr, attn_scr, merged_scr, y_scr, bias_scr, *, nr, alpha):
    R = ROWS_ATTN
    nblk = R // BLOCK
    i = pl.program_id(0)
    n_chunks = pl.num_programs(0) - 1
    r = lax.rem(jnp.minimum(i, n_chunks - 1), nr)
    slot = i % 2

    @pl.when(i == 0)
    def _():
        idx = idx_ref[...]
        for h in range(N_HEADS):
            def pick(bkt, acc):
                return jnp.where(idx == bkt, rel_ref[bkt, h], acc)
            tab = lax.fori_loop(0, N_BUCKETS, pick, jnp.full((BLOCK, 3 * BLOCK), NEG_INF, F32))
            kvh, g = divmod(h, GROUP)
            bias_scr[kvh, g * BLOCK:(g + 1) * BLOCK, :] = tab
        xc_scr[1] = jnp.zeros((R, D_MODEL), BF16)
        attn_scr[1] = jnp.zeros((R, Q_DIM), BF16)

    pen_prev = (r == 0).astype(F32) * NEG_INF
    pen_next = (r == nr - 1).astype(F32) * NEG_INF
    row_head = lax.broadcasted_iota(jnp.int32, (GROUP * BLOCK, 1), 0) // BLOCK

    def cast_x():
        xc_scr[slot] = xa_ref[...].astype(BF16)

    def q_project(kvh):
        cols = slice(kvh * GROUP * HEAD_DIM, (kvh + 1) * GROUP * HEAD_DIM)
        q_scr[:, cols] = (_dot(xc_scr[slot], wq_ref[:, cols]) + bq_ref[:, cols]).astype(BF16)

    def kv_window(ref_prev, ref_main, ref_next, kvh, n):
        cols = slice(kvh * HEAD_DIM, (kvh + 1) * HEAD_DIM)
        prev = ref_main[(n - 1) * BLOCK:n * BLOCK, cols] if n > 0 else ref_prev[:, cols]
        nxt = ref_main[(n + 1) * BLOCK:(n + 2) * BLOCK, cols] if n < nblk - 1 else ref_next[:, cols]
        return jnp.concatenate([prev, ref_main[n * BLOCK:(n + 1) * BLOCK, cols], nxt], axis=0)

    def scores(kvh, n):
        q4 = jnp.concatenate(
            [q_scr[n * BLOCK:(n + 1) * BLOCK, (kvh * GROUP + g) * HEAD_DIM:(kvh * GROUP + g + 1) * HEAD_DIM]
             for g in range(GROUP)], axis=0)
        k_ext = kv_window(kp_ref, km_ref, kn_ref, kvh, n)
        s = lax.dot_general(q4, k_ext, (((1,), (1,)), ((), ())), preferred_element_type=F32)
        s = s + bias_scr[kvh]
        if n == 0:
            s = jnp.concatenate([s[:, 0:BLOCK] + pen_prev, s[:, BLOCK:]], axis=1)
        if n == nblk - 1:
            s = jnp.concatenate([s[:, 0:2 * BLOCK], s[:, 2 * BLOCK:] + pen_next], axis=1)
        s_scr[(kvh * nblk + n) % 2] = s

    def softmax_pv(kvh, n):
        sink = jnp.zeros((GROUP * BLOCK, 1), F32)
        for g in range(GROUP):
            sink = jnp.where(row_head == g, sink_ref[kvh * GROUP + g], sink)
        s = s_scr[(kvh * nblk + n) % 2]
        mx = jnp.maximum(jnp.max(s, axis=1, keepdims=True), sink)
        pr = jnp.exp(s - mx)
        denom = jnp.sum(pr, axis=1, keepdims=True) + jnp.exp(sink - mx)
        v_ext = kv_window(vp_ref, vm_ref, vn_ref, kvh, n)
        o = _dot(pr.astype(BF16), v_ext) * (1.0 / denom)
        for g in range(GROUP):
            h = kvh * GROUP + g
            attn_scr[slot, n * BLOCK:(n + 1) * BLOCK, h * HEAD_DIM:(h + 1) * HEAD_DIM] = (
                o[g * BLOCK:(g + 1) * BLOCK].astype(BF16))

    rows4 = GROUP * BLOCK
    attn_items = [(cast_x, 0, _vpu_cycles(R, D_MODEL, 1))]
    for kvh in range(N_KV_HEADS):
        attn_items.append((functools.partial(q_project, kvh), _mxu_cycles(R, D_MODEL, GROUP * HEAD_DIM),
                           _vpu_cycles(R, GROUP * HEAD_DIM, 2)))
    blocks = [(kvh, n) for kvh in range(N_KV_HEADS) for n in range(nblk)]
    score_cost = (_mxu_cycles(rows4, HEAD_DIM, 3 * BLOCK), _vpu_cycles(rows4, 3 * BLOCK, 1))
    pv_cost = (2 * _mxu_cycles(rows4, 3 * BLOCK, HEAD_DIM), _vpu_cycles(rows4, 3 * BLOCK, 8))
    attn_items.append((functools.partial(scores, *blocks[0]),) + score_cost)
    for j, blk in enumerate(blocks):
        if j + 1 < len(blocks):
            attn_items.append((functools.partial(scores, *blocks[j + 1]),) + score_cost)
        attn_items.append((functools.partial(softmax_pv, *blk),) + pv_cost)

    def merge(c):
        cs = slice(c * ATTN_COLS, (c + 1) * ATTN_COLS)
        y_attn = _dot(attn_scr[1 - slot], wao_ref[:, cs])
        gate = _sigmoid_of_twice(_dot(xc_scr[1 - slot], wga_ref[:, cs]) + bga_ref[:, cs])
        merged_scr[:, cs] = (gate * y_attn + m_ref[:, cs].astype(F32)).astype(BF16)

    def out_project(c):
        cs = slice(c * ATTN_COLS, (c + 1) * ATTN_COLS)
        y_scr[:, cs] = alpha * xb_ref[:, cs] + (_dot(merged_scr[...], wo_ref[:, cs]) + bo_ref[:, cs])

    def norm():
        o_ref[...] = _layer_norm(y_scr[...], g1_ref[...], b1_ref[...])

    ncol = D_MODEL // ATTN_COLS
    merge_items = [(functools.partial(merge, c), _mxu_cycles(R, Q_DIM, ATTN_COLS) + _mxu_cycles(R, D_MODEL, ATTN_COLS),
                    _vpu_cycles(R, ATTN_COLS, 6)) for c in range(ncol)]
    merge_items += [(functools.partial(out_project, c), _mxu_cycles(R, D_MODEL, ATTN_COLS),
                     _vpu_cycles(R, ATTN_COLS, 3)) for c in range(ncol)]
    merge_items.append((norm, 0, _vpu_cycles(R, D_MODEL, 12)))
    _interleave(attn_items, merge_items)


def _attn_call(x, m_lru, k, v, p, alpha):
    B, S, D = x.shape
    R = ROWS_ATTN
    nr = S // R
    bpr = R // BLOCK
    nkb = S // BLOCK
    n_chunks = B * nr
    cur = lambda i: jnp.minimum(i, n_chunks - 1)
    old = lambda i: jnp.maximum(i - 1, 0)
    main_cur = lambda i: (cur(i) // nr, cur(i) % nr, 0)
    main_old = lambda i: (old(i) // nr, old(i) % nr, 0)
    prev = lambda i: (cur(i) // nr, jnp.maximum((cur(i) % nr) * bpr - 1, 0), 0)
    nxt = lambda i: (cur(i) // nr, jnp.minimum((cur(i) % nr + 1) * bpr, nkb - 1), 0)
    kv_main = pl.BlockSpec((None, R, KV_DIM), main_cur)
    kv_prev = pl.BlockSpec((None, BLOCK, KV_DIM), prev)
    kv_next = pl.BlockSpec((None, BLOCK, KV_DIM), nxt)
    in_specs = [
        pl.BlockSpec((None, R, D), main_cur),
        pl.BlockSpec((None, R, D), main_old),
        pl.BlockSpec((None, R, D), main_old),
        kv_prev, kv_main, kv_next,
        kv_prev, kv_main, kv_next,
        _const_spec((BLOCK, 3 * BLOCK)),
        pl.BlockSpec(memory_space=pltpu.SMEM),
        pl.BlockSpec(memory_space=pltpu.SMEM),
        _const_spec((D, Q_DIM)), _const_spec((1, Q_DIM)),
        _const_spec((D, D)), _const_spec((1, D)),
        _const_spec((Q_DIM, D)),
        _const_spec((D, D)), _const_spec((1, D)),
        _const_spec((1, D)), _const_spec((1, D)),
    ]
    scratch = [
        pltpu.VMEM((2, R, D), BF16),
        pltpu.VMEM((R, Q_DIM), BF16),
        pltpu.VMEM((2, GROUP * BLOCK, 3 * BLOCK), F32),
        pltpu.VMEM((2, R, Q_DIM), BF16),
        pltpu.VMEM((R, D), BF16),
        pltpu.VMEM((R, D), F32),
        pltpu.VMEM((N_KV_HEADS, GROUP * BLOCK, 3 * BLOCK), F32),
    ]
    return pl.pallas_call(
        functools.partial(_attn_kernel, nr=nr, alpha=alpha),
        grid=(n_chunks + 1,),
        in_specs=in_specs,
        out_specs=pl.BlockSpec((None, R, D), main_old),
        out_shape=jax.ShapeDtypeStruct((B, S, D), F32),
        scratch_shapes=scratch,
        compiler_params=pltpu.CompilerParams(
            dimension_semantics=("arbitrary",), vmem_limit_bytes=VMEM_LIMIT),
        name="attn_merge",
    )(x, x, m_lru, k, k, k, v, v, v, p["bias_idx"], p["rel_table"], p["sink"],
      p["w_q"], p["b_q"], p["w_mg_attn"], p["b_mg_attn"], p["w_attn_out"], p["w_o"], p["b_o"],
      p["ln1_g"], p["ln1_b"])


def _ffn_kernel(x_ref, xp_ref, xn_ref, wup_ref, bup_ref, wcv_ref, bcv_ref, wdn_ref, bdn_ref, g2_ref, b2_ref,
                o_ref, xe_scr, hs_scr, act_scr, *, nr, alpha):
    R = ROWS_FFN
    H = FFN_HALO
    nt = D_FF // FFN_COLS
    nslab = FFN_COLS // LANES
    r = pl.program_id(1)
    keep_prev = (r > 0).astype(F32)
    keep_next = (r < nr - 1).astype(F32)
    zeros = jnp.zeros((H - SUBLANES, D_MODEL), F32)
    xe_scr[...] = jnp.concatenate(
        [zeros, xp_ref[...] * keep_prev, x_ref[...], xn_ref[...] * keep_next, zeros], axis=0).astype(BF16)

    row = lax.broadcasted_iota(jnp.int32, (SUBLANES, FFN_COLS), 0)
    pad_prev = (row == SUBLANES - 1).astype(F32) * (1.0 - keep_prev)
    pad_next = (row == 0).astype(F32) * (1.0 - keep_next)

    def up_proj(j):
        xe = xe_scr[...]
        for half in range(2):
            y = _dot(xe, wup_ref[half, j])
            b_up = bup_ref[half, j]
            y = jnp.concatenate(
                [y[0:H - SUBLANES], y[H - SUBLANES:H] - pad_prev * b_up, y[H:H + R],
                 y[H + R:H + R + SUBLANES] - pad_next * b_up, y[H + R + SUBLANES:]], axis=0)
            for c in range(nslab):
                hs_scr[j % 2, half, c] = y[:, c * LANES:(c + 1) * LANES]

    def conv(j, half, c):
        lanes = slice(c * LANES, (c + 1) * LANES)
        w = wcv_ref[half, j][:, lanes]
        bias = bcv_ref[half, j][:, lanes] + bup_ref[half, j][:, lanes] * (w[0:1] + w[1:2] + w[2:3])
        return (w[0:1] * hs_scr[j % 2, half, c, H - 1:H - 1 + R, :]
                + w[1:2] * hs_scr[j % 2, half, c, H:H + R, :]
                + w[2:3] * hs_scr[j % 2, half, c, H + 1:H + 1 + R, :] + bias)

    def gate(j):
        for c in range(nslab):
            act = _gelu(conv(j, 1, c)) * conv(j, 0, c)
            act_scr[:, j * FFN_COLS + c * LANES:j * FFN_COLS + (c + 1) * LANES] = act.astype(BF16)

    per = nt // FFN_DOWN_SPLIT
    up_proj(0)
    for j in range(nt):
        if j + 1 < nt:
            up_proj(j + 1)
        gate(j)
        if (j + 1) % per == 0:
            part = (j + 1) // per - 1
            ks = slice(part * per * FFN_COLS, (part + 1) * per * FFN_COLS)
            contrib = _dot(act_scr[:, ks], wdn_ref[ks, :])
            if part == 0:
                o_ref[...] = contrib + (alpha * x_ref[...] + bdn_ref[...])
            elif part < FFN_DOWN_SPLIT - 1:
                o_ref[...] = o_ref[...] + contrib
            else:
                o_ref[...] = _layer_norm(o_ref[...] + contrib, g2_ref[...], b2_ref[...])


def _ffn_call(x, p, alpha):
    B, S, D = x.shape
    R = ROWS_FFN
    nr = S // R
    rpb = R // SUBLANES
    nsb = S // SUBLANES
    nt = D_FF // FFN_COLS
    main = lambda b, r: (b, r, 0)
    prev = lambda b, r: (b, jnp.maximum(r * rpb - 1, 0), 0)
    nxt = lambda b, r: (b, jnp.minimum((r + 1) * rpb, nsb - 1), 0)
    in_specs = [
        pl.BlockSpec((None, R, D), main),
        pl.BlockSpec((None, SUBLANES, D), prev),
        pl.BlockSpec((None, SUBLANES, D), nxt),
        _const_spec((2, nt, D, FFN_COLS)), _const_spec((2, nt, 1, FFN_COLS)),
        _const_spec((2, nt, 3, FFN_COLS)), _const_spec((2, nt, 1, FFN_COLS)),
        _const_spec((D_FF, D)), _const_spec((1, D)),
        _const_spec((1, D)), _const_spec((1, D)),
    ]
    scratch = [
        pltpu.VMEM((R + 2 * FFN_HALO, D), BF16),
        pltpu.VMEM((2, 2, FFN_COLS // LANES, R + 2 * FFN_HALO, LANES), F32),
        pltpu.VMEM((R, D_FF), BF16),
    ]
    return pl.pallas_call(
        functools.partial(_ffn_kernel, nr=nr, alpha=alpha),
        grid=(B, nr),
        in_specs=in_specs,
        out_specs=pl.BlockSpec((None, R, D), main),
        out_shape=jax.ShapeDtypeStruct((B, S, D), F32),
        scratch_shapes=scratch,
        compiler_params=pltpu.CompilerParams(
            dimension_semantics=("arbitrary", "arbitrary"), vmem_limit_bytes=VMEM_LIMIT),
        name="ffn",
    )(x, x, x, p["w_up"], p["b_up"], p["w_ffn_conv"], p["b_ffn_conv"], p["w_down"], p["b_down"],
      p["ln2_g"], p["ln2_b"])


def _t5_bucket(rel):
    nb = N_BUCKETS // 2
    ret = jnp.where(rel > 0, nb, 0)
    n = jnp.abs(rel)
    max_exact = nb // 2
    nf = jnp.maximum(n, 1).astype(jnp.float32)
    large = max_exact + (jnp.log(nf / max_exact) / math.log(MAX_DISTANCE / max_exact)
                         * (nb - max_exact)).astype(jnp.int32)
    large = jnp.minimum(large, nb - 1)
    return ret + jnp.where(n < max_exact, n, large)


def _pack_gate_weights(w_a, w_x):
    sel = np.zeros((N_LRU_BLOCKS, LRU_BLOCK, MXU_DIM), np.float32)
    for i in range(N_LRU_BLOCKS):
        lo = LRU_BLOCK * i - GATE_OFF[i]
        sel[i, np.arange(LRU_BLOCK), lo + np.arange(LRU_BLOCK)] = 1.0
    place = lambda w: jnp.einsum("nik,nij,njl->nkl", sel, w, sel, precision=lax.Precision.HIGHEST)
    return jnp.concatenate([place(w_a), place(w_x)], axis=-1).astype(BF16)


def _prepare(rel_table, w_in, b_in, w_lru_conv, b_lru_conv, w_rg_a, b_rg_a, w_rg_x, b_rg_x, lru_lambda,
             w_lru_out, attn_sink, w_attn_out, w_o, b_o, ln1_g, ln1_b, w_up, b_up, w_ffn_conv, b_ffn_conv,
             w_down, b_down, ln2_g, ln2_b):
    C = D_RNN
    o_g, o_q, o_k, o_gl, o_ga = C, 2 * C, 2 * C + Q_DIM, 2 * C + Q_DIM + 2 * KV_DIM, 2 * C + Q_DIM + 2 * KV_DIM + D_MODEL
    row = lambda a: a.reshape(1, -1).astype(F32)
    scale = HEAD_DIM ** -0.5
    nt = D_FF // FFN_COLS
    pm = _perm_matrix()

    q_off = jnp.arange(BLOCK)[:, None]
    c_off = jnp.arange(3 * BLOCK)[None, :]
    rel = c_off - BLOCK - q_off
    bias_idx = jnp.where(jnp.abs(rel) <= WINDOW, _t5_bucket(rel), -1).astype(jnp.int32)

    def ffn_cols(a):
        lead = a.shape[:-1]
        a = a.reshape(lead + (2, nt, FFN_COLS))
        return jnp.moveaxis(a, (-3, -2), (0, 1))

    return {
        "pm": jnp.asarray(pm, BF16), "pmt": jnp.asarray(pm.T, BF16),
        "w_u": w_in[:, 0:o_g].astype(BF16), "b_u": row(b_in[0:o_g]),
        "w_gl": w_in[:, o_g:o_q].astype(BF16), "b_gl": row(b_in[o_g:o_q]),
        "w_q": (w_in[:, o_q:o_k] * scale).astype(BF16), "b_q": row(b_in[o_q:o_k] * scale),
        "w_kv": w_in[:, o_k:o_gl].astype(BF16), "b_kv": row(b_in[o_k:o_gl]),
        "w_mg_lru": (0.5 * w_in[:, o_gl:o_ga]).astype(BF16), "b_mg_lru": row(0.5 * b_in[o_gl:o_ga]),
        "w_mg_attn": (0.5 * w_in[:, o_ga:]).astype(BF16), "b_mg_attn": row(0.5 * b_in[o_ga:]),
        "w_conv": w_lru_conv.astype(F32), "b_conv": row(b_lru_conv),
        "wg_f": _pack_gate_weights(0.5 * w_rg_a[0], 0.5 * w_rg_x[0]),
        "bg_f": row(0.5 * jnp.concatenate([b_rg_a[0].reshape(-1), b_rg_x[0].reshape(-1)])),
        "wg_b": _pack_gate_weights(0.5 * w_rg_a[1], 0.5 * w_rg_x[1]),
        "bg_b": row(0.5 * jnp.concatenate([b_rg_a[1].reshape(-1), b_rg_x[1].reshape(-1)])),
        "lam_f": row(lru_lambda[0]), "lam_b": row(lru_lambda[1]),
        "w_lru_out": w_lru_out.astype(BF16),
        "bias_idx": bias_idx, "rel_table": rel_table.astype(F32), "sink": attn_sink.astype(F32),
        "w_attn_out": w_attn_out.astype(BF16),
        "w_o": w_o.astype(BF16), "b_o": row(b_o),
        "ln1_g": row(ln1_g), "ln1_b": row(ln1_b),
        "w_up": ffn_cols(w_up).astype(BF16), "b_up": ffn_cols(b_up.reshape(1, -1)).astype(F32),
        "w_ffn_conv": ffn_cols(w_ffn_conv).astype(F32), "b_ffn_conv": ffn_cols(b_ffn_conv.reshape(1, -1)).astype(F32),
        "w_down": w_down.astype(BF16), "b_down": row(b_down),
        "ln2_g": row(ln2_g), "ln2_b": row(ln2_b),
    }


def _encoder_layer(x, p, alpha):
    h_b, u, k, v = _lru_bwd_call(x, p)
    m_lru = _lru_fwd_call(x, u, h_b, p)
    x1 = _attn_call(x, m_lru, k, v, p, alpha)
    return _ffn_call(x1, p, alpha)


def kernel(x_prompt, x_sample, rel_table, w_in, b_in, w_lru_conv, b_lru_conv, w_rg_a, b_rg_a, w_rg_x, b_rg_x, lru_lambda, w_lru_out, attn_sink, w_attn_out, w_o, b_o, ln1_g, ln1_b, w_up, b_up, w_ffn_conv, b_ffn_conv, w_down, b_down, ln2_g, ln2_b):
    depth = w_in.shape[0]
    alpha = (2.0 * depth) ** 0.25
    y_prompt, y_sample = x_prompt, x_sample
    for l in range(depth):
        p = _prepare(rel_table, w_in[l], b_in[l], w_lru_conv[l], b_lru_conv[l], w_rg_a[l], b_rg_a[l],
                     w_rg_x[l], b_rg_x[l], lru_lambda[l], w_lru_out[l], attn_sink[l], w_attn_out[l], w_o[l],
                     b_o[l], ln1_g[l], ln1_b[l], w_up[l], b_up[l], w_ffn_conv[l], b_ffn_conv[l], w_down[l],
                     b_down[l], ln2_g[l], ln2_b[l])
        y_prompt = _encoder_layer(y_prompt, p, alpha)
        y_sample = _encoder_layer(y_sample, p, alpha)
    return (y_prompt, y_sample)
```

```python
import functools
import math

import numpy as np
import jax
import jax.numpy as jnp
from jax import lax
from jax.experimental import pallas as pl
from jax.experimental.pallas import tpu as pltpu

F32 = jnp.float32
BF16 = jnp.bfloat16

D_MODEL = 1024
D_RNN = 1280
N_LRU_BLOCKS = 8
LRU_BLOCK = D_RNN // N_LRU_BLOCKS
LRU_C = 8.0
LRU_CONV_W = 4
HEAD_DIM = 128
N_HEADS = 8
N_KV_HEADS = 2
GROUP = N_HEADS // N_KV_HEADS
WINDOW = 128
BLOCK = 128
N_BUCKETS = 32
MAX_DISTANCE = 128
Q_DIM = N_HEADS * HEAD_DIM
KV_DIM = N_KV_HEADS * HEAD_DIM
D_FF = 3 * D_MODEL
LN_EPS = 1e-5
NEG_INF = -1e30
F32_TINY = float(np.finfo(np.float32).tiny)

LANES = 128
SUBLANES = 8
MXU_DIM = 256

N_STREAMS = SUBLANES
T_SEG = 64
PERM_T = MXU_DIM // N_STREAMS
HALO_ROWS = 4 * N_STREAMS
ROWS_ATTN = 512
ATTN_COLS = 256
ROWS_FFN = 1024
FFN_COLS = 512
FFN_DOWN_SPLIT = 3
FFN_HALO = 16
LRU_TILE = 64
LRU_COLS = 256
SCAN_UNROLL = 8
VMEM_LIMIT = 56 * 1024 * 1024
VMEM_LIMIT_FFN = 60 * 1024 * 1024

GATE_OFF = tuple(LANES * ((LRU_BLOCK * i) // LANES) for i in range(N_LRU_BLOCKS))


def _sigmoid_of_twice(z):
    return 0.5 * jnp.tanh(z) + 0.5


def _gelu(x):
    c = math.sqrt(2.0 / math.pi)
    inner = x * (c + (c * 0.044715) * (x * x))
    return x * (0.5 * jnp.tanh(inner) + 0.5)


def _layer_norm(y, g, b):
    mu = jnp.mean(y, axis=-1, keepdims=True)
    yc = y - mu
    var = jnp.mean(yc * yc, axis=-1, keepdims=True)
    return yc * lax.rsqrt(var + LN_EPS) * g + b


def _dot(a, b):
    return jnp.dot(a, b, preferred_element_type=F32)


def _const_spec(shape):
    nd = len(shape)
    return pl.BlockSpec(shape, lambda *_: (0,) * nd, pipeline_mode=pl.Buffered(1))


def _mxu_cycles(m, k, n):
    return (m // SUBLANES) * (n // LANES) * (-(-k // MXU_DIM))


def _vpu_cycles(rows, cols, ops_per_vreg):
    return rows * cols // (SUBLANES * LANES) * ops_per_vreg // 4


def _interleave(*streams):
    queues = [list(q) for q in streams]
    mxu = vpu = 0
    while any(queues):
        q = min((q for q in queues if q), key=lambda q: abs((mxu + q[0][1]) - (vpu + q[0][2])))
        fn, m, v = q.pop(0)
        fn()
        mxu += m
        vpu += v


def _permute_items(x4_ref, pm_ref, xb_dst, lhs_dst, row0, *, nb, ns):
    T = T_SEG
    R = N_STREAMS * T

    def copy():
        for b in range(nb):
            for sg in range(ns):
                i = b * ns + sg
                xb_dst[i * T:(i + 1) * T, :] = x4_ref[b, sg].astype(BF16)

    def permute(tb):
        src = jnp.concatenate(
            [xb_dst[i * T + PERM_T * tb:i * T + PERM_T * (tb + 1), :] for i in range(N_STREAMS)], axis=0)
        lhs_dst[row0 + MXU_DIM * tb:row0 + MXU_DIM * (tb + 1), :] = _dot(pm_ref[...], src).astype(BF16)

    items = [(copy, 0, _vpu_cycles(R, D_MODEL, 2))]
    for tb in range(T // PERM_T):
        items.append((functools.partial(permute, tb), _mxu_cycles(MXU_DIM, MXU_DIM, D_MODEL),
                      _vpu_cycles(MXU_DIM, D_MODEL, 1)))
    return items


def _input_items(x4_ref, xh_ref, pm_ref, wu_ref, bu_ref, wc_ref, bc_ref,
                 xb_dst, lhs_dst, uall_scr, u_dst, ub_dst, u_out, *, nb, ns, jj, nj):
    T = T_SEG
    R = N_STREAMS * T
    C = D_RNN
    ncol = C // LRU_COLS

    def halo():
        xh = xh_ref[...].astype(BF16)
        lhs_dst[0:2 * N_STREAMS, :] = xh[0:2 * N_STREAMS]
        lhs_dst[2 * N_STREAMS + R:, :] = xh[2 * N_STREAMS:]

    def project(c):
        cs = slice(c * LRU_COLS, (c + 1) * LRU_COLS)
        uall_scr[:, cs] = _dot(lhs_dst[...], wu_ref[:, cs]) + bu_ref[:, cs]
        seg = lax.broadcasted_iota(jnp.int32, (N_STREAMS, LRU_COLS), 0) & (ns - 1)
        keep_prev = 1.0 - (seg == 0).astype(F32) * (jj == 0).astype(F32)
        keep_next = 1.0 - (seg == ns - 1).astype(F32) * (jj == nj - 1).astype(F32)
        prev_rows = slice(0, 2 * N_STREAMS)
        next_rows = slice(2 * N_STREAMS + R, 3 * N_STREAMS + R)
        uall_scr[prev_rows, cs] = uall_scr[prev_rows, cs] * jnp.concatenate([keep_prev, keep_prev], axis=0)
        uall_scr[next_rows, cs] = uall_scr[next_rows, cs] * keep_next

    def conv(c):
        cs = slice(c * LRU_COLS, (c + 1) * LRU_COLS)
        acc = bc_ref[:, cs] + wc_ref[0:1, cs] * uall_scr[0:R, cs]
        for k in range(1, LRU_CONV_W):
            acc = acc + wc_ref[k:k + 1, cs] * uall_scr[N_STREAMS * k:N_STREAMS * k + R, cs]
        u_dst[:, cs] = acc
        ub_dst[:, cs] = acc.astype(BF16)
        u_out[:, cs] = acc.astype(BF16)

    items = _permute_items(x4_ref, pm_ref, xb_dst, lhs_dst, 2 * N_STREAMS, nb=nb, ns=ns)
    items.append((halo, 0, 0))
    project_cost = _mxu_cycles(R + HALO_ROWS, D_MODEL, LRU_COLS)
    conv_cost = _vpu_cycles(R, LRU_COLS, 2 * LRU_CONV_W + 1)
    items.append((functools.partial(project, 0), project_cost, 0))
    for c in range(ncol):
        if c + 1 < ncol:
            items.append((functools.partial(project, c + 1), project_cost, 0))
        items.append((functools.partial(conv, c), 0, conv_cost))
    return items


def _coeff_items(ub_src, u_src, wg_ref, bg_ref, lam_ref, pre_scr):
    R = N_STREAMS * T_SEG
    C = D_RNN

    def coeff(t):
        cols = slice(t * LANES, (t + 1) * LANES)
        xcols = slice(C + t * LANES, C + (t + 1) * LANES)
        lam = lam_ref[:, cols]
        log_sig = jnp.minimum(lam, 0.0) - jnp.log1p(jnp.exp(-jnp.abs(lam)))
        hc = (0.5 * LRU_C) * log_sig
        log_a = hc * jnp.tanh(pre_scr[:, cols]) + hc
        ig = 0.5 * jnp.tanh(pre_scr[:, xcols]) + 0.5
        a = jnp.exp(log_a)
        gain = jnp.tanh(log_a) * (-1.0 - a * a)
        root = gain * lax.rsqrt(jnp.maximum(gain, F32_TINY))
        pre_scr[:, cols] = a
        pre_scr[:, xcols] = root * (ig * u_src[:, cols])

    first_touch = {}
    last_touch = {}
    for i in range(N_LRU_BLOCKS):
        for part in range(MXU_DIM // LANES):
            tile = GATE_OFF[i] // LANES + part
            first_touch.setdefault(tile, i)
            last_touch[tile] = i

    def gate(i):
        off = GATE_OFF[i]
        y = _dot(ub_src[:, off:off + MXU_DIM], wg_ref[i])
        for half in range(2):
            for part in range(MXU_DIM // LANES):
                col = half * C + off + part * LANES
                val = y[:, half * MXU_DIM + part * LANES:half * MXU_DIM + (part + 1) * LANES]
                if first_touch[off // LANES + part] == i:
                    pre_scr[:, col:col + LANES] = val + bg_ref[:, col:col + LANES]
                else:
                    pre_scr[:, col:col + LANES] = pre_scr[:, col:col + LANES] + val

    gate_cost = (_mxu_cycles(R, MXU_DIM, 2 * MXU_DIM), _vpu_cycles(R, 2 * MXU_DIM, 1))
    coeff_cost = _vpu_cycles(R, LANES, 20)
    items = [(functools.partial(gate, 0),) + gate_cost]
    for i in range(N_LRU_BLOCKS):
        if i + 1 < N_LRU_BLOCKS:
            items.append((functools.partial(gate, i + 1),) + gate_cost)
        for t in sorted(t for t, last in last_touch.items() if last == i):
            items.append((functools.partial(coeff, t), 0, coeff_cost))
    return items


def _stage_scan(pre_scr, p_scr, carry_scr, h_dst, *, ns, fwd, first):
    T = T_SEG
    C = D_RNN

    @pl.when(first)
    def _():
        carry_scr[...] = jnp.zeros_like(carry_scr)

    def scan_step(tt, carry):
        h, p = carry
        t = tt if fwd else T - 1 - tt
        r0 = pl.multiple_of(t * N_STREAMS, N_STREAMS)
        a = pre_scr[pl.ds(r0, N_STREAMS), 0:C]
        h = a * h + pre_scr[pl.ds(r0, N_STREAMS), C:2 * C]
        h_dst[pl.ds(r0, N_STREAMS), :] = h
        if ns > 1:
            p = p * a
            p_scr[pl.ds(r0, N_STREAMS), :] = p
        return h, p

    if ns == 1:
        h0 = carry_scr[...]
    else:
        h0 = jnp.zeros((N_STREAMS, C), F32)
    h_end, p_end = lax.fori_loop(0, T, scan_step, (h0, jnp.ones((N_STREAMS, C), F32)), unroll=SCAN_UNROLL)

    if ns == 1:
        carry_scr[...] = h_end
        return None

    sub = lax.broadcasted_iota(jnp.int32, (N_STREAMS, C), 0)
    seg = sub & (ns - 1)
    carry = carry_scr[...]
    head = (seg == 0) if fwd else (seg == ns - 1)
    from_prev = 1 if fwd else N_STREAMS - 1
    final = jnp.zeros((N_STREAMS, C), F32)
    cin = final
    for _ in range(ns):
        cin = jnp.where(head, carry, pltpu.roll(final, from_prev, 0))
        final = h_end + p_end * cin
    to_head = (N_STREAMS - (ns - 1)) % N_STREAMS if fwd else ns - 1
    carry_scr[...] = pltpu.roll(final, to_head, 0)
    return cin


def _add_carry(h, p, cin):
    shape3 = (h.shape[0] // N_STREAMS, N_STREAMS, h.shape[1])
    return (h.reshape(shape3) + p.reshape(shape3) * cin[None]).reshape(h.shape)


def _lru_bwd_kernel(x4_ref, xh_ref, pm_ref, wu_ref, bu_ref, wc_ref, bc_ref, wg_ref, bg_ref, lam_ref,
                    wkv_ref, bkv_ref,
                    h_ref, u_ref, k_ref, v_ref,
                    xb_scr, lhs_scr, uall_scr, u_scr, ub_scr, pre_scr, p_scr, carry_scr,
                    *, nb, ns, nj):
    T = T_SEG
    R = N_STREAMS * T
    s = pl.program_id(1)
    c_in = jnp.minimum(s, nj - 1)
    c_scan = jnp.clip(s - 2, 0, nj - 1)
    slot = s % 2

    @pl.when(s == 0)
    def _():
        pre_scr[...] = jnp.zeros_like(pre_scr)
        u_scr[1] = jnp.zeros((R, D_RNN), F32)
        ub_scr[1] = jnp.zeros((R, D_RNN), BF16)

    cin = _stage_scan(pre_scr, p_scr, carry_scr, h_ref, ns=ns, fwd=False, first=(c_scan == 0))
    if cin is not None:
        for it in range(R // LRU_TILE):
            rows = slice(it * LRU_TILE, (it + 1) * LRU_TILE)
            h_ref[rows, :] = _add_carry(h_ref[rows, :], p_scr[rows, :], cin)

    def kv_project():
        kv = _dot(xb_scr[...], wkv_ref[...]) + bkv_ref[...]
        for b in range(nb):
            for sg in range(ns):
                i = b * ns + sg
                k_ref[b, sg] = kv[i * T:(i + 1) * T, 0:KV_DIM].astype(BF16)
                v_ref[b, sg] = kv[i * T:(i + 1) * T, KV_DIM:2 * KV_DIM].astype(BF16)

    input_items = _input_items(x4_ref, xh_ref, pm_ref, wu_ref, bu_ref, wc_ref, bc_ref,
                               xb_scr, lhs_scr, uall_scr, u_scr.at[slot], ub_scr.at[slot], u_ref,
                               nb=nb, ns=ns, jj=nj - 1 - c_in, nj=nj)
    input_items.append((kv_project, _mxu_cycles(R, D_MODEL, 2 * KV_DIM), _vpu_cycles(R, 2 * KV_DIM, 2)))
    _interleave(_coeff_items(ub_scr.at[1 - slot], u_scr.at[1 - slot], wg_ref, bg_ref, lam_ref, pre_scr),
                input_items)


def _lru_fwd_kernel(x4_ref, u_ref, hb_ref, pm_ref, pmt_ref, wg_ref, bg_ref,
                    lam_ref, wgl_ref, bgl_ref, wmg_ref, bmg_ref, wout_ref,
                    m_ref,
                    xb_scr, lhs_scr, pre_scr, p_scr, carry_scr,
                    h_scr, g_scr, hg_scr, hgn_scr,
                    *, nb, ns, nj):
    T = T_SEG
    R = N_STREAMS * T
    C = D_RNN
    ncol = C // LRU_COLS
    s = pl.program_id(1)
    c_out = jnp.clip(s - 1, 0, nj - 1)
    slot = s % 2

    @pl.when(s == 0)
    def _():
        pre_scr[...] = jnp.zeros_like(pre_scr)
        xb_scr[1] = jnp.zeros((R, D_MODEL), BF16)
        lhs_scr[1] = jnp.zeros((R, D_MODEL), BF16)

    cin = _stage_scan(pre_scr, p_scr, carry_scr, h_scr, ns=ns, fwd=True, first=(c_out == 0))

    def gate_project(c):
        cs = slice(c * LRU_COLS, (c + 1) * LRU_COLS)
        g_scr[c % 2] = _dot(lhs_scr[1 - slot], wgl_ref[:, cs]) + bgl_ref[:, cs]

    def gate_apply(c):
        cs = slice(c * LRU_COLS, (c + 1) * LRU_COLS)
        h = h_scr[:, cs]
        if cin is not None:
            h = _add_carry(h, p_scr[:, cs], cin[:, cs])
        h = h + hb_ref[:, cs]
        hg_scr[:, cs] = (h * _gelu(g_scr[c % 2])).astype(BF16)

    def unpermute(c):
        cs = slice(c * LRU_COLS, (c + 1) * LRU_COLS)
        for tb in range(T // PERM_T):
            nat = _dot(pmt_ref[...], hg_scr[MXU_DIM * tb:MXU_DIM * (tb + 1), cs]).astype(BF16)
            for i in range(N_STREAMS):
                hgn_scr[i * T + PERM_T * tb:i * T + PERM_T * (tb + 1), cs] = nat[i * PERM_T:(i + 1) * PERM_T]

    def out_project(c):
        cs = slice(c * LRU_COLS, (c + 1) * LRU_COLS)
        y_lru = _dot(hgn_scr[...], wout_ref[:, cs])
        gate = _sigmoid_of_twice(_dot(xb_scr[1 - slot], wmg_ref[:, cs]) + bmg_ref[:, cs])
        m = (gate * y_lru).astype(BF16)
        for b in range(nb):
            for sg in range(ns):
                i = b * ns + sg
                m_ref[b, sg, :, cs] = m[i * T:(i + 1) * T]

    gp_cost = _mxu_cycles(R, D_MODEL, LRU_COLS)
    out_items = [(functools.partial(gate_project, 0), gp_cost, 0)]
    for c in range(ncol):
        if c + 1 < ncol:
            out_items.append((functools.partial(gate_project, c + 1), gp_cost, 0))
        out_items.append((functools.partial(gate_apply, c), 0, _vpu_cycles(R, LRU_COLS, 14)))
        out_items.append((functools.partial(unpermute, c), (T // PERM_T) * _mxu_cycles(MXU_DIM, MXU_DIM, LRU_COLS),
                          _vpu_cycles(R, LRU_COLS, 1)))
    for c in range(D_MODEL // LRU_COLS):
        out_items.append((functools.partial(out_project, c),
                          _mxu_cycles(R, C, LRU_COLS) + _mxu_cycles(R, D_MODEL, LRU_COLS),
                          _vpu_cycles(R, LRU_COLS, 5)))

    in_items = _permute_items(x4_ref, pm_ref, xb_scr.at[slot], lhs_scr.at[slot], 0, nb=nb, ns=ns)
    in_items += _coeff_items(u_ref, u_ref, wg_ref, bg_ref, lam_ref, pre_scr)
    _interleave(out_items, in_items)


def _stream_split(batch):
    nb = math.gcd(batch, N_STREAMS)
    return nb, N_STREAMS // nb


def _halo_rows(x4, nb, ns):
    B, Q, T, D = x4.shape
    last2 = x4[:, :, T - 2:, :]
    first = x4[:, :, 0, :]
    prev2 = jnp.pad(last2, ((0, 0), (1, 0), (0, 0), (0, 0)))[:, :Q]
    next1 = jnp.pad(first, ((0, 0), (0, 1), (0, 0)))[:, 1:]
    slots = jnp.stack([prev2[:, :, 0], prev2[:, :, 1], next1, jnp.zeros_like(next1)], axis=2)
    nj = Q // ns
    slots = slots.reshape(B // nb, nb, nj, ns, 4, D).transpose(0, 2, 4, 1, 3, 5)
    return slots.reshape(B // nb, nj, HALO_ROWS, D)


def _perm_matrix():
    pm = np.zeros((MXU_DIM, MXU_DIM), np.float32)
    for i in range(N_STREAMS):
        for t in range(PERM_T):
            pm[t * N_STREAMS + i, i * PERM_T + t] = 1.0
    return pm


def _lru_bwd_call(x, p):
    B, S, D = x.shape
    nb, ns = _stream_split(B)
    T = T_SEG
    R = N_STREAMS * T
    Q = S // T
    nj = Q // ns
    x4 = x.reshape(B, Q, T, D)
    xh = _halo_rows(x4, nb, ns)
    C = D_RNN
    chunk_in = lambda bb, s: (bb, nj - 1 - jnp.minimum(s, nj - 1), 0, 0)
    chunk_scan = lambda bb, s: (bb, nj - 1 - jnp.clip(s - 2, 0, nj - 1), 0, 0)
    in_specs = [
        pl.BlockSpec((nb, ns, T, D), chunk_in),
        pl.BlockSpec((None, None, HALO_ROWS, D), chunk_in),
        _const_spec((MXU_DIM, MXU_DIM)),
        _const_spec((D, C)), _const_spec((1, C)),
        _const_spec((LRU_CONV_W, C)), _const_spec((1, C)),
        _const_spec((N_LRU_BLOCKS, MXU_DIM, 2 * MXU_DIM)), _const_spec((1, 2 * C)),
        _const_spec((1, C)),
        _const_spec((D, 2 * KV_DIM)), _const_spec((1, 2 * KV_DIM)),
    ]
    out_shape = [
        jax.ShapeDtypeStruct((B // nb, nj, R, C), F32),
        jax.ShapeDtypeStruct((B // nb, nj, R, C), BF16),
        jax.ShapeDtypeStruct((B, Q, T, KV_DIM), BF16),
        jax.ShapeDtypeStruct((B, Q, T, KV_DIM), BF16),
    ]
    out_specs = [
        pl.BlockSpec((None, None, R, C), chunk_scan),
        pl.BlockSpec((None, None, R, C), chunk_in),
        pl.BlockSpec((nb, ns, T, KV_DIM), chunk_in),
        pl.BlockSpec((nb, ns, T, KV_DIM), chunk_in),
    ]
    scratch = [
        pltpu.VMEM((R, D), BF16),
        pltpu.VMEM((R + HALO_ROWS, D), BF16),
        pltpu.VMEM((R + HALO_ROWS, C), F32),
        pltpu.VMEM((2, R, C), F32),
        pltpu.VMEM((2, R, C), BF16),
        pltpu.VMEM((R, 2 * C), F32),
        pltpu.VMEM((R, C), F32),
        pltpu.VMEM((N_STREAMS, C), F32),
    ]
    h_b, u, k, v = pl.pallas_call(
        functools.partial(_lru_bwd_kernel, nb=nb, ns=ns, nj=nj),
        grid=(B // nb, nj + 2),
        in_specs=in_specs, out_specs=out_specs, out_shape=out_shape,
        scratch_shapes=scratch,
        compiler_params=pltpu.CompilerParams(
            dimension_semantics=("arbitrary", "arbitrary"), vmem_limit_bytes=VMEM_LIMIT),
        name="lru_bwd",
    )(x4, xh, p["pm"], p["w_u"], p["b_u"], p["w_conv"], p["b_conv"], p["wg_b"], p["bg_b"], p["lam_b"],
      p["w_kv"], p["b_kv"])
    return h_b, u, k.reshape(B, S, KV_DIM), v.reshape(B, S, KV_DIM)


def _lru_fwd_call(x, u, h_b, p):
    B, S, D = x.shape
    nb, ns = _stream_split(B)
    T = T_SEG
    R = N_STREAMS * T
    Q = S // T
    nj = Q // ns
    x4 = x.reshape(B, Q, T, D)
    C = D_RNN
    chunk_in = lambda bb, s: (bb, jnp.minimum(s, nj - 1), 0, 0)
    chunk_out = lambda bb, s: (bb, jnp.clip(s - 1, 0, nj - 1), 0, 0)
    in_specs = [
        pl.BlockSpec((nb, ns, T, D), chunk_in),
        pl.BlockSpec((None, None, R, C), chunk_in),
        pl.BlockSpec((None, None, R, C), chunk_out),
        _const_spec((MXU_DIM, MXU_DIM)), _const_spec((MXU_DIM, MXU_DIM)),
        _const_spec((N_LRU_BLOCKS, MXU_DIM, 2 * MXU_DIM)), _const_spec((1, 2 * C)),
        _const_spec((1, C)),
        _const_spec((D, C)), _const_spec((1, C)),
        _const_spec((D, D)), _const_spec((1, D)),
        _const_spec((C, D)),
    ]
    scratch = [
        pltpu.VMEM((2, R, D), BF16),
        pltpu.VMEM((2, R, D), BF16),
        pltpu.VMEM((R, 2 * C), F32),
        pltpu.VMEM((R, C), F32),
        pltpu.VMEM((N_STREAMS, C), F32),
        pltpu.VMEM((R, C), F32),
        pltpu.VMEM((2, R, LRU_COLS), F32),
        pltpu.VMEM((R, C), BF16),
        pltpu.VMEM((R, C), BF16),
    ]
    m = pl.pallas_call(
        functools.partial(_lru_fwd_kernel, nb=nb, ns=ns, nj=nj),
        grid=(B // nb, nj + 1),
        in_specs=in_specs,
        out_specs=pl.BlockSpec((nb, ns, T, D), chunk_out),
        out_shape=jax.ShapeDtypeStruct((B, Q, T, D), BF16),
        scratch_shapes=scratch,
        compiler_params=pltpu.CompilerParams(
            dimension_semantics=("arbitrary", "arbitrary"), vmem_limit_bytes=VMEM_LIMIT),
        name="lru_fwd",
    )(x4, u, h_b, p["pm"], p["pmt"], p["wg_f"], p["bg_f"],
      p["lam_f"], p["w_gl"], p["b_gl"], p["w_mg_lru"], p["b_mg_lru"], p["w_lru_out"])
    return m.reshape(B, S, D)


def _attn_kernel(xa_ref, xb_ref, m_ref, kp_ref, km_ref, kn_ref, vp_ref, vm_ref, vn_ref, idx_ref, rel_ref, sink_ref,
                 wq_ref, bq_ref, wga_ref, bga_ref, wao_ref, wo_ref, bo_ref, g1_ref, b1_ref,
                 o_ref, xc_scr, q_scr, s_scr, attn_scr, merged_scr, y_scr, bias_scr, *, nr, alpha):
    R = ROWS_ATTN
    nblk = R // BLOCK
    i = pl.program_id(0)
    n_chunks = pl.num_programs(0) - 1
    r = lax.rem(jnp.minimum(i, n_chunks - 1), nr)
    slot = i % 2

    @pl.when(i == 0)
    def _():
        idx = idx_ref[...]
        for h in range(N_HEADS):
            def pick(bkt, acc):
                return jnp.where(idx == bkt, rel_ref[bkt, h], acc)
            tab = lax.fori_loop(0, N_BUCKETS, pick, jnp.full((BLOCK, 3 * BLOCK), NEG_INF, F32))
            kvh, g = divmod(h, GROUP)
            bias_scr[kvh, g * BLOCK:(g + 1) * BLOCK, :] = tab
        xc_scr[1] = jnp.zeros((R, D_MODEL), BF16)
        attn_scr[1] = jnp.zeros((R, Q_DIM), BF16)

    pen_prev = (r == 0).astype(F32) * NEG_INF
    pen_next = (r == nr - 1).astype(F32) * NEG_INF
    row_head = lax.broadcasted_iota(jnp.int32, (GROUP * BLOCK, 1), 0) // BLOCK

    def cast_x():
        xc_scr[slot] = xa_ref[...].astype(BF16)

    def q_project(kvh):
        cols = slice(kvh * GROUP * HEAD_DIM, (kvh + 1) * GROUP * HEAD_DIM)
        q_scr[:, cols] = (_dot(xc_scr[slot], wq_ref[:, cols]) + bq_ref[:, cols]).astype(BF16)

    def kv_window(ref_prev, ref_main, ref_next, kvh, n):
        cols = slice(kvh * HEAD_DIM, (kvh + 1) * HEAD_DIM)
        prev = ref_main[(n - 1) * BLOCK:n * BLOCK, cols] if n > 0 else ref_prev[:, cols]
        nxt = ref_main[(n + 1) * BLOCK:(n + 2) * BLOCK, cols] if n < nblk - 1 else ref_next[:, cols]
        return jnp.concatenate([prev, ref_main[n * BLOCK:(n + 1) * BLOCK, cols], nxt], axis=0)

    def scores(kvh, n):
        q4 = jnp.concatenate(
            [q_scr[n * BLOCK:(n + 1) * BLOCK, (kvh * GROUP + g) * HEAD_DIM:(kvh * GROUP + g + 1) * HEAD_DIM]
             for g in range(GROUP)], axis=0)
        k_ext = kv_window(kp_ref, km_ref, kn_ref, kvh, n)
        s = lax.dot_general(q4, k_ext, (((1,), (1,)), ((), ())), preferred_element_type=F32)
        s = s + bias_scr[kvh]
        if n == 0:
            s = jnp.concatenate([s[:, 0:BLOCK] + pen_prev, s[:, BLOCK:]], axis=1)
        if n == nblk - 1:
            s = jnp.concatenate([s[:, 0:2 * BLOCK], s[:, 2 * BLOCK:] + pen_next], axis=1)
        s_scr[(kvh * nblk + n) % 2] = s

    def softmax_pv(kvh, n):
        sink = jnp.zeros((GROUP * BLOCK, 1), F32)
        for g in range(GROUP):
            sink = jnp.where(row_head == g, sink_ref[kvh * GROUP + g], sink)
        s = s_scr[(kvh * nblk + n) % 2]
        mx = jnp.maximum(jnp.max(s, axis=1, keepdims=True), sink)
        pr = jnp.exp(s - mx)
        denom = jnp.sum(pr, axis=1, keepdims=True) + jnp.exp(sink - mx)
        v_ext = kv_window(vp_ref, vm_ref, vn_ref, kvh, n)
        o = _dot(pr.astype(BF16), v_ext) * (1.0 / denom)
        for g in range(GROUP):
            h = kvh * GROUP + g
            attn_scr[slot, n * BLOCK:(n + 1) * BLOCK, h * HEAD_DIM:(h + 1) * HEAD_DIM] = (
                o[g * BLOCK:(g + 1) * BLOCK].astype(BF16))

    rows4 = GROUP * BLOCK
    attn_items = [(cast_x, 0, _vpu_cycles(R, D_MODEL, 1))]
    for kvh in range(N_KV_HEADS):
        attn_items.append((functools.partial(q_project, kvh), _mxu_cycles(R, D_MODEL, GROUP * HEAD_DIM),
                           _vpu_cycles(R, GROUP * HEAD_DIM, 2)))
    blocks = [(kvh, n) for kvh in range(N_KV_HEADS) for n in range(nblk)]
    score_cost = (_mxu_cycles(rows4, HEAD_DIM, 3 * BLOCK), _vpu_cycles(rows4, 3 * BLOCK, 1))
    pv_cost = (2 * _mxu_cycles(rows4, 3 * BLOCK, HEAD_DIM), _vpu_cycles(rows4, 3 * BLOCK, 8))
    attn_items.append((functools.partial(scores, *blocks[0]),) + score_cost)
    for j, blk in enumerate(blocks):
        if j + 1 < len(blocks):
            attn_items.append((functools.partial(scores, *blocks[j + 1]),) + score_cost)
        attn_items.append((functools.partial(softmax_pv, *blk),) + pv_cost)

    def merge(c):
        cs = slice(c * ATTN_COLS, (c + 1) * ATTN_COLS)
        y_attn = _dot(attn_scr[1 - slot], wao_ref[:, cs])
        gate = _sigmoid_of_twice(_dot(xc_scr[1 - slot], wga_ref[:, cs]) + bga_ref[:, cs])
        merged_scr[:, cs] = (gate * y_attn + m_ref[:, cs].astype(F32)).astype(BF16)

    def out_project(c):
        cs = slice(c * ATTN_COLS, (c + 1) * ATTN_COLS)
        y_scr[:, cs] = alpha * xb_ref[:, cs] + (_dot(merged_scr[...], wo_ref[:, cs]) + bo_ref[:, cs])

    def norm():
        o_ref[...] = _layer_norm(y_scr[...], g1_ref[...], b1_ref[...])

    ncol = D_MODEL // ATTN_COLS
    merge_items = [(functools.partial(merge, c), _mxu_cycles(R, Q_DIM, ATTN_COLS) + _mxu_cycles(R, D_MODEL, ATTN_COLS),
                    _vpu_cycles(R, ATTN_COLS, 6)) for c in range(ncol)]
    merge_items += [(functools.partial(out_project, c), _mxu_cycles(R, D_MODEL, ATTN_COLS),
                     _vpu_cycles(R, ATTN_COLS, 3)) for c in range(ncol)]
    merge_items.append((norm, 0, _vpu_cycles(R, D_MODEL, 12)))
    _interleave(attn_items, merge_items)


def _attn_call(x, m_lru, k, v, p, alpha):
    B, S, D = x.shape
    R = ROWS_ATTN
    nr = S // R
    bpr = R // BLOCK
    nkb = S // BLOCK
    n_chunks = B * nr
    cur = lambda i: jnp.minimum(i, n_chunks - 1)
    old = lambda i: jnp.maximum(i - 1, 0)
    main_cur = lambda i: (cur(i) // nr, cur(i) % nr, 0)
    main_old = lambda i: (old(i) // nr, old(i) % nr, 0)
    prev = lambda i: (cur(i) // nr, jnp.maximum((cur(i) % nr) * bpr - 1, 0), 0)
    nxt = lambda i: (cur(i) // nr, jnp.minimum((cur(i) % nr + 1) * bpr, nkb - 1), 0)
    kv_main = pl.BlockSpec((None, R, KV_DIM), main_cur)
    kv_prev = pl.BlockSpec((None, BLOCK, KV_DIM), prev)
    kv_next = pl.BlockSpec((None, BLOCK, KV_DIM), nxt)
    in_specs = [
        pl.BlockSpec((None, R, D), main_cur),
        pl.BlockSpec((None, R, D), main_old),
        pl.BlockSpec((None, R, D), main_old),
        kv_prev, kv_main, kv_next,
        kv_prev, kv_main, kv_next,
        _const_spec((BLOCK, 3 * BLOCK)),
        pl.BlockSpec(memory_space=pltpu.SMEM),
        pl.BlockSpec(memory_space=pltpu.SMEM),
        _const_spec((D, Q_DIM)), _const_spec((1, Q_DIM)),
        _const_spec((D, D)), _const_spec((1, D)),
        _const_spec((Q_DIM, D)),
        _const_spec((D, D)), _const_spec((1, D)),
        _const_spec((1, D)), _const_spec((1, D)),
    ]
    scratch = [
        pltpu.VMEM((2, R, D), BF16),
        pltpu.VMEM((R, Q_DIM), BF16),
        pltpu.VMEM((2, GROUP * BLOCK, 3 * BLOCK), F32),
        pltpu.VMEM((2, R, Q_DIM), BF16),
        pltpu.VMEM((R, D), BF16),
        pltpu.VMEM((R, D), F32),
        pltpu.VMEM((N_KV_HEADS, GROUP * BLOCK, 3 * BLOCK), F32),
    ]
    return pl.pallas_call(
        functools.partial(_attn_kernel, nr=nr, alpha=alpha),
        grid=(n_chunks + 1,),
        in_specs=in_specs,
        out_specs=pl.BlockSpec((None, R, D), main_old),
        out_shape=jax.ShapeDtypeStruct((B, S, D), F32),
        scratch_shapes=scratch,
        compiler_params=pltpu.CompilerParams(
            dimension_semantics=("arbitrary",), vmem_limit_bytes=VMEM_LIMIT),
        name="attn_merge",
    )(x, x, m_lru, k, k, k, v, v, v, p["bias_idx"], p["rel_table"], p["sink"],
      p["w_q"], p["b_q"], p["w_mg_attn"], p["b_mg_attn"], p["w_attn_out"], p["w_o"], p["b_o"],
      p["ln1_g"], p["ln1_b"])


def _ffn_kernel(x_ref, xp_ref, xn_ref, wup_ref, bup_ref, wcv_ref, bcv_ref, wdn_ref, bdn_ref, g2_ref, b2_ref,
                o_ref, xe_scr, hs_scr, act_scr, *, nr, alpha):
    R = ROWS_FFN
    H = FFN_HALO
    nt = D_FF // FFN_COLS
    nslab = FFN_COLS // LANES
    r = pl.program_id(1)
    keep_prev = (r > 0).astype(F32)
    keep_next = (r < nr - 1).astype(F32)
    zeros = jnp.zeros((H - SUBLANES, D_MODEL), F32)
    xe_scr[...] = jnp.concatenate(
        [zeros, xp_ref[...] * keep_prev, x_ref[...], xn_ref[...] * keep_next, zeros], axis=0).astype(BF16)

    row = lax.broadcasted_iota(jnp.int32, (SUBLANES, FFN_COLS), 0)
    pad_prev = (row == SUBLANES - 1).astype(F32) * (1.0 - keep_prev)
    pad_next = (row == 0).astype(F32) * (1.0 - keep_next)

    def up_proj(j):
        xe = xe_scr[...]
        for half in range(2):
            y = _dot(xe, wup_ref[half, j])
            b_up = bup_ref[half, j]
            y = jnp.concatenate(
                [y[0:H - SUBLANES], y[H - SUBLANES:H] - pad_prev * b_up, y[H:H + R],
                 y[H + R:H + R + SUBLANES] - pad_next * b_up, y[H + R + SUBLANES:]], axis=0)
            for c in range(nslab):
                hs_scr[j % 2, half, c] = y[:, c * LANES:(c + 1) * LANES]

    def conv(j, half, c):
        lanes = slice(c * LANES, (c + 1) * LANES)
        w = wcv_ref[half, j][:, lanes]
        bias = bcv_ref[half, j][:, lanes] + bup_ref[half, j][:, lanes] * (w[0:1] + w[1:2] + w[2:3])
        return (w[0:1] * hs_scr[j % 2, half, c, H - 1:H - 1 + R, :]
                + w[1:2] * hs_scr[j % 2, half, c, H:H + R, :]
                + w[2:3] * hs_scr[j % 2, half, c, H + 1:H + 1 + R, :] + bias)

    def gate(j):
        for c in range(nslab):
            act = _gelu(conv(j, 1, c)) * conv(j, 0, c)
            act_scr[:, j * FFN_COLS + c * LANES:j * FFN_COLS + (c + 1) * LANES] = act.astype(BF16)

    per = nt // FFN_DOWN_SPLIT
    up_proj(0)
    for j in range(nt):
        if j + 1 < nt:
            up_proj(j + 1)
        gate(j)
        if (j + 1) % per == 0:
            part = (j + 1) // per - 1
            ks = slice(part * per * FFN_COLS, (part + 1) * per * FFN_COLS)
            contrib = _dot(act_scr[:, ks], wdn_ref[ks, :])
            if part == 0:
                o_ref[...] = contrib + (alpha * x_ref[...] + bdn_ref[...])
            elif part < FFN_DOWN_SPLIT - 1:
                o_ref[...] = o_ref[...] + contrib
            else:
                o_ref[...] = _layer_norm(o_ref[...] + contrib, g2_ref[...], b2_ref[...])


def _ffn_call(x, p, alpha):
    B, S, D = x.shape
    R = ROWS_FFN
    nr = S // R
    rpb = R // SUBLANES
    nsb = S // SUBLANES
    nt = D_FF // FFN_COLS
    main = lambda b, r: (b, r, 0)
    prev = lambda b, r: (b, jnp.maximum(r * rpb - 1, 0), 0)
    nxt = lambda b, r: (b, jnp.minimum((r + 1) * rpb, nsb - 1), 0)
    in_specs = [
        pl.BlockSpec((None, R, D), main),
        pl.BlockSpec((None, SUBLANES, D), prev),
        pl.BlockSpec((None, SUBLANES, D), nxt),
        _const_spec((2, nt, D, FFN_COLS)), _const_spec((2, nt, 1, FFN_COLS)),
        _const_spec((2, nt, 3, FFN_COLS)), _const_spec((2, nt, 1, FFN_COLS)),
        _const_spec((D_FF, D)), _const_spec((1, D)),
        _const_spec((1, D)), _const_spec((1, D)),
    ]
    scratch = [
        pltpu.VMEM((R + 2 * FFN_HALO, D), BF16),
        pltpu.VMEM((2, 2, FFN_COLS // LANES, R + 2 * FFN_HALO, LANES), F32),
        pltpu.VMEM((R, D_FF), BF16),
    ]
    return pl.pallas_call(
        functools.partial(_ffn_kernel, nr=nr, alpha=alpha),
        grid=(B, nr),
        in_specs=in_specs,
        out_specs=pl.BlockSpec((None, R, D), main),
        out_shape=jax.ShapeDtypeStruct((B, S, D), F32),
        scratch_shapes=scratch,
        compiler_params=pltpu.CompilerParams(
            dimension_semantics=("arbitrary", "arbitrary"), vmem_limit_bytes=VMEM_LIMIT_FFN),
        name="ffn",
    )(x, x, x, p["w_up"], p["b_up"], p["w_ffn_conv"], p["b_ffn_conv"], p["w_down"], p["b_down"],
      p["ln2_g"], p["ln2_b"])


def _t5_bucket(rel):
    nb = N_BUCKETS // 2
    ret = jnp.where(rel > 0, nb, 0)
    n = jnp.abs(rel)
    max_exact = nb // 2
    nf = jnp.maximum(n, 1).astype(jnp.float32)
    large = max_exact + (jnp.log(nf / max_exact) / math.log(MAX_DISTANCE / max_exact)
                         * (nb - max_exact)).astype(jnp.int32)
    large = jnp.minimum(large, nb - 1)
    return ret + jnp.where(n < max_exact, n, large)


def _pack_gate_weights(w_a, w_x):
    sel = np.zeros((N_LRU_BLOCKS, LRU_BLOCK, MXU_DIM), np.float32)
    for i in range(N_LRU_BLOCKS):
        lo = LRU_BLOCK * i - GATE_OFF[i]
        sel[i, np.arange(LRU_BLOCK), lo + np.arange(LRU_BLOCK)] = 1.0
    place = lambda w: jnp.einsum("nik,nij,njl->nkl", sel, w, sel, precision=lax.Precision.HIGHEST)
    return jnp.concatenate([place(w_a), place(w_x)], axis=-1).astype(BF16)


def _prepare(rel_table, w_in, b_in, w_lru_conv, b_lru_conv, w_rg_a, b_rg_a, w_rg_x, b_rg_x, lru_lambda,
             w_lru_out, attn_sink, w_attn_out, w_o, b_o, ln1_g, ln1_b, w_up, b_up, w_ffn_conv, b_ffn_conv,
             w_down, b_down, ln2_g, ln2_b):
    C = D_RNN
    o_g, o_q, o_k, o_gl, o_ga = C, 2 * C, 2 * C + Q_DIM, 2 * C + Q_DIM + 2 * KV_DIM, 2 * C + Q_DIM + 2 * KV_DIM + D_MODEL
    row = lambda a: a.reshape(1, -1).astype(F32)
    scale = HEAD_DIM ** -0.5
    nt = D_FF // FFN_COLS
    pm = _perm_matrix()

    q_off = jnp.arange(BLOCK)[:, None]
    c_off = jnp.arange(3 * BLOCK)[None, :]
    rel = c_off - BLOCK - q_off
    bias_idx = jnp.where(jnp.abs(rel) <= WINDOW, _t5_bucket(rel), -1).astype(jnp.int32)

    def ffn_cols(a):
        lead = a.shape[:-1]
        a = a.reshape(lead + (2, nt, FFN_COLS))
        return jnp.moveaxis(a, (-3, -2), (0, 1))

    return {
        "pm": jnp.asarray(pm, BF16), "pmt": jnp.asarray(pm.T, BF16),
        "w_u": w_in[:, 0:o_g].astype(BF16), "b_u": row(b_in[0:o_g]),
        "w_gl": w_in[:, o_g:o_q].astype(BF16), "b_gl": row(b_in[o_g:o_q]),
        "w_q": (w_in[:, o_q:o_k] * scale).astype(BF16), "b_q": row(b_in[o_q:o_k] * scale),
        "w_kv": w_in[:, o_k:o_gl].astype(BF16), "b_kv": row(b_in[o_k:o_gl]),
        "w_mg_lru": (0.5 * w_in[:, o_gl:o_ga]).astype(BF16), "b_mg_lru": row(0.5 * b_in[o_gl:o_ga]),
        "w_mg_attn": (0.5 * w_in[:, o_ga:]).astype(BF16), "b_mg_attn": row(0.5 * b_in[o_ga:]),
        "w_conv": w_lru_conv.astype(F32), "b_conv": row(b_lru_conv),
        "wg_f": _pack_gate_weights(0.5 * w_rg_a[0], 0.5 * w_rg_x[0]),
        "bg_f": row(0.5 * jnp.concatenate([b_rg_a[0].reshape(-1), b_rg_x[0].reshape(-1)])),
        "wg_b": _pack_gate_weights(0.5 * w_rg_a[1], 0.5 * w_rg_x[1]),
        "bg_b": row(0.5 * jnp.concatenate([b_rg_a[1].reshape(-1), b_rg_x[1].reshape(-1)])),
        "lam_f": row(lru_lambda[0]), "lam_b": row(lru_lambda[1]),
        "w_lru_out": w_lru_out.astype(BF16),
        "bias_idx": bias_idx, "rel_table": rel_table.astype(F32), "sink": attn_sink.astype(F32),
        "w_attn_out": w_attn_out.astype(BF16),
        "w_o": w_o.astype(BF16), "b_o": row(b_o),
        "ln1_g": row(ln1_g), "ln1_b": row(ln1_b),
        "w_up": ffn_cols(w_up).astype(BF16), "b_up": ffn_cols(b_up.reshape(1, -1)).astype(F32),
        "w_ffn_conv": ffn_cols(w_ffn_conv).astype(F32), "b_ffn_conv": ffn_cols(b_ffn_conv.reshape(1, -1)).astype(F32),
        "w_down": w_down.astype(BF16), "b_down": row(b_down),
        "ln2_g": row(ln2_g), "ln2_b": row(ln2_b),
    }


def _encoder_layer(x, p, alpha):
    h_b, u, k, v = _lru_bwd_call(x, p)
    m_lru = _lru_fwd_call(x, u, h_b, p)
    x1 = _attn_call(x, m_lru, k, v, p, alpha)
    return _ffn_call(x1, p, alpha)


def kernel(x_prompt, x_sample, rel_table, w_in, b_in, w_lru_conv, b_lru_conv, w_rg_a, b_rg_a, w_rg_x, b_rg_x, lru_lambda, w_lru_out, attn_sink, w_attn_out, w_o, b_o, ln1_g, ln1_b, w_up, b_up, w_ffn_conv, b_ffn_conv, w_down, b_down, ln2_g, ln2_b):
    depth = w_in.shape[0]
    alpha = (2.0 * depth) ** 0.25
    y_prompt, y_sample = x_prompt, x_sample
    for l in range(depth):
        p = _prepare(rel_table, w_in[l], b_in[l], w_lru_conv[l], b_lru_conv[l], w_rg_a[l], b_rg_a[l],
                     w_rg_x[l], b_rg_x[l], lru_lambda[l], w_lru_out[l], attn_sink[l], w_attn_out[l], w_o[l],
                     b_o[l], ln1_g[l], ln1_b[l], w_up[l], b_up[l], w_ffn_conv[l], b_ffn_conv[l], w_down[l],
                     b_down[l], ln2_g[l], ln2_b[l])
        y_prompt = _encoder_layer(y_prompt, p, alpha)
        y_sample = _encoder_layer(y_sample, p, alpha)
    return (y_prompt, y_sample)
```

```python
import functools
import math

import numpy as np
import jax
import jax.numpy as jnp
from jax import lax
from jax.experimental import pallas as pl
from jax.experimental.pallas import tpu as pltpu

F32 = jnp.float32
BF16 = jnp.bfloat16

D_MODEL = 1024
D_RNN = 1280
N_LRU_BLOCKS = 8
LRU_BLOCK = D_RNN // N_LRU_BLOCKS
LRU_C = 8.0
LRU_CONV_W = 4
HEAD_DIM = 128
N_HEADS = 8
N_KV_HEADS = 2
GROUP = N_HEADS // N_KV_HEADS
WINDOW = 128
BLOCK = 128
N_BUCKETS = 32
MAX_DISTANCE = 128
Q_DIM = N_HEADS * HEAD_DIM
KV_DIM = N_KV_HEADS * HEAD_DIM
D_FF = 3 * D_MODEL
LN_EPS = 1e-5
NEG_INF = -1e30
F32_TINY = float(np.finfo(np.float32).tiny)

LANES = 128
SUBLANES = 8
MXU_DIM = 256

N_STREAMS = SUBLANES
T_SEG = 64
PERM_T = MXU_DIM // N_STREAMS
HALO_ROWS = 4 * N_STREAMS
ROWS_ATTN = 512
ATTN_COLS = 256
ROWS_FFN = 1024
FFN_COLS = 512
FFN_DOWN_SPLIT = 3
FFN_HALO = 16
LRU_TILE = 64
LRU_COLS = 256
SCAN_UNROLL = 8
VMEM_LIMIT = 56 * 1024 * 1024
VMEM_LIMIT_FFN = 60 * 1024 * 1024

GATE_OFF = tuple(LANES * ((LRU_BLOCK * i) // LANES) for i in range(N_LRU_BLOCKS))


def _sigmoid_of_twice(z):
    return 0.5 * jnp.tanh(z) + 0.5


def _gelu(x):
    c = math.sqrt(2.0 / math.pi)
    inner = x * (c + (c * 0.044715) * (x * x))
    return x * (0.5 * jnp.tanh(inner) + 0.5)


def _layer_norm(y, g, b):
    mu = jnp.mean(y, axis=-1, keepdims=True)
    yc = y - mu
    var = jnp.mean(yc * yc, axis=-1, keepdims=True)
    return yc * lax.rsqrt(var + LN_EPS) * g + b


def _dot(a, b):
    return jnp.dot(a, b, preferred_element_type=F32)


def _const_spec(shape):
    nd = len(shape)
    return pl.BlockSpec(shape, lambda *_: (0,) * nd, pipeline_mode=pl.Buffered(1))


def _mxu_cycles(m, k, n):
    return (m // SUBLANES) * (n // LANES) * (-(-k // MXU_DIM))


def _vpu_cycles(rows, cols, ops_per_vreg):
    return rows * cols // (SUBLANES * LANES) * ops_per_vreg // 4


def _interleave(*streams):
    queues = [list(q) for q in streams]
    mxu = vpu = 0
    while any(queues):
        q = min((q for q in queues if q), key=lambda q: abs((mxu + q[0][1]) - (vpu + q[0][2])))
        fn, m, v = q.pop(0)
        fn()
        mxu += m
        vpu += v


def _permute_items(x4_ref, pm_ref, xb_dst, lhs_dst, row0, *, nb, ns):
    T = T_SEG
    R = N_STREAMS * T

    def copy():
        for b in range(nb):
            for sg in range(ns):
                i = b * ns + sg
                xb_dst[i * T:(i + 1) * T, :] = x4_ref[b, sg].astype(BF16)

    def permute(tb):
        src = jnp.concatenate(
            [xb_dst[i * T + PERM_T * tb:i * T + PERM_T * (tb + 1), :] for i in range(N_STREAMS)], axis=0)
        lhs_dst[row0 + MXU_DIM * tb:row0 + MXU_DIM * (tb + 1), :] = _dot(pm_ref[...], src).astype(BF16)

    items = [(copy, 0, _vpu_cycles(R, D_MODEL, 2))]
    for tb in range(T // PERM_T):
        items.append((functools.partial(permute, tb), _mxu_cycles(MXU_DIM, MXU_DIM, D_MODEL),
                      _vpu_cycles(MXU_DIM, D_MODEL, 1)))
    return items


def _input_items(x4_ref, xprev_ref, xnext_ref, hsel_ref, pm_ref, wu_ref, bu_ref, wc_ref, bc_ref,
                 xb_dst, lhs_dst, uall_scr, u_dst, ub_dst, u_out, *, nb, ns, jj, nj):
    T = T_SEG
    R = N_STREAMS * T
    C = D_RNN
    ncol = C // LRU_COLS

    def halo():
        outside = jnp.concatenate([xprev_ref[...].reshape(nb * SUBLANES, D_MODEL),
                                   xnext_ref[...].reshape(nb * SUBLANES, D_MODEL)], axis=0).astype(BF16)
        rows = _dot(hsel_ref[...], outside)
        if ns > 1:
            seg = lax.broadcasted_iota(jnp.int32, (N_STREAMS, D_MODEL), 0) & (ns - 1)
            has_left = (seg > 0).astype(F32)
            has_right = (seg < ns - 1).astype(F32)
            m0 = 2 * N_STREAMS
            tail = lhs_dst[m0 + R - 2 * N_STREAMS:m0 + R, :].astype(F32)
            head = lhs_dst[m0:m0 + 2 * N_STREAMS, :].astype(F32)
            inside = jnp.concatenate(
                [pltpu.roll(tail[0:N_STREAMS], 1, 0) * has_left,
                 pltpu.roll(tail[N_STREAMS:], 1, 0) * has_left,
                 pltpu.roll(head[0:N_STREAMS], N_STREAMS - 1, 0) * has_right,
                 jnp.zeros((N_STREAMS, D_MODEL), F32)], axis=0)
            rows = rows + inside
        rows = rows.astype(BF16)
        lhs_dst[0:2 * N_STREAMS, :] = rows[0:2 * N_STREAMS]
        lhs_dst[2 * N_STREAMS + R:, :] = rows[2 * N_STREAMS:]

    def project(c):
        cs = slice(c * LRU_COLS, (c + 1) * LRU_COLS)
        uall_scr[:, cs] = _dot(lhs_dst[...], wu_ref[:, cs]) + bu_ref[:, cs]
        seg = lax.broadcasted_iota(jnp.int32, (N_STREAMS, LRU_COLS), 0) & (ns - 1)
        keep_prev = 1.0 - (seg == 0).astype(F32) * (jj == 0).astype(F32)
        keep_next = 1.0 - (seg == ns - 1).astype(F32) * (jj == nj - 1).astype(F32)
        prev_rows = slice(0, 2 * N_STREAMS)
        next_rows = slice(2 * N_STREAMS + R, 3 * N_STREAMS + R)
        uall_scr[prev_rows, cs] = uall_scr[prev_rows, cs] * jnp.concatenate([keep_prev, keep_prev], axis=0)
        uall_scr[next_rows, cs] = uall_scr[next_rows, cs] * keep_next

    def conv(c):
        cs = slice(c * LRU_COLS, (c + 1) * LRU_COLS)
        acc = bc_ref[:, cs] + wc_ref[0:1, cs] * uall_scr[0:R, cs]
        for k in range(1, LRU_CONV_W):
            acc = acc + wc_ref[k:k + 1, cs] * uall_scr[N_STREAMS * k:N_STREAMS * k + R, cs]
        u_dst[:, cs] = acc
        ub_dst[:, cs] = acc.astype(BF16)
        u_out[:, cs] = acc.astype(BF16)

    items = _permute_items(x4_ref, pm_ref, xb_dst, lhs_dst, 2 * N_STREAMS, nb=nb, ns=ns)
    items.append((halo, 0, 0))
    project_cost = _mxu_cycles(R + HALO_ROWS, D_MODEL, LRU_COLS)
    conv_cost = _vpu_cycles(R, LRU_COLS, 2 * LRU_CONV_W + 1)
    items.append((functools.partial(project, 0), project_cost, 0))
    for c in range(ncol):
        if c + 1 < ncol:
            items.append((functools.partial(project, c + 1), project_cost, 0))
        items.append((functools.partial(conv, c), 0, conv_cost))
    return items


def _coeff_items(ub_src, u_src, wg_ref, bg_ref, lam_ref, pre_scr):
    R = N_STREAMS * T_SEG
    C = D_RNN

    def coeff(t):
        cols = slice(t * LANES, (t + 1) * LANES)
        xcols = slice(C + t * LANES, C + (t + 1) * LANES)
        lam = lam_ref[:, cols]
        log_sig = jnp.minimum(lam, 0.0) - jnp.log1p(jnp.exp(-jnp.abs(lam)))
        hc = (0.5 * LRU_C) * log_sig
        log_a = hc * jnp.tanh(pre_scr[:, cols]) + hc
        ig = 0.5 * jnp.tanh(pre_scr[:, xcols]) + 0.5
        a = jnp.exp(log_a)
        gain = jnp.tanh(log_a) * (-1.0 - a * a)
        root = gain * lax.rsqrt(jnp.maximum(gain, F32_TINY))
        pre_scr[:, cols] = a
        pre_scr[:, xcols] = root * (ig * u_src[:, cols])

    first_touch = {}
    last_touch = {}
    for i in range(N_LRU_BLOCKS):
        for part in range(MXU_DIM // LANES):
            tile = GATE_OFF[i] // LANES + part
            first_touch.setdefault(tile, i)
            last_touch[tile] = i

    def gate(i):
        off = GATE_OFF[i]
        y = _dot(ub_src[:, off:off + MXU_DIM], wg_ref[i])
        for half in range(2):
            for part in range(MXU_DIM // LANES):
                col = half * C + off + part * LANES
                val = y[:, half * MXU_DIM + part * LANES:half * MXU_DIM + (part + 1) * LANES]
                if first_touch[off // LANES + part] == i:
                    pre_scr[:, col:col + LANES] = val + bg_ref[:, col:col + LANES]
                else:
                    pre_scr[:, col:col + LANES] = pre_scr[:, col:col + LANES] + val

    gate_cost = (_mxu_cycles(R, MXU_DIM, 2 * MXU_DIM), _vpu_cycles(R, 2 * MXU_DIM, 1))
    coeff_cost = _vpu_cycles(R, LANES, 20)
    items = [(functools.partial(gate, 0),) + gate_cost]
    for i in range(N_LRU_BLOCKS):
        if i + 1 < N_LRU_BLOCKS:
            items.append((functools.partial(gate, i + 1),) + gate_cost)
        for t in sorted(t for t, last in last_touch.items() if last == i):
            items.append((functools.partial(coeff, t), 0, coeff_cost))
    return items


def _stage_scan(pre_scr, p_scr, carry_scr, h_dst, *, ns, fwd, first):
    T = T_SEG
    C = D_RNN

    @pl.when(first)
    def _():
        carry_scr[...] = jnp.zeros_like(carry_scr)

    def scan_step(tt, carry):
        h, p = carry
        t = tt if fwd else T - 1 - tt
        r0 = pl.multiple_of(t * N_STREAMS, N_STREAMS)
        a = pre_scr[pl.ds(r0, N_STREAMS), 0:C]
        h = a * h + pre_scr[pl.ds(r0, N_STREAMS), C:2 * C]
        h_dst[pl.ds(r0, N_STREAMS), :] = h
        if ns > 1:
            p = p * a
            p_scr[pl.ds(r0, N_STREAMS), :] = p
        return h, p

    if ns == 1:
        h0 = carry_scr[...]
    else:
        h0 = jnp.zeros((N_STREAMS, C), F32)
    h_end, p_end = lax.fori_loop(0, T, scan_step, (h0, jnp.ones((N_STREAMS, C), F32)), unroll=SCAN_UNROLL)

    if ns == 1:
        carry_scr[...] = h_end
        return None

    sub = lax.broadcasted_iota(jnp.int32, (N_STREAMS, C), 0)
    seg = sub & (ns - 1)
    carry = carry_scr[...]
    head = (seg == 0) if fwd else (seg == ns - 1)
    from_prev = 1 if fwd else N_STREAMS - 1
    final = jnp.zeros((N_STREAMS, C), F32)
    cin = final
    for _ in range(ns):
        cin = jnp.where(head, carry, pltpu.roll(final, from_prev, 0))
        final = h_end + p_end * cin
    to_head = (N_STREAMS - (ns - 1)) % N_STREAMS if fwd else ns - 1
    carry_scr[...] = pltpu.roll(final, to_head, 0)
    return cin


def _add_carry(h, p, cin):
    shape3 = (h.shape[0] // N_STREAMS, N_STREAMS, h.shape[1])
    return (h.reshape(shape3) + p.reshape(shape3) * cin[None]).reshape(h.shape)


def _lru_bwd_kernel(x4_ref, xprev_ref, xnext_ref, hsel_ref, pm_ref, wu_ref, bu_ref, wc_ref, bc_ref,
                    wg_ref, bg_ref, lam_ref,
                    wkv_ref, bkv_ref,
                    h_ref, u_ref, k_ref, v_ref,
                    xb_scr, lhs_scr, uall_scr, u_scr, ub_scr, pre_scr, p_scr, carry_scr,
                    *, nb, ns, nj):
    T = T_SEG
    R = N_STREAMS * T
    s = pl.program_id(1)
    c_in = jnp.minimum(s, nj - 1)
    c_scan = jnp.clip(s - 2, 0, nj - 1)
    slot = s % 2

    @pl.when(s == 0)
    def _():
        pre_scr[...] = jnp.zeros_like(pre_scr)
        u_scr[1] = jnp.zeros((R, D_RNN), F32)
        ub_scr[1] = jnp.zeros((R, D_RNN), BF16)

    cin = _stage_scan(pre_scr, p_scr, carry_scr, h_ref, ns=ns, fwd=False, first=(c_scan == 0))
    if cin is not None:
        for it in range(R // LRU_TILE):
            rows = slice(it * LRU_TILE, (it + 1) * LRU_TILE)
            h_ref[rows, :] = _add_carry(h_ref[rows, :], p_scr[rows, :], cin)

    def kv_project():
        kv = _dot(xb_scr[...], wkv_ref[...]) + bkv_ref[...]
        for b in range(nb):
            for sg in range(ns):
                i = b * ns + sg
                k_ref[b, sg] = kv[i * T:(i + 1) * T, 0:KV_DIM].astype(BF16)
                v_ref[b, sg] = kv[i * T:(i + 1) * T, KV_DIM:2 * KV_DIM].astype(BF16)

    input_items = _input_items(x4_ref, xprev_ref, xnext_ref, hsel_ref, pm_ref, wu_ref, bu_ref, wc_ref, bc_ref,
                               xb_scr, lhs_scr, uall_scr, u_scr.at[slot], ub_scr.at[slot], u_ref,
                               nb=nb, ns=ns, jj=nj - 1 - c_in, nj=nj)
    input_items.append((kv_project, _mxu_cycles(R, D_MODEL, 2 * KV_DIM), _vpu_cycles(R, 2 * KV_DIM, 2)))
    _interleave(_coeff_items(ub_scr.at[1 - slot], u_scr.at[1 - slot], wg_ref, bg_ref, lam_ref, pre_scr),
                input_items)


def _lru_fwd_kernel(x4_ref, u_ref, hb_ref, pm_ref, pmt_ref, wg_ref, bg_ref,
                    lam_ref, wgl_ref, bgl_ref, wmg_ref, bmg_ref, wout_ref,
                    m_ref,
                    xb_scr, lhs_scr, pre_scr, p_scr, carry_scr,
                    h_scr, g_scr, hg_scr, hgn_scr,
                    *, nb, ns, nj):
    T = T_SEG
    R = N_STREAMS * T
    C = D_RNN
    ncol = C // LRU_COLS
    s = pl.program_id(1)
    c_out = jnp.clip(s - 1, 0, nj - 1)
    slot = s % 2

    @pl.when(s == 0)
    def _():
        pre_scr[...] = jnp.zeros_like(pre_scr)
        xb_scr[1] = jnp.zeros((R, D_MODEL), BF16)
        lhs_scr[1] = jnp.zeros((R, D_MODEL), BF16)

    cin = _stage_scan(pre_scr, p_scr, carry_scr, h_scr, ns=ns, fwd=True, first=(c_out == 0))

    def gate_project(c):
        cs = slice(c * LRU_COLS, (c + 1) * LRU_COLS)
        g_scr[c % 2] = _dot(lhs_scr[1 - slot], wgl_ref[:, cs]) + bgl_ref[:, cs]

    def gate_apply(c):
        cs = slice(c * LRU_COLS, (c + 1) * LRU_COLS)
        h = h_scr[:, cs]
        if cin is not None:
            h = _add_carry(h, p_scr[:, cs], cin[:, cs])
        h = h + hb_ref[:, cs]
        hg_scr[:, cs] = (h * _gelu(g_scr[c % 2])).astype(BF16)

    def unpermute(c):
        cs = slice(c * LRU_COLS, (c + 1) * LRU_COLS)
        for tb in range(T // PERM_T):
            nat = _dot(pmt_ref[...], hg_scr[MXU_DIM * tb:MXU_DIM * (tb + 1), cs]).astype(BF16)
            for i in range(N_STREAMS):
                hgn_scr[i * T + PERM_T * tb:i * T + PERM_T * (tb + 1), cs] = nat[i * PERM_T:(i + 1) * PERM_T]

    def out_project(c):
        cs = slice(c * LRU_COLS, (c + 1) * LRU_COLS)
        y_lru = _dot(hgn_scr[...], wout_ref[:, cs])
        gate = _sigmoid_of_twice(_dot(xb_scr[1 - slot], wmg_ref[:, cs]) + bmg_ref[:, cs])
        m = (gate * y_lru).astype(BF16)
        for b in range(nb):
            for sg in range(ns):
                i = b * ns + sg
                m_ref[b, sg, :, cs] = m[i * T:(i + 1) * T]

    gp_cost = _mxu_cycles(R, D_MODEL, LRU_COLS)
    out_items = [(functools.partial(gate_project, 0), gp_cost, 0)]
    for c in range(ncol):
        if c + 1 < ncol:
            out_items.append((functools.partial(gate_project, c + 1), gp_cost, 0))
        out_items.append((functools.partial(gate_apply, c), 0, _vpu_cycles(R, LRU_COLS, 14)))
        out_items.append((functools.partial(unpermute, c), (T // PERM_T) * _mxu_cycles(MXU_DIM, MXU_DIM, LRU_COLS),
                          _vpu_cycles(R, LRU_COLS, 1)))
    for c in range(D_MODEL // LRU_COLS):
        out_items.append((functools.partial(out_project, c),
                          _mxu_cycles(R, C, LRU_COLS) + _mxu_cycles(R, D_MODEL, LRU_COLS),
                          _vpu_cycles(R, LRU_COLS, 5)))

    in_items = _permute_items(x4_ref, pm_ref, xb_scr.at[slot], lhs_scr.at[slot], 0, nb=nb, ns=ns)
    in_items += _coeff_items(u_ref, u_ref, wg_ref, bg_ref, lam_ref, pre_scr)
    _interleave(out_items, in_items)


def _stream_split(batch):
    nb = math.gcd(batch, N_STREAMS)
    return nb, N_STREAMS // nb


def _halo_select(nb, ns):
    sel = np.zeros((HALO_ROWS, 2 * nb * SUBLANES), np.float32)
    for b in range(nb):
        first, last = b * ns, b * ns + ns - 1
        sel[0 * N_STREAMS + first, b * SUBLANES + SUBLANES - 2] = 1.0
        sel[1 * N_STREAMS + first, b * SUBLANES + SUBLANES - 1] = 1.0
        sel[2 * N_STREAMS + last, nb * SUBLANES + b * SUBLANES] = 1.0
    return sel


def _perm_matrix():
    pm = np.zeros((MXU_DIM, MXU_DIM), np.float32)
    for i in range(N_STREAMS):
        for t in range(PERM_T):
            pm[t * N_STREAMS + i, i * PERM_T + t] = 1.0
    return pm


def _lru_bwd_call(x, p):
    B, S, D = x.shape
    nb, ns = _stream_split(B)
    T = T_SEG
    R = N_STREAMS * T
    Q = S // T
    nj = Q // ns
    x4 = x.reshape(B, Q, T, D)
    x8 = x.reshape(B, S // SUBLANES, SUBLANES, D)
    hsel = jnp.asarray(_halo_select(nb, ns), BF16)
    C = D_RNN
    chunk_in = lambda bb, s: (bb, nj - 1 - jnp.minimum(s, nj - 1), 0, 0)
    chunk_scan = lambda bb, s: (bb, nj - 1 - jnp.clip(s - 2, 0, nj - 1), 0, 0)
    blocks_per_chunk = ns * T // SUBLANES
    before = lambda bb, s: (bb, jnp.maximum(chunk_in(bb, s)[1] * blocks_per_chunk - 1, 0), 0, 0)
    after = lambda bb, s: (bb, jnp.minimum((chunk_in(bb, s)[1] + 1) * blocks_per_chunk, S // SUBLANES - 1), 0, 0)
    in_specs = [
        pl.BlockSpec((nb, ns, T, D), chunk_in),
        pl.BlockSpec((nb, None, SUBLANES, D), before),
        pl.BlockSpec((nb, None, SUBLANES, D), after),
        _const_spec((HALO_ROWS, 2 * nb * SUBLANES)),
        _const_spec((MXU_DIM, MXU_DIM)),
        _const_spec((D, C)), _const_spec((1, C)),
        _const_spec((LRU_CONV_W, C)), _const_spec((1, C)),
        _const_spec((N_LRU_BLOCKS, MXU_DIM, 2 * MXU_DIM)), _const_spec((1, 2 * C)),
        _const_spec((1, C)),
        _const_spec((D, 2 * KV_DIM)), _const_spec((1, 2 * KV_DIM)),
    ]
    out_shape = [
        jax.ShapeDtypeStruct((B // nb, nj, R, C), F32),
        jax.ShapeDtypeStruct((B // nb, nj, R, C), BF16),
        jax.ShapeDtypeStruct((B, Q, T, KV_DIM), BF16),
        jax.ShapeDtypeStruct((B, Q, T, KV_DIM), BF16),
    ]
    out_specs = [
        pl.BlockSpec((None, None, R, C), chunk_scan),
        pl.BlockSpec((None, None, R, C), chunk_in),
        pl.BlockSpec((nb, ns, T, KV_DIM), chunk_in),
        pl.BlockSpec((nb, ns, T, KV_DIM), chunk_in),
    ]
    scratch = [
        pltpu.VMEM((R, D), BF16),
        pltpu.VMEM((R + HALO_ROWS, D), BF16),
        pltpu.VMEM((R + HALO_ROWS, C), F32),
        pltpu.VMEM((2, R, C), F32),
        pltpu.VMEM((2, R, C), BF16),
        pltpu.VMEM((R, 2 * C), F32),
        pltpu.VMEM((R, C), F32),
        pltpu.VMEM((N_STREAMS, C), F32),
    ]
    h_b, u, k, v = pl.pallas_call(
        functools.partial(_lru_bwd_kernel, nb=nb, ns=ns, nj=nj),
        grid=(B // nb, nj + 2),
        in_specs=in_specs, out_specs=out_specs, out_shape=out_shape,
        scratch_shapes=scratch,
        compiler_params=pltpu.CompilerParams(
            dimension_semantics=("arbitrary", "arbitrary"), vmem_limit_bytes=VMEM_LIMIT),
        name="lru_bwd",
    )(x4, x8, x8, hsel, p["pm"], p["w_u"], p["b_u"], p["w_conv"], p["b_conv"], p["wg_b"], p["bg_b"], p["lam_b"],
      p["w_kv"], p["b_kv"])
    return h_b, u, k.reshape(B, S, KV_DIM), v.reshape(B, S, KV_DIM)


def _lru_fwd_call(x, u, h_b, p):
    B, S, D = x.shape
    nb, ns = _stream_split(B)
    T = T_SEG
    R = N_STREAMS * T
    Q = S // T
    nj = Q // ns
    x4 = x.reshape(B, Q, T, D)
    C = D_RNN
    chunk_in = lambda bb, s: (bb, jnp.minimum(s, nj - 1), 0, 0)
    chunk_out = lambda bb, s: (bb, jnp.clip(s - 1, 0, nj - 1), 0, 0)
    in_specs = [
        pl.BlockSpec((nb, ns, T, D), chunk_in),
        pl.BlockSpec((None, None, R, C), chunk_in),
        pl.BlockSpec((None, None, R, C), chunk_out),
        _const_spec((MXU_DIM, MXU_DIM)), _const_spec((MXU_DIM, MXU_DIM)),
        _const_spec((N_LRU_BLOCKS, MXU_DIM, 2 * MXU_DIM)), _const_spec((1, 2 * C)),
        _const_spec((1, C)),
        _const_spec((D, C)), _const_spec((1, C)),
        _const_spec((D, D)), _const_spec((1, D)),
        _const_spec((C, D)),
    ]
    scratch = [
        pltpu.VMEM((2, R, D), BF16),
        pltpu.VMEM((2, R, D), BF16),
        pltpu.VMEM((R, 2 * C), F32),
        pltpu.VMEM((R, C), F32),
        pltpu.VMEM((N_STREAMS, C), F32),
        pltpu.VMEM((R, C), F32),
        pltpu.VMEM((2, R, LRU_COLS), F32),
        pltpu.VMEM((R, C), BF16),
        pltpu.VMEM((R, C), BF16),
    ]
    m = pl.pallas_call(
        functools.partial(_lru_fwd_kernel, nb=nb, ns=ns, nj=nj),
        grid=(B // nb, nj + 1),
        in_specs=in_specs,
        out_specs=pl.BlockSpec((nb, ns, T, D), chunk_out),
        out_shape=jax.ShapeDtypeStruct((B, Q, T, D), BF16),
        scratch_shapes=scratch,
        compiler_params=pltpu.CompilerParams(
            dimension_semantics=("arbitrary", "arbitrary"), vmem_limit_bytes=VMEM_LIMIT),
        name="lru_fwd",
    )(x4, u, h_b, p["pm"], p["pmt"], p["wg_f"], p["bg_f"],
      p["lam_f"], p["w_gl"], p["b_gl"], p["w_mg_lru"], p["b_mg_lru"], p["w_lru_out"])
    return m.reshape(B, S, D)


def _attn_kernel(xa_ref, xb_ref, m_ref, kp_ref, km_ref, kn_ref, vp_ref, vm_ref, vn_ref, idx_ref, rel_ref, sink_ref,
                 wq_ref, bq_ref, wga_ref, bga_ref, wao_ref, wo_ref, bo_ref, g1_ref, b1_ref,
                 o_ref, xc_scr, q_scr, s_scr, attn_scr, merged_scr, y_scr, bias_scr, *, nr, alpha):
    R = ROWS_ATTN
    nblk = R // BLOCK
    i = pl.program_id(0)
    n_chunks = pl.num_programs(0) - 1
    r = lax.rem(jnp.minimum(i, n_chunks - 1), nr)
    slot = i % 2

    @pl.when(i == 0)
    def _():
        idx = idx_ref[...]
        for h in range(N_HEADS):
            def pick(bkt, acc):
                return jnp.where(idx == bkt, rel_ref[bkt, h], acc)
            tab = lax.fori_loop(0, N_BUCKETS, pick, jnp.full((BLOCK, 3 * BLOCK), NEG_INF, F32))
            kvh, g = divmod(h, GROUP)
            bias_scr[kvh, g * BLOCK:(g + 1) * BLOCK, :] = tab
        xc_scr[1] = jnp.zeros((R, D_MODEL), BF16)
        attn_scr[1] = jnp.zeros((R, Q_DIM), BF16)

    pen_prev = (r == 0).astype(F32) * NEG_INF
    pen_next = (r == nr - 1).astype(F32) * NEG_INF
    row_head = lax.broadcasted_iota(jnp.int32, (GROUP * BLOCK, 1), 0) // BLOCK

    def cast_x():
        xc_scr[slot] = xa_ref[...].astype(BF16)

    def q_project(kvh):
        cols = slice(kvh * GROUP * HEAD_DIM, (kvh + 1) * GROUP * HEAD_DIM)
        q_scr[:, cols] = (_dot(xc_scr[slot], wq_ref[:, cols]) + bq_ref[:, cols]).astype(BF16)

    def kv_window(ref_prev, ref_main, ref_next, kvh, n):
        cols = slice(kvh * HEAD_DIM, (kvh + 1) * HEAD_DIM)
        prev = ref_main[(n - 1) * BLOCK:n * BLOCK, cols] if n > 0 else ref_prev[:, cols]
        nxt = ref_main[(n + 1) * BLOCK:(n + 2) * BLOCK, cols] if n < nblk - 1 else ref_next[:, cols]
        return jnp.concatenate([prev, ref_main[n * BLOCK:(n + 1) * BLOCK, cols], nxt], axis=0)

    def scores(kvh, n):
        q4 = jnp.concatenate(
            [q_scr[n * BLOCK:(n + 1) * BLOCK, (kvh * GROUP + g) * HEAD_DIM:(kvh * GROUP + g + 1) * HEAD_DIM]
             for g in range(GROUP)], axis=0)
        k_ext = kv_window(kp_ref, km_ref, kn_ref, kvh, n)
        s = lax.dot_general(q4, k_ext, (((1,), (1,)), ((), ())), preferred_element_type=F32)
        s = s + bias_scr[kvh]
        if n == 0:
            s = jnp.concatenate([s[:, 0:BLOCK] + pen_prev, s[:, BLOCK:]], axis=1)
        if n == nblk - 1:
            s = jnp.concatenate([s[:, 0:2 * BLOCK], s[:, 2 * BLOCK:] + pen_next], axis=1)
        s_scr[(kvh * nblk + n) % 2] = s

    def softmax_pv(kvh, n):
        sink = jnp.zeros((GROUP * BLOCK, 1), F32)
        for g in range(GROUP):
            sink = jnp.where(row_head == g, sink_ref[kvh * GROUP + g], sink)
        s = s_scr[(kvh * nblk + n) % 2]
        mx = jnp.maximum(jnp.max(s, axis=1, keepdims=True), sink)
        pr = jnp.exp(s - mx)
        denom = jnp.sum(pr, axis=1, keepdims=True) + jnp.exp(sink - mx)
        v_ext = kv_window(vp_ref, vm_ref, vn_ref, kvh, n)
        o = _dot(pr.astype(BF16), v_ext) * (1.0 / denom)
        for g in range(GROUP):
            h = kvh * GROUP + g
            attn_scr[slot, n * BLOCK:(n + 1) * BLOCK, h * HEAD_DIM:(h + 1) * HEAD_DIM] = (
                o[g * BLOCK:(g + 1) * BLOCK].astype(BF16))

    rows4 = GROUP * BLOCK
    attn_items = [(cast_x, 0, _vpu_cycles(R, D_MODEL, 1))]
    for kvh in range(N_KV_HEADS):
        attn_items.append((functools.partial(q_project, kvh), _mxu_cycles(R, D_MODEL, GROUP * HEAD_DIM),
                           _vpu_cycles(R, GROUP * HEAD_DIM, 2)))
    blocks = [(kvh, n) for kvh in range(N_KV_HEADS) for n in range(nblk)]
    score_cost = (_mxu_cycles(rows4, HEAD_DIM, 3 * BLOCK), _vpu_cycles(rows4, 3 * BLOCK, 1))
    pv_cost = (2 * _mxu_cycles(rows4, 3 * BLOCK, HEAD_DIM), _vpu_cycles(rows4, 3 * BLOCK, 8))
    attn_items.append((functools.partial(scores, *blocks[0]),) + score_cost)
    for j, blk in enumerate(blocks):
        if j + 1 < len(blocks):
            attn_items.append((functools.partial(scores, *blocks[j + 1]),) + score_cost)
        attn_items.append((functools.partial(softmax_pv, *blk),) + pv_cost)

    def merge(c):
        cs = slice(c * ATTN_COLS, (c + 1) * ATTN_COLS)
        y_attn = _dot(attn_scr[1 - slot], wao_ref[:, cs])
        gate = _sigmoid_of_twice(_dot(xc_scr[1 - slot], wga_ref[:, cs]) + bga_ref[:, cs])
        merged_scr[:, cs] = (gate * y_attn + m_ref[:, cs].astype(F32)).astype(BF16)

    def out_project(c):
        cs = slice(c * ATTN_COLS, (c + 1) * ATTN_COLS)
        y_scr[:, cs] = alpha * xb_ref[:, cs] + (_dot(merged_scr[...], wo_ref[:, cs]) + bo_ref[:, cs])

    def norm():
        o_ref[...] = _layer_norm(y_scr[...], g1_ref[...], b1_ref[...])

    ncol = D_MODEL // ATTN_COLS
    merge_items = [(functools.partial(merge, c), _mxu_cycles(R, Q_DIM, ATTN_COLS) + _mxu_cycles(R, D_MODEL, ATTN_COLS),
                    _vpu_cycles(R, ATTN_COLS, 6)) for c in range(ncol)]
    merge_items += [(functools.partial(out_project, c), _mxu_cycles(R, D_MODEL, ATTN_COLS),
                     _vpu_cycles(R, ATTN_COLS, 3)) for c in range(ncol)]
    merge_items.append((norm, 0, _vpu_cycles(R, D_MODEL, 12)))
    _interleave(attn_items, merge_items)


def _attn_call(x, m_lru, k, v, p, alpha):
    B, S, D = x.shape
    R = ROWS_ATTN
    nr = S // R
    bpr = R // BLOCK
    nkb = S // BLOCK
    n_chunks = B * nr
    cur = lambda i: jnp.minimum(i, n_chunks - 1)
    old = lambda i: jnp.maximum(i - 1, 0)
    main_cur = lambda i: (cur(i) // nr, cur(i) % nr, 0)
    main_old = lambda i: (old(i) // nr, old(i) % nr, 0)
    prev = lambda i: (cur(i) // nr, jnp.maximum((cur(i) % nr) * bpr - 1, 0), 0)
    nxt = lambda i: (cur(i) // nr, jnp.minimum((cur(i) % nr + 1) * bpr, nkb - 1), 0)
    kv_main = pl.BlockSpec((None, R, KV_DIM), main_cur)
    kv_prev = pl.BlockSpec((None, BLOCK, KV_DIM), prev)
    kv_next = pl.BlockSpec((None, BLOCK, KV_DIM), nxt)
    in_specs = [
        pl.BlockSpec((None, R, D), main_cur),
        pl.BlockSpec((None, R, D), main_old),
        pl.BlockSpec((None, R, D), main_old),
        kv_prev, kv_main, kv_next,
        kv_prev, kv_main, kv_next,
        _const_spec((BLOCK, 3 * BLOCK)),
        pl.BlockSpec(memory_space=pltpu.SMEM),
        pl.BlockSpec(memory_space=pltpu.SMEM),
        _const_spec((D, Q_DIM)), _const_spec((1, Q_DIM)),
        _const_spec((D, D)), _const_spec((1, D)),
        _const_spec((Q_DIM, D)),
        _const_spec((D, D)), _const_spec((1, D)),
        _const_spec((1, D)), _const_spec((1, D)),
    ]
    scratch = [
        pltpu.VMEM((2, R, D), BF16),
        pltpu.VMEM((R, Q_DIM), BF16),
        pltpu.VMEM((2, GROUP * BLOCK, 3 * BLOCK), F32),
        pltpu.VMEM((2, R, Q_DIM), BF16),
        pltpu.VMEM((R, D), BF16),
        pltpu.VMEM((R, D), F32),
        pltpu.VMEM((N_KV_HEADS, GROUP * BLOCK, 3 * BLOCK), F32),
    ]
    return pl.pallas_call(
        functools.partial(_attn_kernel, nr=nr, alpha=alpha),
        grid=(n_chunks + 1,),
        in_specs=in_specs,
        out_specs=pl.BlockSpec((None, R, D), main_old),
        out_shape=jax.ShapeDtypeStruct((B, S, D), F32),
        scratch_shapes=scratch,
        compiler_params=pltpu.CompilerParams(
            dimension_semantics=("arbitrary",), vmem_limit_bytes=VMEM_LIMIT),
        name="attn_merge",
    )(x, x, m_lru, k, k, k, v, v, v, p["bias_idx"], p["rel_table"], p["sink"],
      p["w_q"], p["b_q"], p["w_mg_attn"], p["b_mg_attn"], p["w_attn_out"], p["w_o"], p["b_o"],
      p["ln1_g"], p["ln1_b"])


def _ffn_kernel(x_ref, xp_ref, xn_ref, wup_ref, bup_ref, wcv_ref, bcv_ref, wdn_ref, bdn_ref, g2_ref, b2_ref,
                o_ref, xe_scr, hs_scr, act_scr, *, nr, alpha):
    R = ROWS_FFN
    H = FFN_HALO
    nt = D_FF // FFN_COLS
    nslab = FFN_COLS // LANES
    r = pl.program_id(1)
    keep_prev = (r > 0).astype(F32)
    keep_next = (r < nr - 1).astype(F32)
    zeros = jnp.zeros((H - SUBLANES, D_MODEL), F32)
    xe_scr[...] = jnp.concatenate(
        [zeros, xp_ref[...] * keep_prev, x_ref[...], xn_ref[...] * keep_next, zeros], axis=0).astype(BF16)

    row = lax.broadcasted_iota(jnp.int32, (SUBLANES, FFN_COLS), 0)
    pad_prev = (row == SUBLANES - 1).astype(F32) * (1.0 - keep_prev)
    pad_next = (row == 0).astype(F32) * (1.0 - keep_next)

    def up_proj(j):
        xe = xe_scr[...]
        for half in range(2):
            y = _dot(xe, wup_ref[half, j])
            b_up = bup_ref[half, j]
            y = jnp.concatenate(
                [y[0:H - SUBLANES], y[H - SUBLANES:H] - pad_prev * b_up, y[H:H + R],
                 y[H + R:H + R + SUBLANES] - pad_next * b_up, y[H + R + SUBLANES:]], axis=0)
            for c in range(nslab):
                hs_scr[j % 2, half, c] = y[:, c * LANES:(c + 1) * LANES]

    def conv(j, half, c):
        lanes = slice(c * LANES, (c + 1) * LANES)
        w = wcv_ref[half, j][:, lanes]
        bias = bcv_ref[half, j][:, lanes] + bup_ref[half, j][:, lanes] * (w[0:1] + w[1:2] + w[2:3])
        return (w[0:1] * hs_scr[j % 2, half, c, H - 1:H - 1 + R, :]
                + w[1:2] * hs_scr[j % 2, half, c, H:H + R, :]
                + w[2:3] * hs_scr[j % 2, half, c, H + 1:H + 1 + R, :] + bias)

    def gate(j):
        for c in range(nslab):
            act = _gelu(conv(j, 1, c)) * conv(j, 0, c)
            act_scr[:, j * FFN_COLS + c * LANES:j * FFN_COLS + (c + 1) * LANES] = act.astype(BF16)

    per = nt // FFN_DOWN_SPLIT
    up_proj(0)
    for j in range(nt):
        if j + 1 < nt:
            up_proj(j + 1)
        gate(j)
        if (j + 1) % per == 0:
            part = (j + 1) // per - 1
            ks = slice(part * per * FFN_COLS, (part + 1) * per * FFN_COLS)
            contrib = _dot(act_scr[:, ks], wdn_ref[ks, :])
            if part == 0:
                o_ref[...] = contrib + (alpha * x_ref[...] + bdn_ref[...])
            elif part < FFN_DOWN_SPLIT - 1:
                o_ref[...] = o_ref[...] + contrib
            else:
                o_ref[...] = _layer_norm(o_ref[...] + contrib, g2_ref[...], b2_ref[...])


def _ffn_call(x, p, alpha):
    B, S, D = x.shape
    R = ROWS_FFN
    nr = S // R
    rpb = R // SUBLANES
    nsb = S // SUBLANES
    nt = D_FF // FFN_COLS
    main = lambda b, r: (b, r, 0)
    prev = lambda b, r: (b, jnp.maximum(r * rpb - 1, 0), 0)
    nxt = lambda b, r: (b, jnp.minimum((r + 1) * rpb, nsb - 1), 0)
    in_specs = [
        pl.BlockSpec((None, R, D), main),
        pl.BlockSpec((None, SUBLANES, D), prev),
        pl.BlockSpec((None, SUBLANES, D), nxt),
        _const_spec((2, nt, D, FFN_COLS)), _const_spec((2, nt, 1, FFN_COLS)),
        _const_spec((2, nt, 3, FFN_COLS)), _const_spec((2, nt, 1, FFN_COLS)),
        _const_spec((D_FF, D)), _const_spec((1, D)),
        _const_spec((1, D)), _const_spec((1, D)),
    ]
    scratch = [
        pltpu.VMEM((R + 2 * FFN_HALO, D), BF16),
        pltpu.VMEM((2, 2, FFN_COLS // LANES, R + 2 * FFN_HALO, LANES), F32),
        pltpu.VMEM((R, D_FF), BF16),
    ]
    return pl.pallas_call(
        functools.partial(_ffn_kernel, nr=nr, alpha=alpha),
        grid=(B, nr),
        in_specs=in_specs,
        out_specs=pl.BlockSpec((None, R, D), main),
        out_shape=jax.ShapeDtypeStruct((B, S, D), F32),
        scratch_shapes=scratch,
        compiler_params=pltpu.CompilerParams(
            dimension_semantics=("arbitrary", "arbitrary"), vmem_limit_bytes=VMEM_LIMIT_FFN),
        name="ffn",
    )(x, x, x, p["w_up"], p["b_up"], p["w_ffn_conv"], p["b_ffn_conv"], p["w_down"], p["b_down"],
      p["ln2_g"], p["ln2_b"])


def _t5_bucket(rel):
    nb = N_BUCKETS // 2
    ret = jnp.where(rel > 0, nb, 0)
    n = jnp.abs(rel)
    max_exact = nb // 2
    nf = jnp.maximum(n, 1).astype(jnp.float32)
    large = max_exact + (jnp.log(nf / max_exact) / math.log(MAX_DISTANCE / max_exact)
                         * (nb - max_exact)).astype(jnp.int32)
    large = jnp.minimum(large, nb - 1)
    return ret + jnp.where(n < max_exact, n, large)


def _pack_gate_weights(w_a, w_x):
    sel = np.zeros((N_LRU_BLOCKS, LRU_BLOCK, MXU_DIM), np.float32)
    for i in range(N_LRU_BLOCKS):
        lo = LRU_BLOCK * i - GATE_OFF[i]
        sel[i, np.arange(LRU_BLOCK), lo + np.arange(LRU_BLOCK)] = 1.0
    place = lambda w: jnp.einsum("nik,nij,njl->nkl", sel, w, sel, precision=lax.Precision.HIGHEST)
    return jnp.concatenate([place(w_a), place(w_x)], axis=-1).astype(BF16)


def _prepare(rel_table, w_in, b_in, w_lru_conv, b_lru_conv, w_rg_a, b_rg_a, w_rg_x, b_rg_x, lru_lambda,
             w_lru_out, attn_sink, w_attn_out, w_o, b_o, ln1_g, ln1_b, w_up, b_up, w_ffn_conv, b_ffn_conv,
             w_down, b_down, ln2_g, ln2_b):
    C = D_RNN
    o_g, o_q, o_k, o_gl, o_ga = C, 2 * C, 2 * C + Q_DIM, 2 * C + Q_DIM + 2 * KV_DIM, 2 * C + Q_DIM + 2 * KV_DIM + D_MODEL
    row = lambda a: a.reshape(1, -1).astype(F32)
    scale = HEAD_DIM ** -0.5
    nt = D_FF // FFN_COLS
    pm = _perm_matrix()

    q_off = jnp.arange(BLOCK)[:, None]
    c_off = jnp.arange(3 * BLOCK)[None, :]
    rel = c_off - BLOCK - q_off
    bias_idx = jnp.where(jnp.abs(rel) <= WINDOW, _t5_bucket(rel), -1).astype(jnp.int32)

    def ffn_cols(a):
        lead = a.shape[:-1]
        a = a.reshape(lead + (2, nt, FFN_COLS))
        return jnp.moveaxis(a, (-3, -2), (0, 1))

    return {
        "pm": jnp.asarray(pm, BF16), "pmt": jnp.asarray(pm.T, BF16),
        "w_u": w_in[:, 0:o_g].astype(BF16), "b_u": row(b_in[0:o_g]),
        "w_gl": w_in[:, o_g:o_q].astype(BF16), "b_gl": row(b_in[o_g:o_q]),
        "w_q": (w_in[:, o_q:o_k] * scale).astype(BF16), "b_q": row(b_in[o_q:o_k] * scale),
        "w_kv": w_in[:, o_k:o_gl].astype(BF16), "b_kv": row(b_in[o_k:o_gl]),
        "w_mg_lru": (0.5 * w_in[:, o_gl:o_ga]).astype(BF16), "b_mg_lru": row(0.5 * b_in[o_gl:o_ga]),
        "w_mg_attn": (0.5 * w_in[:, o_ga:]).astype(BF16), "b_mg_attn": row(0.5 * b_in[o_ga:]),
        "w_conv": w_lru_conv.astype(F32), "b_conv": row(b_lru_conv),
        "wg_f": _pack_gate_weights(0.5 * w_rg_a[0], 0.5 * w_rg_x[0]),
        "bg_f": row(0.5 * jnp.concatenate([b_rg_a[0].reshape(-1), b_rg_x[0].reshape(-1)])),
        "wg_b": _pack_gate_weights(0.5 * w_rg_a[1], 0.5 * w_rg_x[1]),
        "bg_b": row(0.5 * jnp.concatenate([b_rg_a[1].reshape(-1), b_rg_x[1].reshape(-1)])),
        "lam_f": row(lru_lambda[0]), "lam_b": row(lru_lambda[1]),
        "w_lru_out": w_lru_out.astype(BF16),
        "bias_idx": bias_idx, "rel_table": rel_table.astype(F32), "sink": attn_sink.astype(F32),
        "w_attn_out": w_attn_out.astype(BF16),
        "w_o": w_o.astype(BF16), "b_o": row(b_o),
        "ln1_g": row(ln1_g), "ln1_b": row(ln1_b),
        "w_up": ffn_cols(w_up).astype(BF16), "b_up": ffn_cols(b_up.reshape(1, -1)).astype(F32),
        "w_ffn_conv": ffn_cols(w_ffn_conv).astype(F32), "b_ffn_conv": ffn_cols(b_ffn_conv.reshape(1, -1)).astype(F32),
        "w_down": w_down.astype(BF16), "b_down": row(b_down),
        "ln2_g": row(ln2_g), "ln2_b": row(ln2_b),
    }


def _encoder_layer(x, p, alpha):
    h_b, u, k, v = _lru_bwd_call(x, p)
    m_lru = _lru_fwd_call(x, u, h_b, p)
    x1 = _attn_call(x, m_lru, k, v, p, alpha)
    return _ffn_call(x1, p, alpha)


def kernel(x_prompt, x_sample, rel_table, w_in, b_in, w_lru_conv, b_lru_conv, w_rg_a, b_rg_a, w_rg_x, b_rg_x, lru_lambda, w_lru_out, attn_sink, w_attn_out, w_o, b_o, ln1_g, ln1_b, w_up, b_up, w_ffn_conv, b_ffn_conv, w_down, b_down, ln2_g, ln2_b):
    depth = w_in.shape[0]
    alpha = (2.0 * depth) ** 0.25
    y_prompt, y_sample = x_prompt, x_sample
    for l in range(depth):
        p = _prepare(rel_table, w_in[l], b_in[l], w_lru_conv[l], b_lru_conv[l], w_rg_a[l], b_rg_a[l],
                     w_rg_x[l], b_rg_x[l], lru_lambda[l], w_lru_out[l], attn_sink[l], w_attn_out[l], w_o[l],
                     b_o[l], ln1_g[l], ln1_b[l], w_up[l], b_up[l], w_ffn_conv[l], b_ffn_conv[l], w_down[l],
                     b_down[l], ln2_g[l], ln2_b[l])
        y_prompt = _encoder_layer(y_prompt, p, alpha)
        y_sample = _encoder_layer(y_sample, p, alpha)
    return (y_prompt, y_sample)
```

```python
import functools
import math

import numpy as np
import jax
import jax.numpy as jnp
from jax import lax
from jax.experimental import pallas as pl
from jax.experimental.pallas import tpu as pltpu

F32 = jnp.float32
BF16 = jnp.bfloat16

D_MODEL = 1024
D_RNN = 1280
N_LRU_BLOCKS = 8
LRU_BLOCK = D_RNN // N_LRU_BLOCKS
LRU_C = 8.0
LRU_CONV_W = 4
HEAD_DIM = 128
N_HEADS = 8
N_KV_HEADS = 2
GROUP = N_HEADS // N_KV_HEADS
WINDOW = 128
BLOCK = 128
N_BUCKETS = 32
MAX_DISTANCE = 128
Q_DIM = N_HEADS * HEAD_DIM
KV_DIM = N_KV_HEADS * HEAD_DIM
D_FF = 3 * D_MODEL
LN_EPS = 1e-5
NEG_INF = -1e30
F32_TINY = float(np.finfo(np.float32).tiny)

LANES = 128
SUBLANES = 8
MXU_DIM = 256

N_STREAMS = SUBLANES
T_SEG = 64
PERM_T = MXU_DIM // N_STREAMS
HALO_ROWS = 4 * N_STREAMS
ROWS_ATTN = 512
ATTN_COLS = 256
ROWS_FFN = 1024
FFN_COLS = 512
FFN_DOWN_SPLIT = 3
FFN_HALO = 16
LRU_TILE = 64
LRU_COLS = 256
SCAN_UNROLL = 8
VMEM_LIMIT = 56 * 1024 * 1024
VMEM_LIMIT_FFN = 60 * 1024 * 1024

GATE_OFF = tuple(LANES * ((LRU_BLOCK * i) // LANES) for i in range(N_LRU_BLOCKS))


def _sigmoid_of_twice(z):
    return 0.5 * jnp.tanh(z) + 0.5


def _gelu(x):
    c = math.sqrt(2.0 / math.pi)
    inner = x * (c + (c * 0.044715) * (x * x))
    return x * (0.5 * jnp.tanh(inner) + 0.5)


def _layer_norm(y, g, b):
    mu = jnp.mean(y, axis=-1, keepdims=True)
    yc = y - mu
    var = jnp.mean(yc * yc, axis=-1, keepdims=True)
    return yc * lax.rsqrt(var + LN_EPS) * g + b


def _dot(a, b):
    return jnp.dot(a, b, preferred_element_type=F32)


def _const_spec(shape):
    nd = len(shape)
    return pl.BlockSpec(shape, lambda *_: (0,) * nd, pipeline_mode=pl.Buffered(1))


def _mxu_cycles(m, k, n):
    return (m // SUBLANES) * (n // LANES) * (-(-k // MXU_DIM))


def _vpu_cycles(rows, cols, ops_per_vreg):
    return rows * cols // (SUBLANES * LANES) * ops_per_vreg // 4


def _interleave(*streams):
    queues = [list(q) for q in streams]
    mxu = vpu = 0
    while any(queues):
        q = min((q for q in queues if q), key=lambda q: abs((mxu + q[0][1]) - (vpu + q[0][2])))
        fn, m, v = q.pop(0)
        fn()
        mxu += m
        vpu += v


def _permute_items(x4_ref, pm_ref, xb_dst, lhs_dst, row0, *, nb, ns):
    T = T_SEG
    R = N_STREAMS * T

    def copy():
        for b in range(nb):
            for sg in range(ns):
                i = b * ns + sg
                xb_dst[i * T:(i + 1) * T, :] = x4_ref[b, sg].astype(BF16)

    def permute(tb):
        src = jnp.concatenate(
            [xb_dst[i * T + PERM_T * tb:i * T + PERM_T * (tb + 1), :] for i in range(N_STREAMS)], axis=0)
        lhs_dst[row0 + MXU_DIM * tb:row0 + MXU_DIM * (tb + 1), :] = _dot(pm_ref[...], src).astype(BF16)

    items = [(copy, 0, _vpu_cycles(R, D_MODEL, 2))]
    for tb in range(T // PERM_T):
        items.append((functools.partial(permute, tb), _mxu_cycles(MXU_DIM, MXU_DIM, D_MODEL),
                      _vpu_cycles(MXU_DIM, D_MODEL, 1)))
    return items


def _input_items(x4_ref, xprev_ref, xnext_ref, hsel_ref, pm_ref, wu_ref, bu_ref, wc_ref, bc_ref,
                 xb_dst, lhs_dst, uall_scr, u_dst, ub_dst, u_out, *, nb, ns, jj, nj):
    T = T_SEG
    R = N_STREAMS * T
    C = D_RNN
    ncol = C // LRU_COLS

    def halo():
        outside = jnp.concatenate([xprev_ref[...].reshape(nb * SUBLANES, D_MODEL),
                                   xnext_ref[...].reshape(nb * SUBLANES, D_MODEL)], axis=0).astype(BF16)
        rows = _dot(hsel_ref[...], outside)
        if ns > 1:
            seg = lax.broadcasted_iota(jnp.int32, (N_STREAMS, D_MODEL), 0) & (ns - 1)
            has_left = (seg > 0).astype(F32)
            has_right = (seg < ns - 1).astype(F32)
            m0 = 2 * N_STREAMS
            tail = lhs_dst[m0 + R - 2 * N_STREAMS:m0 + R, :].astype(F32)
            head = lhs_dst[m0:m0 + 2 * N_STREAMS, :].astype(F32)
            inside = jnp.concatenate(
                [pltpu.roll(tail[0:N_STREAMS], 1, 0) * has_left,
                 pltpu.roll(tail[N_STREAMS:], 1, 0) * has_left,
                 pltpu.roll(head[0:N_STREAMS], N_STREAMS - 1, 0) * has_right,
                 jnp.zeros((N_STREAMS, D_MODEL), F32)], axis=0)
            rows = rows + inside
        rows = rows.astype(BF16)
        lhs_dst[0:2 * N_STREAMS, :] = rows[0:2 * N_STREAMS]
        lhs_dst[2 * N_STREAMS + R:, :] = rows[2 * N_STREAMS:]

    def project(c):
        cs = slice(c * LRU_COLS, (c + 1) * LRU_COLS)
        uall_scr[:, cs] = _dot(lhs_dst[...], wu_ref[:, cs]) + bu_ref[:, cs]
        seg = lax.broadcasted_iota(jnp.int32, (N_STREAMS, LRU_COLS), 0) & (ns - 1)
        keep_prev = 1.0 - (seg == 0).astype(F32) * (jj == 0).astype(F32)
        keep_next = 1.0 - (seg == ns - 1).astype(F32) * (jj == nj - 1).astype(F32)
        prev_rows = slice(0, 2 * N_STREAMS)
        next_rows = slice(2 * N_STREAMS + R, 3 * N_STREAMS + R)
        uall_scr[prev_rows, cs] = uall_scr[prev_rows, cs] * jnp.concatenate([keep_prev, keep_prev], axis=0)
        uall_scr[next_rows, cs] = uall_scr[next_rows, cs] * keep_next

    def conv(c):
        cs = slice(c * LRU_COLS, (c + 1) * LRU_COLS)
        acc = bc_ref[:, cs] + wc_ref[0:1, cs] * uall_scr[0:R, cs]
        for k in range(1, LRU_CONV_W):
            acc = acc + wc_ref[k:k + 1, cs] * uall_scr[N_STREAMS * k:N_STREAMS * k + R, cs]
        u_dst[:, cs] = acc
        ub_dst[:, cs] = acc.astype(BF16)
        u_out[:, cs] = acc.astype(BF16)

    items = _permute_items(x4_ref, pm_ref, xb_dst, lhs_dst, 2 * N_STREAMS, nb=nb, ns=ns)
    items.append((halo, 0, 0))
    project_cost = _mxu_cycles(R + HALO_ROWS, D_MODEL, LRU_COLS)
    conv_cost = _vpu_cycles(R, LRU_COLS, 2 * LRU_CONV_W + 1)
    items.append((functools.partial(project, 0), project_cost, 0))
    for c in range(ncol):
        if c + 1 < ncol:
            items.append((functools.partial(project, c + 1), project_cost, 0))
        items.append((functools.partial(conv, c), 0, conv_cost))
    return items


def _coeff_items(ub_src, u_src, wg_ref, bg_ref, lam_ref, pre_scr):
    R = N_STREAMS * T_SEG
    C = D_RNN

    def coeff(t):
        cols = slice(t * LANES, (t + 1) * LANES)
        xcols = slice(C + t * LANES, C + (t + 1) * LANES)
        lam = lam_ref[:, cols]
        log_sig = jnp.minimum(lam, 0.0) - jnp.log1p(jnp.exp(-jnp.abs(lam)))
        hc = (0.5 * LRU_C) * log_sig
        log_a = hc * jnp.tanh(pre_scr[:, cols]) + hc
        ig = 0.5 * jnp.tanh(pre_scr[:, xcols]) + 0.5
        a = jnp.exp(log_a)
        gain = jnp.tanh(log_a) * (-1.0 - a * a)
        root = gain * lax.rsqrt(jnp.maximum(gain, F32_TINY))
        pre_scr[:, cols] = a
        pre_scr[:, xcols] = root * (ig * u_src[:, cols])

    first_touch = {}
    last_touch = {}
    for i in range(N_LRU_BLOCKS):
        for part in range(MXU_DIM // LANES):
            tile = GATE_OFF[i] // LANES + part
            first_touch.setdefault(tile, i)
            last_touch[tile] = i

    def gate(i):
        off = GATE_OFF[i]
        y = _dot(ub_src[:, off:off + MXU_DIM], wg_ref[i])
        for half in range(2):
            for part in range(MXU_DIM // LANES):
                col = half * C + off + part * LANES
                val = y[:, half * MXU_DIM + part * LANES:half * MXU_DIM + (part + 1) * LANES]
                if first_touch[off // LANES + part] == i:
                    pre_scr[:, col:col + LANES] = val + bg_ref[:, col:col + LANES]
                else:
                    pre_scr[:, col:col + LANES] = pre_scr[:, col:col + LANES] + val

    gate_cost = (_mxu_cycles(R, MXU_DIM, 2 * MXU_DIM), _vpu_cycles(R, 2 * MXU_DIM, 1))
    coeff_cost = _vpu_cycles(R, LANES, 20)
    items = [(functools.partial(gate, 0),) + gate_cost]
    for i in range(N_LRU_BLOCKS):
        if i + 1 < N_LRU_BLOCKS:
            items.append((functools.partial(gate, i + 1),) + gate_cost)
        for t in sorted(t for t, last in last_touch.items() if last == i):
            items.append((functools.partial(coeff, t), 0, coeff_cost))
    return items


def _stage_scan(pre_scr, p_scr, carry_scr, h_dst, *, ns, fwd, first):
    T = T_SEG
    C = D_RNN

    @pl.when(first)
    def _():
        carry_scr[...] = jnp.zeros_like(carry_scr)

    def scan_step(tt, carry):
        h, p = carry
        t = tt if fwd else T - 1 - tt
        r0 = pl.multiple_of(t * N_STREAMS, N_STREAMS)
        a = pre_scr[pl.ds(r0, N_STREAMS), 0:C]
        h = a * h + pre_scr[pl.ds(r0, N_STREAMS), C:2 * C]
        h_dst[pl.ds(r0, N_STREAMS), :] = h
        if ns > 1:
            p = p * a
            p_scr[pl.ds(r0, N_STREAMS), :] = p
        return h, p

    if ns == 1:
        h0 = carry_scr[...]
    else:
        h0 = jnp.zeros((N_STREAMS, C), F32)
    h_end, p_end = lax.fori_loop(0, T, scan_step, (h0, jnp.ones((N_STREAMS, C), F32)), unroll=SCAN_UNROLL)

    if ns == 1:
        carry_scr[...] = h_end
        return None

    sub = lax.broadcasted_iota(jnp.int32, (N_STREAMS, C), 0)
    seg = sub & (ns - 1)
    carry = carry_scr[...]
    head = (seg == 0) if fwd else (seg == ns - 1)
    from_prev = 1 if fwd else N_STREAMS - 1
    final = jnp.zeros((N_STREAMS, C), F32)
    cin = final
    for _ in range(ns):
        cin = jnp.where(head, carry, pltpu.roll(final, from_prev, 0))
        final = h_end + p_end * cin
    to_head = (N_STREAMS - (ns - 1)) % N_STREAMS if fwd else ns - 1
    carry_scr[...] = pltpu.roll(final, to_head, 0)
    return cin


def _add_carry(h, p, cin):
    shape3 = (h.shape[0] // N_STREAMS, N_STREAMS, h.shape[1])
    return (h.reshape(shape3) + p.reshape(shape3) * cin[None]).reshape(h.shape)


def _lru_bwd_kernel(x4_ref, xprev_ref, xnext_ref, hsel_ref, pm_ref, wu_ref, bu_ref, wc_ref, bc_ref,
                    wg_ref, bg_ref, lam_ref,
                    wkv_ref, bkv_ref,
                    h_ref, u_ref, kv_ref,
                    xb_scr, lhs_scr, uall_scr, u_scr, ub_scr, pre_scr, p_scr, carry_scr,
                    *, nb, ns, nj):
    T = T_SEG
    R = N_STREAMS * T
    s = pl.program_id(1)
    c_in = jnp.minimum(s, nj - 1)
    c_scan = jnp.clip(s - 2, 0, nj - 1)
    slot = s % 2

    @pl.when(s == 0)
    def _():
        pre_scr[...] = jnp.zeros_like(pre_scr)
        u_scr[1] = jnp.zeros((R, D_RNN), F32)
        ub_scr[1] = jnp.zeros((R, D_RNN), BF16)

    cin = _stage_scan(pre_scr, p_scr, carry_scr, h_ref, ns=ns, fwd=False, first=(c_scan == 0))
    if cin is not None:
        for it in range(R // LRU_TILE):
            rows = slice(it * LRU_TILE, (it + 1) * LRU_TILE)
            h_ref[rows, :] = _add_carry(h_ref[rows, :], p_scr[rows, :], cin)

    def kv_project():
        kv = _dot(xb_scr[...], wkv_ref[...]) + bkv_ref[...]
        for b in range(nb):
            for sg in range(ns):
                i = b * ns + sg
                kv_ref[b, sg] = kv[i * T:(i + 1) * T].astype(BF16)

    input_items = _input_items(x4_ref, xprev_ref, xnext_ref, hsel_ref, pm_ref, wu_ref, bu_ref, wc_ref, bc_ref,
                               xb_scr, lhs_scr, uall_scr, u_scr.at[slot], ub_scr.at[slot], u_ref,
                               nb=nb, ns=ns, jj=nj - 1 - c_in, nj=nj)
    input_items.append((kv_project, _mxu_cycles(R, D_MODEL, 2 * KV_DIM), _vpu_cycles(R, 2 * KV_DIM, 2)))
    _interleave(_coeff_items(ub_scr.at[1 - slot], u_scr.at[1 - slot], wg_ref, bg_ref, lam_ref, pre_scr),
                input_items)


def _lru_fwd_kernel(x4_ref, u_ref, hb_ref, pm_ref, pmt_ref, wg_ref, bg_ref,
                    lam_ref, wgl_ref, bgl_ref, wmg_ref, bmg_ref, wout_ref,
                    m_ref,
                    xb_scr, lhs_scr, pre_scr, p_scr, carry_scr,
                    h_scr, g_scr, hg_scr, hgn_scr,
                    *, nb, ns, nj):
    T = T_SEG
    R = N_STREAMS * T
    C = D_RNN
    ncol = C // LRU_COLS
    s = pl.program_id(1)
    c_out = jnp.clip(s - 1, 0, nj - 1)
    slot = s % 2

    @pl.when(s == 0)
    def _():
        pre_scr[...] = jnp.zeros_like(pre_scr)
        xb_scr[1] = jnp.zeros((R, D_MODEL), BF16)
        lhs_scr[1] = jnp.zeros((R, D_MODEL), BF16)

    cin = _stage_scan(pre_scr, p_scr, carry_scr, h_scr, ns=ns, fwd=True, first=(c_out == 0))

    def gate_project(c):
        cs = slice(c * LRU_COLS, (c + 1) * LRU_COLS)
        g_scr[c % 2] = _dot(lhs_scr[1 - slot], wgl_ref[:, cs]) + bgl_ref[:, cs]

    def gate_apply(c):
        cs = slice(c * LRU_COLS, (c + 1) * LRU_COLS)
        h = h_scr[:, cs]
        if cin is not None:
            h = _add_carry(h, p_scr[:, cs], cin[:, cs])
        h = h + hb_ref[:, cs]
        hg_scr[:, cs] = (h * _gelu(g_scr[c % 2])).astype(BF16)

    def unpermute(c):
        cs = slice(c * LRU_COLS, (c + 1) * LRU_COLS)
        for tb in range(T // PERM_T):
            nat = _dot(pmt_ref[...], hg_scr[MXU_DIM * tb:MXU_DIM * (tb + 1), cs]).astype(BF16)
            for i in range(N_STREAMS):
                hgn_scr[i * T + PERM_T * tb:i * T + PERM_T * (tb + 1), cs] = nat[i * PERM_T:(i + 1) * PERM_T]

    def out_project(c):
        cs = slice(c * LRU_COLS, (c + 1) * LRU_COLS)
        y_lru = _dot(hgn_scr[...], wout_ref[:, cs])
        gate = _sigmoid_of_twice(_dot(xb_scr[1 - slot], wmg_ref[:, cs]) + bmg_ref[:, cs])
        m = (gate * y_lru).astype(BF16)
        for b in range(nb):
            for sg in range(ns):
                i = b * ns + sg
                m_ref[b, sg, :, cs] = m[i * T:(i + 1) * T]

    gp_cost = _mxu_cycles(R, D_MODEL, LRU_COLS)
    out_items = [(functools.partial(gate_project, 0), gp_cost, 0)]
    for c in range(ncol):
        if c + 1 < ncol:
            out_items.append((functools.partial(gate_project, c + 1), gp_cost, 0))
        out_items.append((functools.partial(gate_apply, c), 0, _vpu_cycles(R, LRU_COLS, 14)))
        out_items.append((functools.partial(unpermute, c), (T // PERM_T) * _mxu_cycles(MXU_DIM, MXU_DIM, LRU_COLS),
                          _vpu_cycles(R, LRU_COLS, 1)))
    for c in range(D_MODEL // LRU_COLS):
        out_items.append((functools.partial(out_project, c),
                          _mxu_cycles(R, C, LRU_COLS) + _mxu_cycles(R, D_MODEL, LRU_COLS),
                          _vpu_cycles(R, LRU_COLS, 5)))

    in_items = _permute_items(x4_ref, pm_ref, xb_scr.at[slot], lhs_scr.at[slot], 0, nb=nb, ns=ns)
    in_items += _coeff_items(u_ref, u_ref, wg_ref, bg_ref, lam_ref, pre_scr)
    _interleave(out_items, in_items)


def _stream_split(batch):
    nb = math.gcd(batch, N_STREAMS)
    return nb, N_STREAMS // nb


def _halo_select(nb, ns):
    sel = np.zeros((HALO_ROWS, 2 * nb * SUBLANES), np.float32)
    for b in range(nb):
        first, last = b * ns, b * ns + ns - 1
        sel[0 * N_STREAMS + first, b * SUBLANES + SUBLANES - 2] = 1.0
        sel[1 * N_STREAMS + first, b * SUBLANES + SUBLANES - 1] = 1.0
        sel[2 * N_STREAMS + last, nb * SUBLANES + b * SUBLANES] = 1.0
    return sel


def _perm_matrix():
    pm = np.zeros((MXU_DIM, MXU_DIM), np.float32)
    for i in range(N_STREAMS):
        for t in range(PERM_T):
            pm[t * N_STREAMS + i, i * PERM_T + t] = 1.0
    return pm


def _lru_bwd_call(x, p):
    B, S, D = x.shape
    nb, ns = _stream_split(B)
    T = T_SEG
    R = N_STREAMS * T
    Q = S // T
    nj = Q // ns
    x4 = x.reshape(B, Q, T, D)
    x8 = x.reshape(B, S // SUBLANES, SUBLANES, D)
    hsel = jnp.asarray(_halo_select(nb, ns), BF16)
    C = D_RNN
    chunk_in = lambda bb, s: (bb, nj - 1 - jnp.minimum(s, nj - 1), 0, 0)
    chunk_scan = lambda bb, s: (bb, nj - 1 - jnp.clip(s - 2, 0, nj - 1), 0, 0)
    blocks_per_chunk = ns * T // SUBLANES
    before = lambda bb, s: (bb, jnp.maximum(chunk_in(bb, s)[1] * blocks_per_chunk - 1, 0), 0, 0)
    after = lambda bb, s: (bb, jnp.minimum((chunk_in(bb, s)[1] + 1) * blocks_per_chunk, S // SUBLANES - 1), 0, 0)
    in_specs = [
        pl.BlockSpec((nb, ns, T, D), chunk_in),
        pl.BlockSpec((nb, None, SUBLANES, D), before),
        pl.BlockSpec((nb, None, SUBLANES, D), after),
        _const_spec((HALO_ROWS, 2 * nb * SUBLANES)),
        _const_spec((MXU_DIM, MXU_DIM)),
        _const_spec((D, C)), _const_spec((1, C)),
        _const_spec((LRU_CONV_W, C)), _const_spec((1, C)),
        _const_spec((N_LRU_BLOCKS, MXU_DIM, 2 * MXU_DIM)), _const_spec((1, 2 * C)),
        _const_spec((1, C)),
        _const_spec((D, 2 * KV_DIM)), _const_spec((1, 2 * KV_DIM)),
    ]
    out_shape = [
        jax.ShapeDtypeStruct((B // nb, nj, R, C), F32),
        jax.ShapeDtypeStruct((B // nb, nj, R, C), BF16),
        jax.ShapeDtypeStruct((B, Q, T, 2 * KV_DIM), BF16),
    ]
    out_specs = [
        pl.BlockSpec((None, None, R, C), chunk_scan),
        pl.BlockSpec((None, None, R, C), chunk_in),
        pl.BlockSpec((nb, ns, T, 2 * KV_DIM), chunk_in),
    ]
    scratch = [
        pltpu.VMEM((R, D), BF16),
        pltpu.VMEM((R + HALO_ROWS, D), BF16),
        pltpu.VMEM((R + HALO_ROWS, C), F32),
        pltpu.VMEM((2, R, C), F32),
        pltpu.VMEM((2, R, C), BF16),
        pltpu.VMEM((R, 2 * C), F32),
        pltpu.VMEM((R, C), F32),
        pltpu.VMEM((N_STREAMS, C), F32),
    ]
    h_b, u, kv = pl.pallas_call(
        functools.partial(_lru_bwd_kernel, nb=nb, ns=ns, nj=nj),
        grid=(B // nb, nj + 2),
        in_specs=in_specs, out_specs=out_specs, out_shape=out_shape,
        scratch_shapes=scratch,
        compiler_params=pltpu.CompilerParams(
            dimension_semantics=("arbitrary", "arbitrary"), vmem_limit_bytes=VMEM_LIMIT),
        name="lru_bwd",
    )(x4, x8, x8, hsel, p["pm"], p["w_u"], p["b_u"], p["w_conv"], p["b_conv"], p["wg_b"], p["bg_b"], p["lam_b"],
      p["w_kv"], p["b_kv"])
    return h_b, u, kv.reshape(B, S, 2 * KV_DIM)


def _lru_fwd_call(x, u, h_b, p):
    B, S, D = x.shape
    nb, ns = _stream_split(B)
    T = T_SEG
    R = N_STREAMS * T
    Q = S // T
    nj = Q // ns
    x4 = x.reshape(B, Q, T, D)
    C = D_RNN
    chunk_in = lambda bb, s: (bb, jnp.minimum(s, nj - 1), 0, 0)
    chunk_out = lambda bb, s: (bb, jnp.clip(s - 1, 0, nj - 1), 0, 0)
    in_specs = [
        pl.BlockSpec((nb, ns, T, D), chunk_in),
        pl.BlockSpec((None, None, R, C), chunk_in),
        pl.BlockSpec((None, None, R, C), chunk_out),
        _const_spec((MXU_DIM, MXU_DIM)), _const_spec((MXU_DIM, MXU_DIM)),
        _const_spec((N_LRU_BLOCKS, MXU_DIM, 2 * MXU_DIM)), _const_spec((1, 2 * C)),
        _const_spec((1, C)),
        _const_spec((D, C)), _const_spec((1, C)),
        _const_spec((D, D)), _const_spec((1, D)),
        _const_spec((C, D)),
    ]
    scratch = [
        pltpu.VMEM((2, R, D), BF16),
        pltpu.VMEM((2, R, D), BF16),
        pltpu.VMEM((R, 2 * C), F32),
        pltpu.VMEM((R, C), F32),
        pltpu.VMEM((N_STREAMS, C), F32),
        pltpu.VMEM((R, C), F32),
        pltpu.VMEM((2, R, LRU_COLS), F32),
        pltpu.VMEM((R, C), BF16),
        pltpu.VMEM((R, C), BF16),
    ]
    m = pl.pallas_call(
        functools.partial(_lru_fwd_kernel, nb=nb, ns=ns, nj=nj),
        grid=(B // nb, nj + 1),
        in_specs=in_specs,
        out_specs=pl.BlockSpec((nb, ns, T, D), chunk_out),
        out_shape=jax.ShapeDtypeStruct((B, Q, T, D), BF16),
        scratch_shapes=scratch,
        compiler_params=pltpu.CompilerParams(
            dimension_semantics=("arbitrary", "arbitrary"), vmem_limit_bytes=VMEM_LIMIT),
        name="lru_fwd",
    )(x4, u, h_b, p["pm"], p["pmt"], p["wg_f"], p["bg_f"],
      p["lam_f"], p["w_gl"], p["b_gl"], p["w_mg_lru"], p["b_mg_lru"], p["w_lru_out"])
    return m.reshape(B, S, D)


def _attn_kernel(xa_ref, xb_ref, m_ref, kvp_ref, kvm_ref, kvn_ref, idx_ref, rel_ref, sink_ref,
                 wq_ref, bq_ref, wga_ref, bga_ref, wao_ref, wo_ref, bo_ref, g1_ref, b1_ref,
                 o_ref, xc_scr, q_scr, s_scr, attn_scr, merged_scr, y_scr, bias_scr, *, nr, alpha):
    R = ROWS_ATTN
    nblk = R // BLOCK
    i = pl.program_id(0)
    n_chunks = pl.num_programs(0) - 1
    r = lax.rem(jnp.minimum(i, n_chunks - 1), nr)
    slot = i % 2

    @pl.when(i == 0)
    def _():
        idx = idx_ref[...]
        for h in range(N_HEADS):
            def pick(bkt, acc):
                return jnp.where(idx == bkt, rel_ref[bkt, h], acc)
            tab = lax.fori_loop(0, N_BUCKETS, pick, jnp.full((BLOCK, 3 * BLOCK), NEG_INF, F32))
            kvh, g = divmod(h, GROUP)
            bias_scr[kvh, g * BLOCK:(g + 1) * BLOCK, :] = tab
        xc_scr[1] = jnp.zeros((R, D_MODEL), BF16)
        attn_scr[1] = jnp.zeros((R, Q_DIM), BF16)

    pen_prev = (r == 0).astype(F32) * NEG_INF
    pen_next = (r == nr - 1).astype(F32) * NEG_INF
    row_head = lax.broadcasted_iota(jnp.int32, (GROUP * BLOCK, 1), 0) // BLOCK

    def cast_x():
        xc_scr[slot] = xa_ref[...].astype(BF16)

    def q_project(kvh):
        cols = slice(kvh * GROUP * HEAD_DIM, (kvh + 1) * GROUP * HEAD_DIM)
        q_scr[:, cols] = (_dot(xc_scr[slot], wq_ref[:, cols]) + bq_ref[:, cols]).astype(BF16)

    def kv_window(col0, kvh, n):
        cols = slice(col0 + kvh * HEAD_DIM, col0 + (kvh + 1) * HEAD_DIM)
        prev = kvm_ref[(n - 1) * BLOCK:n * BLOCK, cols] if n > 0 else kvp_ref[:, cols]
        nxt = kvm_ref[(n + 1) * BLOCK:(n + 2) * BLOCK, cols] if n < nblk - 1 else kvn_ref[:, cols]
        return jnp.concatenate([prev, kvm_ref[n * BLOCK:(n + 1) * BLOCK, cols], nxt], axis=0)

    def scores(kvh, n):
        q4 = jnp.concatenate(
            [q_scr[n * BLOCK:(n + 1) * BLOCK, (kvh * GROUP + g) * HEAD_DIM:(kvh * GROUP + g + 1) * HEAD_DIM]
             for g in range(GROUP)], axis=0)
        k_ext = kv_window(0, kvh, n)
        s = lax.dot_general(q4, k_ext, (((1,), (1,)), ((), ())), preferred_element_type=F32)
        s = s + bias_scr[kvh]
        if n == 0:
            s = jnp.concatenate([s[:, 0:BLOCK] + pen_prev, s[:, BLOCK:]], axis=1)
        if n == nblk - 1:
            s = jnp.concatenate([s[:, 0:2 * BLOCK], s[:, 2 * BLOCK:] + pen_next], axis=1)
        s_scr[(kvh * nblk + n) % 2] = s

    def softmax_pv(kvh, n):
        sink = jnp.zeros((GROUP * BLOCK, 1), F32)
        for g in range(GROUP):
            sink = jnp.where(row_head == g, sink_ref[kvh * GROUP + g], sink)
        s = s_scr[(kvh * nblk + n) % 2]
        mx = jnp.maximum(jnp.max(s, axis=1, keepdims=True), sink)
        pr = jnp.exp(s - mx)
        denom = jnp.sum(pr, axis=1, keepdims=True) + jnp.exp(sink - mx)
        v_ext = kv_window(KV_DIM, kvh, n)
        o = _dot(pr.astype(BF16), v_ext) * (1.0 / denom)
        for g in range(GROUP):
            h = kvh * GROUP + g
            attn_scr[slot, n * BLOCK:(n + 1) * BLOCK, h * HEAD_DIM:(h + 1) * HEAD_DIM] = (
                o[g * BLOCK:(g + 1) * BLOCK].astype(BF16))

    rows4 = GROUP * BLOCK
    attn_items = [(cast_x, 0, _vpu_cycles(R, D_MODEL, 1))]
    for kvh in range(N_KV_HEADS):
        attn_items.append((functools.partial(q_project, kvh), _mxu_cycles(R, D_MODEL, GROUP * HEAD_DIM),
                           _vpu_cycles(R, GROUP * HEAD_DIM, 2)))
    blocks = [(kvh, n) for kvh in range(N_KV_HEADS) for n in range(nblk)]
    score_cost = (_mxu_cycles(rows4, HEAD_DIM, 3 * BLOCK), _vpu_cycles(rows4, 3 * BLOCK, 1))
    pv_cost = (2 * _mxu_cycles(rows4, 3 * BLOCK, HEAD_DIM), _vpu_cycles(rows4, 3 * BLOCK, 8))
    attn_items.append((functools.partial(scores, *blocks[0]),) + score_cost)
    for j, blk in enumerate(blocks):
        if j + 1 < len(blocks):
            attn_items.append((functools.partial(scores, *blocks[j + 1]),) + score_cost)
        attn_items.append((functools.partial(softmax_pv, *blk),) + pv_cost)

    def merge(c):
        cs = slice(c * ATTN_COLS, (c + 1) * ATTN_COLS)
        y_attn = _dot(attn_scr[1 - slot], wao_ref[:, cs])
        gate = _sigmoid_of_twice(_dot(xc_scr[1 - slot], wga_ref[:, cs]) + bga_ref[:, cs])
        merged_scr[:, cs] = (gate * y_attn + m_ref[:, cs].astype(F32)).astype(BF16)

    def out_project(c):
        cs = slice(c * ATTN_COLS, (c + 1) * ATTN_COLS)
        y_scr[:, cs] = alpha * xb_ref[:, cs] + (_dot(merged_scr[...], wo_ref[:, cs]) + bo_ref[:, cs])

    def norm():
        o_ref[...] = _layer_norm(y_scr[...], g1_ref[...], b1_ref[...])

    ncol = D_MODEL // ATTN_COLS
    merge_items = [(functools.partial(merge, c), _mxu_cycles(R, Q_DIM, ATTN_COLS) + _mxu_cycles(R, D_MODEL, ATTN_COLS),
                    _vpu_cycles(R, ATTN_COLS, 6)) for c in range(ncol)]
    merge_items += [(functools.partial(out_project, c), _mxu_cycles(R, D_MODEL, ATTN_COLS),
                     _vpu_cycles(R, ATTN_COLS, 3)) for c in range(ncol)]
    merge_items.append((norm, 0, _vpu_cycles(R, D_MODEL, 12)))
    _interleave(attn_items, merge_items)


def _attn_call(x, m_lru, kv, p, alpha):
    B, S, D = x.shape
    R = ROWS_ATTN
    nr = S // R
    bpr = R // BLOCK
    nkb = S // BLOCK
    n_chunks = B * nr
    cur = lambda i: jnp.minimum(i, n_chunks - 1)
    old = lambda i: jnp.maximum(i - 1, 0)
    main_cur = lambda i: (cur(i) // nr, cur(i) % nr, 0)
    main_old = lambda i: (old(i) // nr, old(i) % nr, 0)
    prev = lambda i: (cur(i) // nr, jnp.maximum((cur(i) % nr) * bpr - 1, 0), 0)
    nxt = lambda i: (cur(i) // nr, jnp.minimum((cur(i) % nr + 1) * bpr, nkb - 1), 0)
    in_specs = [
        pl.BlockSpec((None, R, D), main_cur),
        pl.BlockSpec((None, R, D), main_old),
        pl.BlockSpec((None, R, D), main_old),
        pl.BlockSpec((None, BLOCK, 2 * KV_DIM), prev),
        pl.BlockSpec((None, R, 2 * KV_DIM), main_cur),
        pl.BlockSpec((None, BLOCK, 2 * KV_DIM), nxt),
        _const_spec((BLOCK, 3 * BLOCK)),
        pl.BlockSpec(memory_space=pltpu.SMEM),
        pl.BlockSpec(memory_space=pltpu.SMEM),
        _const_spec((D, Q_DIM)), _const_spec((1, Q_DIM)),
        _const_spec((D, D)), _const_spec((1, D)),
        _const_spec((Q_DIM, D)),
        _const_spec((D, D)), _const_spec((1, D)),
        _const_spec((1, D)), _const_spec((1, D)),
    ]
    scratch = [
        pltpu.VMEM((2, R, D), BF16),
        pltpu.VMEM((R, Q_DIM), BF16),
        pltpu.VMEM((2, GROUP * BLOCK, 3 * BLOCK), F32),
        pltpu.VMEM((2, R, Q_DIM), BF16),
        pltpu.VMEM((R, D), BF16),
        pltpu.VMEM((R, D), F32),
        pltpu.VMEM((N_KV_HEADS, GROUP * BLOCK, 3 * BLOCK), F32),
    ]
    return pl.pallas_call(
        functools.partial(_attn_kernel, nr=nr, alpha=alpha),
        grid=(n_chunks + 1,),
        in_specs=in_specs,
        out_specs=pl.BlockSpec((None, R, D), main_old),
        out_shape=jax.ShapeDtypeStruct((B, S, D), F32),
        scratch_shapes=scratch,
        compiler_params=pltpu.CompilerParams(
            dimension_semantics=("arbitrary",), vmem_limit_bytes=VMEM_LIMIT),
        name="attn_merge",
    )(x, x, m_lru, kv, kv, kv, p["bias_idx"], p["rel_table"], p["sink"],
      p["w_q"], p["b_q"], p["w_mg_attn"], p["b_mg_attn"], p["w_attn_out"], p["w_o"], p["b_o"],
      p["ln1_g"], p["ln1_b"])


def _ffn_kernel(x_ref, xp_ref, xn_ref, wup_ref, bup_ref, wcv_ref, bcv_ref, wdn_ref, bdn_ref, g2_ref, b2_ref,
                o_ref, xe_scr, hs_scr, act_scr, *, nr, alpha):
    R = ROWS_FFN
    H = FFN_HALO
    nt = D_FF // FFN_COLS
    nslab = FFN_COLS // LANES
    r = pl.program_id(1)
    keep_prev = (r > 0).astype(F32)
    keep_next = (r < nr - 1).astype(F32)
    zeros = jnp.zeros((H - SUBLANES, D_MODEL), F32)
    xe_scr[...] = jnp.concatenate(
        [zeros, xp_ref[...] * keep_prev, x_ref[...], xn_ref[...] * keep_next, zeros], axis=0).astype(BF16)

    row = lax.broadcasted_iota(jnp.int32, (SUBLANES, FFN_COLS), 0)
    pad_prev = (row == SUBLANES - 1).astype(F32) * (1.0 - keep_prev)
    pad_next = (row == 0).astype(F32) * (1.0 - keep_next)

    def cols(j, half):
        return slice(half * D_FF + j * FFN_COLS, half * D_FF + (j + 1) * FFN_COLS)

    def up_proj(j):
        xe = xe_scr[...]
        for half in range(2):
            y = _dot(xe, wup_ref[:, cols(j, half)])
            b_up = bup_ref[:, cols(j, half)]
            y = jnp.concatenate(
                [y[0:H - SUBLANES], y[H - SUBLANES:H] - pad_prev * b_up, y[H:H + R],
                 y[H + R:H + R + SUBLANES] - pad_next * b_up, y[H + R + SUBLANES:]], axis=0)
            for c in range(nslab):
                hs_scr[j % 2, half, c] = y[:, c * LANES:(c + 1) * LANES]

    def conv(j, half, c):
        start = cols(j, half).start + c * LANES
        lanes = slice(start, start + LANES)
        w = wcv_ref[:, lanes]
        bias = bcv_ref[:, lanes] + bup_ref[:, lanes] * (w[0:1] + w[1:2] + w[2:3])
        return (w[0:1] * hs_scr[j % 2, half, c, H - 1:H - 1 + R, :]
                + w[1:2] * hs_scr[j % 2, half, c, H:H + R, :]
                + w[2:3] * hs_scr[j % 2, half, c, H + 1:H + 1 + R, :] + bias)

    def gate(j):
        for c in range(nslab):
            act = _gelu(conv(j, 1, c)) * conv(j, 0, c)
            act_scr[:, j * FFN_COLS + c * LANES:j * FFN_COLS + (c + 1) * LANES] = act.astype(BF16)

    per = nt // FFN_DOWN_SPLIT
    up_proj(0)
    for j in range(nt):
        if j + 1 < nt:
            up_proj(j + 1)
        gate(j)
        if (j + 1) % per == 0:
            part = (j + 1) // per - 1
            ks = slice(part * per * FFN_COLS, (part + 1) * per * FFN_COLS)
            contrib = _dot(act_scr[:, ks], wdn_ref[ks, :])
            if part == 0:
                o_ref[...] = contrib + (alpha * x_ref[...] + bdn_ref[...])
            elif part < FFN_DOWN_SPLIT - 1:
                o_ref[...] = o_ref[...] + contrib
            else:
                o_ref[...] = _layer_norm(o_ref[...] + contrib, g2_ref[...], b2_ref[...])


def _ffn_call(x, p, alpha):
    B, S, D = x.shape
    R = ROWS_FFN
    nr = S // R
    rpb = R // SUBLANES
    nsb = S // SUBLANES
    main = lambda b, r: (b, r, 0)
    prev = lambda b, r: (b, jnp.maximum(r * rpb - 1, 0), 0)
    nxt = lambda b, r: (b, jnp.minimum((r + 1) * rpb, nsb - 1), 0)
    in_specs = [
        pl.BlockSpec((None, R, D), main),
        pl.BlockSpec((None, SUBLANES, D), prev),
        pl.BlockSpec((None, SUBLANES, D), nxt),
        _const_spec((D, 2 * D_FF)), _const_spec((1, 2 * D_FF)),
        _const_spec((3, 2 * D_FF)), _const_spec((1, 2 * D_FF)),
        _const_spec((D_FF, D)), _const_spec((1, D)),
        _const_spec((1, D)), _const_spec((1, D)),
    ]
    scratch = [
        pltpu.VMEM((R + 2 * FFN_HALO, D), BF16),
        pltpu.VMEM((2, 2, FFN_COLS // LANES, R + 2 * FFN_HALO, LANES), F32),
        pltpu.VMEM((R, D_FF), BF16),
    ]
    return pl.pallas_call(
        functools.partial(_ffn_kernel, nr=nr, alpha=alpha),
        grid=(B, nr),
        in_specs=in_specs,
        out_specs=pl.BlockSpec((None, R, D), main),
        out_shape=jax.ShapeDtypeStruct((B, S, D), F32),
        scratch_shapes=scratch,
        compiler_params=pltpu.CompilerParams(
            dimension_semantics=("arbitrary", "arbitrary"), vmem_limit_bytes=VMEM_LIMIT_FFN),
        name="ffn",
    )(x, x, x, p["w_up"], p["b_up"], p["w_ffn_conv"], p["b_ffn_conv"], p["w_down"], p["b_down"],
      p["ln2_g"], p["ln2_b"])


def _t5_bucket(rel):
    nb = N_BUCKETS // 2
    ret = jnp.where(rel > 0, nb, 0)
    n = jnp.abs(rel)
    max_exact = nb // 2
    nf = jnp.maximum(n, 1).astype(jnp.float32)
    large = max_exact + (jnp.log(nf / max_exact) / math.log(MAX_DISTANCE / max_exact)
                         * (nb - max_exact)).astype(jnp.int32)
    large = jnp.minimum(large, nb - 1)
    return ret + jnp.where(n < max_exact, n, large)


def _pack_gate_weights(w_a, w_x):
    sel = np.zeros((N_LRU_BLOCKS, LRU_BLOCK, MXU_DIM), np.float32)
    for i in range(N_LRU_BLOCKS):
        lo = LRU_BLOCK * i - GATE_OFF[i]
        sel[i, np.arange(LRU_BLOCK), lo + np.arange(LRU_BLOCK)] = 1.0
    place = lambda w: jnp.einsum("nik,nij,njl->nkl", sel, w, sel, precision=lax.Precision.HIGHEST)
    return jnp.concatenate([place(w_a), place(w_x)], axis=-1).astype(BF16)


def _prepare(rel_table, w_in, b_in, w_lru_conv, b_lru_conv, w_rg_a, b_rg_a, w_rg_x, b_rg_x, lru_lambda,
             w_lru_out, attn_sink, w_attn_out, w_o, b_o, ln1_g, ln1_b, w_up, b_up, w_ffn_conv, b_ffn_conv,
             w_down, b_down, ln2_g, ln2_b):
    C = D_RNN
    o_g, o_q, o_k, o_gl, o_ga = C, 2 * C, 2 * C + Q_DIM, 2 * C + Q_DIM + 2 * KV_DIM, 2 * C + Q_DIM + 2 * KV_DIM + D_MODEL
    row = lambda a: a.reshape(1, -1).astype(F32)
    scale = HEAD_DIM ** -0.5
    pm = _perm_matrix()

    q_off = jnp.arange(BLOCK)[:, None]
    c_off = jnp.arange(3 * BLOCK)[None, :]
    rel = c_off - BLOCK - q_off
    bias_idx = jnp.where(jnp.abs(rel) <= WINDOW, _t5_bucket(rel), -1).astype(jnp.int32)

    return {
        "pm": jnp.asarray(pm, BF16), "pmt": jnp.asarray(pm.T, BF16),
        "w_u": w_in[:, 0:o_g].astype(BF16), "b_u": row(b_in[0:o_g]),
        "w_gl": w_in[:, o_g:o_q].astype(BF16), "b_gl": row(b_in[o_g:o_q]),
        "w_q": (w_in[:, o_q:o_k] * scale).astype(BF16), "b_q": row(b_in[o_q:o_k] * scale),
        "w_kv": w_in[:, o_k:o_gl].astype(BF16), "b_kv": row(b_in[o_k:o_gl]),
        "w_mg_lru": (0.5 * w_in[:, o_gl:o_ga]).astype(BF16), "b_mg_lru": row(0.5 * b_in[o_gl:o_ga]),
        "w_mg_attn": (0.5 * w_in[:, o_ga:]).astype(BF16), "b_mg_attn": row(0.5 * b_in[o_ga:]),
        "w_conv": w_lru_conv.astype(F32), "b_conv": row(b_lru_conv),
        "wg_f": _pack_gate_weights(0.5 * w_rg_a[0], 0.5 * w_rg_x[0]),
        "bg_f": row(0.5 * jnp.concatenate([b_rg_a[0].reshape(-1), b_rg_x[0].reshape(-1)])),
        "wg_b": _pack_gate_weights(0.5 * w_rg_a[1], 0.5 * w_rg_x[1]),
        "bg_b": row(0.5 * jnp.concatenate([b_rg_a[1].reshape(-1), b_rg_x[1].reshape(-1)])),
        "lam_f": row(lru_lambda[0]), "lam_b": row(lru_lambda[1]),
        "w_lru_out": w_lru_out.astype(BF16),
        "bias_idx": bias_idx, "rel_table": rel_table.astype(F32), "sink": attn_sink.astype(F32),
        "w_attn_out": w_attn_out.astype(BF16),
        "w_o": w_o.astype(BF16), "b_o": row(b_o),
        "ln1_g": row(ln1_g), "ln1_b": row(ln1_b),
        "w_up": w_up.astype(BF16), "b_up": row(b_up),
        "w_ffn_conv": w_ffn_conv.astype(F32), "b_ffn_conv": row(b_ffn_conv),
        "w_down": w_down.astype(BF16), "b_down": row(b_down),
        "ln2_g": row(ln2_g), "ln2_b": row(ln2_b),
    }


def _encoder_layer(x, p, alpha):
    h_b, u, kv = _lru_bwd_call(x, p)
    m_lru = _lru_fwd_call(x, u, h_b, p)
    x1 = _attn_call(x, m_lru, kv, p, alpha)
    return _ffn_call(x1, p, alpha)


def kernel(x_prompt, x_sample, rel_table, w_in, b_in, w_lru_conv, b_lru_conv, w_rg_a, b_rg_a, w_rg_x, b_rg_x, lru_lambda, w_lru_out, attn_sink, w_attn_out, w_o, b_o, ln1_g, ln1_b, w_up, b_up, w_ffn_conv, b_ffn_conv, w_down, b_down, ln2_g, ln2_b):
    depth = w_in.shape[0]
    alpha = (2.0 * depth) ** 0.25
    y_prompt, y_sample = x_prompt, x_sample
    for l in range(depth):
        p = _prepare(rel_table, w_in[l], b_in[l], w_lru_conv[l], b_lru_conv[l], w_rg_a[l], b_rg_a[l],
                     w_rg_x[l], b_rg_x[l], lru_lambda[l], w_lru_out[l], attn_sink[l], w_attn_out[l], w_o[l],
                     b_o[l], ln1_g[l], ln1_b[l], w_up[l], b_up[l], w_ffn_conv[l], b_ffn_conv[l], w_down[l],
                     b_down[l], ln2_g[l], ln2_b[l])
        y_prompt = _encoder_layer(y_prompt, p, alpha)
        y_sample = _encoder_layer(y_sample, p, alpha)
    return (y_prompt, y_sample)
```

```python
import functools
import math

import numpy as np
import jax
import jax.numpy as jnp
from jax import lax
from jax.experimental import pallas as pl
from jax.experimental.pallas import tpu as pltpu

F32 = jnp.float32
BF16 = jnp.bfloat16

D_MODEL = 1024
D_RNN = 1280
N_LRU_BLOCKS = 8
LRU_BLOCK = D_RNN // N_LRU_BLOCKS
LRU_C = 8.0
LRU_CONV_W = 4
HEAD_DIM = 128
N_HEADS = 8
N_KV_HEADS = 2
GROUP = N_HEADS // N_KV_HEADS
WINDOW = 128
BLOCK = 128
N_BUCKETS = 32
MAX_DISTANCE = 128
Q_DIM = N_HEADS * HEAD_DIM
KV_DIM = N_KV_HEADS * HEAD_DIM
D_FF = 3 * D_MODEL
LN_EPS = 1e-5
NEG_INF = -1e30
F32_TINY = float(np.finfo(np.float32).tiny)

LANES = 128
SUBLANES = 8
MXU_DIM = 256

N_STREAMS = SUBLANES
T_SEG = 64
PERM_T = MXU_DIM // N_STREAMS
HALO_ROWS = 4 * N_STREAMS
ROWS_ATTN = 512
ATTN_COLS = 256
ROWS_FFN = 1024
FFN_COLS = 512
FFN_DOWN_SPLIT = 3
FFN_HALO = 16
LRU_TILE = 64
LRU_COLS = 256
VMEM_LIMIT = 56 * 1024 * 1024
VMEM_LIMIT_FFN = 60 * 1024 * 1024

GATE_OFF = tuple(LANES * ((LRU_BLOCK * i) // LANES) for i in range(N_LRU_BLOCKS))


def _sigmoid_of_twice(z):
    return 0.5 * jnp.tanh(z) + 0.5


def _gelu(x):
    c = math.sqrt(2.0 / math.pi)
    inner = x * (c + (c * 0.044715) * (x * x))
    return x * (0.5 * jnp.tanh(inner) + 0.5)


def _layer_norm(y, g, b):
    mu = jnp.mean(y, axis=-1, keepdims=True)
    yc = y - mu
    var = jnp.mean(yc * yc, axis=-1, keepdims=True)
    return yc * lax.rsqrt(var + LN_EPS) * g + b


def _dot(a, b):
    return jnp.dot(a, b, preferred_element_type=F32)


def _const_spec(shape):
    nd = len(shape)
    return pl.BlockSpec(shape, lambda *_: (0,) * nd, pipeline_mode=pl.Buffered(1))


def _mxu_cycles(m, k, n):
    return (m // SUBLANES) * (n // LANES) * (-(-k // MXU_DIM))


def _vpu_cycles(rows, cols, ops_per_vreg):
    return rows * cols // (SUBLANES * LANES) * ops_per_vreg // 4


def _interleave(*streams):
    queues = [list(q) for q in streams]
    mxu = vpu = 0
    while any(queues):
        q = min((q for q in queues if q), key=lambda q: abs((mxu + q[0][1]) - (vpu + q[0][2])))
        fn, m, v = q.pop(0)
        fn()
        mxu += m
        vpu += v


def _permute_items(x4_ref, pm_ref, xb_dst, lhs_dst, row0, *, nb, ns):
    T = T_SEG
    R = N_STREAMS * T

    def copy():
        for b in range(nb):
            for sg in range(ns):
                i = b * ns + sg
                xb_dst[i * T:(i + 1) * T, :] = x4_ref[b, sg].astype(BF16)

    def permute(tb):
        src = jnp.concatenate(
            [xb_dst[i * T + PERM_T * tb:i * T + PERM_T * (tb + 1), :] for i in range(N_STREAMS)], axis=0)
        lhs_dst[row0 + MXU_DIM * tb:row0 + MXU_DIM * (tb + 1), :] = _dot(pm_ref[...], src).astype(BF16)

    items = [(copy, 0, _vpu_cycles(R, D_MODEL, 2))]
    for tb in range(T // PERM_T):
        items.append((functools.partial(permute, tb), _mxu_cycles(MXU_DIM, MXU_DIM, D_MODEL),
                      _vpu_cycles(MXU_DIM, D_MODEL, 1)))
    return items


def _input_items(x4_ref, xprev_ref, xnext_ref, hsel_ref, pm_ref, wu_ref, bu_ref, wc_ref, bc_ref,
                 xb_dst, lhs_dst, uall_scr, u_dst, ub_dst, u_out, *, nb, ns, jj, nj):
    T = T_SEG
    R = N_STREAMS * T
    C = D_RNN
    ncol = C // LRU_COLS

    def halo():
        outside = jnp.concatenate([xprev_ref[...].reshape(nb * SUBLANES, D_MODEL),
                                   xnext_ref[...].reshape(nb * SUBLANES, D_MODEL)], axis=0).astype(BF16)
        rows = _dot(hsel_ref[...], outside)
        if ns > 1:
            seg = lax.broadcasted_iota(jnp.int32, (N_STREAMS, D_MODEL), 0) & (ns - 1)
            has_left = (seg > 0).astype(F32)
            has_right = (seg < ns - 1).astype(F32)
            m0 = 2 * N_STREAMS
            tail = lhs_dst[m0 + R - 2 * N_STREAMS:m0 + R, :].astype(F32)
            head = lhs_dst[m0:m0 + 2 * N_STREAMS, :].astype(F32)
            inside = jnp.concatenate(
                [pltpu.roll(tail[0:N_STREAMS], 1, 0) * has_left,
                 pltpu.roll(tail[N_STREAMS:], 1, 0) * has_left,
                 pltpu.roll(head[0:N_STREAMS], N_STREAMS - 1, 0) * has_right,
                 jnp.zeros((N_STREAMS, D_MODEL), F32)], axis=0)
            rows = rows + inside
        rows = rows.astype(BF16)
        lhs_dst[0:2 * N_STREAMS, :] = rows[0:2 * N_STREAMS]
        lhs_dst[2 * N_STREAMS + R:, :] = rows[2 * N_STREAMS:]

    def project(c):
        cs = slice(c * LRU_COLS, (c + 1) * LRU_COLS)
        uall_scr[:, cs] = _dot(lhs_dst[...], wu_ref[:, cs]) + bu_ref[:, cs]
        seg = lax.broadcasted_iota(jnp.int32, (N_STREAMS, LRU_COLS), 0) & (ns - 1)
        keep_prev = 1.0 - (seg == 0).astype(F32) * (jj == 0).astype(F32)
        keep_next = 1.0 - (seg == ns - 1).astype(F32) * (jj == nj - 1).astype(F32)
        prev_rows = slice(0, 2 * N_STREAMS)
        next_rows = slice(2 * N_STREAMS + R, 3 * N_STREAMS + R)
        uall_scr[prev_rows, cs] = uall_scr[prev_rows, cs] * jnp.concatenate([keep_prev, keep_prev], axis=0)
        uall_scr[next_rows, cs] = uall_scr[next_rows, cs] * keep_next

    def conv(c):
        cs = slice(c * LRU_COLS, (c + 1) * LRU_COLS)
        acc = bc_ref[:, cs] + wc_ref[0:1, cs] * uall_scr[0:R, cs]
        for k in range(1, LRU_CONV_W):
            acc = acc + wc_ref[k:k + 1, cs] * uall_scr[N_STREAMS * k:N_STREAMS * k + R, cs]
        u_dst[:, cs] = acc
        ub_dst[:, cs] = acc.astype(BF16)
        u_out[:, cs] = acc.astype(BF16)

    items = _permute_items(x4_ref, pm_ref, xb_dst, lhs_dst, 2 * N_STREAMS, nb=nb, ns=ns)
    items.append((halo, 0, 0))
    project_cost = _mxu_cycles(R + HALO_ROWS, D_MODEL, LRU_COLS)
    conv_cost = _vpu_cycles(R, LRU_COLS, 2 * LRU_CONV_W + 1)
    items.append((functools.partial(project, 0), project_cost, 0))
    for c in range(ncol):
        if c + 1 < ncol:
            items.append((functools.partial(project, c + 1), project_cost, 0))
        items.append((functools.partial(conv, c), 0, conv_cost))
    return items


def _coeff_items(ub_src, u_src, wg_ref, bg_ref, lam_ref, pre_scr):
    R = N_STREAMS * T_SEG
    C = D_RNN

    def coeff(t):
        cols = slice(t * LANES, (t + 1) * LANES)
        xcols = slice(C + t * LANES, C + (t + 1) * LANES)
        lam = lam_ref[:, cols]
        log_sig = jnp.minimum(lam, 0.0) - jnp.log1p(jnp.exp(-jnp.abs(lam)))
        hc = (0.5 * LRU_C) * log_sig
        log_a = hc * jnp.tanh(pre_scr[:, cols]) + hc
        ig = 0.5 * jnp.tanh(pre_scr[:, xcols]) + 0.5
        a = jnp.exp(log_a)
        gain = jnp.tanh(log_a) * (-1.0 - a * a)
        root = gain * lax.rsqrt(jnp.maximum(gain, F32_TINY))
        pre_scr[:, cols] = a
        pre_scr[:, xcols] = root * (ig * u_src[:, cols])

    first_touch = {}
    last_touch = {}
    for i in range(N_LRU_BLOCKS):
        for part in range(MXU_DIM // LANES):
            tile = GATE_OFF[i] // LANES + part
            first_touch.setdefault(tile, i)
            last_touch[tile] = i

    def gate(i):
        off = GATE_OFF[i]
        y = _dot(ub_src[:, off:off + MXU_DIM], wg_ref[i])
        for half in range(2):
            for part in range(MXU_DIM // LANES):
                col = half * C + off + part * LANES
                val = y[:, half * MXU_DIM + part * LANES:half * MXU_DIM + (part + 1) * LANES]
                if first_touch[off // LANES + part] == i:
                    pre_scr[:, col:col + LANES] = val + bg_ref[:, col:col + LANES]
                else:
                    pre_scr[:, col:col + LANES] = pre_scr[:, col:col + LANES] + val

    gate_cost = (_mxu_cycles(R, MXU_DIM, 2 * MXU_DIM), _vpu_cycles(R, 2 * MXU_DIM, 1))
    coeff_cost = _vpu_cycles(R, LANES, 20)
    items = [(functools.partial(gate, 0),) + gate_cost]
    for i in range(N_LRU_BLOCKS):
        if i + 1 < N_LRU_BLOCKS:
            items.append((functools.partial(gate, i + 1),) + gate_cost)
        for t in sorted(t for t, last in last_touch.items() if last == i):
            items.append((functools.partial(coeff, t), 0, coeff_cost))
    return items


def _stage_scan(pre_scr, p_scr, carry_scr, h_dst, *, ns, fwd, first):
    T = T_SEG
    C = D_RNN

    keep = 1.0 - first.astype(F32)
    carry = carry_scr[...] * keep

    def scan_step(t, h, p):
        rows = slice(t * N_STREAMS, (t + 1) * N_STREAMS)
        a = pre_scr[rows, 0:C]
        h = a * h + pre_scr[rows, C:2 * C]
        h_dst[rows, :] = h
        if ns > 1:
            p = p * a
            p_scr[rows, :] = p
        return h, p

    h_end = carry if ns == 1 else jnp.zeros((N_STREAMS, C), F32)
    p_end = jnp.ones((N_STREAMS, C), F32)
    for tt in range(T):
        h_end, p_end = scan_step(tt if fwd else T - 1 - tt, h_end, p_end)

    if ns == 1:
        carry_scr[...] = h_end
        return None

    sub = lax.broadcasted_iota(jnp.int32, (N_STREAMS, C), 0)
    seg = sub & (ns - 1)
    head = (seg == 0) if fwd else (seg == ns - 1)
    from_prev = 1 if fwd else N_STREAMS - 1
    final = jnp.zeros((N_STREAMS, C), F32)
    cin = final
    for _ in range(ns):
        cin = jnp.where(head, carry, pltpu.roll(final, from_prev, 0))
        final = h_end + p_end * cin
    to_head = (N_STREAMS - (ns - 1)) % N_STREAMS if fwd else ns - 1
    carry_scr[...] = pltpu.roll(final, to_head, 0)
    return cin


def _add_carry(h, p, cin):
    shape3 = (h.shape[0] // N_STREAMS, N_STREAMS, h.shape[1])
    return (h.reshape(shape3) + p.reshape(shape3) * cin[None]).reshape(h.shape)


def _lru_bwd_kernel(x4_ref, xprev_ref, xnext_ref, hsel_ref, pm_ref, wu_ref, bu_ref, wc_ref, bc_ref,
                    wg_ref, bg_ref, lam_ref,
                    wkv_ref, bkv_ref,
                    h_ref, u_ref, kv_ref,
                    xb_scr, lhs_scr, uall_scr, u_scr, ub_scr, pre_scr, p_scr, carry_scr,
                    *, nb, ns, nj):
    T = T_SEG
    R = N_STREAMS * T
    s = pl.program_id(1)
    c_in = jnp.minimum(s, nj - 1)
    c_scan = jnp.clip(s - 2, 0, nj - 1)
    slot = s % 2

    @pl.when(s == 0)
    def _():
        pre_scr[...] = jnp.zeros_like(pre_scr)
        carry_scr[...] = jnp.zeros_like(carry_scr)
        u_scr[1] = jnp.zeros((R, D_RNN), F32)
        ub_scr[1] = jnp.zeros((R, D_RNN), BF16)

    cin = _stage_scan(pre_scr, p_scr, carry_scr, h_ref, ns=ns, fwd=False, first=(c_scan == 0))
    if cin is not None:
        for it in range(R // LRU_TILE):
            rows = slice(it * LRU_TILE, (it + 1) * LRU_TILE)
            h_ref[rows, :] = _add_carry(h_ref[rows, :], p_scr[rows, :], cin)

    def kv_project():
        kv = _dot(xb_scr[...], wkv_ref[...]) + bkv_ref[...]
        for b in range(nb):
            for sg in range(ns):
                i = b * ns + sg
                kv_ref[b, sg] = kv[i * T:(i + 1) * T].astype(BF16)

    input_items = _input_items(x4_ref, xprev_ref, xnext_ref, hsel_ref, pm_ref, wu_ref, bu_ref, wc_ref, bc_ref,
                               xb_scr, lhs_scr, uall_scr, u_scr.at[slot], ub_scr.at[slot], u_ref,
                               nb=nb, ns=ns, jj=nj - 1 - c_in, nj=nj)
    input_items.append((kv_project, _mxu_cycles(R, D_MODEL, 2 * KV_DIM), _vpu_cycles(R, 2 * KV_DIM, 2)))
    _interleave(_coeff_items(ub_scr.at[1 - slot], u_scr.at[1 - slot], wg_ref, bg_ref, lam_ref, pre_scr),
                input_items)


def _lru_fwd_kernel(x4_ref, u_ref, hb_ref, pm_ref, pmt_ref, wg_ref, bg_ref,
                    lam_ref, wgl_ref, bgl_ref, wmg_ref, bmg_ref, wout_ref,
                    m_ref,
                    xb_scr, lhs_scr, pre_scr, p_scr, carry_scr,
                    h_scr, g_scr, hg_scr, hgn_scr,
                    *, nb, ns, nj):
    T = T_SEG
    R = N_STREAMS * T
    C = D_RNN
    ncol = C // LRU_COLS
    s = pl.program_id(1)
    c_out = jnp.clip(s - 1, 0, nj - 1)
    slot = s % 2

    @pl.when(s == 0)
    def _():
        pre_scr[...] = jnp.zeros_like(pre_scr)
        carry_scr[...] = jnp.zeros_like(carry_scr)
        xb_scr[1] = jnp.zeros((R, D_MODEL), BF16)
        lhs_scr[1] = jnp.zeros((R, D_MODEL), BF16)

    cin = _stage_scan(pre_scr, p_scr, carry_scr, h_scr, ns=ns, fwd=True, first=(c_out == 0))

    def gate_project(c):
        cs = slice(c * LRU_COLS, (c + 1) * LRU_COLS)
        g_scr[c % 2] = _dot(lhs_scr[1 - slot], wgl_ref[:, cs]) + bgl_ref[:, cs]

    def gate_apply(c):
        cs = slice(c * LRU_COLS, (c + 1) * LRU_COLS)
        h = h_scr[:, cs]
        if cin is not None:
            h = _add_carry(h, p_scr[:, cs], cin[:, cs])
        h = h + hb_ref[:, cs]
        hg_scr[:, cs] = (h * _gelu(g_scr[c % 2])).astype(BF16)

    def unpermute(c):
        cs = slice(c * LRU_COLS, (c + 1) * LRU_COLS)
        for tb in range(T // PERM_T):
            nat = _dot(pmt_ref[...], hg_scr[MXU_DIM * tb:MXU_DIM * (tb + 1), cs]).astype(BF16)
            for i in range(N_STREAMS):
                hgn_scr[i * T + PERM_T * tb:i * T + PERM_T * (tb + 1), cs] = nat[i * PERM_T:(i + 1) * PERM_T]

    def out_project(c):
        cs = slice(c * LRU_COLS, (c + 1) * LRU_COLS)
        y_lru = _dot(hgn_scr[...], wout_ref[:, cs])
        gate = _sigmoid_of_twice(_dot(xb_scr[1 - slot], wmg_ref[:, cs]) + bmg_ref[:, cs])
        m = (gate * y_lru).astype(BF16)
        for b in range(nb):
            for sg in range(ns):
                i = b * ns + sg
                m_ref[b, sg, :, cs] = m[i * T:(i + 1) * T]

    gp_cost = _mxu_cycles(R, D_MODEL, LRU_COLS)
    out_items = [(functools.partial(gate_project, 0), gp_cost, 0)]
    for c in range(ncol):
        if c + 1 < ncol:
            out_items.append((functools.partial(gate_project, c + 1), gp_cost, 0))
        out_items.append((functools.partial(gate_apply, c), 0, _vpu_cycles(R, LRU_COLS, 14)))
        out_items.append((functools.partial(unpermute, c), (T // PERM_T) * _mxu_cycles(MXU_DIM, MXU_DIM, LRU_COLS),
                          _vpu_cycles(R, LRU_COLS, 1)))
    for c in range(D_MODEL // LRU_COLS):
        out_items.append((functools.partial(out_project, c),
                          _mxu_cycles(R, C, LRU_COLS) + _mxu_cycles(R, D_MODEL, LRU_COLS),
                          _vpu_cycles(R, LRU_COLS, 5)))

    in_items = _permute_items(x4_ref, pm_ref, xb_scr.at[slot], lhs_scr.at[slot], 0, nb=nb, ns=ns)
    in_items += _coeff_items(u_ref, u_ref, wg_ref, bg_ref, lam_ref, pre_scr)
    _interleave(out_items, in_items)


def _stream_split(batch):
    nb = math.gcd(batch, N_STREAMS)
    return nb, N_STREAMS // nb


def _halo_select(nb, ns):
    sel = np.zeros((HALO_ROWS, 2 * nb * SUBLANES), np.float32)
    for b in range(nb):
        first, last = b * ns, b * ns + ns - 1
        sel[0 * N_STREAMS + first, b * SUBLANES + SUBLANES - 2] = 1.0
        sel[1 * N_STREAMS + first, b * SUBLANES + SUBLANES - 1] = 1.0
        sel[2 * N_STREAMS + last, nb * SUBLANES + b * SUBLANES] = 1.0
    return sel


def _perm_matrix():
    pm = np.zeros((MXU_DIM, MXU_DIM), np.float32)
    for i in range(N_STREAMS):
        for t in range(PERM_T):
            pm[t * N_STREAMS + i, i * PERM_T + t] = 1.0
    return pm


def _lru_bwd_call(x, p):
    B, S, D = x.shape
    nb, ns = _stream_split(B)
    T = T_SEG
    R = N_STREAMS * T
    Q = S // T
    nj = Q // ns
    x4 = x.reshape(B, Q, T, D)
    x8 = x.reshape(B, S // SUBLANES, SUBLANES, D)
    hsel = jnp.asarray(_halo_select(nb, ns), BF16)
    C = D_RNN
    chunk_in = lambda bb, s: (bb, nj - 1 - jnp.minimum(s, nj - 1), 0, 0)
    chunk_scan = lambda bb, s: (bb, nj - 1 - jnp.clip(s - 2, 0, nj - 1), 0, 0)
    blocks_per_chunk = ns * T // SUBLANES
    before = lambda bb, s: (bb, jnp.maximum(chunk_in(bb, s)[1] * blocks_per_chunk - 1, 0), 0, 0)
    after = lambda bb, s: (bb, jnp.minimum((chunk_in(bb, s)[1] + 1) * blocks_per_chunk, S // SUBLANES - 1), 0, 0)
    in_specs = [
        pl.BlockSpec((nb, ns, T, D), chunk_in),
        pl.BlockSpec((nb, None, SUBLANES, D), before),
        pl.BlockSpec((nb, None, SUBLANES, D), after),
        _const_spec((HALO_ROWS, 2 * nb * SUBLANES)),
        _const_spec((MXU_DIM, MXU_DIM)),
        _const_spec((D, C)), _const_spec((1, C)),
        _const_spec((LRU_CONV_W, C)), _const_spec((1, C)),
        _const_spec((N_LRU_BLOCKS, MXU_DIM, 2 * MXU_DIM)), _const_spec((1, 2 * C)),
        _const_spec((1, C)),
        _const_spec((D, 2 * KV_DIM)), _const_spec((1, 2 * KV_DIM)),
    ]
    out_shape = [
        jax.ShapeDtypeStruct((B // nb, nj, R, C), F32),
        jax.ShapeDtypeStruct((B // nb, nj, R, C), BF16),
        jax.ShapeDtypeStruct((B, Q, T, 2 * KV_DIM), BF16),
    ]
    out_specs = [
        pl.BlockSpec((None, None, R, C), chunk_scan),
        pl.BlockSpec((None, None, R, C), chunk_in),
        pl.BlockSpec((nb, ns, T, 2 * KV_DIM), chunk_in),
    ]
    scratch = [
        pltpu.VMEM((R, D), BF16),
        pltpu.VMEM((R + HALO_ROWS, D), BF16),
        pltpu.VMEM((R + HALO_ROWS, C), F32),
        pltpu.VMEM((2, R, C), F32),
        pltpu.VMEM((2, R, C), BF16),
        pltpu.VMEM((R, 2 * C), F32),
        pltpu.VMEM((R, C), F32),
        pltpu.VMEM((N_STREAMS, C), F32),
    ]
    h_b, u, kv = pl.pallas_call(
        functools.partial(_lru_bwd_kernel, nb=nb, ns=ns, nj=nj),
        grid=(B // nb, nj + 2),
        in_specs=in_specs, out_specs=out_specs, out_shape=out_shape,
        scratch_shapes=scratch,
        compiler_params=pltpu.CompilerParams(
            dimension_semantics=("arbitrary", "arbitrary"), vmem_limit_bytes=VMEM_LIMIT),
        name="lru_bwd",
    )(x4, x8, x8, hsel, p["pm"], p["w_u"], p["b_u"], p["w_conv"], p["b_conv"], p["wg_b"], p["bg_b"], p["lam_b"],
      p["w_kv"], p["b_kv"])
    return h_b, u, kv.reshape(B, S, 2 * KV_DIM)


def _lru_fwd_call(x, u, h_b, p):
    B, S, D = x.shape
    nb, ns = _stream_split(B)
    T = T_SEG
    R = N_STREAMS * T
    Q = S // T
    nj = Q // ns
    x4 = x.reshape(B, Q, T, D)
    C = D_RNN
    chunk_in = lambda bb, s: (bb, jnp.minimum(s, nj - 1), 0, 0)
    chunk_out = lambda bb, s: (bb, jnp.clip(s - 1, 0, nj - 1), 0, 0)
    in_specs = [
        pl.BlockSpec((nb, ns, T, D), chunk_in),
        pl.BlockSpec((None, None, R, C), chunk_in),
        pl.BlockSpec((None, None, R, C), chunk_out),
        _const_spec((MXU_DIM, MXU_DIM)), _const_spec((MXU_DIM, MXU_DIM)),
        _const_spec((N_LRU_BLOCKS, MXU_DIM, 2 * MXU_DIM)), _const_spec((1, 2 * C)),
        _const_spec((1, C)),
        _const_spec((D, C)), _const_spec((1, C)),
        _const_spec((D, D)), _const_spec((1, D)),
        _const_spec((C, D)),
    ]
    scratch = [
        pltpu.VMEM((2, R, D), BF16),
        pltpu.VMEM((2, R, D), BF16),
        pltpu.VMEM((R, 2 * C), F32),
        pltpu.VMEM((R, C), F32),
        pltpu.VMEM((N_STREAMS, C), F32),
        pltpu.VMEM((R, C), F32),
        pltpu.VMEM((2, R, LRU_COLS), F32),
        pltpu.VMEM((R, C), BF16),
        pltpu.VMEM((R, C), BF16),
    ]
    m = pl.pallas_call(
        functools.partial(_lru_fwd_kernel, nb=nb, ns=ns, nj=nj),
        grid=(B // nb, nj + 1),
        in_specs=in_specs,
        out_specs=pl.BlockSpec((nb, ns, T, D), chunk_out),
        out_shape=jax.ShapeDtypeStruct((B, Q, T, D), BF16),
        scratch_shapes=scratch,
        compiler_params=pltpu.CompilerParams(
            dimension_semantics=("arbitrary", "arbitrary"), vmem_limit_bytes=VMEM_LIMIT),
        name="lru_fwd",
    )(x4, u, h_b, p["pm"], p["pmt"], p["wg_f"], p["bg_f"],
      p["lam_f"], p["w_gl"], p["b_gl"], p["w_mg_lru"], p["b_mg_lru"], p["w_lru_out"])
    return m.reshape(B, S, D)


def _attn_kernel(xa_ref, xb_ref, m_ref, kvp_ref, kvm_ref, kvn_ref, idx_ref, rel_ref, sink_ref,
                 wq_ref, bq_ref, wga_ref, bga_ref, wao_ref, wo_ref, bo_ref, g1_ref, b1_ref,
                 o_ref, xc_scr, q_scr, s_scr, attn_scr, merged_scr, y_scr, bias_scr, *, nr, alpha):
    R = ROWS_ATTN
    nblk = R // BLOCK
    i = pl.program_id(0)
    n_chunks = pl.num_programs(0) - 1
    r = lax.rem(jnp.minimum(i, n_chunks - 1), nr)
    slot = i % 2

    @pl.when(i == 0)
    def _():
        idx = idx_ref[...]
        for h in range(N_HEADS):
            def pick(bkt, acc):
                return jnp.where(idx == bkt, rel_ref[bkt, h], acc)
            tab = lax.fori_loop(0, N_BUCKETS, pick, jnp.full((BLOCK, 3 * BLOCK), NEG_INF, F32))
            kvh, g = divmod(h, GROUP)
            bias_scr[kvh, g * BLOCK:(g + 1) * BLOCK, :] = tab
        xc_scr[1] = jnp.zeros((R, D_MODEL), BF16)
        attn_scr[1] = jnp.zeros((R, Q_DIM), BF16)

    pen_prev = (r == 0).astype(F32) * NEG_INF
    pen_next = (r == nr - 1).astype(F32) * NEG_INF
    row_head = lax.broadcasted_iota(jnp.int32, (GROUP * BLOCK, 1), 0) // BLOCK

    def cast_x():
        xc_scr[slot] = xa_ref[...].astype(BF16)

    def q_project(kvh):
        cols = slice(kvh * GROUP * HEAD_DIM, (kvh + 1) * GROUP * HEAD_DIM)
        q_scr[:, cols] = (_dot(xc_scr[slot], wq_ref[:, cols]) + bq_ref[:, cols]).astype(BF16)

    def kv_window(col0, kvh, n):
        cols = slice(col0 + kvh * HEAD_DIM, col0 + (kvh + 1) * HEAD_DIM)
        prev = kvm_ref[(n - 1) * BLOCK:n * BLOCK, cols] if n > 0 else kvp_ref[:, cols]
        nxt = kvm_ref[(n + 1) * BLOCK:(n + 2) * BLOCK, cols] if n < nblk - 1 else kvn_ref[:, cols]
        return jnp.concatenate([prev, kvm_ref[n * BLOCK:(n + 1) * BLOCK, cols], nxt], axis=0)

    def scores(kvh, n):
        q4 = jnp.concatenate(
            [q_scr[n * BLOCK:(n + 1) * BLOCK, (kvh * GROUP + g) * HEAD_DIM:(kvh * GROUP + g + 1) * HEAD_DIM]
             for g in range(GROUP)], axis=0)
        k_ext = kv_window(0, kvh, n)
        s = lax.dot_general(q4, k_ext, (((1,), (1,)), ((), ())), preferred_element_type=F32)
        s = s + bias_scr[kvh]
        if n == 0:
            s = jnp.concatenate([s[:, 0:BLOCK] + pen_prev, s[:, BLOCK:]], axis=1)
        if n == nblk - 1:
            s = jnp.concatenate([s[:, 0:2 * BLOCK], s[:, 2 * BLOCK:] + pen_next], axis=1)
        s_scr[(kvh * nblk + n) % 2] = s

    def softmax_pv(kvh, n):
        sink = jnp.zeros((GROUP * BLOCK, 1), F32)
        for g in range(GROUP):
            sink = jnp.where(row_head == g, sink_ref[kvh * GROUP + g], sink)
        s = s_scr[(kvh * nblk + n) % 2]
        mx = jnp.maximum(jnp.max(s, axis=1, keepdims=True), sink)
        pr = jnp.exp(s - mx)
        denom = jnp.sum(pr, axis=1, keepdims=True) + jnp.exp(sink - mx)
        v_ext = kv_window(KV_DIM, kvh, n)
        o = _dot(pr.astype(BF16), v_ext) * (1.0 / denom)
        for g in range(GROUP):
            h = kvh * GROUP + g
            attn_scr[slot, n * BLOCK:(n + 1) * BLOCK, h * HEAD_DIM:(h + 1) * HEAD_DIM] = (
                o[g * BLOCK:(g + 1) * BLOCK].astype(BF16))

    rows4 = GROUP * BLOCK
    attn_items = [(cast_x, 0, _vpu_cycles(R, D_MODEL, 1))]
    for kvh in range(N_KV_HEADS):
        attn_items.append((functools.partial(q_project, kvh), _mxu_cycles(R, D_MODEL, GROUP * HEAD_DIM),
                           _vpu_cycles(R, GROUP * HEAD_DIM, 2)))
    blocks = [(kvh, n) for kvh in range(N_KV_HEADS) for n in range(nblk)]
    score_cost = (_mxu_cycles(rows4, HEAD_DIM, 3 * BLOCK), _vpu_cycles(rows4, 3 * BLOCK, 1))
    pv_cost = (2 * _mxu_cycles(rows4, 3 * BLOCK, HEAD_DIM), _vpu_cycles(rows4, 3 * BLOCK, 8))
    attn_items.append((functools.partial(scores, *blocks[0]),) + score_cost)
    for j, blk in enumerate(blocks):
        if j + 1 < len(blocks):
            attn_items.append((functools.partial(scores, *blocks[j + 1]),) + score_cost)
        attn_items.append((functools.partial(softmax_pv, *blk),) + pv_cost)

    def merge(c):
        cs = slice(c * ATTN_COLS, (c + 1) * ATTN_COLS)
        y_attn = _dot(attn_scr[1 - slot], wao_ref[:, cs])
        gate = _sigmoid_of_twice(_dot(xc_scr[1 - slot], wga_ref[:, cs]) + bga_ref[:, cs])
        merged_scr[:, cs] = (gate * y_attn + m_ref[:, cs].astype(F32)).astype(BF16)

    def out_project(c):
        cs = slice(c * ATTN_COLS, (c + 1) * ATTN_COLS)
        y_scr[:, cs] = alpha * xb_ref[:, cs] + (_dot(merged_scr[...], wo_ref[:, cs]) + bo_ref[:, cs])

    def norm():
        o_ref[...] = _layer_norm(y_scr[...], g1_ref[...], b1_ref[...])

    ncol = D_MODEL // ATTN_COLS
    merge_items = [(functools.partial(merge, c), _mxu_cycles(R, Q_DIM, ATTN_COLS) + _mxu_cycles(R, D_MODEL, ATTN_COLS),
                    _vpu_cycles(R, ATTN_COLS, 6)) for c in range(ncol)]
    merge_items += [(functools.partial(out_project, c), _mxu_cycles(R, D_MODEL, ATTN_COLS),
                     _vpu_cycles(R, ATTN_COLS, 3)) for c in range(ncol)]
    merge_items.append((norm, 0, _vpu_cycles(R, D_MODEL, 12)))
    _interleave(attn_items, merge_items)


def _attn_call(x, m_lru, kv, p, alpha):
    B, S, D = x.shape
    R = ROWS_ATTN
    nr = S // R
    bpr = R // BLOCK
    nkb = S // BLOCK
    n_chunks = B * nr
    cur = lambda i: jnp.minimum(i, n_chunks - 1)
    old = lambda i: jnp.maximum(i - 1, 0)
    main_cur = lambda i: (cur(i) // nr, cur(i) % nr, 0)
    main_old = lambda i: (old(i) // nr, old(i) % nr, 0)
    prev = lambda i: (cur(i) // nr, jnp.maximum((cur(i) % nr) * bpr - 1, 0), 0)
    nxt = lambda i: (cur(i) // nr, jnp.minimum((cur(i) % nr + 1) * bpr, nkb - 1), 0)
    in_specs = [
        pl.BlockSpec((None, R, D), main_cur),
        pl.BlockSpec((None, R, D), main_old),
        pl.BlockSpec((None, R, D), main_old),
        pl.BlockSpec((None, BLOCK, 2 * KV_DIM), prev),
        pl.BlockSpec((None, R, 2 * KV_DIM), main_cur),
        pl.BlockSpec((None, BLOCK, 2 * KV_DIM), nxt),
        _const_spec((BLOCK, 3 * BLOCK)),
        pl.BlockSpec(memory_space=pltpu.SMEM),
        pl.BlockSpec(memory_space=pltpu.SMEM),
        _const_spec((D, Q_DIM)), _const_spec((1, Q_DIM)),
        _const_spec((D, D)), _const_spec((1, D)),
        _const_spec((Q_DIM, D)),
        _const_spec((D, D)), _const_spec((1, D)),
        _const_spec((1, D)), _const_spec((1, D)),
    ]
    scratch = [
        pltpu.VMEM((2, R, D), BF16),
        pltpu.VMEM((R, Q_DIM), BF16),
        pltpu.VMEM((2, GROUP * BLOCK, 3 * BLOCK), F32),
        pltpu.VMEM((2, R, Q_DIM), BF16),
        pltpu.VMEM((R, D), BF16),
        pltpu.VMEM((R, D), F32),
        pltpu.VMEM((N_KV_HEADS, GROUP * BLOCK, 3 * BLOCK), F32),
    ]
    return pl.pallas_call(
        functools.partial(_attn_kernel, nr=nr, alpha=alpha),
        grid=(n_chunks + 1,),
        in_specs=in_specs,
        out_specs=pl.BlockSpec((None, R, D), main_old),
        out_shape=jax.ShapeDtypeStruct((B, S, D), F32),
        scratch_shapes=scratch,
        compiler_params=pltpu.CompilerParams(
            dimension_semantics=("arbitrary",), vmem_limit_bytes=VMEM_LIMIT),
        name="attn_merge",
    )(x, x, m_lru, kv, kv, kv, p["bias_idx"], p["rel_table"], p["sink"],
      p["w_q"], p["b_q"], p["w_mg_attn"], p["b_mg_attn"], p["w_attn_out"], p["w_o"], p["b_o"],
      p["ln1_g"], p["ln1_b"])


def _ffn_kernel(x_ref, xp_ref, xn_ref, wup_ref, bup_ref, wcv_ref, bcv_ref, wdn_ref, bdn_ref, g2_ref, b2_ref,
                o_ref, xe_scr, hs_scr, act_scr, *, nr, alpha):
    R = ROWS_FFN
    H = FFN_HALO
    nt = D_FF // FFN_COLS
    nslab = FFN_COLS // LANES
    r = pl.program_id(1)
    keep_prev = (r > 0).astype(F32)
    keep_next = (r < nr - 1).astype(F32)
    zeros = jnp.zeros((H - SUBLANES, D_MODEL), F32)
    xe_scr[...] = jnp.concatenate(
        [zeros, xp_ref[...] * keep_prev, x_ref[...], xn_ref[...] * keep_next, zeros], axis=0).astype(BF16)

    row = lax.broadcasted_iota(jnp.int32, (SUBLANES, FFN_COLS), 0)
    pad_prev = (row == SUBLANES - 1).astype(F32) * (1.0 - keep_prev)
    pad_next = (row == 0).astype(F32) * (1.0 - keep_next)

    def cols(j, half):
        return slice(half * D_FF + j * FFN_COLS, half * D_FF + (j + 1) * FFN_COLS)

    def up_proj(j):
        xe = xe_scr[...]
        for half in range(2):
            y = _dot(xe, wup_ref[:, cols(j, half)])
            b_up = bup_ref[:, cols(j, half)]
            y = jnp.concatenate(
                [y[0:H - SUBLANES], y[H - SUBLANES:H] - pad_prev * b_up, y[H:H + R],
                 y[H + R:H + R + SUBLANES] - pad_next * b_up, y[H + R + SUBLANES:]], axis=0)
            for c in range(nslab):
                hs_scr[j % 2, half, c] = y[:, c * LANES:(c + 1) * LANES]

    def conv(j, half, c):
        start = cols(j, half).start + c * LANES
        lanes = slice(start, start + LANES)
        w = wcv_ref[:, lanes]
        bias = bcv_ref[:, lanes] + bup_ref[:, lanes] * (w[0:1] + w[1:2] + w[2:3])
        return (w[0:1] * hs_scr[j % 2, half, c, H - 1:H - 1 + R, :]
                + w[1:2] * hs_scr[j % 2, half, c, H:H + R, :]
                + w[2:3] * hs_scr[j % 2, half, c, H + 1:H + 1 + R, :] + bias)

    def gate(j):
        for c in range(nslab):
            act = _gelu(conv(j, 1, c)) * conv(j, 0, c)
            act_scr[:, j * FFN_COLS + c * LANES:j * FFN_COLS + (c + 1) * LANES] = act.astype(BF16)

    per = nt // FFN_DOWN_SPLIT
    up_proj(0)
    for j in range(nt):
        if j + 1 < nt:
            up_proj(j + 1)
        gate(j)
        if (j + 1) % per == 0:
            part = (j + 1) // per - 1
            ks = slice(part * per * FFN_COLS, (part + 1) * per * FFN_COLS)
            contrib = _dot(act_scr[:, ks], wdn_ref[ks, :])
            if part == 0:
                o_ref[...] = contrib + (alpha * x_ref[...] + bdn_ref[...])
            elif part < FFN_DOWN_SPLIT - 1:
                o_ref[...] = o_ref[...] + contrib
            else:
                o_ref[...] = _layer_norm(o_ref[...] + contrib, g2_ref[...], b2_ref[...])


def _ffn_call(x, p, alpha):
    B, S, D = x.shape
    R = ROWS_FFN
    nr = S // R
    rpb = R // SUBLANES
    nsb = S // SUBLANES
    main = lambda b, r: (b, r, 0)
    prev = lambda b, r: (b, jnp.maximum(r * rpb - 1, 0), 0)
    nxt = lambda b, r: (b, jnp.minimum((r + 1) * rpb, nsb - 1), 0)
    in_specs = [
        pl.BlockSpec((None, R, D), main),
        pl.BlockSpec((None, SUBLANES, D), prev),
        pl.BlockSpec((None, SUBLANES, D), nxt),
        _const_spec((D, 2 * D_FF)), _const_spec((1, 2 * D_FF)),
        _const_spec((3, 2 * D_FF)), _const_spec((1, 2 * D_FF)),
        _const_spec((D_FF, D)), _const_spec((1, D)),
        _const_spec((1, D)), _const_spec((1, D)),
    ]
    scratch = [
        pltpu.VMEM((R + 2 * FFN_HALO, D), BF16),
        pltpu.VMEM((2, 2, FFN_COLS // LANES, R + 2 * FFN_HALO, LANES), F32),
        pltpu.VMEM((R, D_FF), BF16),
    ]
    return pl.pallas_call(
        functools.partial(_ffn_kernel, nr=nr, alpha=alpha),
        grid=(B, nr),
        in_specs=in_specs,
        out_specs=pl.BlockSpec((None, R, D), main),
        out_shape=jax.ShapeDtypeStruct((B, S, D), F32),
        scratch_shapes=scratch,
        compiler_params=pltpu.CompilerParams(
            dimension_semantics=("arbitrary", "arbitrary"), vmem_limit_bytes=VMEM_LIMIT_FFN),
        name="ffn",
    )(x, x, x, p["w_up"], p["b_up"], p["w_ffn_conv"], p["b_ffn_conv"], p["w_down"], p["b_down"],
      p["ln2_g"], p["ln2_b"])


def _t5_bucket(rel):
    nb = N_BUCKETS // 2
    ret = jnp.where(rel > 0, nb, 0)
    n = jnp.abs(rel)
    max_exact = nb // 2
    nf = jnp.maximum(n, 1).astype(jnp.float32)
    large = max_exact + (jnp.log(nf / max_exact) / math.log(MAX_DISTANCE / max_exact)
                         * (nb - max_exact)).astype(jnp.int32)
    large = jnp.minimum(large, nb - 1)
    return ret + jnp.where(n < max_exact, n, large)


def _pack_gate_weights(w_a, w_x):
    sel = np.zeros((N_LRU_BLOCKS, LRU_BLOCK, MXU_DIM), np.float32)
    for i in range(N_LRU_BLOCKS):
        lo = LRU_BLOCK * i - GATE_OFF[i]
        sel[i, np.arange(LRU_BLOCK), lo + np.arange(LRU_BLOCK)] = 1.0
    place = lambda w: jnp.einsum("nik,nij,njl->nkl", sel, w, sel, precision=lax.Precision.HIGHEST)
    return jnp.concatenate([place(w_a), place(w_x)], axis=-1).astype(BF16)


def _prepare(rel_table, w_in, b_in, w_lru_conv, b_lru_conv, w_rg_a, b_rg_a, w_rg_x, b_rg_x, lru_lambda,
             w_lru_out, attn_sink, w_attn_out, w_o, b_o, ln1_g, ln1_b, w_up, b_up, w_ffn_conv, b_ffn_conv,
             w_down, b_down, ln2_g, ln2_b):
    C = D_RNN
    o_g, o_q, o_k, o_gl, o_ga = C, 2 * C, 2 * C + Q_DIM, 2 * C + Q_DIM + 2 * KV_DIM, 2 * C + Q_DIM + 2 * KV_DIM + D_MODEL
    row = lambda a: a.reshape(1, -1).astype(F32)
    scale = HEAD_DIM ** -0.5
    pm = _perm_matrix()

    q_off = jnp.arange(BLOCK)[:, None]
    c_off = jnp.arange(3 * BLOCK)[None, :]
    rel = c_off - BLOCK - q_off
    bias_idx = jnp.where(jnp.abs(rel) <= WINDOW, _t5_bucket(rel), -1).astype(jnp.int32)

    return {
        "pm": jnp.asarray(pm, BF16), "pmt": jnp.asarray(pm.T, BF16),
        "w_u": w_in[:, 0:o_g].astype(BF16), "b_u": row(b_in[0:o_g]),
        "w_gl": w_in[:, o_g:o_q].astype(BF16), "b_gl": row(b_in[o_g:o_q]),
        "w_q": (w_in[:, o_q:o_k] * scale).astype(BF16), "b_q": row(b_in[o_q:o_k] * scale),
        "w_kv": w_in[:, o_k:o_gl].astype(BF16), "b_kv": row(b_in[o_k:o_gl]),
        "w_mg_lru": (0.5 * w_in[:, o_gl:o_ga]).astype(BF16), "b_mg_lru": row(0.5 * b_in[o_gl:o_ga]),
        "w_mg_attn": (0.5 * w_in[:, o_ga:]).astype(BF16), "b_mg_attn": row(0.5 * b_in[o_ga:]),
        "w_conv": w_lru_conv.astype(F32), "b_conv": row(b_lru_conv),
        "wg_f": _pack_gate_weights(0.5 * w_rg_a[0], 0.5 * w_rg_x[0]),
        "bg_f": row(0.5 * jnp.concatenate([b_rg_a[0].reshape(-1), b_rg_x[0].reshape(-1)])),
        "wg_b": _pack_gate_weights(0.5 * w_rg_a[1], 0.5 * w_rg_x[1]),
        "bg_b": row(0.5 * jnp.concatenate([b_rg_a[1].reshape(-1), b_rg_x[1].reshape(-1)])),
        "lam_f": row(lru_lambda[0]), "lam_b": row(lru_lambda[1]),
        "w_lru_out": w_lru_out.astype(BF16),
        "bias_idx": bias_idx, "rel_table": rel_table.astype(F32), "sink": attn_sink.astype(F32),
        "w_attn_out": w_attn_out.astype(BF16),
        "w_o": w_o.astype(BF16), "b_o": row(b_o),
        "ln1_g": row(ln1_g), "ln1_b": row(ln1_b),
        "w_up": w_up.astype(BF16), "b_up": row(b_up),
        "w_ffn_conv": w_ffn_conv.astype(F32), "b_ffn_conv": row(b_ffn_conv),
        "w_down": w_down.astype(BF16), "b_down": row(b_down),
        "ln2_g": row(ln2_g), "ln2_b": row(ln2_b),
    }


def _encoder_layer(x, p, alpha):
    h_b, u, kv = _lru_bwd_call(x, p)
    m_lru = _lru_fwd_call(x, u, h_b, p)
    x1 = _attn_call(x, m_lru, kv, p, alpha)
    return _ffn_call(x1, p, alpha)


def kernel(x_prompt, x_sample, rel_table, w_in, b_in, w_lru_conv, b_lru_conv, w_rg_a, b_rg_a, w_rg_x, b_rg_x, lru_lambda, w_lru_out, attn_sink, w_attn_out, w_o, b_o, ln1_g, ln1_b, w_up, b_up, w_ffn_conv, b_ffn_conv, w_down, b_down, ln2_g, ln2_b):
    depth = w_in.shape[0]
    alpha = (2.0 * depth) ** 0.25
    y_prompt, y_sample = x_prompt, x_sample
    for l in range(depth):
        p = _prepare(rel_table, w_in[l], b_in[l], w_lru_conv[l], b_lru_conv[l], w_rg_a[l], b_rg_a[l],
                     w_rg_x[l], b_rg_x[l], lru_lambda[l], w_lru_out[l], attn_sink[l], w_attn_out[l], w_o[l],
                     b_o[l], ln1_g[l], ln1_b[l], w_up[l], b_up[l], w_ffn_conv[l], b_ffn_conv[l], w_down[l],
                     b_down[l], ln2_g[l], ln2_b[l])
        y_prompt = _encoder_layer(y_prompt, p, alpha)
        y_sample = _encoder_layer(y_sample, p, alpha)
    return (y_prompt, y_sample)
```

```python
import functools
import math

import numpy as np
import jax
import jax.numpy as jnp
from jax import lax
from jax.experimental import pallas as pl
from jax.experimental.pallas import tpu as pltpu

F32 = jnp.float32
BF16 = jnp.bfloat16

D_MODEL = 1024
D_RNN = 1280
N_LRU_BLOCKS = 8
LRU_BLOCK = D_RNN // N_LRU_BLOCKS
LRU_C = 8.0
LRU_CONV_W = 4
HEAD_DIM = 128
N_HEADS = 8
N_KV_HEADS = 2
GROUP = N_HEADS // N_KV_HEADS
WINDOW = 128
BLOCK = 128
N_BUCKETS = 32
MAX_DISTANCE = 128
Q_DIM = N_HEADS * HEAD_DIM
KV_DIM = N_KV_HEADS * HEAD_DIM
D_FF = 3 * D_MODEL
LN_EPS = 1e-5
NEG_INF = -1e30
F32_TINY = float(np.finfo(np.float32).tiny)

LANES = 128
SUBLANES = 8
MXU_DIM = 256

N_STREAMS = SUBLANES
T_SEG = 64
PERM_T = MXU_DIM // N_STREAMS
HALO_ROWS = 4 * N_STREAMS
ROWS_ATTN = 512
ATTN_COLS = 256
ROWS_FFN = 1024
FFN_COLS = 512
FFN_DOWN_SPLIT = 3
FFN_HALO = 16
LRU_TILE = 64
LRU_COLS = 256
V7X_VMEM_BYTES = 64 * 1024 * 1024
VMEM_LIMIT = V7X_VMEM_BYTES - 8 * 1024 * 1024
VMEM_LIMIT_FFN = V7X_VMEM_BYTES - 4 * 1024 * 1024

GATE_OFF = tuple(LANES * ((LRU_BLOCK * i) // LANES) for i in range(N_LRU_BLOCKS))


def _sigmoid_of_twice(z):
    return 0.5 * jnp.tanh(z) + 0.5


def _gelu(x):
    c = math.sqrt(2.0 / math.pi)
    inner = x * (c + (c * 0.044715) * (x * x))
    return x * (0.5 * jnp.tanh(inner) + 0.5)


def _layer_norm(y, g, b):
    mu = jnp.mean(y, axis=-1, keepdims=True)
    yc = y - mu
    var = jnp.mean(yc * yc, axis=-1, keepdims=True)
    return yc * lax.rsqrt(var + LN_EPS) * g + b


def _dot(a, b):
    return jnp.dot(a, b, preferred_element_type=F32)


def _const_spec(shape):
    nd = len(shape)
    return pl.BlockSpec(shape, lambda *_: (0,) * nd, pipeline_mode=pl.Buffered(1))


def _mxu_cycles(m, k, n):
    return (m // SUBLANES) * (n // LANES) * (-(-k // MXU_DIM))


def _vpu_cycles(rows, cols, ops_per_vreg):
    return rows * cols // (SUBLANES * LANES) * ops_per_vreg // 4


def _interleave(*streams):
    queues = [list(q) for q in streams]
    mxu = vpu = 0
    while any(queues):
        q = min((q for q in queues if q), key=lambda q: abs((mxu + q[0][1]) - (vpu + q[0][2])))
        fn, m, v = q.pop(0)
        fn()
        mxu += m
        vpu += v


def _permute_items(x4_ref, pm_ref, xb_dst, lhs_dst, row0, *, nb, ns):
    T = T_SEG
    R = N_STREAMS * T

    def copy():
        for b in range(nb):
            for sg in range(ns):
                i = b * ns + sg
                xb_dst[i * T:(i + 1) * T, :] = x4_ref[b, sg].astype(BF16)

    def permute(tb):
        src = jnp.concatenate(
            [xb_dst[i * T + PERM_T * tb:i * T + PERM_T * (tb + 1), :] for i in range(N_STREAMS)], axis=0)
        lhs_dst[row0 + MXU_DIM * tb:row0 + MXU_DIM * (tb + 1), :] = _dot(pm_ref[...], src).astype(BF16)

    items = [(copy, 0, _vpu_cycles(R, D_MODEL, 2))]
    for tb in range(T // PERM_T):
        items.append((functools.partial(permute, tb), _mxu_cycles(MXU_DIM, MXU_DIM, D_MODEL),
                      _vpu_cycles(MXU_DIM, D_MODEL, 1)))
    return items


def _input_items(x4_ref, xprev_ref, xnext_ref, hsel_ref, pm_ref, wu_ref, bu_ref, wc_ref, bc_ref,
                 xb_dst, lhs_dst, uall_scr, u_dst, ub_dst, u_out, *, nb, ns, jj, nj):
    T = T_SEG
    R = N_STREAMS * T
    C = D_RNN
    ncol = C // LRU_COLS

    def halo():
        outside = jnp.concatenate([xprev_ref[...].reshape(nb * SUBLANES, D_MODEL),
                                   xnext_ref[...].reshape(nb * SUBLANES, D_MODEL)], axis=0).astype(BF16)
        rows = _dot(hsel_ref[...], outside)
        if ns > 1:
            seg = lax.broadcasted_iota(jnp.int32, (N_STREAMS, D_MODEL), 0) & (ns - 1)
            has_left = (seg > 0).astype(F32)
            has_right = (seg < ns - 1).astype(F32)
            m0 = 2 * N_STREAMS
            tail = lhs_dst[m0 + R - 2 * N_STREAMS:m0 + R, :].astype(F32)
            head = lhs_dst[m0:m0 + 2 * N_STREAMS, :].astype(F32)
            inside = jnp.concatenate(
                [pltpu.roll(tail[0:N_STREAMS], 1, 0) * has_left,
                 pltpu.roll(tail[N_STREAMS:], 1, 0) * has_left,
                 pltpu.roll(head[0:N_STREAMS], N_STREAMS - 1, 0) * has_right,
                 jnp.zeros((N_STREAMS, D_MODEL), F32)], axis=0)
            rows = rows + inside
        rows = rows.astype(BF16)
        lhs_dst[0:2 * N_STREAMS, :] = rows[0:2 * N_STREAMS]
        lhs_dst[2 * N_STREAMS + R:, :] = rows[2 * N_STREAMS:]

    def project(c):
        cs = slice(c * LRU_COLS, (c + 1) * LRU_COLS)
        uall_scr[:, cs] = _dot(lhs_dst[...], wu_ref[:, cs]) + bu_ref[:, cs]
        seg = lax.broadcasted_iota(jnp.int32, (N_STREAMS, LRU_COLS), 0) & (ns - 1)
        keep_prev = 1.0 - (seg == 0).astype(F32) * (jj == 0).astype(F32)
        keep_next = 1.0 - (seg == ns - 1).astype(F32) * (jj == nj - 1).astype(F32)
        prev_rows = slice(0, 2 * N_STREAMS)
        next_rows = slice(2 * N_STREAMS + R, 3 * N_STREAMS + R)
        uall_scr[prev_rows, cs] = uall_scr[prev_rows, cs] * jnp.concatenate([keep_prev, keep_prev], axis=0)
        uall_scr[next_rows, cs] = uall_scr[next_rows, cs] * keep_next

    def conv(c):
        cs = slice(c * LRU_COLS, (c + 1) * LRU_COLS)
        acc = bc_ref[:, cs] + wc_ref[0:1, cs] * uall_scr[0:R, cs]
        for k in range(1, LRU_CONV_W):
            acc = acc + wc_ref[k:k + 1, cs] * uall_scr[N_STREAMS * k:N_STREAMS * k + R, cs]
        u_dst[:, cs] = acc
        ub_dst[:, cs] = acc.astype(BF16)
        u_out[:, cs] = acc.astype(BF16)

    items = _permute_items(x4_ref, pm_ref, xb_dst, lhs_dst, 2 * N_STREAMS, nb=nb, ns=ns)
    items.append((halo, 0, 0))
    project_cost = _mxu_cycles(R + HALO_ROWS, D_MODEL, LRU_COLS)
    conv_cost = _vpu_cycles(R, LRU_COLS, 2 * LRU_CONV_W + 1)
    items.append((functools.partial(project, 0), project_cost, 0))
    for c in range(ncol):
        if c + 1 < ncol:
            items.append((functools.partial(project, c + 1), project_cost, 0))
        items.append((functools.partial(conv, c), 0, conv_cost))
    return items


def _coeff_items(ub_src, u_src, wg_ref, bg_ref, lam_ref, pre_scr):
    R = N_STREAMS * T_SEG
    C = D_RNN

    def coeff(t):
        cols = slice(t * LANES, (t + 1) * LANES)
        xcols = slice(C + t * LANES, C + (t + 1) * LANES)
        lam = lam_ref[:, cols]
        log_sig = jnp.minimum(lam, 0.0) - jnp.log1p(jnp.exp(-jnp.abs(lam)))
        hc = (0.5 * LRU_C) * log_sig
        log_a = hc * jnp.tanh(pre_scr[:, cols]) + hc
        ig = 0.5 * jnp.tanh(pre_scr[:, xcols]) + 0.5
        a = jnp.exp(log_a)
        gain = jnp.tanh(log_a) * (-1.0 - a * a)
        root = gain * lax.rsqrt(jnp.maximum(gain, F32_TINY))
        pre_scr[:, cols] = a
        pre_scr[:, xcols] = root * (ig * u_src[:, cols])

    first_touch = {}
    last_touch = {}
    for i in range(N_LRU_BLOCKS):
        for part in range(MXU_DIM // LANES):
            tile = GATE_OFF[i] // LANES + part
            first_touch.setdefault(tile, i)
            last_touch[tile] = i

    def gate(i):
        off = GATE_OFF[i]
        y = _dot(ub_src[:, off:off + MXU_DIM], wg_ref[i])
        for half in range(2):
            for part in range(MXU_DIM // LANES):
                col = half * C + off + part * LANES
                val = y[:, half * MXU_DIM + part * LANES:half * MXU_DIM + (part + 1) * LANES]
                if first_touch[off // LANES + part] == i:
                    pre_scr[:, col:col + LANES] = val + bg_ref[:, col:col + LANES]
                else:
                    pre_scr[:, col:col + LANES] = pre_scr[:, col:col + LANES] + val

    gate_cost = (_mxu_cycles(R, MXU_DIM, 2 * MXU_DIM), _vpu_cycles(R, 2 * MXU_DIM, 1))
    coeff_cost = _vpu_cycles(R, LANES, 20)
    items = [(functools.partial(gate, 0),) + gate_cost]
    for i in range(N_LRU_BLOCKS):
        if i + 1 < N_LRU_BLOCKS:
            items.append((functools.partial(gate, i + 1),) + gate_cost)
        for t in sorted(t for t, last in last_touch.items() if last == i):
            items.append((functools.partial(coeff, t), 0, coeff_cost))
    return items


def _stage_scan(pre_scr, p_scr, carry_scr, h_dst, *, ns, fwd, first):
    T = T_SEG
    C = D_RNN

    keep = 1.0 - first.astype(F32)
    carry = carry_scr[...] * keep

    def scan_step(t, h, p):
        rows = slice(t * N_STREAMS, (t + 1) * N_STREAMS)
        a = pre_scr[rows, 0:C]
        h = a * h + pre_scr[rows, C:2 * C]
        h_dst[rows, :] = h
        if ns > 1:
            p = p * a
            p_scr[rows, :] = p
        return h, p

    h_end = carry if ns == 1 else jnp.zeros((N_STREAMS, C), F32)
    p_end = jnp.ones((N_STREAMS, C), F32)
    for tt in range(T):
        h_end, p_end = scan_step(tt if fwd else T - 1 - tt, h_end, p_end)

    if ns == 1:
        carry_scr[...] = h_end
        return None

    sub = lax.broadcasted_iota(jnp.int32, (N_STREAMS, C), 0)
    seg = sub & (ns - 1)
    head = (seg == 0) if fwd else (seg == ns - 1)
    from_prev = 1 if fwd else N_STREAMS - 1
    final = jnp.zeros((N_STREAMS, C), F32)
    cin = final
    for _ in range(ns):
        cin = jnp.where(head, carry, pltpu.roll(final, from_prev, 0))
        final = h_end + p_end * cin
    to_head = (N_STREAMS - (ns - 1)) % N_STREAMS if fwd else ns - 1
    carry_scr[...] = pltpu.roll(final, to_head, 0)
    return cin


def _add_carry(h, p, cin):
    shape3 = (h.shape[0] // N_STREAMS, N_STREAMS, h.shape[1])
    return (h.reshape(shape3) + p.reshape(shape3) * cin[None]).reshape(h.shape)


def _lru_bwd_kernel(x4_ref, xprev_ref, xnext_ref, hsel_ref, pm_ref, wu_ref, bu_ref, wc_ref, bc_ref,
                    wg_ref, bg_ref, lam_ref,
                    wkv_ref, bkv_ref,
                    h_ref, u_ref, kv_ref,
                    xb_scr, lhs_scr, uall_scr, u_scr, ub_scr, pre_scr, p_scr, carry_scr,
                    *, nb, ns, nj):
    T = T_SEG
    R = N_STREAMS * T
    s = pl.program_id(1)
    c_in = jnp.minimum(s, nj - 1)
    c_scan = jnp.clip(s - 2, 0, nj - 1)
    slot = s % 2

    @pl.when(s == 0)
    def _():
        pre_scr[...] = jnp.zeros_like(pre_scr)
        carry_scr[...] = jnp.zeros_like(carry_scr)
        u_scr[1] = jnp.zeros((R, D_RNN), F32)
        ub_scr[1] = jnp.zeros((R, D_RNN), BF16)

    cin = _stage_scan(pre_scr, p_scr, carry_scr, h_ref, ns=ns, fwd=False, first=(c_scan == 0))
    if cin is not None:
        for it in range(R // LRU_TILE):
            rows = slice(it * LRU_TILE, (it + 1) * LRU_TILE)
            h_ref[rows, :] = _add_carry(h_ref[rows, :], p_scr[rows, :], cin)

    def kv_project():
        kv = _dot(xb_scr[...], wkv_ref[...]) + bkv_ref[...]
        for b in range(nb):
            for sg in range(ns):
                i = b * ns + sg
                kv_ref[b, sg] = kv[i * T:(i + 1) * T].astype(BF16)

    input_items = _input_items(x4_ref, xprev_ref, xnext_ref, hsel_ref, pm_ref, wu_ref, bu_ref, wc_ref, bc_ref,
                               xb_scr, lhs_scr, uall_scr, u_scr.at[slot], ub_scr.at[slot], u_ref,
                               nb=nb, ns=ns, jj=nj - 1 - c_in, nj=nj)
    input_items.append((kv_project, _mxu_cycles(R, D_MODEL, 2 * KV_DIM), _vpu_cycles(R, 2 * KV_DIM, 2)))
    _interleave(_coeff_items(ub_scr.at[1 - slot], u_scr.at[1 - slot], wg_ref, bg_ref, lam_ref, pre_scr),
                input_items)


def _lru_fwd_kernel(x4_ref, u_ref, hb_ref, pm_ref, pmt_ref, wg_ref, bg_ref,
                    lam_ref, wgl_ref, bgl_ref, wmg_ref, bmg_ref, wout_ref,
                    m_ref,
                    xb_scr, lhs_scr, pre_scr, p_scr, carry_scr,
                    h_scr, g_scr, hg_scr, hgn_scr,
                    *, nb, ns, nj):
    T = T_SEG
    R = N_STREAMS * T
    C = D_RNN
    ncol = C // LRU_COLS
    s = pl.program_id(1)
    c_out = jnp.clip(s - 1, 0, nj - 1)
    slot = s % 2

    @pl.when(s == 0)
    def _():
        pre_scr[...] = jnp.zeros_like(pre_scr)
        carry_scr[...] = jnp.zeros_like(carry_scr)
        xb_scr[1] = jnp.zeros((R, D_MODEL), BF16)
        lhs_scr[1] = jnp.zeros((R, D_MODEL), BF16)

    cin = _stage_scan(pre_scr, p_scr, carry_scr, h_scr, ns=ns, fwd=True, first=(c_out == 0))

    def gate_project(c):
        cs = slice(c * LRU_COLS, (c + 1) * LRU_COLS)
        g_scr[c % 2] = _dot(lhs_scr[1 - slot], wgl_ref[:, cs]) + bgl_ref[:, cs]

    def gate_apply(c):
        cs = slice(c * LRU_COLS, (c + 1) * LRU_COLS)
        h = h_scr[:, cs]
        if cin is not None:
            h = _add_carry(h, p_scr[:, cs], cin[:, cs])
        h = h + hb_ref[:, cs]
        hg_scr[:, cs] = (h * _gelu(g_scr[c % 2])).astype(BF16)

    def unpermute(c):
        cs = slice(c * LRU_COLS, (c + 1) * LRU_COLS)
        for tb in range(T // PERM_T):
            nat = _dot(pmt_ref[...], hg_scr[MXU_DIM * tb:MXU_DIM * (tb + 1), cs]).astype(BF16)
            for i in range(N_STREAMS):
                hgn_scr[i * T + PERM_T * tb:i * T + PERM_T * (tb + 1), cs] = nat[i * PERM_T:(i + 1) * PERM_T]

    def out_project(c):
        cs = slice(c * LRU_COLS, (c + 1) * LRU_COLS)
        y_lru = _dot(hgn_scr[...], wout_ref[:, cs])
        gate = _sigmoid_of_twice(_dot(xb_scr[1 - slot], wmg_ref[:, cs]) + bmg_ref[:, cs])
        m = (gate * y_lru).astype(BF16)
        for b in range(nb):
            for sg in range(ns):
                i = b * ns + sg
                m_ref[b, sg, :, cs] = m[i * T:(i + 1) * T]

    gp_cost = _mxu_cycles(R, D_MODEL, LRU_COLS)
    out_items = [(functools.partial(gate_project, 0), gp_cost, 0)]
    for c in range(ncol):
        if c + 1 < ncol:
            out_items.append((functools.partial(gate_project, c + 1), gp_cost, 0))
        out_items.append((functools.partial(gate_apply, c), 0, _vpu_cycles(R, LRU_COLS, 14)))
        out_items.append((functools.partial(unpermute, c), (T // PERM_T) * _mxu_cycles(MXU_DIM, MXU_DIM, LRU_COLS),
                          _vpu_cycles(R, LRU_COLS, 1)))
    for c in range(D_MODEL // LRU_COLS):
        out_items.append((functools.partial(out_project, c),
                          _mxu_cycles(R, C, LRU_COLS) + _mxu_cycles(R, D_MODEL, LRU_COLS),
                          _vpu_cycles(R, LRU_COLS, 5)))

    in_items = _permute_items(x4_ref, pm_ref, xb_scr.at[slot], lhs_scr.at[slot], 0, nb=nb, ns=ns)
    in_items += _coeff_items(u_ref, u_ref, wg_ref, bg_ref, lam_ref, pre_scr)
    _interleave(out_items, in_items)


def _stream_split(batch):
    nb = math.gcd(batch, N_STREAMS)
    return nb, N_STREAMS // nb


def _halo_select(nb, ns):
    sel = np.zeros((HALO_ROWS, 2 * nb * SUBLANES), np.float32)
    for b in range(nb):
        first, last = b * ns, b * ns + ns - 1
        sel[0 * N_STREAMS + first, b * SUBLANES + SUBLANES - 2] = 1.0
        sel[1 * N_STREAMS + first, b * SUBLANES + SUBLANES - 1] = 1.0
        sel[2 * N_STREAMS + last, nb * SUBLANES + b * SUBLANES] = 1.0
    return sel


def _perm_matrix():
    pm = np.zeros((MXU_DIM, MXU_DIM), np.float32)
    for i in range(N_STREAMS):
        for t in range(PERM_T):
            pm[t * N_STREAMS + i, i * PERM_T + t] = 1.0
    return pm


def _lru_bwd_call(x, p):
    B, S, D = x.shape
    nb, ns = _stream_split(B)
    T = T_SEG
    R = N_STREAMS * T
    Q = S // T
    nj = Q // ns
    x4 = x.reshape(B, Q, T, D)
    x8 = x.reshape(B, S // SUBLANES, SUBLANES, D)
    hsel = jnp.asarray(_halo_select(nb, ns), BF16)
    C = D_RNN
    chunk_in = lambda bb, s: (bb, nj - 1 - jnp.minimum(s, nj - 1), 0, 0)
    chunk_scan = lambda bb, s: (bb, nj - 1 - jnp.clip(s - 2, 0, nj - 1), 0, 0)
    blocks_per_chunk = ns * T // SUBLANES
    before = lambda bb, s: (bb, jnp.maximum(chunk_in(bb, s)[1] * blocks_per_chunk - 1, 0), 0, 0)
    after = lambda bb, s: (bb, jnp.minimum((chunk_in(bb, s)[1] + 1) * blocks_per_chunk, S // SUBLANES - 1), 0, 0)
    in_specs = [
        pl.BlockSpec((nb, ns, T, D), chunk_in),
        pl.BlockSpec((nb, None, SUBLANES, D), before),
        pl.BlockSpec((nb, None, SUBLANES, D), after),
        _const_spec((HALO_ROWS, 2 * nb * SUBLANES)),
        _const_spec((MXU_DIM, MXU_DIM)),
        _const_spec((D, C)), _const_spec((1, C)),
        _const_spec((LRU_CONV_W, C)), _const_spec((1, C)),
        _const_spec((N_LRU_BLOCKS, MXU_DIM, 2 * MXU_DIM)), _const_spec((1, 2 * C)),
        _const_spec((1, C)),
        _const_spec((D, 2 * KV_DIM)), _const_spec((1, 2 * KV_DIM)),
    ]
    out_shape = [
        jax.ShapeDtypeStruct((B // nb, nj, R, C), F32),
        jax.ShapeDtypeStruct((B // nb, nj, R, C), BF16),
        jax.ShapeDtypeStruct((B, Q, T, 2 * KV_DIM), BF16),
    ]
    out_specs = [
        pl.BlockSpec((None, None, R, C), chunk_scan),
        pl.BlockSpec((None, None, R, C), chunk_in),
        pl.BlockSpec((nb, ns, T, 2 * KV_DIM), chunk_in),
    ]
    scratch = [
        pltpu.VMEM((R, D), BF16),
        pltpu.VMEM((R + HALO_ROWS, D), BF16),
        pltpu.VMEM((R + HALO_ROWS, C), F32),
        pltpu.VMEM((2, R, C), F32),
        pltpu.VMEM((2, R, C), BF16),
        pltpu.VMEM((R, 2 * C), F32),
        pltpu.VMEM((R, C), F32),
        pltpu.VMEM((N_STREAMS, C), F32),
    ]
    h_b, u, kv = pl.pallas_call(
        functools.partial(_lru_bwd_kernel, nb=nb, ns=ns, nj=nj),
        grid=(B // nb, nj + 2),
        in_specs=in_specs, out_specs=out_specs, out_shape=out_shape,
        scratch_shapes=scratch,
        compiler_params=pltpu.CompilerParams(
            dimension_semantics=("arbitrary", "arbitrary"), vmem_limit_bytes=VMEM_LIMIT),
        name="lru_bwd",
    )(x4, x8, x8, hsel, p["pm"], p["w_u"], p["b_u"], p["w_conv"], p["b_conv"], p["wg_b"], p["bg_b"], p["lam_b"],
      p["w_kv"], p["b_kv"])
    return h_b, u, kv.reshape(B, S, 2 * KV_DIM)


def _lru_fwd_call(x, u, h_b, p):
    B, S, D = x.shape
    nb, ns = _stream_split(B)
    T = T_SEG
    R = N_STREAMS * T
    Q = S // T
    nj = Q // ns
    x4 = x.reshape(B, Q, T, D)
    C = D_RNN
    chunk_in = lambda bb, s: (bb, jnp.minimum(s, nj - 1), 0, 0)
    chunk_out = lambda bb, s: (bb, jnp.clip(s - 1, 0, nj - 1), 0, 0)
    in_specs = [
        pl.BlockSpec((nb, ns, T, D), chunk_in),
        pl.BlockSpec((None, None, R, C), chunk_in),
        pl.BlockSpec((None, None, R, C), chunk_out),
        _const_spec((MXU_DIM, MXU_DIM)), _const_spec((MXU_DIM, MXU_DIM)),
        _const_spec((N_LRU_BLOCKS, MXU_DIM, 2 * MXU_DIM)), _const_spec((1, 2 * C)),
        _const_spec((1, C)),
        _const_spec((D, C)), _const_spec((1, C)),
        _const_spec((D, D)), _const_spec((1, D)),
        _const_spec((C, D)),
    ]
    scratch = [
        pltpu.VMEM((2, R, D), BF16),
        pltpu.VMEM((2, R, D), BF16),
        pltpu.VMEM((R, 2 * C), F32),
        pltpu.VMEM((R, C), F32),
        pltpu.VMEM((N_STREAMS, C), F32),
        pltpu.VMEM((R, C), F32),
        pltpu.VMEM((2, R, LRU_COLS), F32),
        pltpu.VMEM((R, C), BF16),
        pltpu.VMEM((R, C), BF16),
    ]
    m = pl.pallas_call(
        functools.partial(_lru_fwd_kernel, nb=nb, ns=ns, nj=nj),
        grid=(B // nb, nj + 1),
        in_specs=in_specs,
        out_specs=pl.BlockSpec((nb, ns, T, D), chunk_out),
        out_shape=jax.ShapeDtypeStruct((B, Q, T, D), BF16),
        scratch_shapes=scratch,
        compiler_params=pltpu.CompilerParams(
            dimension_semantics=("arbitrary", "arbitrary"), vmem_limit_bytes=VMEM_LIMIT),
        name="lru_fwd",
    )(x4, u, h_b, p["pm"], p["pmt"], p["wg_f"], p["bg_f"],
      p["lam_f"], p["w_gl"], p["b_gl"], p["w_mg_lru"], p["b_mg_lru"], p["w_lru_out"])
    return m.reshape(B, S, D)


def _attn_kernel(xa_ref, xb_ref, m_ref, kvp_ref, kvm_ref, kvn_ref, idx_ref, rel_ref, sink_ref,
                 wq_ref, bq_ref, wga_ref, bga_ref, wao_ref, wo_ref, bo_ref, g1_ref, b1_ref,
                 o_ref, xc_scr, q_scr, s_scr, attn_scr, merged_scr, y_scr, bias_scr, *, nr, alpha):
    R = ROWS_ATTN
    nblk = R // BLOCK
    i = pl.program_id(0)
    n_chunks = pl.num_programs(0) - 1
    r = lax.rem(jnp.minimum(i, n_chunks - 1), nr)
    slot = i % 2

    @pl.when(i == 0)
    def _():
        idx = idx_ref[...]
        for h in range(N_HEADS):
            def pick(bkt, acc):
                return jnp.where(idx == bkt, rel_ref[bkt, h], acc)
            tab = lax.fori_loop(0, N_BUCKETS, pick, jnp.full((BLOCK, 3 * BLOCK), NEG_INF, F32))
            kvh, g = divmod(h, GROUP)
            bias_scr[kvh, g * BLOCK:(g + 1) * BLOCK, :] = tab
        xc_scr[1] = jnp.zeros((R, D_MODEL), BF16)
        attn_scr[1] = jnp.zeros((R, Q_DIM), BF16)

    pen_prev = (r == 0).astype(F32) * NEG_INF
    pen_next = (r == nr - 1).astype(F32) * NEG_INF
    row_head = lax.broadcasted_iota(jnp.int32, (GROUP * BLOCK, 1), 0) // BLOCK

    def cast_x():
        xc_scr[slot] = xa_ref[...].astype(BF16)

    def q_project(kvh):
        cols = slice(kvh * GROUP * HEAD_DIM, (kvh + 1) * GROUP * HEAD_DIM)
        q_scr[:, cols] = (_dot(xc_scr[slot], wq_ref[:, cols]) + bq_ref[:, cols]).astype(BF16)

    def kv_window(col0, kvh, n):
        cols = slice(col0 + kvh * HEAD_DIM, col0 + (kvh + 1) * HEAD_DIM)
        prev = kvm_ref[(n - 1) * BLOCK:n * BLOCK, cols] if n > 0 else kvp_ref[:, cols]
        nxt = kvm_ref[(n + 1) * BLOCK:(n + 2) * BLOCK, cols] if n < nblk - 1 else kvn_ref[:, cols]
        return jnp.concatenate([prev, kvm_ref[n * BLOCK:(n + 1) * BLOCK, cols], nxt], axis=0)

    def scores(kvh, n):
        q4 = jnp.concatenate(
            [q_scr[n * BLOCK:(n + 1) * BLOCK, (kvh * GROUP + g) * HEAD_DIM:(kvh * GROUP + g + 1) * HEAD_DIM]
             for g in range(GROUP)], axis=0)
        k_ext = kv_window(0, kvh, n)
        s = lax.dot_general(q4, k_ext, (((1,), (1,)), ((), ())), preferred_element_type=F32)
        s = s + bias_scr[kvh]
        if n == 0:
            s = jnp.concatenate([s[:, 0:BLOCK] + pen_prev, s[:, BLOCK:]], axis=1)
        if n == nblk - 1:
            s = jnp.concatenate([s[:, 0:2 * BLOCK], s[:, 2 * BLOCK:] + pen_next], axis=1)
        s_scr[(kvh * nblk + n) % 2] = s

    def softmax_pv(kvh, n):
        sink = jnp.zeros((GROUP * BLOCK, 1), F32)
        for g in range(GROUP):
            sink = jnp.where(row_head == g, sink_ref[kvh * GROUP + g], sink)
        s = s_scr[(kvh * nblk + n) % 2]
        mx = jnp.maximum(jnp.max(s, axis=1, keepdims=True), sink)
        pr = jnp.exp(s - mx)
        denom = jnp.sum(pr, axis=1, keepdims=True) + jnp.exp(sink - mx)
        v_ext = kv_window(KV_DIM, kvh, n)
        o = _dot(pr.astype(BF16), v_ext) * (1.0 / denom)
        for g in range(GROUP):
            h = kvh * GROUP + g
            attn_scr[slot, n * BLOCK:(n + 1) * BLOCK, h * HEAD_DIM:(h + 1) * HEAD_DIM] = (
                o[g * BLOCK:(g + 1) * BLOCK].astype(BF16))

    rows4 = GROUP * BLOCK
    attn_items = [(cast_x, 0, _vpu_cycles(R, D_MODEL, 1))]
    for kvh in range(N_KV_HEADS):
        attn_items.append((functools.partial(q_project, kvh), _mxu_cycles(R, D_MODEL, GROUP * HEAD_DIM),
                           _vpu_cycles(R, GROUP * HEAD_DIM, 2)))
    blocks = [(kvh, n) for kvh in range(N_KV_HEADS) for n in range(nblk)]
    score_cost = (_mxu_cycles(rows4, HEAD_DIM, 3 * BLOCK), _vpu_cycles(rows4, 3 * BLOCK, 1))
    pv_cost = (2 * _mxu_cycles(rows4, 3 * BLOCK, HEAD_DIM), _vpu_cycles(rows4, 3 * BLOCK, 8))
    attn_items.append((functools.partial(scores, *blocks[0]),) + score_cost)
    for j, blk in enumerate(blocks):
        if j + 1 < len(blocks):
            attn_items.append((functools.partial(scores, *blocks[j + 1]),) + score_cost)
        attn_items.append((functools.partial(softmax_pv, *blk),) + pv_cost)

    def merge(c):
        cs = slice(c * ATTN_COLS, (c + 1) * ATTN_COLS)
        y_attn = _dot(attn_scr[1 - slot], wao_ref[:, cs])
        gate = _sigmoid_of_twice(_dot(xc_scr[1 - slot], wga_ref[:, cs]) + bga_ref[:, cs])
        merged_scr[:, cs] = (gate * y_attn + m_ref[:, cs].astype(F32)).astype(BF16)

    def out_project(c):
        cs = slice(c * ATTN_COLS, (c + 1) * ATTN_COLS)
        y_scr[:, cs] = alpha * xb_ref[:, cs] + (_dot(merged_scr[...], wo_ref[:, cs]) + bo_ref[:, cs])

    def norm():
        o_ref[...] = _layer_norm(y_scr[...], g1_ref[...], b1_ref[...])

    ncol = D_MODEL // ATTN_COLS
    merge_items = [(functools.partial(merge, c), _mxu_cycles(R, Q_DIM, ATTN_COLS) + _mxu_cycles(R, D_MODEL, ATTN_COLS),
                    _vpu_cycles(R, ATTN_COLS, 6)) for c in range(ncol)]
    merge_items += [(functools.partial(out_project, c), _mxu_cycles(R, D_MODEL, ATTN_COLS),
                     _vpu_cycles(R, ATTN_COLS, 3)) for c in range(ncol)]
    merge_items.append((norm, 0, _vpu_cycles(R, D_MODEL, 12)))
    _interleave(attn_items, merge_items)


def _attn_call(x, m_lru, kv, p, alpha):
    B, S, D = x.shape
    R = ROWS_ATTN
    nr = S // R
    bpr = R // BLOCK
    nkb = S // BLOCK
    n_chunks = B * nr
    cur = lambda i: jnp.minimum(i, n_chunks - 1)
    old = lambda i: jnp.maximum(i - 1, 0)
    main_cur = lambda i: (cur(i) // nr, cur(i) % nr, 0)
    main_old = lambda i: (old(i) // nr, old(i) % nr, 0)
    prev = lambda i: (cur(i) // nr, jnp.maximum((cur(i) % nr) * bpr - 1, 0), 0)
    nxt = lambda i: (cur(i) // nr, jnp.minimum((cur(i) % nr + 1) * bpr, nkb - 1), 0)
    in_specs = [
        pl.BlockSpec((None, R, D), main_cur),
        pl.BlockSpec((None, R, D), main_old),
        pl.BlockSpec((None, R, D), main_old),
        pl.BlockSpec((None, BLOCK, 2 * KV_DIM), prev),
        pl.BlockSpec((None, R, 2 * KV_DIM), main_cur),
        pl.BlockSpec((None, BLOCK, 2 * KV_DIM), nxt),
        _const_spec((BLOCK, 3 * BLOCK)),
        pl.BlockSpec(memory_space=pltpu.SMEM),
        pl.BlockSpec(memory_space=pltpu.SMEM),
        _const_spec((D, Q_DIM)), _const_spec((1, Q_DIM)),
        _const_spec((D, D)), _const_spec((1, D)),
        _const_spec((Q_DIM, D)),
        _const_spec((D, D)), _const_spec((1, D)),
        _const_spec((1, D)), _const_spec((1, D)),
    ]
    scratch = [
        pltpu.VMEM((2, R, D), BF16),
        pltpu.VMEM((R, Q_DIM), BF16),
        pltpu.VMEM((2, GROUP * BLOCK, 3 * BLOCK), F32),
        pltpu.VMEM((2, R, Q_DIM), BF16),
        pltpu.VMEM((R, D), BF16),
        pltpu.VMEM((R, D), F32),
        pltpu.VMEM((N_KV_HEADS, GROUP * BLOCK, 3 * BLOCK), F32),
    ]
    return pl.pallas_call(
        functools.partial(_attn_kernel, nr=nr, alpha=alpha),
        grid=(n_chunks + 1,),
        in_specs=in_specs,
        out_specs=pl.BlockSpec((None, R, D), main_old),
        out_shape=jax.ShapeDtypeStruct((B, S, D), F32),
        scratch_shapes=scratch,
        compiler_params=pltpu.CompilerParams(
            dimension_semantics=("arbitrary",), vmem_limit_bytes=VMEM_LIMIT),
        name="attn_merge",
    )(x, x, m_lru, kv, kv, kv, p["bias_idx"], p["rel_table"], p["sink"],
      p["w_q"], p["b_q"], p["w_mg_attn"], p["b_mg_attn"], p["w_attn_out"], p["w_o"], p["b_o"],
      p["ln1_g"], p["ln1_b"])


def _ffn_kernel(x_ref, xp_ref, xn_ref, wup_ref, bup_ref, wcv_ref, bcv_ref, wdn_ref, bdn_ref, g2_ref, b2_ref,
                o_ref, xe_scr, hs_scr, act_scr, *, nr, alpha):
    R = ROWS_FFN
    H = FFN_HALO
    nt = D_FF // FFN_COLS
    nslab = FFN_COLS // LANES
    r = pl.program_id(1)
    keep_prev = (r > 0).astype(F32)
    keep_next = (r < nr - 1).astype(F32)
    zeros = jnp.zeros((H - SUBLANES, D_MODEL), F32)
    xe_scr[...] = jnp.concatenate(
        [zeros, xp_ref[...] * keep_prev, x_ref[...], xn_ref[...] * keep_next, zeros], axis=0).astype(BF16)

    row = lax.broadcasted_iota(jnp.int32, (SUBLANES, FFN_COLS), 0)
    pad_prev = (row == SUBLANES - 1).astype(F32) * (1.0 - keep_prev)
    pad_next = (row == 0).astype(F32) * (1.0 - keep_next)

    def cols(j, half):
        return slice(half * D_FF + j * FFN_COLS, half * D_FF + (j + 1) * FFN_COLS)

    def up_proj(j):
        xe = xe_scr[...]
        for half in range(2):
            y = _dot(xe, wup_ref[:, cols(j, half)])
            b_up = bup_ref[:, cols(j, half)]
            y = jnp.concatenate(
                [y[0:H - SUBLANES], y[H - SUBLANES:H] - pad_prev * b_up, y[H:H + R],
                 y[H + R:H + R + SUBLANES] - pad_next * b_up, y[H + R + SUBLANES:]], axis=0)
            for c in range(nslab):
                hs_scr[j % 2, half, c] = y[:, c * LANES:(c + 1) * LANES]

    def conv(j, half, c):
        start = cols(j, half).start + c * LANES
        lanes = slice(start, start + LANES)
        w = wcv_ref[:, lanes]
        bias = bcv_ref[:, lanes] + bup_ref[:, lanes] * (w[0:1] + w[1:2] + w[2:3])
        return (w[0:1] * hs_scr[j % 2, half, c, H - 1:H - 1 + R, :]
                + w[1:2] * hs_scr[j % 2, half, c, H:H + R, :]
                + w[2:3] * hs_scr[j % 2, half, c, H + 1:H + 1 + R, :] + bias)

    def gate(j):
        for c in range(nslab):
            act = _gelu(conv(j, 1, c)) * conv(j, 0, c)
            act_scr[:, j * FFN_COLS + c * LANES:j * FFN_COLS + (c + 1) * LANES] = act.astype(BF16)

    per = nt // FFN_DOWN_SPLIT
    up_proj(0)
    for j in range(nt):
        if j + 1 < nt:
            up_proj(j + 1)
        gate(j)
        if (j + 1) % per == 0:
            part = (j + 1) // per - 1
            ks = slice(part * per * FFN_COLS, (part + 1) * per * FFN_COLS)
            contrib = _dot(act_scr[:, ks], wdn_ref[ks, :])
            if part == 0:
                o_ref[...] = contrib + (alpha * x_ref[...] + bdn_ref[...])
            elif part < FFN_DOWN_SPLIT - 1:
                o_ref[...] = o_ref[...] + contrib
            else:
                o_ref[...] = _layer_norm(o_ref[...] + contrib, g2_ref[...], b2_ref[...])


def _ffn_call(x, p, alpha):
    B, S, D = x.shape
    R = ROWS_FFN
    nr = S // R
    rpb = R // SUBLANES
    nsb = S // SUBLANES
    main = lambda b, r: (b, r, 0)
    prev = lambda b, r: (b, jnp.maximum(r * rpb - 1, 0), 0)
    nxt = lambda b, r: (b, jnp.minimum((r + 1) * rpb, nsb - 1), 0)
    in_specs = [
        pl.BlockSpec((None, R, D), main),
        pl.BlockSpec((None, SUBLANES, D), prev),
        pl.BlockSpec((None, SUBLANES, D), nxt),
        _const_spec((D, 2 * D_FF)), _const_spec((1, 2 * D_FF)),
        _const_spec((3, 2 * D_FF)), _const_spec((1, 2 * D_FF)),
        _const_spec((D_FF, D)), _const_spec((1, D)),
        _const_spec((1, D)), _const_spec((1, D)),
    ]
    scratch = [
        pltpu.VMEM((R + 2 * FFN_HALO, D), BF16),
        pltpu.VMEM((2, 2, FFN_COLS // LANES, R + 2 * FFN_HALO, LANES), F32),
        pltpu.VMEM((R, D_FF), BF16),
    ]
    return pl.pallas_call(
        functools.partial(_ffn_kernel, nr=nr, alpha=alpha),
        grid=(B, nr),
        in_specs=in_specs,
        out_specs=pl.BlockSpec((None, R, D), main),
        out_shape=jax.ShapeDtypeStruct((B, S, D), F32),
        scratch_shapes=scratch,
        compiler_params=pltpu.CompilerParams(
            dimension_semantics=("arbitrary", "arbitrary"), vmem_limit_bytes=VMEM_LIMIT_FFN),
        name="ffn",
    )(x, x, x, p["w_up"], p["b_up"], p["w_ffn_conv"], p["b_ffn_conv"], p["w_down"], p["b_down"],
      p["ln2_g"], p["ln2_b"])


def _t5_bucket(rel):
    nb = N_BUCKETS // 2
    ret = jnp.where(rel > 0, nb, 0)
    n = jnp.abs(rel)
    max_exact = nb // 2
    nf = jnp.maximum(n, 1).astype(jnp.float32)
    large = max_exact + (jnp.log(nf / max_exact) / math.log(MAX_DISTANCE / max_exact)
                         * (nb - max_exact)).astype(jnp.int32)
    large = jnp.minimum(large, nb - 1)
    return ret + jnp.where(n < max_exact, n, large)


def _pack_gate_weights(w_a, w_x):
    sel = np.zeros((N_LRU_BLOCKS, LRU_BLOCK, MXU_DIM), np.float32)
    for i in range(N_LRU_BLOCKS):
        lo = LRU_BLOCK * i - GATE_OFF[i]
        sel[i, np.arange(LRU_BLOCK), lo + np.arange(LRU_BLOCK)] = 1.0
    place = lambda w: jnp.einsum("nik,nij,njl->nkl", sel, w, sel, precision=lax.Precision.HIGHEST)
    return jnp.concatenate([place(w_a), place(w_x)], axis=-1).astype(BF16)


def _prepare(rel_table, w_in, b_in, w_lru_conv, b_lru_conv, w_rg_a, b_rg_a, w_rg_x, b_rg_x, lru_lambda,
             w_lru_out, attn_sink, w_attn_out, w_o, b_o, ln1_g, ln1_b, w_up, b_up, w_ffn_conv, b_ffn_conv,
             w_down, b_down, ln2_g, ln2_b):
    C = D_RNN
    o_g, o_q, o_k, o_gl, o_ga = C, 2 * C, 2 * C + Q_DIM, 2 * C + Q_DIM + 2 * KV_DIM, 2 * C + Q_DIM + 2 * KV_DIM + D_MODEL
    row = lambda a: a.reshape(1, -1).astype(F32)
    scale = HEAD_DIM ** -0.5
    pm = _perm_matrix()

    q_off = jnp.arange(BLOCK)[:, None]
    c_off = jnp.arange(3 * BLOCK)[None, :]
    rel = c_off - BLOCK - q_off
    bias_idx = jnp.where(jnp.abs(rel) <= WINDOW, _t5_bucket(rel), -1).astype(jnp.int32)

    return {
        "pm": jnp.asarray(pm, BF16), "pmt": jnp.asarray(pm.T, BF16),
        "w_u": w_in[:, 0:o_g].astype(BF16), "b_u": row(b_in[0:o_g]),
        "w_gl": w_in[:, o_g:o_q].astype(BF16), "b_gl": row(b_in[o_g:o_q]),
        "w_q": (w_in[:, o_q:o_k] * scale).astype(BF16), "b_q": row(b_in[o_q:o_k] * scale),
        "w_kv": w_in[:, o_k:o_gl].astype(BF16), "b_kv": row(b_in[o_k:o_gl]),
        "w_mg_lru": (0.5 * w_in[:, o_gl:o_ga]).astype(BF16), "b_mg_lru": row(0.5 * b_in[o_gl:o_ga]),
        "w_mg_attn": (0.5 * w_in[:, o_ga:]).astype(BF16), "b_mg_attn": row(0.5 * b_in[o_ga:]),
        "w_conv": w_lru_conv.astype(F32), "b_conv": row(b_lru_conv),
        "wg_f": _pack_gate_weights(0.5 * w_rg_a[0], 0.5 * w_rg_x[0]),
        "bg_f": row(0.5 * jnp.concatenate([b_rg_a[0].reshape(-1), b_rg_x[0].reshape(-1)])),
        "wg_b": _pack_gate_weights(0.5 * w_rg_a[1], 0.5 * w_rg_x[1]),
        "bg_b": row(0.5 * jnp.concatenate([b_rg_a[1].reshape(-1), b_rg_x[1].reshape(-1)])),
        "lam_f": row(lru_lambda[0]), "lam_b": row(lru_lambda[1]),
        "w_lru_out": w_lru_out.astype(BF16),
        "bias_idx": bias_idx, "rel_table": rel_table.astype(F32), "sink": attn_sink.astype(F32),
        "w_attn_out": w_attn_out.astype(BF16),
        "w_o": w_o.astype(BF16), "b_o": row(b_o),
        "ln1_g": row(ln1_g), "ln1_b": row(ln1_b),
        "w_up": w_up.astype(BF16), "b_up": row(b_up),
        "w_ffn_conv": w_ffn_conv.astype(F32), "b_ffn_conv": row(b_ffn_conv),
        "w_down": w_down.astype(BF16), "b_down": row(b_down),
        "ln2_g": row(ln2_g), "ln2_b": row(ln2_b),
    }


def _encoder_layer(x, p, alpha):
    h_b, u, kv = _lru_bwd_call(x, p)
    m_lru = _lru_fwd_call(x, u, h_b, p)
    x1 = _attn_call(x, m_lru, kv, p, alpha)
    return _ffn_call(x1, p, alpha)


def kernel(x_prompt, x_sample, rel_table, w_in, b_in, w_lru_conv, b_lru_conv, w_rg_a, b_rg_a, w_rg_x, b_rg_x, lru_lambda, w_lru_out, attn_sink, w_attn_out, w_o, b_o, ln1_g, ln1_b, w_up, b_up, w_ffn_conv, b_ffn_conv, w_down, b_down, ln2_g, ln2_b):
    depth = w_in.shape[0]
    alpha = (2.0 * depth) ** 0.25
    y_prompt, y_sample = x_prompt, x_sample
    for l in range(depth):
        p = _prepare(rel_table, w_in[l], b_in[l], w_lru_conv[l], b_lru_conv[l], w_rg_a[l], b_rg_a[l],
                     w_rg_x[l], b_rg_x[l], lru_lambda[l], w_lru_out[l], attn_sink[l], w_attn_out[l], w_o[l],
                     b_o[l], ln1_g[l], ln1_b[l], w_up[l], b_up[l], w_ffn_conv[l], b_ffn_conv[l], w_down[l],
                     b_down[l], ln2_g[l], ln2_b[l])
        y_prompt = _encoder_layer(y_prompt, p, alpha)
        y_sample = _encoder_layer(y_sample, p, alpha)
    return (y_prompt, y_sample)
```

```python
import functools
import math

import numpy as np
import jax
import jax.numpy as jnp
from jax import lax
from jax.experimental import pallas as pl
from jax.experimental.pallas import tpu as pltpu

F32 = jnp.float32
BF16 = jnp.bfloat16

D_MODEL = 1024
D_RNN = 1280
N_LRU_BLOCKS = 8
LRU_BLOCK = D_RNN // N_LRU_BLOCKS
LRU_C = 8.0
LRU_CONV_W = 4
HEAD_DIM = 128
N_HEADS = 8
N_KV_HEADS = 2
GROUP = N_HEADS // N_KV_HEADS
WINDOW = 128
BLOCK = 128
N_BUCKETS = 32
MAX_DISTANCE = 128
Q_DIM = N_HEADS * HEAD_DIM
KV_DIM = N_KV_HEADS * HEAD_DIM
D_FF = 3 * D_MODEL
LN_EPS = 1e-5
NEG_INF = -1e30
F32_TINY = float(np.finfo(np.float32).tiny)

LANES = 128
SUBLANES = 8
MXU_DIM = 256

N_STREAMS = SUBLANES
T_SEG = 64
PERM_T = MXU_DIM // N_STREAMS
HALO_ROWS = 4 * N_STREAMS
ROWS_ATTN = 512
ATTN_COLS = 512
ROWS_FFN = 1024
FFN_COLS = 512
FFN_DOWN_SPLIT = 3
FFN_HALO = 16
LRU_TILE = 64
LRU_COLS = 256
V7X_VMEM_BYTES = 64 * 1024 * 1024
VMEM_LIMIT = V7X_VMEM_BYTES - 8 * 1024 * 1024
VMEM_LIMIT_FFN = V7X_VMEM_BYTES - 4 * 1024 * 1024

GATE_OFF = tuple(LANES * ((LRU_BLOCK * i) // LANES) for i in range(N_LRU_BLOCKS))


def _sigmoid_of_twice(z):
    return 0.5 * jnp.tanh(z) + 0.5


def _gelu(x):
    c = math.sqrt(2.0 / math.pi)
    inner = x * (c + (c * 0.044715) * (x * x))
    return x * (0.5 * jnp.tanh(inner) + 0.5)


def _layer_norm(y, g, b):
    mu = jnp.mean(y, axis=-1, keepdims=True)
    yc = y - mu
    var = jnp.mean(yc * yc, axis=-1, keepdims=True)
    return yc * lax.rsqrt(var + LN_EPS) * g + b


def _dot(a, b):
    return jnp.dot(a, b, preferred_element_type=F32)


def _const_spec(shape):
    nd = len(shape)
    return pl.BlockSpec(shape, lambda *_: (0,) * nd, pipeline_mode=pl.Buffered(1))


def _mxu_cycles(m, k, n):
    return (m // SUBLANES) * (n // LANES) * (-(-k // MXU_DIM))


def _vpu_cycles(rows, cols, ops_per_vreg):
    return rows * cols // (SUBLANES * LANES) * ops_per_vreg // 4


def _interleave(*streams):
    queues = [list(q) for q in streams]
    mxu = vpu = 0
    while any(queues):
        q = min((q for q in queues if q), key=lambda q: abs((mxu + q[0][1]) - (vpu + q[0][2])))
        fn, m, v = q.pop(0)
        fn()
        mxu += m
        vpu += v


def _permute_items(x4_ref, pm_ref, xb_dst, lhs_dst, row0, *, nb, ns):
    T = T_SEG
    R = N_STREAMS * T

    def copy():
        for b in range(nb):
            for sg in range(ns):
                i = b * ns + sg
                xb_dst[i * T:(i + 1) * T, :] = x4_ref[b, sg].astype(BF16)

    def permute(tb):
        src = jnp.concatenate(
            [xb_dst[i * T + PERM_T * tb:i * T + PERM_T * (tb + 1), :] for i in range(N_STREAMS)], axis=0)
        lhs_dst[row0 + MXU_DIM * tb:row0 + MXU_DIM * (tb + 1), :] = _dot(pm_ref[...], src).astype(BF16)

    items = [(copy, 0, _vpu_cycles(R, D_MODEL, 2))]
    for tb in range(T // PERM_T):
        items.append((functools.partial(permute, tb), _mxu_cycles(MXU_DIM, MXU_DIM, D_MODEL),
                      _vpu_cycles(MXU_DIM, D_MODEL, 1)))
    return items


def _input_items(x4_ref, xprev_ref, xnext_ref, hsel_ref, pm_ref, wu_ref, bu_ref, wc_ref, bc_ref,
                 xb_dst, lhs_dst, uall_scr, u_dst, ub_dst, u_out, *, nb, ns, jj, nj):
    T = T_SEG
    R = N_STREAMS * T
    C = D_RNN
    ncol = C // LRU_COLS

    def halo():
        outside = jnp.concatenate([xprev_ref[...].reshape(nb * SUBLANES, D_MODEL),
                                   xnext_ref[...].reshape(nb * SUBLANES, D_MODEL)], axis=0).astype(BF16)
        rows = _dot(hsel_ref[...], outside)
        if ns > 1:
            seg = lax.broadcasted_iota(jnp.int32, (N_STREAMS, D_MODEL), 0) & (ns - 1)
            has_left = (seg > 0).astype(F32)
            has_right = (seg < ns - 1).astype(F32)
            m0 = 2 * N_STREAMS
            tail = lhs_dst[m0 + R - 2 * N_STREAMS:m0 + R, :].astype(F32)
            head = lhs_dst[m0:m0 + 2 * N_STREAMS, :].astype(F32)
            inside = jnp.concatenate(
                [pltpu.roll(tail[0:N_STREAMS], 1, 0) * has_left,
                 pltpu.roll(tail[N_STREAMS:], 1, 0) * has_left,
                 pltpu.roll(head[0:N_STREAMS], N_STREAMS - 1, 0) * has_right,
                 jnp.zeros((N_STREAMS, D_MODEL), F32)], axis=0)
            rows = rows + inside
        rows = rows.astype(BF16)
        lhs_dst[0:2 * N_STREAMS, :] = rows[0:2 * N_STREAMS]
        lhs_dst[2 * N_STREAMS + R:, :] = rows[2 * N_STREAMS:]

    def project(c):
        cs = slice(c * LRU_COLS, (c + 1) * LRU_COLS)
        uall_scr[:, cs] = _dot(lhs_dst[...], wu_ref[:, cs]) + bu_ref[:, cs]
        seg = lax.broadcasted_iota(jnp.int32, (N_STREAMS, LRU_COLS), 0) & (ns - 1)
        keep_prev = 1.0 - (seg == 0).astype(F32) * (jj == 0).astype(F32)
        keep_next = 1.0 - (seg == ns - 1).astype(F32) * (jj == nj - 1).astype(F32)
        prev_rows = slice(0, 2 * N_STREAMS)
        next_rows = slice(2 * N_STREAMS + R, 3 * N_STREAMS + R)
        uall_scr[prev_rows, cs] = uall_scr[prev_rows, cs] * jnp.concatenate([keep_prev, keep_prev], axis=0)
        uall_scr[next_rows, cs] = uall_scr[next_rows, cs] * keep_next

    def conv(c):
        cs = slice(c * LRU_COLS, (c + 1) * LRU_COLS)
        acc = bc_ref[:, cs] + wc_ref[0:1, cs] * uall_scr[0:R, cs]
        for k in range(1, LRU_CONV_W):
            acc = acc + wc_ref[k:k + 1, cs] * uall_scr[N_STREAMS * k:N_STREAMS * k + R, cs]
        u_dst[:, cs] = acc
        ub_dst[:, cs] = acc.astype(BF16)
        u_out[:, cs] = acc.astype(BF16)

    items = _permute_items(x4_ref, pm_ref, xb_dst, lhs_dst, 2 * N_STREAMS, nb=nb, ns=ns)
    items.append((halo, 0, 0))
    project_cost = _mxu_cycles(R + HALO_ROWS, D_MODEL, LRU_COLS)
    conv_cost = _vpu_cycles(R, LRU_COLS, 2 * LRU_CONV_W + 1)
    items.append((functools.partial(project, 0), project_cost, 0))
    for c in range(ncol):
        if c + 1 < ncol:
            items.append((functools.partial(project, c + 1), project_cost, 0))
        items.append((functools.partial(conv, c), 0, conv_cost))
    return items


def _coeff_items(ub_src, u_src, wg_ref, bg_ref, lam_ref, pre_scr):
    R = N_STREAMS * T_SEG
    C = D_RNN

    def coeff(t):
        cols = slice(t * LANES, (t + 1) * LANES)
        xcols = slice(C + t * LANES, C + (t + 1) * LANES)
        lam = lam_ref[:, cols]
        log_sig = jnp.minimum(lam, 0.0) - jnp.log1p(jnp.exp(-jnp.abs(lam)))
        hc = (0.5 * LRU_C) * log_sig
        log_a = hc * jnp.tanh(pre_scr[:, cols]) + hc
        ig = 0.5 * jnp.tanh(pre_scr[:, xcols]) + 0.5
        a = jnp.exp(log_a)
        gain = jnp.tanh(log_a) * (-1.0 - a * a)
        root = gain * lax.rsqrt(jnp.maximum(gain, F32_TINY))
        pre_scr[:, cols] = a
        pre_scr[:, xcols] = root * (ig * u_src[:, cols])

    first_touch = {}
    last_touch = {}
    for i in range(N_LRU_BLOCKS):
        for part in range(MXU_DIM // LANES):
            tile = GATE_OFF[i] // LANES + part
            first_touch.setdefault(tile, i)
            last_touch[tile] = i

    def gate(i):
        off = GATE_OFF[i]
        y = _dot(ub_src[:, off:off + MXU_DIM], wg_ref[i])
        for half in range(2):
            for part in range(MXU_DIM // LANES):
                col = half * C + off + part * LANES
                val = y[:, half * MXU_DIM + part * LANES:half * MXU_DIM + (part + 1) * LANES]
                if first_touch[off // LANES + part] == i:
                    pre_scr[:, col:col + LANES] = val + bg_ref[:, col:col + LANES]
                else:
                    pre_scr[:, col:col + LANES] = pre_scr[:, col:col + LANES] + val

    gate_cost = (_mxu_cycles(R, MXU_DIM, 2 * MXU_DIM), _vpu_cycles(R, 2 * MXU_DIM, 1))
    coeff_cost = _vpu_cycles(R, LANES, 20)
    items = [(functools.partial(gate, 0),) + gate_cost]
    for i in range(N_LRU_BLOCKS):
        if i + 1 < N_LRU_BLOCKS:
            items.append((functools.partial(gate, i + 1),) + gate_cost)
        for t in sorted(t for t, last in last_touch.items() if last == i):
            items.append((functools.partial(coeff, t), 0, coeff_cost))
    return items


def _stage_scan(pre_scr, p_scr, carry_scr, h_dst, *, ns, fwd, first):
    T = T_SEG
    C = D_RNN

    keep = 1.0 - first.astype(F32)
    carry = carry_scr[...] * keep

    def scan_step(t, h, p):
        rows = slice(t * N_STREAMS, (t + 1) * N_STREAMS)
        a = pre_scr[rows, 0:C]
        h = a * h + pre_scr[rows, C:2 * C]
        h_dst[rows, :] = h
        if ns > 1:
            p = p * a
            p_scr[rows, :] = p
        return h, p

    h_end = carry if ns == 1 else jnp.zeros((N_STREAMS, C), F32)
    p_end = jnp.ones((N_STREAMS, C), F32)
    for tt in range(T):
        h_end, p_end = scan_step(tt if fwd else T - 1 - tt, h_end, p_end)

    if ns == 1:
        carry_scr[...] = h_end
        return None

    sub = lax.broadcasted_iota(jnp.int32, (N_STREAMS, C), 0)
    seg = sub & (ns - 1)
    head = (seg == 0) if fwd else (seg == ns - 1)
    from_prev = 1 if fwd else N_STREAMS - 1
    final = jnp.zeros((N_STREAMS, C), F32)
    cin = final
    for _ in range(ns):
        cin = jnp.where(head, carry, pltpu.roll(final, from_prev, 0))
        final = h_end + p_end * cin
    to_head = (N_STREAMS - (ns - 1)) % N_STREAMS if fwd else ns - 1
    carry_scr[...] = pltpu.roll(final, to_head, 0)
    return cin


def _add_carry(h, p, cin):
    shape3 = (h.shape[0] // N_STREAMS, N_STREAMS, h.shape[1])
    return (h.reshape(shape3) + p.reshape(shape3) * cin[None]).reshape(h.shape)


def _lru_bwd_kernel(x4_ref, xprev_ref, xnext_ref, hsel_ref, pm_ref, wu_ref, bu_ref, wc_ref, bc_ref,
                    wg_ref, bg_ref, lam_ref,
                    wkv_ref, bkv_ref,
                    h_ref, u_ref, kv_ref,
                    xb_scr, lhs_scr, uall_scr, u_scr, ub_scr, pre_scr, p_scr, carry_scr,
                    *, nb, ns, nj):
    T = T_SEG
    R = N_STREAMS * T
    s = pl.program_id(1)
    c_in = jnp.minimum(s, nj - 1)
    c_scan = jnp.clip(s - 2, 0, nj - 1)
    slot = s % 2

    @pl.when(s == 0)
    def _():
        pre_scr[...] = jnp.zeros_like(pre_scr)
        carry_scr[...] = jnp.zeros_like(carry_scr)
        u_scr[1] = jnp.zeros((R, D_RNN), F32)
        ub_scr[1] = jnp.zeros((R, D_RNN), BF16)

    cin = _stage_scan(pre_scr, p_scr, carry_scr, h_ref, ns=ns, fwd=False, first=(c_scan == 0))
    if cin is not None:
        for it in range(R // LRU_TILE):
            rows = slice(it * LRU_TILE, (it + 1) * LRU_TILE)
            h_ref[rows, :] = _add_carry(h_ref[rows, :], p_scr[rows, :], cin)

    def kv_project():
        kv = _dot(xb_scr[...], wkv_ref[...]) + bkv_ref[...]
        for b in range(nb):
            for sg in range(ns):
                i = b * ns + sg
                kv_ref[b, sg] = kv[i * T:(i + 1) * T].astype(BF16)

    input_items = _input_items(x4_ref, xprev_ref, xnext_ref, hsel_ref, pm_ref, wu_ref, bu_ref, wc_ref, bc_ref,
                               xb_scr, lhs_scr, uall_scr, u_scr.at[slot], ub_scr.at[slot], u_ref,
                               nb=nb, ns=ns, jj=nj - 1 - c_in, nj=nj)
    input_items.append((kv_project, _mxu_cycles(R, D_MODEL, 2 * KV_DIM), _vpu_cycles(R, 2 * KV_DIM, 2)))
    _interleave(_coeff_items(ub_scr.at[1 - slot], u_scr.at[1 - slot], wg_ref, bg_ref, lam_ref, pre_scr),
                input_items)


def _lru_fwd_kernel(x4_ref, u_ref, hb_ref, pm_ref, pmt_ref, wg_ref, bg_ref,
                    lam_ref, wgl_ref, bgl_ref, wmg_ref, bmg_ref, wout_ref,
                    m_ref,
                    xb_scr, lhs_scr, pre_scr, p_scr, carry_scr,
                    h_scr, g_scr, hg_scr, hgn_scr,
                    *, nb, ns, nj):
    T = T_SEG
    R = N_STREAMS * T
    C = D_RNN
    ncol = C // LRU_COLS
    s = pl.program_id(1)
    c_out = jnp.clip(s - 1, 0, nj - 1)
    slot = s % 2

    @pl.when(s == 0)
    def _():
        pre_scr[...] = jnp.zeros_like(pre_scr)
        carry_scr[...] = jnp.zeros_like(carry_scr)
        xb_scr[1] = jnp.zeros((R, D_MODEL), BF16)
        lhs_scr[1] = jnp.zeros((R, D_MODEL), BF16)

    cin = _stage_scan(pre_scr, p_scr, carry_scr, h_scr, ns=ns, fwd=True, first=(c_out == 0))

    def gate_project(c):
        cs = slice(c * LRU_COLS, (c + 1) * LRU_COLS)
        g_scr[c % 2] = _dot(lhs_scr[1 - slot], wgl_ref[:, cs]) + bgl_ref[:, cs]

    def gate_apply(c):
        cs = slice(c * LRU_COLS, (c + 1) * LRU_COLS)
        h = h_scr[:, cs]
        if cin is not None:
            h = _add_carry(h, p_scr[:, cs], cin[:, cs])
        h = h + hb_ref[:, cs]
        hg_scr[:, cs] = (h * _gelu(g_scr[c % 2])).astype(BF16)

    def unpermute(c):
        cs = slice(c * LRU_COLS, (c + 1) * LRU_COLS)
        for tb in range(T // PERM_T):
            nat = _dot(pmt_ref[...], hg_scr[MXU_DIM * tb:MXU_DIM * (tb + 1), cs]).astype(BF16)
            for i in range(N_STREAMS):
                hgn_scr[i * T + PERM_T * tb:i * T + PERM_T * (tb + 1), cs] = nat[i * PERM_T:(i + 1) * PERM_T]

    def out_project(c):
        cs = slice(c * LRU_COLS, (c + 1) * LRU_COLS)
        y_lru = _dot(hgn_scr[...], wout_ref[:, cs])
        gate = _sigmoid_of_twice(_dot(xb_scr[1 - slot], wmg_ref[:, cs]) + bmg_ref[:, cs])
        m = (gate * y_lru).astype(BF16)
        for b in range(nb):
            for sg in range(ns):
                i = b * ns + sg
                m_ref[b, sg, :, cs] = m[i * T:(i + 1) * T]

    gp_cost = _mxu_cycles(R, D_MODEL, LRU_COLS)
    out_items = [(functools.partial(gate_project, 0), gp_cost, 0)]
    for c in range(ncol):
        if c + 1 < ncol:
            out_items.append((functools.partial(gate_project, c + 1), gp_cost, 0))
        out_items.append((functools.partial(gate_apply, c), 0, _vpu_cycles(R, LRU_COLS, 14)))
        out_items.append((functools.partial(unpermute, c), (T // PERM_T) * _mxu_cycles(MXU_DIM, MXU_DIM, LRU_COLS),
                          _vpu_cycles(R, LRU_COLS, 1)))
    for c in range(D_MODEL // LRU_COLS):
        out_items.append((functools.partial(out_project, c),
                          _mxu_cycles(R, C, LRU_COLS) + _mxu_cycles(R, D_MODEL, LRU_COLS),
                          _vpu_cycles(R, LRU_COLS, 5)))

    in_items = _permute_items(x4_ref, pm_ref, xb_scr.at[slot], lhs_scr.at[slot], 0, nb=nb, ns=ns)
    in_items += _coeff_items(u_ref, u_ref, wg_ref, bg_ref, lam_ref, pre_scr)
    _interleave(out_items, in_items)


def _stream_split(batch):
    nb = math.gcd(batch, N_STREAMS)
    return nb, N_STREAMS // nb


def _halo_select(nb, ns):
    sel = np.zeros((HALO_ROWS, 2 * nb * SUBLANES), np.float32)
    for b in range(nb):
        first, last = b * ns, b * ns + ns - 1
        sel[0 * N_STREAMS + first, b * SUBLANES + SUBLANES - 2] = 1.0
        sel[1 * N_STREAMS + first, b * SUBLANES + SUBLANES - 1] = 1.0
        sel[2 * N_STREAMS + last, nb * SUBLANES + b * SUBLANES] = 1.0
    return sel


def _perm_matrix():
    pm = np.zeros((MXU_DIM, MXU_DIM), np.float32)
    for i in range(N_STREAMS):
        for t in range(PERM_T):
            pm[t * N_STREAMS + i, i * PERM_T + t] = 1.0
    return pm


def _lru_bwd_call(x, p):
    B, S, D = x.shape
    nb, ns = _stream_split(B)
    T = T_SEG
    R = N_STREAMS * T
    Q = S // T
    nj = Q // ns
    x4 = x.reshape(B, Q, T, D)
    x8 = x.reshape(B, S // SUBLANES, SUBLANES, D)
    hsel = jnp.asarray(_halo_select(nb, ns), BF16)
    C = D_RNN
    chunk_in = lambda bb, s: (bb, nj - 1 - jnp.minimum(s, nj - 1), 0, 0)
    chunk_scan = lambda bb, s: (bb, nj - 1 - jnp.clip(s - 2, 0, nj - 1), 0, 0)
    blocks_per_chunk = ns * T // SUBLANES
    before = lambda bb, s: (bb, jnp.maximum(chunk_in(bb, s)[1] * blocks_per_chunk - 1, 0), 0, 0)
    after = lambda bb, s: (bb, jnp.minimum((chunk_in(bb, s)[1] + 1) * blocks_per_chunk, S // SUBLANES - 1), 0, 0)
    in_specs = [
        pl.BlockSpec((nb, ns, T, D), chunk_in),
        pl.BlockSpec((nb, None, SUBLANES, D), before),
        pl.BlockSpec((nb, None, SUBLANES, D), after),
        _const_spec((HALO_ROWS, 2 * nb * SUBLANES)),
        _const_spec((MXU_DIM, MXU_DIM)),
        _const_spec((D, C)), _const_spec((1, C)),
        _const_spec((LRU_CONV_W, C)), _const_spec((1, C)),
        _const_spec((N_LRU_BLOCKS, MXU_DIM, 2 * MXU_DIM)), _const_spec((1, 2 * C)),
        _const_spec((1, C)),
        _const_spec((D, 2 * KV_DIM)), _const_spec((1, 2 * KV_DIM)),
    ]
    out_shape = [
        jax.ShapeDtypeStruct((B // nb, nj, R, C), F32),
        jax.ShapeDtypeStruct((B // nb, nj, R, C), BF16),
        jax.ShapeDtypeStruct((B, Q, T, 2 * KV_DIM), BF16),
    ]
    out_specs = [
        pl.BlockSpec((None, None, R, C), chunk_scan),
        pl.BlockSpec((None, None, R, C), chunk_in),
        pl.BlockSpec((nb, ns, T, 2 * KV_DIM), chunk_in),
    ]
    scratch = [
        pltpu.VMEM((R, D), BF16),
        pltpu.VMEM((R + HALO_ROWS, D), BF16),
        pltpu.VMEM((R + HALO_ROWS, C), F32),
        pltpu.VMEM((2, R, C), F32),
        pltpu.VMEM((2, R, C), BF16),
        pltpu.VMEM((R, 2 * C), F32),
        pltpu.VMEM((R, C), F32),
        pltpu.VMEM((N_STREAMS, C), F32),
    ]
    h_b, u, kv = pl.pallas_call(
        functools.partial(_lru_bwd_kernel, nb=nb, ns=ns, nj=nj),
        grid=(B // nb, nj + 2),
        in_specs=in_specs, out_specs=out_specs, out_shape=out_shape,
        scratch_shapes=scratch,
        compiler_params=pltpu.CompilerParams(
            dimension_semantics=("arbitrary", "arbitrary"), vmem_limit_bytes=VMEM_LIMIT),
        name="lru_bwd",
    )(x4, x8, x8, hsel, p["pm"], p["w_u"], p["b_u"], p["w_conv"], p["b_conv"], p["wg_b"], p["bg_b"], p["lam_b"],
      p["w_kv"], p["b_kv"])
    return h_b, u, kv.reshape(B, S, 2 * KV_DIM)


def _lru_fwd_call(x, u, h_b, p):
    B, S, D = x.shape
    nb, ns = _stream_split(B)
    T = T_SEG
    R = N_STREAMS * T
    Q = S // T
    nj = Q // ns
    x4 = x.reshape(B, Q, T, D)
    C = D_RNN
    chunk_in = lambda bb, s: (bb, jnp.minimum(s, nj - 1), 0, 0)
    chunk_out = lambda bb, s: (bb, jnp.clip(s - 1, 0, nj - 1), 0, 0)
    in_specs = [
        pl.BlockSpec((nb, ns, T, D), chunk_in),
        pl.BlockSpec((None, None, R, C), chunk_in),
        pl.BlockSpec((None, None, R, C), chunk_out),
        _const_spec((MXU_DIM, MXU_DIM)), _const_spec((MXU_DIM, MXU_DIM)),
        _const_spec((N_LRU_BLOCKS, MXU_DIM, 2 * MXU_DIM)), _const_spec((1, 2 * C)),
        _const_spec((1, C)),
        _const_spec((D, C)), _const_spec((1, C)),
        _const_spec((D, D)), _const_spec((1, D)),
        _const_spec((C, D)),
    ]
    scratch = [
        pltpu.VMEM((2, R, D), BF16),
        pltpu.VMEM((2, R, D), BF16),
        pltpu.VMEM((R, 2 * C), F32),
        pltpu.VMEM((R, C), F32),
        pltpu.VMEM((N_STREAMS, C), F32),
        pltpu.VMEM((R, C), F32),
        pltpu.VMEM((2, R, LRU_COLS), F32),
        pltpu.VMEM((R, C), BF16),
        pltpu.VMEM((R, C), BF16),
    ]
    m = pl.pallas_call(
        functools.partial(_lru_fwd_kernel, nb=nb, ns=ns, nj=nj),
        grid=(B // nb, nj + 1),
        in_specs=in_specs,
        out_specs=pl.BlockSpec((nb, ns, T, D), chunk_out),
        out_shape=jax.ShapeDtypeStruct((B, Q, T, D), BF16),
        scratch_shapes=scratch,
        compiler_params=pltpu.CompilerParams(
            dimension_semantics=("arbitrary", "arbitrary"), vmem_limit_bytes=VMEM_LIMIT),
        name="lru_fwd",
    )(x4, u, h_b, p["pm"], p["pmt"], p["wg_f"], p["bg_f"],
      p["lam_f"], p["w_gl"], p["b_gl"], p["w_mg_lru"], p["b_mg_lru"], p["w_lru_out"])
    return m.reshape(B, S, D)


def _attn_kernel(xa_ref, xb_ref, m_ref, kvp_ref, kvm_ref, kvn_ref, idx_ref, rel_ref, sink_ref,
                 wq_ref, bq_ref, wga_ref, bga_ref, wao_ref, wo_ref, bo_ref, g1_ref, b1_ref,
                 o_ref, xc_scr, q_scr, s_scr, attn_scr, merged_scr, y_scr, bias_scr, *, nr, alpha):
    R = ROWS_ATTN
    nblk = R // BLOCK
    i = pl.program_id(0)
    n_chunks = pl.num_programs(0) - 1
    r = lax.rem(jnp.minimum(i, n_chunks - 1), nr)
    slot = i % 2

    @pl.when(i == 0)
    def _():
        idx = idx_ref[...]
        for h in range(N_HEADS):
            def pick(bkt, acc):
                return jnp.where(idx == bkt, rel_ref[bkt, h], acc)
            tab = lax.fori_loop(0, N_BUCKETS, pick, jnp.full((BLOCK, 3 * BLOCK), NEG_INF, F32))
            kvh, g = divmod(h, GROUP)
            bias_scr[kvh, g * BLOCK:(g + 1) * BLOCK, :] = tab
        xc_scr[1] = jnp.zeros((R, D_MODEL), BF16)
        attn_scr[1] = jnp.zeros((R, Q_DIM), BF16)

    pen_prev = (r == 0).astype(F32) * NEG_INF
    pen_next = (r == nr - 1).astype(F32) * NEG_INF
    row_head = lax.broadcasted_iota(jnp.int32, (GROUP * BLOCK, 1), 0) // BLOCK

    def cast_x():
        xc_scr[slot] = xa_ref[...].astype(BF16)

    def q_project(kvh):
        cols = slice(kvh * GROUP * HEAD_DIM, (kvh + 1) * GROUP * HEAD_DIM)
        q_scr[:, cols] = (_dot(xc_scr[slot], wq_ref[:, cols]) + bq_ref[:, cols]).astype(BF16)

    def kv_window(col0, kvh, n):
        cols = slice(col0 + kvh * HEAD_DIM, col0 + (kvh + 1) * HEAD_DIM)
        prev = kvm_ref[(n - 1) * BLOCK:n * BLOCK, cols] if n > 0 else kvp_ref[:, cols]
        nxt = kvm_ref[(n + 1) * BLOCK:(n + 2) * BLOCK, cols] if n < nblk - 1 else kvn_ref[:, cols]
        return jnp.concatenate([prev, kvm_ref[n * BLOCK:(n + 1) * BLOCK, cols], nxt], axis=0)

    def scores(kvh, n):
        q4 = jnp.concatenate(
            [q_scr[n * BLOCK:(n + 1) * BLOCK, (kvh * GROUP + g) * HEAD_DIM:(kvh * GROUP + g + 1) * HEAD_DIM]
             for g in range(GROUP)], axis=0)
        k_ext = kv_window(0, kvh, n)
        s = lax.dot_general(q4, k_ext, (((1,), (1,)), ((), ())), preferred_element_type=F32)
        s = s + bias_scr[kvh]
        if n == 0:
            s = jnp.concatenate([s[:, 0:BLOCK] + pen_prev, s[:, BLOCK:]], axis=1)
        if n == nblk - 1:
            s = jnp.concatenate([s[:, 0:2 * BLOCK], s[:, 2 * BLOCK:] + pen_next], axis=1)
        s_scr[(kvh * nblk + n) % 2] = s

    def softmax_pv(kvh, n):
        sink = jnp.zeros((GROUP * BLOCK, 1), F32)
        for g in range(GROUP):
            sink = jnp.where(row_head == g, sink_ref[kvh * GROUP + g], sink)
        s = s_scr[(kvh * nblk + n) % 2]
        mx = jnp.maximum(jnp.max(s, axis=1, keepdims=True), sink)
        pr = jnp.exp(s - mx)
        denom = jnp.sum(pr, axis=1, keepdims=True) + jnp.exp(sink - mx)
        v_ext = kv_window(KV_DIM, kvh, n)
        o = _dot(pr.astype(BF16), v_ext) * (1.0 / denom)
        for g in range(GROUP):
            h = kvh * GROUP + g
            attn_scr[slot, n * BLOCK:(n + 1) * BLOCK, h * HEAD_DIM:(h + 1) * HEAD_DIM] = (
                o[g * BLOCK:(g + 1) * BLOCK].astype(BF16))

    rows4 = GROUP * BLOCK
    attn_items = [(cast_x, 0, _vpu_cycles(R, D_MODEL, 1))]
    for kvh in range(N_KV_HEADS):
        attn_items.append((functools.partial(q_project, kvh), _mxu_cycles(R, D_MODEL, GROUP * HEAD_DIM),
                           _vpu_cycles(R, GROUP * HEAD_DIM, 2)))
    blocks = [(kvh, n) for kvh in range(N_KV_HEADS) for n in range(nblk)]
    score_cost = (_mxu_cycles(rows4, HEAD_DIM, 3 * BLOCK), _vpu_cycles(rows4, 3 * BLOCK, 1))
    pv_cost = (2 * _mxu_cycles(rows4, 3 * BLOCK, HEAD_DIM), _vpu_cycles(rows4, 3 * BLOCK, 8))
    attn_items.append((functools.partial(scores, *blocks[0]),) + score_cost)
    for j, blk in enumerate(blocks):
        if j + 1 < len(blocks):
            attn_items.append((functools.partial(scores, *blocks[j + 1]),) + score_cost)
        attn_items.append((functools.partial(softmax_pv, *blk),) + pv_cost)

    def merge(c):
        cs = slice(c * ATTN_COLS, (c + 1) * ATTN_COLS)
        y_attn = _dot(attn_scr[1 - slot], wao_ref[:, cs])
        gate = _sigmoid_of_twice(_dot(xc_scr[1 - slot], wga_ref[:, cs]) + bga_ref[:, cs])
        merged_scr[:, cs] = (gate * y_attn + m_ref[:, cs].astype(F32)).astype(BF16)

    def out_project(c):
        cs = slice(c * ATTN_COLS, (c + 1) * ATTN_COLS)
        y_scr[:, cs] = alpha * xb_ref[:, cs] + (_dot(merged_scr[...], wo_ref[:, cs]) + bo_ref[:, cs])

    def norm():
        o_ref[...] = _layer_norm(y_scr[...], g1_ref[...], b1_ref[...])

    ncol = D_MODEL // ATTN_COLS
    merge_items = [(functools.partial(merge, c), _mxu_cycles(R, Q_DIM, ATTN_COLS) + _mxu_cycles(R, D_MODEL, ATTN_COLS),
                    _vpu_cycles(R, ATTN_COLS, 6)) for c in range(ncol)]
    merge_items += [(functools.partial(out_project, c), _mxu_cycles(R, D_MODEL, ATTN_COLS),
                     _vpu_cycles(R, ATTN_COLS, 3)) for c in range(ncol)]
    merge_items.append((norm, 0, _vpu_cycles(R, D_MODEL, 12)))
    _interleave(attn_items, merge_items)


def _attn_call(x, m_lru, kv, p, alpha):
    B, S, D = x.shape
    R = ROWS_ATTN
    nr = S // R
    bpr = R // BLOCK
    nkb = S // BLOCK
    n_chunks = B * nr
    cur = lambda i: jnp.minimum(i, n_chunks - 1)
    old = lambda i: jnp.maximum(i - 1, 0)
    main_cur = lambda i: (cur(i) // nr, cur(i) % nr, 0)
    main_old = lambda i: (old(i) // nr, old(i) % nr, 0)
    prev = lambda i: (cur(i) // nr, jnp.maximum((cur(i) % nr) * bpr - 1, 0), 0)
    nxt = lambda i: (cur(i) // nr, jnp.minimum((cur(i) % nr + 1) * bpr, nkb - 1), 0)
    in_specs = [
        pl.BlockSpec((None, R, D), main_cur),
        pl.BlockSpec((None, R, D), main_old),
        pl.BlockSpec((None, R, D), main_old),
        pl.BlockSpec((None, BLOCK, 2 * KV_DIM), prev),
        pl.BlockSpec((None, R, 2 * KV_DIM), main_cur),
        pl.BlockSpec((None, BLOCK, 2 * KV_DIM), nxt),
        _const_spec((BLOCK, 3 * BLOCK)),
        pl.BlockSpec(memory_space=pltpu.SMEM),
        pl.BlockSpec(memory_space=pltpu.SMEM),
        _const_spec((D, Q_DIM)), _const_spec((1, Q_DIM)),
        _const_spec((D, D)), _const_spec((1, D)),
        _const_spec((Q_DIM, D)),
        _const_spec((D, D)), _const_spec((1, D)),
        _const_spec((1, D)), _const_spec((1, D)),
    ]
    scratch = [
        pltpu.VMEM((2, R, D), BF16),
        pltpu.VMEM((R, Q_DIM), BF16),
        pltpu.VMEM((2, GROUP * BLOCK, 3 * BLOCK), F32),
        pltpu.VMEM((2, R, Q_DIM), BF16),
        pltpu.VMEM((R, D), BF16),
        pltpu.VMEM((R, D), F32),
        pltpu.VMEM((N_KV_HEADS, GROUP * BLOCK, 3 * BLOCK), F32),
    ]
    return pl.pallas_call(
        functools.partial(_attn_kernel, nr=nr, alpha=alpha),
        grid=(n_chunks + 1,),
        in_specs=in_specs,
        out_specs=pl.BlockSpec((None, R, D), main_old),
        out_shape=jax.ShapeDtypeStruct((B, S, D), F32),
        scratch_shapes=scratch,
        compiler_params=pltpu.CompilerParams(
            dimension_semantics=("arbitrary",), vmem_limit_bytes=VMEM_LIMIT),
        name="attn_merge",
    )(x, x, m_lru, kv, kv, kv, p["bias_idx"], p["rel_table"], p["sink"],
      p["w_q"], p["b_q"], p["w_mg_attn"], p["b_mg_attn"], p["w_attn_out"], p["w_o"], p["b_o"],
      p["ln1_g"], p["ln1_b"])


def _ffn_kernel(x_ref, xp_ref, xn_ref, wup_ref, bup_ref, wcv_ref, bcv_ref, wdn_ref, bdn_ref, g2_ref, b2_ref,
                o_ref, xe_scr, hs_scr, act_scr, *, nr, alpha):
    R = ROWS_FFN
    H = FFN_HALO
    nt = D_FF // FFN_COLS
    nslab = FFN_COLS // LANES
    r = pl.program_id(1)
    keep_prev = (r > 0).astype(F32)
    keep_next = (r < nr - 1).astype(F32)
    zeros = jnp.zeros((H - SUBLANES, D_MODEL), F32)
    xe_scr[...] = jnp.concatenate(
        [zeros, xp_ref[...] * keep_prev, x_ref[...], xn_ref[...] * keep_next, zeros], axis=0).astype(BF16)

    row = lax.broadcasted_iota(jnp.int32, (SUBLANES, FFN_COLS), 0)
    pad_prev = (row == SUBLANES - 1).astype(F32) * (1.0 - keep_prev)
    pad_next = (row == 0).astype(F32) * (1.0 - keep_next)

    def cols(j, half):
        return slice(half * D_FF + j * FFN_COLS, half * D_FF + (j + 1) * FFN_COLS)

    def up_proj(j):
        xe = xe_scr[...]
        for half in range(2):
            y = _dot(xe, wup_ref[:, cols(j, half)])
            b_up = bup_ref[:, cols(j, half)]
            y = jnp.concatenate(
                [y[0:H - SUBLANES], y[H - SUBLANES:H] - pad_prev * b_up, y[H:H + R],
                 y[H + R:H + R + SUBLANES] - pad_next * b_up, y[H + R + SUBLANES:]], axis=0)
            for c in range(nslab):
                hs_scr[j % 2, half, c] = y[:, c * LANES:(c + 1) * LANES]

    def conv(j, half, c):
        start = cols(j, half).start + c * LANES
        lanes = slice(start, start + LANES)
        w = wcv_ref[:, lanes]
        bias = bcv_ref[:, lanes] + bup_ref[:, lanes] * (w[0:1] + w[1:2] + w[2:3])
        return (w[0:1] * hs_scr[j % 2, half, c, H - 1:H - 1 + R, :]
                + w[1:2] * hs_scr[j % 2, half, c, H:H + R, :]
                + w[2:3] * hs_scr[j % 2, half, c, H + 1:H + 1 + R, :] + bias)

    def gate(j):
        for c in range(nslab):
            act = _gelu(conv(j, 1, c)) * conv(j, 0, c)
            act_scr[:, j * FFN_COLS + c * LANES:j * FFN_COLS + (c + 1) * LANES] = act.astype(BF16)

    per = nt // FFN_DOWN_SPLIT
    up_proj(0)
    for j in range(nt):
        if j + 1 < nt:
            up_proj(j + 1)
        gate(j)
        if (j + 1) % per == 0:
            part = (j + 1) // per - 1
            ks = slice(part * per * FFN_COLS, (part + 1) * per * FFN_COLS)
            contrib = _dot(act_scr[:, ks], wdn_ref[ks, :])
            if part == 0:
                o_ref[...] = contrib + (alpha * x_ref[...] + bdn_ref[...])
            elif part < FFN_DOWN_SPLIT - 1:
                o_ref[...] = o_ref[...] + contrib
            else:
                o_ref[...] = _layer_norm(o_ref[...] + contrib, g2_ref[...], b2_ref[...])


def _ffn_call(x, p, alpha):
    B, S, D = x.shape
    R = ROWS_FFN
    nr = S // R
    rpb = R // SUBLANES
    nsb = S // SUBLANES
    main = lambda b, r: (b, r, 0)
    prev = lambda b, r: (b, jnp.maximum(r * rpb - 1, 0), 0)
    nxt = lambda b, r: (b, jnp.minimum((r + 1) * rpb, nsb - 1), 0)
    in_specs = [
        pl.BlockSpec((None, R, D), main),
        pl.BlockSpec((None, SUBLANES, D), prev),
        pl.BlockSpec((None, SUBLANES, D), nxt),
        _const_spec((D, 2 * D_FF)), _const_spec((1, 2 * D_FF)),
        _const_spec((3, 2 * D_FF)), _const_spec((1, 2 * D_FF)),
        _const_spec((D_FF, D)), _const_spec((1, D)),
        _const_spec((1, D)), _const_spec((1, D)),
    ]
    scratch = [
        pltpu.VMEM((R + 2 * FFN_HALO, D), BF16),
        pltpu.VMEM((2, 2, FFN_COLS // LANES, R + 2 * FFN_HALO, LANES), F32),
        pltpu.VMEM((R, D_FF), BF16),
    ]
    return pl.pallas_call(
        functools.partial(_ffn_kernel, nr=nr, alpha=alpha),
        grid=(B, nr),
        in_specs=in_specs,
        out_specs=pl.BlockSpec((None, R, D), main),
        out_shape=jax.ShapeDtypeStruct((B, S, D), F32),
        scratch_shapes=scratch,
        compiler_params=pltpu.CompilerParams(
            dimension_semantics=("arbitrary", "arbitrary"), vmem_limit_bytes=VMEM_LIMIT_FFN),
        name="ffn",
    )(x, x, x, p["w_up"], p["b_up"], p["w_ffn_conv"], p["b_ffn_conv"], p["w_down"], p["b_down"],
      p["ln2_g"], p["ln2_b"])


def _t5_bucket(rel):
    nb = N_BUCKETS // 2
    ret = jnp.where(rel > 0, nb, 0)
    n = jnp.abs(rel)
    max_exact = nb // 2
    nf = jnp.maximum(n, 1).astype(jnp.float32)
    large = max_exact + (jnp.log(nf / max_exact) / math.log(MAX_DISTANCE / max_exact)
                         * (nb - max_exact)).astype(jnp.int32)
    large = jnp.minimum(large, nb - 1)
    return ret + jnp.where(n < max_exact, n, large)


def _pack_gate_weights(w_a, w_x):
    sel = np.zeros((N_LRU_BLOCKS, LRU_BLOCK, MXU_DIM), np.float32)
    for i in range(N_LRU_BLOCKS):
        lo = LRU_BLOCK * i - GATE_OFF[i]
        sel[i, np.arange(LRU_BLOCK), lo + np.arange(LRU_BLOCK)] = 1.0
    place = lambda w: jnp.einsum("nik,nij,njl->nkl", sel, w, sel, precision=lax.Precision.HIGHEST)
    return jnp.concatenate([place(w_a), place(w_x)], axis=-1).astype(BF16)


def _prepare(rel_table, w_in, b_in, w_lru_conv, b_lru_conv, w_rg_a, b_rg_a, w_rg_x, b_rg_x, lru_lambda,
             w_lru_out, attn_sink, w_attn_out, w_o, b_o, ln1_g, ln1_b, w_up, b_up, w_ffn_conv, b_ffn_conv,
             w_down, b_down, ln2_g, ln2_b):
    C = D_RNN
    o_g, o_q, o_k, o_gl, o_ga = C, 2 * C, 2 * C + Q_DIM, 2 * C + Q_DIM + 2 * KV_DIM, 2 * C + Q_DIM + 2 * KV_DIM + D_MODEL
    row = lambda a: a.reshape(1, -1).astype(F32)
    scale = HEAD_DIM ** -0.5
    pm = _perm_matrix()

    q_off = jnp.arange(BLOCK)[:, None]
    c_off = jnp.arange(3 * BLOCK)[None, :]
    rel = c_off - BLOCK - q_off
    bias_idx = jnp.where(jnp.abs(rel) <= WINDOW, _t5_bucket(rel), -1).astype(jnp.int32)

    return {
        "pm": jnp.asarray(pm, BF16), "pmt": jnp.asarray(pm.T, BF16),
        "w_u": w_in[:, 0:o_g].astype(BF16), "b_u": row(b_in[0:o_g]),
        "w_gl": w_in[:, o_g:o_q].astype(BF16), "b_gl": row(b_in[o_g:o_q]),
        "w_q": (w_in[:, o_q:o_k] * scale).astype(BF16), "b_q": row(b_in[o_q:o_k] * scale),
        "w_kv": w_in[:, o_k:o_gl].astype(BF16), "b_kv": row(b_in[o_k:o_gl]),
        "w_mg_lru": (0.5 * w_in[:, o_gl:o_ga]).astype(BF16), "b_mg_lru": row(0.5 * b_in[o_gl:o_ga]),
        "w_mg_attn": (0.5 * w_in[:, o_ga:]).astype(BF16), "b_mg_attn": row(0.5 * b_in[o_ga:]),
        "w_conv": w_lru_conv.astype(F32), "b_conv": row(b_lru_conv),
        "wg_f": _pack_gate_weights(0.5 * w_rg_a[0], 0.5 * w_rg_x[0]),
        "bg_f": row(0.5 * jnp.concatenate([b_rg_a[0].reshape(-1), b_rg_x[0].reshape(-1)])),
        "wg_b": _pack_gate_weights(0.5 * w_rg_a[1], 0.5 * w_rg_x[1]),
        "bg_b": row(0.5 * jnp.concatenate([b_rg_a[1].reshape(-1), b_rg_x[1].reshape(-1)])),
        "lam_f": row(lru_lambda[0]), "lam_b": row(lru_lambda[1]),
        "w_lru_out": w_lru_out.astype(BF16),
        "bias_idx": bias_idx, "rel_table": rel_table.astype(F32), "sink": attn_sink.astype(F32),
        "w_attn_out": w_attn_out.astype(BF16),
        "w_o": w_o.astype(BF16), "b_o": row(b_o),
        "ln1_g": row(ln1_g), "ln1_b": row(ln1_b),
        "w_up": w_up.astype(BF16), "b_up": row(b_up),
        "w_ffn_conv": w_ffn_conv.astype(F32), "b_ffn_conv": row(b_ffn_conv),
        "w_down": w_down.astype(BF16), "b_down": row(b_down),
        "ln2_g": row(ln2_g), "ln2_b": row(ln2_b),
    }


def _encoder_layer(x, p, alpha):
    h_b, u, kv = _lru_bwd_call(x, p)
    m_lru = _lru_fwd_call(x, u, h_b, p)
    x1 = _attn_call(x, m_lru, kv, p, alpha)
    return _ffn_call(x1, p, alpha)


def kernel(x_prompt, x_sample, rel_table, w_in, b_in, w_lru_conv, b_lru_conv, w_rg_a, b_rg_a, w_rg_x, b_rg_x, lru_lambda, w_lru_out, attn_sink, w_attn_out, w_o, b_o, ln1_g, ln1_b, w_up, b_up, w_ffn_conv, b_ffn_conv, w_down, b_down, ln2_g, ln2_b):
    depth = w_in.shape[0]
    alpha = (2.0 * depth) ** 0.25
    y_prompt, y_sample = x_prompt, x_sample
    for l in range(depth):
        p = _prepare(rel_table, w_in[l], b_in[l], w_lru_conv[l], b_lru_conv[l], w_rg_a[l], b_rg_a[l],
                     w_rg_x[l], b_rg_x[l], lru_lambda[l], w_lru_out[l], attn_sink[l], w_attn_out[l], w_o[l],
                     b_o[l], ln1_g[l], ln1_b[l], w_up[l], b_up[l], w_ffn_conv[l], b_ffn_conv[l], w_down[l],
                     b_down[l], ln2_g[l], ln2_b[l])
        y_prompt = _encoder_layer(y_prompt, p, alpha)
        y_sample = _encoder_layer(y_sample, p, alpha)
    return (y_prompt, y_sample)
```

```python
import functools
import math

import numpy as np
import jax
import jax.numpy as jnp
from jax import lax
from jax.experimental import pallas as pl
from jax.experimental.pallas import tpu as pltpu

F32 = jnp.float32
BF16 = jnp.bfloat16

D_MODEL = 1024
D_RNN = 1280
N_LRU_BLOCKS = 8
LRU_BLOCK = D_RNN // N_LRU_BLOCKS
LRU_C = 8.0
LRU_CONV_W = 4
HEAD_DIM = 128
N_HEADS = 8
N_KV_HEADS = 2
GROUP = N_HEADS // N_KV_HEADS
WINDOW = 128
BLOCK = 128
N_BUCKETS = 32
MAX_DISTANCE = 128
Q_DIM = N_HEADS * HEAD_DIM
KV_DIM = N_KV_HEADS * HEAD_DIM
D_FF = 3 * D_MODEL
LN_EPS = 1e-5
NEG_INF = -1e30
F32_TINY = float(np.finfo(np.float32).tiny)

LANES = 128
SUBLANES = 8
MXU_DIM = 256

N_STREAMS = SUBLANES
T_SEG = 64
PERM_T = MXU_DIM // N_STREAMS
HALO_ROWS = 4 * N_STREAMS
ROWS_ATTN = 512
ATTN_COLS = 512
ROWS_FFN = 512
FFN_COLS = 512
FFN_DOWN_SPLIT = 3
FFN_HALO = 16
LRU_TILE = 64
LRU_COLS = 256
V7X_VMEM_BYTES = 64 * 1024 * 1024
VMEM_LIMIT = V7X_VMEM_BYTES - 8 * 1024 * 1024
VMEM_LIMIT_FFN = V7X_VMEM_BYTES - 4 * 1024 * 1024

GATE_OFF = tuple(LANES * ((LRU_BLOCK * i) // LANES) for i in range(N_LRU_BLOCKS))


def _sigmoid_of_twice(z):
    return 0.5 * jnp.tanh(z) + 0.5


def _gelu(x):
    c = math.sqrt(2.0 / math.pi)
    inner = x * (c + (c * 0.044715) * (x * x))
    return x * (0.5 * jnp.tanh(inner) + 0.5)


def _layer_norm(y, g, b):
    mu = jnp.mean(y, axis=-1, keepdims=True)
    yc = y - mu
    var = jnp.mean(yc * yc, axis=-1, keepdims=True)
    return yc * lax.rsqrt(var + LN_EPS) * g + b


def _dot(a, b):
    return jnp.dot(a, b, preferred_element_type=F32)


def _const_spec(shape):
    nd = len(shape)
    return pl.BlockSpec(shape, lambda *_: (0,) * nd, pipeline_mode=pl.Buffered(1))


def _mxu_cycles(m, k, n):
    return (m // SUBLANES) * (n // LANES) * (-(-k // MXU_DIM))


def _vpu_cycles(rows, cols, ops_per_vreg):
    return rows * cols // (SUBLANES * LANES) * ops_per_vreg // 4


def _interleave(*streams):
    queues = [list(q) for q in streams]
    mxu = vpu = 0
    while any(queues):
        q = min((q for q in queues if q), key=lambda q: abs((mxu + q[0][1]) - (vpu + q[0][2])))
        fn, m, v = q.pop(0)
        fn()
        mxu += m
        vpu += v


def _permute_items(x4_ref, pm_ref, xb_dst, lhs_dst, row0, *, nb, ns):
    T = T_SEG
    R = N_STREAMS * T

    def copy():
        for b in range(nb):
            for sg in range(ns):
                i = b * ns + sg
                xb_dst[i * T:(i + 1) * T, :] = x4_ref[b, sg].astype(BF16)

    def permute(tb):
        src = jnp.concatenate(
            [xb_dst[i * T + PERM_T * tb:i * T + PERM_T * (tb + 1), :] for i in range(N_STREAMS)], axis=0)
        lhs_dst[row0 + MXU_DIM * tb:row0 + MXU_DIM * (tb + 1), :] = _dot(pm_ref[...], src).astype(BF16)

    items = [(copy, 0, _vpu_cycles(R, D_MODEL, 2))]
    for tb in range(T // PERM_T):
        items.append((functools.partial(permute, tb), _mxu_cycles(MXU_DIM, MXU_DIM, D_MODEL),
                      _vpu_cycles(MXU_DIM, D_MODEL, 1)))
    return items


def _input_items(x4_ref, xprev_ref, xnext_ref, hsel_ref, pm_ref, wu_ref, bu_ref, wc_ref, bc_ref,
                 xb_dst, lhs_dst, uall_scr, u_dst, ub_dst, u_out, *, nb, ns, jj, nj):
    T = T_SEG
    R = N_STREAMS * T
    C = D_RNN
    ncol = C // LRU_COLS

    def halo():
        outside = jnp.concatenate([xprev_ref[...].reshape(nb * SUBLANES, D_MODEL),
                                   xnext_ref[...].reshape(nb * SUBLANES, D_MODEL)], axis=0).astype(BF16)
        rows = _dot(hsel_ref[...], outside)
        if ns > 1:
            seg = lax.broadcasted_iota(jnp.int32, (N_STREAMS, D_MODEL), 0) & (ns - 1)
            has_left = (seg > 0).astype(F32)
            has_right = (seg < ns - 1).astype(F32)
            m0 = 2 * N_STREAMS
            tail = lhs_dst[m0 + R - 2 * N_STREAMS:m0 + R, :].astype(F32)
            head = lhs_dst[m0:m0 + 2 * N_STREAMS, :].astype(F32)
            inside = jnp.concatenate(
                [pltpu.roll(tail[0:N_STREAMS], 1, 0) * has_left,
                 pltpu.roll(tail[N_STREAMS:], 1, 0) * has_left,
                 pltpu.roll(head[0:N_STREAMS], N_STREAMS - 1, 0) * has_right,
                 jnp.zeros((N_STREAMS, D_MODEL), F32)], axis=0)
            rows = rows + inside
        rows = rows.astype(BF16)
        lhs_dst[0:2 * N_STREAMS, :] = rows[0:2 * N_STREAMS]
        lhs_dst[2 * N_STREAMS + R:, :] = rows[2 * N_STREAMS:]

    def project(c):
        cs = slice(c * LRU_COLS, (c + 1) * LRU_COLS)
        uall_scr[:, cs] = _dot(lhs_dst[...], wu_ref[:, cs]) + bu_ref[:, cs]
        seg = lax.broadcasted_iota(jnp.int32, (N_STREAMS, LRU_COLS), 0) & (ns - 1)
        keep_prev = 1.0 - (seg == 0).astype(F32) * (jj == 0).astype(F32)
        keep_next = 1.0 - (seg == ns - 1).astype(F32) * (jj == nj - 1).astype(F32)
        prev_rows = slice(0, 2 * N_STREAMS)
        next_rows = slice(2 * N_STREAMS + R, 3 * N_STREAMS + R)
        uall_scr[prev_rows, cs] = uall_scr[prev_rows, cs] * jnp.concatenate([keep_prev, keep_prev], axis=0)
        uall_scr[next_rows, cs] = uall_scr[next_rows, cs] * keep_next

    def conv(c):
        cs = slice(c * LRU_COLS, (c + 1) * LRU_COLS)
        acc = bc_ref[:, cs] + wc_ref[0:1, cs] * uall_scr[0:R, cs]
        for k in range(1, LRU_CONV_W):
            acc = acc + wc_ref[k:k + 1, cs] * uall_scr[N_STREAMS * k:N_STREAMS * k + R, cs]
        u_dst[:, cs] = acc
        ub_dst[:, cs] = acc.astype(BF16)
        u_out[:, cs] = acc.astype(BF16)

    items = _permute_items(x4_ref, pm_ref, xb_dst, lhs_dst, 2 * N_STREAMS, nb=nb, ns=ns)
    items.append((halo, 0, 0))
    project_cost = _mxu_cycles(R + HALO_ROWS, D_MODEL, LRU_COLS)
    conv_cost = _vpu_cycles(R, LRU_COLS, 2 * LRU_CONV_W + 1)
    items.append((functools.partial(project, 0), project_cost, 0))
    for c in range(ncol):
        if c + 1 < ncol:
            items.append((functools.partial(project, c + 1), project_cost, 0))
        items.append((functools.partial(conv, c), 0, conv_cost))
    return items


def _coeff_items(ub_src, u_src, wg_ref, bg_ref, lam_ref, pre_scr):
    R = N_STREAMS * T_SEG
    C = D_RNN

    def coeff(t):
        cols = slice(t * LANES, (t + 1) * LANES)
        xcols = slice(C + t * LANES, C + (t + 1) * LANES)
        lam = lam_ref[:, cols]
        log_sig = jnp.minimum(lam, 0.0) - jnp.log1p(jnp.exp(-jnp.abs(lam)))
        hc = (0.5 * LRU_C) * log_sig
        log_a = hc * jnp.tanh(pre_scr[:, cols]) + hc
        ig = 0.5 * jnp.tanh(pre_scr[:, xcols]) + 0.5
        a = jnp.exp(log_a)
        gain = jnp.tanh(log_a) * (-1.0 - a * a)
        root = gain * lax.rsqrt(jnp.maximum(gain, F32_TINY))
        pre_scr[:, cols] = a
        pre_scr[:, xcols] = root * (ig * u_src[:, cols])

    first_touch = {}
    last_touch = {}
    for i in range(N_LRU_BLOCKS):
        for part in range(MXU_DIM // LANES):
            tile = GATE_OFF[i] // LANES + part
            first_touch.setdefault(tile, i)
            last_touch[tile] = i

    def gate(i):
        off = GATE_OFF[i]
        y = _dot(ub_src[:, off:off + MXU_DIM], wg_ref[i])
        for half in range(2):
            for part in range(MXU_DIM // LANES):
                col = half * C + off + part * LANES
                val = y[:, half * MXU_DIM + part * LANES:half * MXU_DIM + (part + 1) * LANES]
                if first_touch[off // LANES + part] == i:
                    pre_scr[:, col:col + LANES] = val + bg_ref[:, col:col + LANES]
                else:
                    pre_scr[:, col:col + LANES] = pre_scr[:, col:col + LANES] + val

    gate_cost = (_mxu_cycles(R, MXU_DIM, 2 * MXU_DIM), _vpu_cycles(R, 2 * MXU_DIM, 1))
    coeff_cost = _vpu_cycles(R, LANES, 20)
    items = [(functools.partial(gate, 0),) + gate_cost]
    for i in range(N_LRU_BLOCKS):
        if i + 1 < N_LRU_BLOCKS:
            items.append((functools.partial(gate, i + 1),) + gate_cost)
        for t in sorted(t for t, last in last_touch.items() if last == i):
            items.append((functools.partial(coeff, t), 0, coeff_cost))
    return items


def _stage_scan(pre_scr, p_scr, carry_scr, h_dst, *, ns, fwd, first):
    T = T_SEG
    C = D_RNN

    keep = 1.0 - first.astype(F32)
    carry = carry_scr[...] * keep

    def scan_step(t, h, p):
        rows = slice(t * N_STREAMS, (t + 1) * N_STREAMS)
        a = pre_scr[rows, 0:C]
        h = a * h + pre_scr[rows, C:2 * C]
        h_dst[rows, :] = h
        if ns > 1:
            p = p * a
            p_scr[rows, :] = p
        return h, p

    h_end = carry if ns == 1 else jnp.zeros((N_STREAMS, C), F32)
    p_end = jnp.ones((N_STREAMS, C), F32)
    for tt in range(T):
        h_end, p_end = scan_step(tt if fwd else T - 1 - tt, h_end, p_end)

    if ns == 1:
        carry_scr[...] = h_end
        return None

    sub = lax.broadcasted_iota(jnp.int32, (N_STREAMS, C), 0)
    seg = sub & (ns - 1)
    head = (seg == 0) if fwd else (seg == ns - 1)
    from_prev = 1 if fwd else N_STREAMS - 1
    final = jnp.zeros((N_STREAMS, C), F32)
    cin = final
    for _ in range(ns):
        cin = jnp.where(head, carry, pltpu.roll(final, from_prev, 0))
        final = h_end + p_end * cin
    to_head = (N_STREAMS - (ns - 1)) % N_STREAMS if fwd else ns - 1
    carry_scr[...] = pltpu.roll(final, to_head, 0)
    return cin


def _add_carry(h, p, cin):
    shape3 = (h.shape[0] // N_STREAMS, N_STREAMS, h.shape[1])
    return (h.reshape(shape3) + p.reshape(shape3) * cin[None]).reshape(h.shape)


def _lru_bwd_kernel(x4_ref, xprev_ref, xnext_ref, hsel_ref, pm_ref, wu_ref, bu_ref, wc_ref, bc_ref,
                    wg_ref, bg_ref, lam_ref,
                    wkv_ref, bkv_ref,
                    h_ref, u_ref, kv_ref,
                    xb_scr, lhs_scr, uall_scr, u_scr, ub_scr, pre_scr, p_scr, carry_scr,
                    *, nb, ns, nj):
    T = T_SEG
    R = N_STREAMS * T
    s = pl.program_id(1)
    c_in = jnp.minimum(s, nj - 1)
    c_scan = jnp.clip(s - 2, 0, nj - 1)
    slot = s % 2

    @pl.when(s == 0)
    def _():
        pre_scr[...] = jnp.zeros_like(pre_scr)
        carry_scr[...] = jnp.zeros_like(carry_scr)
        u_scr[1] = jnp.zeros((R, D_RNN), F32)
        ub_scr[1] = jnp.zeros((R, D_RNN), BF16)

    cin = _stage_scan(pre_scr, p_scr, carry_scr, h_ref, ns=ns, fwd=False, first=(c_scan == 0))
    if cin is not None:
        for it in range(R // LRU_TILE):
            rows = slice(it * LRU_TILE, (it + 1) * LRU_TILE)
            h_ref[rows, :] = _add_carry(h_ref[rows, :], p_scr[rows, :], cin)

    def kv_project():
        kv = _dot(xb_scr[...], wkv_ref[...]) + bkv_ref[...]
        for b in range(nb):
            for sg in range(ns):
                i = b * ns + sg
                kv_ref[b, sg] = kv[i * T:(i + 1) * T].astype(BF16)

    input_items = _input_items(x4_ref, xprev_ref, xnext_ref, hsel_ref, pm_ref, wu_ref, bu_ref, wc_ref, bc_ref,
                               xb_scr, lhs_scr, uall_scr, u_scr.at[slot], ub_scr.at[slot], u_ref,
                               nb=nb, ns=ns, jj=nj - 1 - c_in, nj=nj)
    input_items.append((kv_project, _mxu_cycles(R, D_MODEL, 2 * KV_DIM), _vpu_cycles(R, 2 * KV_DIM, 2)))
    _interleave(_coeff_items(ub_scr.at[1 - slot], u_scr.at[1 - slot], wg_ref, bg_ref, lam_ref, pre_scr),
                input_items)


def _lru_fwd_kernel(x4_ref, u_ref, hb_ref, pm_ref, pmt_ref, wg_ref, bg_ref,
                    lam_ref, wgl_ref, bgl_ref, wmg_ref, bmg_ref, wout_ref,
                    m_ref,
                    xb_scr, lhs_scr, pre_scr, p_scr, carry_scr,
                    h_scr, g_scr, hg_scr, hgn_scr,
                    *, nb, ns, nj):
    T = T_SEG
    R = N_STREAMS * T
    C = D_RNN
    ncol = C // LRU_COLS
    s = pl.program_id(1)
    c_out = jnp.clip(s - 1, 0, nj - 1)
    slot = s % 2

    @pl.when(s == 0)
    def _():
        pre_scr[...] = jnp.zeros_like(pre_scr)
        carry_scr[...] = jnp.zeros_like(carry_scr)
        xb_scr[1] = jnp.zeros((R, D_MODEL), BF16)
        lhs_scr[1] = jnp.zeros((R, D_MODEL), BF16)

    cin = _stage_scan(pre_scr, p_scr, carry_scr, h_scr, ns=ns, fwd=True, first=(c_out == 0))

    def gate_project(c):
        cs = slice(c * LRU_COLS, (c + 1) * LRU_COLS)
        g_scr[c % 2] = _dot(lhs_scr[1 - slot], wgl_ref[:, cs]) + bgl_ref[:, cs]

    def gate_apply(c):
        cs = slice(c * LRU_COLS, (c + 1) * LRU_COLS)
        h = h_scr[:, cs]
        if cin is not None:
            h = _add_carry(h, p_scr[:, cs], cin[:, cs])
        h = h + hb_ref[:, cs]
        hg_scr[:, cs] = (h * _gelu(g_scr[c % 2])).astype(BF16)

    def unpermute(c):
        cs = slice(c * LRU_COLS, (c + 1) * LRU_COLS)
        for tb in range(T // PERM_T):
            nat = _dot(pmt_ref[...], hg_scr[MXU_DIM * tb:MXU_DIM * (tb + 1), cs]).astype(BF16)
            for i in range(N_STREAMS):
                hgn_scr[i * T + PERM_T * tb:i * T + PERM_T * (tb + 1), cs] = nat[i * PERM_T:(i + 1) * PERM_T]

    def out_project(c):
        cs = slice(c * LRU_COLS, (c + 1) * LRU_COLS)
        y_lru = _dot(hgn_scr[...], wout_ref[:, cs])
        gate = _sigmoid_of_twice(_dot(xb_scr[1 - slot], wmg_ref[:, cs]) + bmg_ref[:, cs])
        m = (gate * y_lru).astype(BF16)
        for b in range(nb):
            for sg in range(ns):
                i = b * ns + sg
                m_ref[b, sg, :, cs] = m[i * T:(i + 1) * T]

    gp_cost = _mxu_cycles(R, D_MODEL, LRU_COLS)
    out_items = [(functools.partial(gate_project, 0), gp_cost, 0)]
    for c in range(ncol):
        if c + 1 < ncol:
            out_items.append((functools.partial(gate_project, c + 1), gp_cost, 0))
        out_items.append((functools.partial(gate_apply, c), 0, _vpu_cycles(R, LRU_COLS, 14)))
        out_items.append((functools.partial(unpermute, c), (T // PERM_T) * _mxu_cycles(MXU_DIM, MXU_DIM, LRU_COLS),
                          _vpu_cycles(R, LRU_COLS, 1)))
    for c in range(D_MODEL // LRU_COLS):
        out_items.append((functools.partial(out_project, c),
                          _mxu_cycles(R, C, LRU_COLS) + _mxu_cycles(R, D_MODEL, LRU_COLS),
                          _vpu_cycles(R, LRU_COLS, 5)))

    in_items = _permute_items(x4_ref, pm_ref, xb_scr.at[slot], lhs_scr.at[slot], 0, nb=nb, ns=ns)
    in_items += _coeff_items(u_ref, u_ref, wg_ref, bg_ref, lam_ref, pre_scr)
    _interleave(out_items, in_items)


def _stream_split(batch):
    nb = math.gcd(batch, N_STREAMS)
    return nb, N_STREAMS // nb


def _halo_select(nb, ns):
    sel = np.zeros((HALO_ROWS, 2 * nb * SUBLANES), np.float32)
    for b in range(nb):
        first, last = b * ns, b * ns + ns - 1
        sel[0 * N_STREAMS + first, b * SUBLANES + SUBLANES - 2] = 1.0
        sel[1 * N_STREAMS + first, b * SUBLANES + SUBLANES - 1] = 1.0
        sel[2 * N_STREAMS + last, nb * SUBLANES + b * SUBLANES] = 1.0
    return sel


def _perm_matrix():
    pm = np.zeros((MXU_DIM, MXU_DIM), np.float32)
    for i in range(N_STREAMS):
        for t in range(PERM_T):
            pm[t * N_STREAMS + i, i * PERM_T + t] = 1.0
    return pm


def _lru_bwd_call(x, p):
    B, S, D = x.shape
    nb, ns = _stream_split(B)
    T = T_SEG
    R = N_STREAMS * T
    Q = S // T
    nj = Q // ns
    x4 = x.reshape(B, Q, T, D)
    x8 = x.reshape(B, S // SUBLANES, SUBLANES, D)
    hsel = jnp.asarray(_halo_select(nb, ns), BF16)
    C = D_RNN
    chunk_in = lambda bb, s: (bb, nj - 1 - jnp.minimum(s, nj - 1), 0, 0)
    chunk_scan = lambda bb, s: (bb, nj - 1 - jnp.clip(s - 2, 0, nj - 1), 0, 0)
    blocks_per_chunk = ns * T // SUBLANES
    before = lambda bb, s: (bb, jnp.maximum(chunk_in(bb, s)[1] * blocks_per_chunk - 1, 0), 0, 0)
    after = lambda bb, s: (bb, jnp.minimum((chunk_in(bb, s)[1] + 1) * blocks_per_chunk, S // SUBLANES - 1), 0, 0)
    in_specs = [
        pl.BlockSpec((nb, ns, T, D), chunk_in),
        pl.BlockSpec((nb, None, SUBLANES, D), before),
        pl.BlockSpec((nb, None, SUBLANES, D), after),
        _const_spec((HALO_ROWS, 2 * nb * SUBLANES)),
        _const_spec((MXU_DIM, MXU_DIM)),
        _const_spec((D, C)), _const_spec((1, C)),
        _const_spec((LRU_CONV_W, C)), _const_spec((1, C)),
        _const_spec((N_LRU_BLOCKS, MXU_DIM, 2 * MXU_DIM)), _const_spec((1, 2 * C)),
        _const_spec((1, C)),
        _const_spec((D, 2 * KV_DIM)), _const_spec((1, 2 * KV_DIM)),
    ]
    out_shape = [
        jax.ShapeDtypeStruct((B // nb, nj, R, C), F32),
        jax.ShapeDtypeStruct((B // nb, nj, R, C), BF16),
        jax.ShapeDtypeStruct((B, Q, T, 2 * KV_DIM), BF16),
    ]
    out_specs = [
        pl.BlockSpec((None, None, R, C), chunk_scan),
        pl.BlockSpec((None, None, R, C), chunk_in),
        pl.BlockSpec((nb, ns, T, 2 * KV_DIM), chunk_in),
    ]
    scratch = [
        pltpu.VMEM((R, D), BF16),
        pltpu.VMEM((R + HALO_ROWS, D), BF16),
        pltpu.VMEM((R + HALO_ROWS, C), F32),
        pltpu.VMEM((2, R, C), F32),
        pltpu.VMEM((2, R, C), BF16),
        pltpu.VMEM((R, 2 * C), F32),
        pltpu.VMEM((R, C), F32),
        pltpu.VMEM((N_STREAMS, C), F32),
    ]
    h_b, u, kv = pl.pallas_call(
        functools.partial(_lru_bwd_kernel, nb=nb, ns=ns, nj=nj),
        grid=(B // nb, nj + 2),
        in_specs=in_specs, out_specs=out_specs, out_shape=out_shape,
        scratch_shapes=scratch,
        compiler_params=pltpu.CompilerParams(
            dimension_semantics=("arbitrary", "arbitrary"), vmem_limit_bytes=VMEM_LIMIT),
        name="lru_bwd",
    )(x4, x8, x8, hsel, p["pm"], p["w_u"], p["b_u"], p["w_conv"], p["b_conv"], p["wg_b"], p["bg_b"], p["lam_b"],
      p["w_kv"], p["b_kv"])
    return h_b, u, kv.reshape(B, S, 2 * KV_DIM)


def _lru_fwd_call(x, u, h_b, p):
    B, S, D = x.shape
    nb, ns = _stream_split(B)
    T = T_SEG
    R = N_STREAMS * T
    Q = S // T
    nj = Q // ns
    x4 = x.reshape(B, Q, T, D)
    C = D_RNN
    chunk_in = lambda bb, s: (bb, jnp.minimum(s, nj - 1), 0, 0)
    chunk_out = lambda bb, s: (bb, jnp.clip(s - 1, 0, nj - 1), 0, 0)
    in_specs = [
        pl.BlockSpec((nb, ns, T, D), chunk_in),
        pl.BlockSpec((None, None, R, C), chunk_in),
        pl.BlockSpec((None, None, R, C), chunk_out),
        _const_spec((MXU_DIM, MXU_DIM)), _const_spec((MXU_DIM, MXU_DIM)),
        _const_spec((N_LRU_BLOCKS, MXU_DIM, 2 * MXU_DIM)), _const_spec((1, 2 * C)),
        _const_spec((1, C)),
        _const_spec((D, C)), _const_spec((1, C)),
        _const_spec((D, D)), _const_spec((1, D)),
        _const_spec((C, D)),
    ]
    scratch = [
        pltpu.VMEM((2, R, D), BF16),
        pltpu.VMEM((2, R, D), BF16),
        pltpu.VMEM((R, 2 * C), F32),
        pltpu.VMEM((R, C), F32),
        pltpu.VMEM((N_STREAMS, C), F32),
        pltpu.VMEM((R, C), F32),
        pltpu.VMEM((2, R, LRU_COLS), F32),
        pltpu.VMEM((R, C), BF16),
        pltpu.VMEM((R, C), BF16),
    ]
    m = pl.pallas_call(
        functools.partial(_lru_fwd_kernel, nb=nb, ns=ns, nj=nj),
        grid=(B // nb, nj + 1),
        in_specs=in_specs,
        out_specs=pl.BlockSpec((nb, ns, T, D), chunk_out),
        out_shape=jax.ShapeDtypeStruct((B, Q, T, D), BF16),
        scratch_shapes=scratch,
        compiler_params=pltpu.CompilerParams(
            dimension_semantics=("arbitrary", "arbitrary"), vmem_limit_bytes=VMEM_LIMIT),
        name="lru_fwd",
    )(x4, u, h_b, p["pm"], p["pmt"], p["wg_f"], p["bg_f"],
      p["lam_f"], p["w_gl"], p["b_gl"], p["w_mg_lru"], p["b_mg_lru"], p["w_lru_out"])
    return m.reshape(B, S, D)


def _attn_kernel(xa_ref, xb_ref, m_ref, kvp_ref, kvm_ref, kvn_ref, idx_ref, rel_ref, sink_ref,
                 wq_ref, bq_ref, wga_ref, bga_ref, wao_ref, wo_ref, bo_ref, g1_ref, b1_ref,
                 o_ref, xc_scr, q_scr, s_scr, attn_scr, merged_scr, y_scr, bias_scr, *, nr, alpha):
    R = ROWS_ATTN
    nblk = R // BLOCK
    i = pl.program_id(0)
    n_chunks = pl.num_programs(0) - 1
    r = lax.rem(jnp.minimum(i, n_chunks - 1), nr)
    slot = i % 2

    @pl.when(i == 0)
    def _():
        idx = idx_ref[...]
        for h in range(N_HEADS):
            def pick(bkt, acc):
                return jnp.where(idx == bkt, rel_ref[bkt, h], acc)
            tab = lax.fori_loop(0, N_BUCKETS, pick, jnp.full((BLOCK, 3 * BLOCK), NEG_INF, F32))
            kvh, g = divmod(h, GROUP)
            bias_scr[kvh, g * BLOCK:(g + 1) * BLOCK, :] = tab
        xc_scr[1] = jnp.zeros((R, D_MODEL), BF16)
        attn_scr[1] = jnp.zeros((R, Q_DIM), BF16)

    pen_prev = (r == 0).astype(F32) * NEG_INF
    pen_next = (r == nr - 1).astype(F32) * NEG_INF
    row_head = lax.broadcasted_iota(jnp.int32, (GROUP * BLOCK, 1), 0) // BLOCK

    def cast_x():
        xc_scr[slot] = xa_ref[...].astype(BF16)

    def q_project(kvh):
        cols = slice(kvh * GROUP * HEAD_DIM, (kvh + 1) * GROUP * HEAD_DIM)
        q_scr[:, cols] = (_dot(xc_scr[slot], wq_ref[:, cols]) + bq_ref[:, cols]).astype(BF16)

    def kv_window(col0, kvh, n):
        cols = slice(col0 + kvh * HEAD_DIM, col0 + (kvh + 1) * HEAD_DIM)
        prev = kvm_ref[(n - 1) * BLOCK:n * BLOCK, cols] if n > 0 else kvp_ref[:, cols]
        nxt = kvm_ref[(n + 1) * BLOCK:(n + 2) * BLOCK, cols] if n < nblk - 1 else kvn_ref[:, cols]
        return jnp.concatenate([prev, kvm_ref[n * BLOCK:(n + 1) * BLOCK, cols], nxt], axis=0)

    def scores(kvh, n):
        q4 = jnp.concatenate(
            [q_scr[n * BLOCK:(n + 1) * BLOCK, (kvh * GROUP + g) * HEAD_DIM:(kvh * GROUP + g + 1) * HEAD_DIM]
             for g in range(GROUP)], axis=0)
        k_ext = kv_window(0, kvh, n)
        s = lax.dot_general(q4, k_ext, (((1,), (1,)), ((), ())), preferred_element_type=F32)
        s = s + bias_scr[kvh]
        if n == 0:
            s = jnp.concatenate([s[:, 0:BLOCK] + pen_prev, s[:, BLOCK:]], axis=1)
        if n == nblk - 1:
            s = jnp.concatenate([s[:, 0:2 * BLOCK], s[:, 2 * BLOCK:] + pen_next], axis=1)
        s_scr[(kvh * nblk + n) % 2] = s

    def softmax_pv(kvh, n):
        sink = jnp.zeros((GROUP * BLOCK, 1), F32)
        for g in range(GROUP):
            sink = jnp.where(row_head == g, sink_ref[kvh * GROUP + g], sink)
        s = s_scr[(kvh * nblk + n) % 2]
        mx = jnp.maximum(jnp.max(s, axis=1, keepdims=True), sink)
        pr = jnp.exp(s - mx)
        denom = jnp.sum(pr, axis=1, keepdims=True) + jnp.exp(sink - mx)
        v_ext = kv_window(KV_DIM, kvh, n)
        o = _dot(pr.astype(BF16), v_ext) * (1.0 / denom)
        for g in range(GROUP):
            h = kvh * GROUP + g
            attn_scr[slot, n * BLOCK:(n + 1) * BLOCK, h * HEAD_DIM:(h + 1) * HEAD_DIM] = (
                o[g * BLOCK:(g + 1) * BLOCK].astype(BF16))

    rows4 = GROUP * BLOCK
    attn_items = [(cast_x, 0, _vpu_cycles(R, D_MODEL, 1))]
    for kvh in range(N_KV_HEADS):
        attn_items.append((functools.partial(q_project, kvh), _mxu_cycles(R, D_MODEL, GROUP * HEAD_DIM),
                           _vpu_cycles(R, GROUP * HEAD_DIM, 2)))
    blocks = [(kvh, n) for kvh in range(N_KV_HEADS) for n in range(nblk)]
    score_cost = (_mxu_cycles(rows4, HEAD_DIM, 3 * BLOCK), _vpu_cycles(rows4, 3 * BLOCK, 1))
    pv_cost = (2 * _mxu_cycles(rows4, 3 * BLOCK, HEAD_DIM), _vpu_cycles(rows4, 3 * BLOCK, 8))
    attn_items.append((functools.partial(scores, *blocks[0]),) + score_cost)
    for j, blk in enumerate(blocks):
        if j + 1 < len(blocks):
            attn_items.append((functools.partial(scores, *blocks[j + 1]),) + score_cost)
        attn_items.append((functools.partial(softmax_pv, *blk),) + pv_cost)

    def merge(c):
        cs = slice(c * ATTN_COLS, (c + 1) * ATTN_COLS)
        y_attn = _dot(attn_scr[1 - slot], wao_ref[:, cs])
        gate = _sigmoid_of_twice(_dot(xc_scr[1 - slot], wga_ref[:, cs]) + bga_ref[:, cs])
        merged_scr[:, cs] = (gate * y_attn + m_ref[:, cs].astype(F32)).astype(BF16)

    def out_project(c):
        cs = slice(c * ATTN_COLS, (c + 1) * ATTN_COLS)
        y_scr[:, cs] = alpha * xb_ref[:, cs] + (_dot(merged_scr[...], wo_ref[:, cs]) + bo_ref[:, cs])

    def norm():
        o_ref[...] = _layer_norm(y_scr[...], g1_ref[...], b1_ref[...])

    ncol = D_MODEL // ATTN_COLS
    merge_items = [(functools.partial(merge, c), _mxu_cycles(R, Q_DIM, ATTN_COLS) + _mxu_cycles(R, D_MODEL, ATTN_COLS),
                    _vpu_cycles(R, ATTN_COLS, 6)) for c in range(ncol)]
    merge_items += [(functools.partial(out_project, c), _mxu_cycles(R, D_MODEL, ATTN_COLS),
                     _vpu_cycles(R, ATTN_COLS, 3)) for c in range(ncol)]
    merge_items.append((norm, 0, _vpu_cycles(R, D_MODEL, 12)))
    _interleave(attn_items, merge_items)


def _attn_call(x, m_lru, kv, p, alpha):
    B, S, D = x.shape
    R = ROWS_ATTN
    nr = S // R
    bpr = R // BLOCK
    nkb = S // BLOCK
    n_chunks = B * nr
    cur = lambda i: jnp.minimum(i, n_chunks - 1)
    old = lambda i: jnp.maximum(i - 1, 0)
    main_cur = lambda i: (cur(i) // nr, cur(i) % nr, 0)
    main_old = lambda i: (old(i) // nr, old(i) % nr, 0)
    prev = lambda i: (cur(i) // nr, jnp.maximum((cur(i) % nr) * bpr - 1, 0), 0)
    nxt = lambda i: (cur(i) // nr, jnp.minimum((cur(i) % nr + 1) * bpr, nkb - 1), 0)
    in_specs = [
        pl.BlockSpec((None, R, D), main_cur),
        pl.BlockSpec((None, R, D), main_old),
        pl.BlockSpec((None, R, D), main_old),
        pl.BlockSpec((None, BLOCK, 2 * KV_DIM), prev),
        pl.BlockSpec((None, R, 2 * KV_DIM), main_cur),
        pl.BlockSpec((None, BLOCK, 2 * KV_DIM), nxt),
        _const_spec((BLOCK, 3 * BLOCK)),
        pl.BlockSpec(memory_space=pltpu.SMEM),
        pl.BlockSpec(memory_space=pltpu.SMEM),
        _const_spec((D, Q_DIM)), _const_spec((1, Q_DIM)),
        _const_spec((D, D)), _const_spec((1, D)),
        _const_spec((Q_DIM, D)),
        _const_spec((D, D)), _const_spec((1, D)),
        _const_spec((1, D)), _const_spec((1, D)),
    ]
    scratch = [
        pltpu.VMEM((2, R, D), BF16),
        pltpu.VMEM((R, Q_DIM), BF16),
        pltpu.VMEM((2, GROUP * BLOCK, 3 * BLOCK), F32),
        pltpu.VMEM((2, R, Q_DIM), BF16),
        pltpu.VMEM((R, D), BF16),
        pltpu.VMEM((R, D), F32),
        pltpu.VMEM((N_KV_HEADS, GROUP * BLOCK, 3 * BLOCK), F32),
    ]
    return pl.pallas_call(
        functools.partial(_attn_kernel, nr=nr, alpha=alpha),
        grid=(n_chunks + 1,),
        in_specs=in_specs,
        out_specs=pl.BlockSpec((None, R, D), main_old),
        out_shape=jax.ShapeDtypeStruct((B, S, D), F32),
        scratch_shapes=scratch,
        compiler_params=pltpu.CompilerParams(
            dimension_semantics=("arbitrary",), vmem_limit_bytes=VMEM_LIMIT),
        name="attn_merge",
    )(x, x, m_lru, kv, kv, kv, p["bias_idx"], p["rel_table"], p["sink"],
      p["w_q"], p["b_q"], p["w_mg_attn"], p["b_mg_attn"], p["w_attn_out"], p["w_o"], p["b_o"],
      p["ln1_g"], p["ln1_b"])


def _ffn_kernel(x_ref, xp_ref, xn_ref, xo_ref, wup_ref, bup_ref, wcv_ref, bcv_ref, wdn_ref, bdn_ref, g2_ref, b2_ref,
                o_ref, xe_scr, hs_scr, act_scr, acc_scr, *, nr, alpha):
    R = ROWS_FFN
    H = FFN_HALO
    nt = D_FF // FFN_COLS
    nslab = FFN_COLS // LANES
    i = pl.program_id(0)
    r = lax.rem(jnp.minimum(i, pl.num_programs(0) - 2), nr)
    slot = i % 2

    @pl.when(i == 0)
    def _():
        act_scr[1] = jnp.zeros((R, D_FF), BF16)

    keep_prev = (r > 0).astype(F32)
    keep_next = (r < nr - 1).astype(F32)
    zeros = jnp.zeros((H - SUBLANES, D_MODEL), F32)
    xe_scr[...] = jnp.concatenate(
        [zeros, xp_ref[...] * keep_prev, x_ref[...], xn_ref[...] * keep_next, zeros], axis=0).astype(BF16)

    row = lax.broadcasted_iota(jnp.int32, (SUBLANES, FFN_COLS), 0)
    pad_prev = (row == SUBLANES - 1).astype(F32) * (1.0 - keep_prev)
    pad_next = (row == 0).astype(F32) * (1.0 - keep_next)

    def cols(j, half):
        return slice(half * D_FF + j * FFN_COLS, half * D_FF + (j + 1) * FFN_COLS)

    def up_proj(j):
        xe = xe_scr[...]
        for half in range(2):
            y = _dot(xe, wup_ref[:, cols(j, half)])
            b_up = bup_ref[:, cols(j, half)]
            y = jnp.concatenate(
                [y[0:H - SUBLANES], y[H - SUBLANES:H] - pad_prev * b_up, y[H:H + R],
                 y[H + R:H + R + SUBLANES] - pad_next * b_up, y[H + R + SUBLANES:]], axis=0)
            for c in range(nslab):
                hs_scr[j % 2, half, c] = y[:, c * LANES:(c + 1) * LANES]

    def conv(j, half, c):
        start = cols(j, half).start + c * LANES
        lanes = slice(start, start + LANES)
        w = wcv_ref[:, lanes]
        bias = bcv_ref[:, lanes] + bup_ref[:, lanes] * (w[0:1] + w[1:2] + w[2:3])
        return (w[0:1] * hs_scr[j % 2, half, c, H - 1:H - 1 + R, :]
                + w[1:2] * hs_scr[j % 2, half, c, H:H + R, :]
                + w[2:3] * hs_scr[j % 2, half, c, H + 1:H + 1 + R, :] + bias)

    def gate(j):
        for c in range(nslab):
            act = _gelu(conv(j, 1, c)) * conv(j, 0, c)
            act_scr[slot, :, j * FFN_COLS + c * LANES:j * FFN_COLS + (c + 1) * LANES] = act.astype(BF16)

    per_k = D_FF // FFN_DOWN_SPLIT

    def down(part):
        ks = slice(part * per_k, (part + 1) * per_k)
        contrib = _dot(act_scr[1 - slot, :, ks], wdn_ref[ks, :])
        if part == 0:
            acc_scr[...] = contrib + (alpha * xo_ref[...] + bdn_ref[...])
        elif part < FFN_DOWN_SPLIT - 1:
            acc_scr[...] = acc_scr[...] + contrib
        else:
            o_ref[...] = _layer_norm(acc_scr[...] + contrib, g2_ref[...], b2_ref[...])

    up_cost = (2 * _mxu_cycles(R + 2 * H, D_MODEL, FFN_COLS), _vpu_cycles(R, FFN_COLS, 2))
    gate_cost = (0, _vpu_cycles(R, FFN_COLS, 22))
    up_items = [(functools.partial(up_proj, 0),) + up_cost]
    for j in range(nt):
        if j + 1 < nt:
            up_items.append((functools.partial(up_proj, j + 1),) + up_cost)
        up_items.append((functools.partial(gate, j),) + gate_cost)
    down_items = [(functools.partial(down, part), _mxu_cycles(R, per_k, D_MODEL), _vpu_cycles(R, D_MODEL, 4))
                  for part in range(FFN_DOWN_SPLIT)]
    _interleave(up_items, down_items)


def _ffn_call(x, p, alpha):
    B, S, D = x.shape
    R = ROWS_FFN
    nr = S // R
    rpb = R // SUBLANES
    nsb = S // SUBLANES
    n_chunks = B * nr
    cur = lambda i: jnp.minimum(i, n_chunks - 1)
    old = lambda i: jnp.maximum(i - 1, 0)
    main = lambda i: (cur(i) // nr, cur(i) % nr, 0)
    main_old = lambda i: (old(i) // nr, old(i) % nr, 0)
    prev = lambda i: (cur(i) // nr, jnp.maximum((cur(i) % nr) * rpb - 1, 0), 0)
    nxt = lambda i: (cur(i) // nr, jnp.minimum((cur(i) % nr + 1) * rpb, nsb - 1), 0)
    in_specs = [
        pl.BlockSpec((None, R, D), main),
        pl.BlockSpec((None, SUBLANES, D), prev),
        pl.BlockSpec((None, SUBLANES, D), nxt),
        pl.BlockSpec((None, R, D), main_old),
        _const_spec((D, 2 * D_FF)), _const_spec((1, 2 * D_FF)),
        _const_spec((3, 2 * D_FF)), _const_spec((1, 2 * D_FF)),
        _const_spec((D_FF, D)), _const_spec((1, D)),
        _const_spec((1, D)), _const_spec((1, D)),
    ]
    scratch = [
        pltpu.VMEM((R + 2 * FFN_HALO, D), BF16),
        pltpu.VMEM((2, 2, FFN_COLS // LANES, R + 2 * FFN_HALO, LANES), F32),
        pltpu.VMEM((2, R, D_FF), BF16),
        pltpu.VMEM((R, D), F32),
    ]
    return pl.pallas_call(
        functools.partial(_ffn_kernel, nr=nr, alpha=alpha),
        grid=(n_chunks + 1,),
        in_specs=in_specs,
        out_specs=pl.BlockSpec((None, R, D), main_old),
        out_shape=jax.ShapeDtypeStruct((B, S, D), F32),
        scratch_shapes=scratch,
        compiler_params=pltpu.CompilerParams(
            dimension_semantics=("arbitrary",), vmem_limit_bytes=VMEM_LIMIT_FFN),
        name="ffn",
    )(x, x, x, x, p["w_up"], p["b_up"], p["w_ffn_conv"], p["b_ffn_conv"], p["w_down"], p["b_down"],
      p["ln2_g"], p["ln2_b"])


def _t5_bucket(rel):
    nb = N_BUCKETS // 2
    ret = jnp.where(rel > 0, nb, 0)
    n = jnp.abs(rel)
    max_exact = nb // 2
    nf = jnp.maximum(n, 1).astype(jnp.float32)
    large = max_exact + (jnp.log(nf / max_exact) / math.log(MAX_DISTANCE / max_exact)
                         * (nb - max_exact)).astype(jnp.int32)
    large = jnp.minimum(large, nb - 1)
    return ret + jnp.where(n < max_exact, n, large)


def _pack_gate_weights(w_a, w_x):
    sel = np.zeros((N_LRU_BLOCKS, LRU_BLOCK, MXU_DIM), np.float32)
    for i in range(N_LRU_BLOCKS):
        lo = LRU_BLOCK * i - GATE_OFF[i]
        sel[i, np.arange(LRU_BLOCK), lo + np.arange(LRU_BLOCK)] = 1.0
    place = lambda w: jnp.einsum("nik,nij,njl->nkl", sel, w, sel, precision=lax.Precision.HIGHEST)
    return jnp.concatenate([place(w_a), place(w_x)], axis=-1).astype(BF16)


def _prepare(rel_table, w_in, b_in, w_lru_conv, b_lru_conv, w_rg_a, b_rg_a, w_rg_x, b_rg_x, lru_lambda,
             w_lru_out, attn_sink, w_attn_out, w_o, b_o, ln1_g, ln1_b, w_up, b_up, w_ffn_conv, b_ffn_conv,
             w_down, b_down, ln2_g, ln2_b):
    C = D_RNN
    o_g, o_q, o_k, o_gl, o_ga = C, 2 * C, 2 * C + Q_DIM, 2 * C + Q_DIM + 2 * KV_DIM, 2 * C + Q_DIM + 2 * KV_DIM + D_MODEL
    row = lambda a: a.reshape(1, -1).astype(F32)
    scale = HEAD_DIM ** -0.5
    pm = _perm_matrix()

    q_off = jnp.arange(BLOCK)[:, None]
    c_off = jnp.arange(3 * BLOCK)[None, :]
    rel = c_off - BLOCK - q_off
    bias_idx = jnp.where(jnp.abs(rel) <= WINDOW, _t5_bucket(rel), -1).astype(jnp.int32)

    return {
        "pm": jnp.asarray(pm, BF16), "pmt": jnp.asarray(pm.T, BF16),
        "w_u": w_in[:, 0:o_g].astype(BF16), "b_u": row(b_in[0:o_g]),
        "w_gl": w_in[:, o_g:o_q].astype(BF16), "b_gl": row(b_in[o_g:o_q]),
        "w_q": (w_in[:, o_q:o_k] * scale).astype(BF16), "b_q": row(b_in[o_q:o_k] * scale),
        "w_kv": w_in[:, o_k:o_gl].astype(BF16), "b_kv": row(b_in[o_k:o_gl]),
        "w_mg_lru": (0.5 * w_in[:, o_gl:o_ga]).astype(BF16), "b_mg_lru": row(0.5 * b_in[o_gl:o_ga]),
        "w_mg_attn": (0.5 * w_in[:, o_ga:]).astype(BF16), "b_mg_attn": row(0.5 * b_in[o_ga:]),
        "w_conv": w_lru_conv.astype(F32), "b_conv": row(b_lru_conv),
        "wg_f": _pack_gate_weights(0.5 * w_rg_a[0], 0.5 * w_rg_x[0]),
        "bg_f": row(0.5 * jnp.concatenate([b_rg_a[0].reshape(-1), b_rg_x[0].reshape(-1)])),
        "wg_b": _pack_gate_weights(0.5 * w_rg_a[1], 0.5 * w_rg_x[1]),
        "bg_b": row(0.5 * jnp.concatenate([b_rg_a[1].reshape(-1), b_rg_x[1].reshape(-1)])),
        "lam_f": row(lru_lambda[0]), "lam_b": row(lru_lambda[1]),
        "w_lru_out": w_lru_out.astype(BF16),
        "bias_idx": bias_idx, "rel_table": rel_table.astype(F32), "sink": attn_sink.astype(F32),
        "w_attn_out": w_attn_out.astype(BF16),
        "w_o": w_o.astype(BF16), "b_o": row(b_o),
        "ln1_g": row(ln1_g), "ln1_b": row(ln1_b),
        "w_up": w_up.astype(BF16), "b_up": row(b_up),
        "w_ffn_conv": w_ffn_conv.astype(F32), "b_ffn_conv": row(b_ffn_conv),
        "w_down": w_down.astype(BF16), "b_down": row(b_down),
        "ln2_g": row(ln2_g), "ln2_b": row(ln2_b),
    }


def _encoder_layer(x, p, alpha):
    h_b, u, kv = _lru_bwd_call(x, p)
    m_lru = _lru_fwd_call(x, u, h_b, p)
    x1 = _attn_call(x, m_lru, kv, p, alpha)
    return _ffn_call(x1, p, alpha)


def kernel(x_prompt, x_sample, rel_table, w_in, b_in, w_lru_conv, b_lru_conv, w_rg_a, b_rg_a, w_rg_x, b_rg_x, lru_lambda, w_lru_out, attn_sink, w_attn_out, w_o, b_o, ln1_g, ln1_b, w_up, b_up, w_ffn_conv, b_ffn_conv, w_down, b_down, ln2_g, ln2_b):
    depth = w_in.shape[0]
    alpha = (2.0 * depth) ** 0.25
    y_prompt, y_sample = x_prompt, x_sample
    for l in range(depth):
        p = _prepare(rel_table, w_in[l], b_in[l], w_lru_conv[l], b_lru_conv[l], w_rg_a[l], b_rg_a[l],
                     w_rg_x[l], b_rg_x[l], lru_lambda[l], w_lru_out[l], attn_sink[l], w_attn_out[l], w_o[l],
                     b_o[l], ln1_g[l], ln1_b[l], w_up[l], b_up[l], w_ffn_conv[l], b_ffn_conv[l], w_down[l],
                     b_down[l], ln2_g[l], ln2_b[l])
        y_prompt = _encoder_layer(y_prompt, p, alpha)
        y_sample = _encoder_layer(y_sample, p, alpha)
    return (y_prompt, y_sample)
```

```python
import functools
import math

import numpy as np
import jax
import jax.numpy as jnp
from jax import lax
from jax.experimental import pallas as pl
from jax.experimental.pallas import tpu as pltpu

F32 = jnp.float32
BF16 = jnp.bfloat16

D_MODEL = 1024
D_RNN = 1280
N_LRU_BLOCKS = 8
LRU_BLOCK = D_RNN // N_LRU_BLOCKS
LRU_C = 8.0
LRU_CONV_W = 4
HEAD_DIM = 128
N_HEADS = 8
N_KV_HEADS = 2
GROUP = N_HEADS // N_KV_HEADS
WINDOW = 128
BLOCK = 128
N_BUCKETS = 32
MAX_DISTANCE = 128
Q_DIM = N_HEADS * HEAD_DIM
KV_DIM = N_KV_HEADS * HEAD_DIM
D_FF = 3 * D_MODEL
LN_EPS = 1e-5
NEG_INF = -1e30
F32_TINY = float(np.finfo(np.float32).tiny)

LANES = 128
SUBLANES = 8
MXU_DIM = 256

N_STREAMS = SUBLANES
T_SEG = 64
PERM_T = MXU_DIM // N_STREAMS
HALO_ROWS = 4 * N_STREAMS
ROWS_ATTN = 512
ATTN_COLS = 512
ROWS_FFN = 1024
FFN_COLS = 512
FFN_DOWN_SPLIT = 3
FFN_HALO = 16
LRU_TILE = 64
LRU_COLS = 256
V7X_VMEM_BYTES = 64 * 1024 * 1024
VMEM_LIMIT = V7X_VMEM_BYTES - 8 * 1024 * 1024
VMEM_LIMIT_FFN = V7X_VMEM_BYTES - 4 * 1024 * 1024

GATE_OFF = tuple(LANES * ((LRU_BLOCK * i) // LANES) for i in range(N_LRU_BLOCKS))


def _sigmoid_of_twice(z):
    return 0.5 * jnp.tanh(z) + 0.5


def _gelu(x):
    c = math.sqrt(2.0 / math.pi)
    inner = x * (c + (c * 0.044715) * (x * x))
    return x * (0.5 * jnp.tanh(inner) + 0.5)


def _layer_norm(y, g, b):
    mu = jnp.mean(y, axis=-1, keepdims=True)
    yc = y - mu
    var = jnp.mean(yc * yc, axis=-1, keepdims=True)
    return yc * lax.rsqrt(var + LN_EPS) * g + b


def _dot(a, b):
    return jnp.dot(a, b, preferred_element_type=F32)


def _const_spec(shape):
    nd = len(shape)
    return pl.BlockSpec(shape, lambda *_: (0,) * nd, pipeline_mode=pl.Buffered(1))


def _mxu_cycles(m, k, n):
    return (m // SUBLANES) * (n // LANES) * (-(-k // MXU_DIM))


def _vpu_cycles(rows, cols, ops_per_vreg):
    return rows * cols // (SUBLANES * LANES) * ops_per_vreg // 4


def _interleave(*streams):
    queues = [list(q) for q in streams]
    mxu = vpu = 0
    while any(queues):
        q = min((q for q in queues if q), key=lambda q: abs((mxu + q[0][1]) - (vpu + q[0][2])))
        fn, m, v = q.pop(0)
        fn()
        mxu += m
        vpu += v


def _permute_items(x4_ref, pm_ref, xb_dst, lhs_dst, row0, *, nb, ns):
    T = T_SEG
    R = N_STREAMS * T

    def copy():
        for b in range(nb):
            for sg in range(ns):
                i = b * ns + sg
                xb_dst[i * T:(i + 1) * T, :] = x4_ref[b, sg].astype(BF16)

    def permute(tb):
        src = jnp.concatenate(
            [xb_dst[i * T + PERM_T * tb:i * T + PERM_T * (tb + 1), :] for i in range(N_STREAMS)], axis=0)
        lhs_dst[row0 + MXU_DIM * tb:row0 + MXU_DIM * (tb + 1), :] = _dot(pm_ref[...], src).astype(BF16)

    items = [(copy, 0, _vpu_cycles(R, D_MODEL, 2))]
    for tb in range(T // PERM_T):
        items.append((functools.partial(permute, tb), _mxu_cycles(MXU_DIM, MXU_DIM, D_MODEL),
                      _vpu_cycles(MXU_DIM, D_MODEL, 1)))
    return items


def _input_items(x4_ref, xprev_ref, xnext_ref, hsel_ref, pm_ref, wu_ref, bu_ref, wc_ref, bc_ref,
                 xb_dst, lhs_dst, uall_scr, u_dst, ub_dst, u_out, *, nb, ns, jj, nj):
    T = T_SEG
    R = N_STREAMS * T
    C = D_RNN
    ncol = C // LRU_COLS

    def halo():
        outside = jnp.concatenate([xprev_ref[...].reshape(nb * SUBLANES, D_MODEL),
                                   xnext_ref[...].reshape(nb * SUBLANES, D_MODEL)], axis=0).astype(BF16)
        rows = _dot(hsel_ref[...], outside)
        if ns > 1:
            seg = lax.broadcasted_iota(jnp.int32, (N_STREAMS, D_MODEL), 0) & (ns - 1)
            has_left = (seg > 0).astype(F32)
            has_right = (seg < ns - 1).astype(F32)
            m0 = 2 * N_STREAMS
            tail = lhs_dst[m0 + R - 2 * N_STREAMS:m0 + R, :].astype(F32)
            head = lhs_dst[m0:m0 + 2 * N_STREAMS, :].astype(F32)
            inside = jnp.concatenate(
                [pltpu.roll(tail[0:N_STREAMS], 1, 0) * has_left,
                 pltpu.roll(tail[N_STREAMS:], 1, 0) * has_left,
                 pltpu.roll(head[0:N_STREAMS], N_STREAMS - 1, 0) * has_right,
                 jnp.zeros((N_STREAMS, D_MODEL), F32)], axis=0)
            rows = rows + inside
        rows = rows.astype(BF16)
        lhs_dst[0:2 * N_STREAMS, :] = rows[0:2 * N_STREAMS]
        lhs_dst[2 * N_STREAMS + R:, :] = rows[2 * N_STREAMS:]

    def project(c):
        cs = slice(c * LRU_COLS, (c + 1) * LRU_COLS)
        uall_scr[:, cs] = _dot(lhs_dst[...], wu_ref[:, cs]) + bu_ref[:, cs]
        seg = lax.broadcasted_iota(jnp.int32, (N_STREAMS, LRU_COLS), 0) & (ns - 1)
        keep_prev = 1.0 - (seg == 0).astype(F32) * (jj == 0).astype(F32)
        keep_next = 1.0 - (seg == ns - 1).astype(F32) * (jj == nj - 1).astype(F32)
        prev_rows = slice(0, 2 * N_STREAMS)
        next_rows = slice(2 * N_STREAMS + R, 3 * N_STREAMS + R)
        uall_scr[prev_rows, cs] = uall_scr[prev_rows, cs] * jnp.concatenate([keep_prev, keep_prev], axis=0)
        uall_scr[next_rows, cs] = uall_scr[next_rows, cs] * keep_next

    def conv(c):
        cs = slice(c * LRU_COLS, (c + 1) * LRU_COLS)
        acc = bc_ref[:, cs] + wc_ref[0:1, cs] * uall_scr[0:R, cs]
        for k in range(1, LRU_CONV_W):
            acc = acc + wc_ref[k:k + 1, cs] * uall_scr[N_STREAMS * k:N_STREAMS * k + R, cs]
        u_dst[:, cs] = acc
        ub_dst[:, cs] = acc.astype(BF16)
        u_out[:, cs] = acc.astype(BF16)

    items = _permute_items(x4_ref, pm_ref, xb_dst, lhs_dst, 2 * N_STREAMS, nb=nb, ns=ns)
    items.append((halo, 0, 0))
    project_cost = _mxu_cycles(R + HALO_ROWS, D_MODEL, LRU_COLS)
    conv_cost = _vpu_cycles(R, LRU_COLS, 2 * LRU_CONV_W + 1)
    items.append((functools.partial(project, 0), project_cost, 0))
    for c in range(ncol):
        if c + 1 < ncol:
            items.append((functools.partial(project, c + 1), project_cost, 0))
        items.append((functools.partial(conv, c), 0, conv_cost))
    return items


def _coeff_items(ub_src, u_src, wg_ref, bg_ref, lam_ref, pre_scr):
    R = N_STREAMS * T_SEG
    C = D_RNN

    def coeff(t):
        cols = slice(t * LANES, (t + 1) * LANES)
        xcols = slice(C + t * LANES, C + (t + 1) * LANES)
        lam = lam_ref[:, cols]
        log_sig = jnp.minimum(lam, 0.0) - jnp.log1p(jnp.exp(-jnp.abs(lam)))
        hc = (0.5 * LRU_C) * log_sig
        log_a = hc * jnp.tanh(pre_scr[:, cols]) + hc
        ig = 0.5 * jnp.tanh(pre_scr[:, xcols]) + 0.5
        a = jnp.exp(log_a)
        gain = jnp.tanh(log_a) * (-1.0 - a * a)
        root = gain * lax.rsqrt(jnp.maximum(gain, F32_TINY))
        pre_scr[:, cols] = a
        pre_scr[:, xcols] = root * (ig * u_src[:, cols])

    first_touch = {}
    last_touch = {}
    for i in range(N_LRU_BLOCKS):
        for part in range(MXU_DIM // LANES):
            tile = GATE_OFF[i] // LANES + part
            first_touch.setdefault(tile, i)
            last_touch[tile] = i

    def gate(i):
        off = GATE_OFF[i]
        y = _dot(ub_src[:, off:off + MXU_DIM], wg_ref[i])
        for half in range(2):
            for part in range(MXU_DIM // LANES):
                col = half * C + off + part * LANES
                val = y[:, half * MXU_DIM + part * LANES:half * MXU_DIM + (part + 1) * LANES]
                if first_touch[off // LANES + part] == i:
                    pre_scr[:, col:col + LANES] = val + bg_ref[:, col:col + LANES]
                else:
                    pre_scr[:, col:col + LANES] = pre_scr[:, col:col + LANES] + val

    gate_cost = (_mxu_cycles(R, MXU_DIM, 2 * MXU_DIM), _vpu_cycles(R, 2 * MXU_DIM, 1))
    coeff_cost = _vpu_cycles(R, LANES, 20)
    items = [(functools.partial(gate, 0),) + gate_cost]
    for i in range(N_LRU_BLOCKS):
        if i + 1 < N_LRU_BLOCKS:
            items.append((functools.partial(gate, i + 1),) + gate_cost)
        for t in sorted(t for t, last in last_touch.items() if last == i):
            items.append((functools.partial(coeff, t), 0, coeff_cost))
    return items


def _stage_scan(pre_scr, p_scr, carry_scr, h_dst, *, ns, fwd, first):
    T = T_SEG
    C = D_RNN

    keep = 1.0 - first.astype(F32)
    carry = carry_scr[...] * keep

    def scan_step(t, h, p):
        rows = slice(t * N_STREAMS, (t + 1) * N_STREAMS)
        a = pre_scr[rows, 0:C]
        h = a * h + pre_scr[rows, C:2 * C]
        h_dst[rows, :] = h
        if ns > 1:
            p = p * a
            p_scr[rows, :] = p
        return h, p

    h_end = carry if ns == 1 else jnp.zeros((N_STREAMS, C), F32)
    p_end = jnp.ones((N_STREAMS, C), F32)
    for tt in range(T):
        h_end, p_end = scan_step(tt if fwd else T - 1 - tt, h_end, p_end)

    if ns == 1:
        carry_scr[...] = h_end
        return None

    sub = lax.broadcasted_iota(jnp.int32, (N_STREAMS, C), 0)
    seg = sub & (ns - 1)
    head = (seg == 0) if fwd else (seg == ns - 1)
    from_prev = 1 if fwd else N_STREAMS - 1
    final = jnp.zeros((N_STREAMS, C), F32)
    cin = final
    for _ in range(ns):
        cin = jnp.where(head, carry, pltpu.roll(final, from_prev, 0))
        final = h_end + p_end * cin
    to_head = (N_STREAMS - (ns - 1)) % N_STREAMS if fwd else ns - 1
    carry_scr[...] = pltpu.roll(final, to_head, 0)
    return cin


def _add_carry(h, p, cin):
    shape3 = (h.shape[0] // N_STREAMS, N_STREAMS, h.shape[1])
    return (h.reshape(shape3) + p.reshape(shape3) * cin[None]).reshape(h.shape)


def _lru_bwd_kernel(x4_ref, xprev_ref, xnext_ref, hsel_ref, pm_ref, wu_ref, bu_ref, wc_ref, bc_ref,
                    wg_ref, bg_ref, lam_ref,
                    wkv_ref, bkv_ref,
                    h_ref, u_ref, kv_ref,
                    xb_scr, lhs_scr, uall_scr, u_scr, ub_scr, pre_scr, p_scr, carry_scr, h_scr,
                    *, nb, ns, nj):
    T = T_SEG
    R = N_STREAMS * T
    s = pl.program_id(1)
    c_in = jnp.minimum(s, nj - 1)
    c_scan = jnp.clip(s - 2, 0, nj - 1)
    slot = s % 2

    @pl.when(s == 0)
    def _():
        pre_scr[...] = jnp.zeros_like(pre_scr)
        carry_scr[...] = jnp.zeros_like(carry_scr)
        u_scr[1] = jnp.zeros((R, D_RNN), F32)
        ub_scr[1] = jnp.zeros((R, D_RNN), BF16)

    cin = _stage_scan(pre_scr, p_scr, carry_scr, h_scr, ns=ns, fwd=False, first=(c_scan == 0))
    for it in range(R // LRU_TILE):
        rows = slice(it * LRU_TILE, (it + 1) * LRU_TILE)
        h = h_scr[rows, :]
        if cin is not None:
            h = _add_carry(h, p_scr[rows, :], cin)
        h_ref[rows, :] = h.astype(BF16)

    def kv_project():
        kv = _dot(xb_scr[...], wkv_ref[...]) + bkv_ref[...]
        for b in range(nb):
            for sg in range(ns):
                i = b * ns + sg
                kv_ref[b, sg] = kv[i * T:(i + 1) * T].astype(BF16)

    input_items = _input_items(x4_ref, xprev_ref, xnext_ref, hsel_ref, pm_ref, wu_ref, bu_ref, wc_ref, bc_ref,
                               xb_scr, lhs_scr, uall_scr, u_scr.at[slot], ub_scr.at[slot], u_ref,
                               nb=nb, ns=ns, jj=nj - 1 - c_in, nj=nj)
    input_items.append((kv_project, _mxu_cycles(R, D_MODEL, 2 * KV_DIM), _vpu_cycles(R, 2 * KV_DIM, 2)))
    _interleave(_coeff_items(ub_scr.at[1 - slot], u_scr.at[1 - slot], wg_ref, bg_ref, lam_ref, pre_scr),
                input_items)


def _lru_fwd_kernel(x4_ref, u_ref, hb_ref, pm_ref, pmt_ref, wg_ref, bg_ref,
                    lam_ref, wgl_ref, bgl_ref, wmg_ref, bmg_ref, wout_ref,
                    m_ref,
                    xb_scr, lhs_scr, pre_scr, p_scr, carry_scr,
                    h_scr, g_scr, hg_scr, hgn_scr,
                    *, nb, ns, nj):
    T = T_SEG
    R = N_STREAMS * T
    C = D_RNN
    ncol = C // LRU_COLS
    s = pl.program_id(1)
    c_out = jnp.clip(s - 1, 0, nj - 1)
    slot = s % 2

    @pl.when(s == 0)
    def _():
        pre_scr[...] = jnp.zeros_like(pre_scr)
        carry_scr[...] = jnp.zeros_like(carry_scr)
        xb_scr[1] = jnp.zeros((R, D_MODEL), BF16)
        lhs_scr[1] = jnp.zeros((R, D_MODEL), BF16)

    cin = _stage_scan(pre_scr, p_scr, carry_scr, h_scr, ns=ns, fwd=True, first=(c_out == 0))

    def gate_project(c):
        cs = slice(c * LRU_COLS, (c + 1) * LRU_COLS)
        g_scr[c % 2] = _dot(lhs_scr[1 - slot], wgl_ref[:, cs]) + bgl_ref[:, cs]

    def gate_apply(c):
        cs = slice(c * LRU_COLS, (c + 1) * LRU_COLS)
        h = h_scr[:, cs]
        if cin is not None:
            h = _add_carry(h, p_scr[:, cs], cin[:, cs])
        h = h + hb_ref[:, cs]
        hg_scr[:, cs] = (h * _gelu(g_scr[c % 2])).astype(BF16)

    def unpermute(c):
        cs = slice(c * LRU_COLS, (c + 1) * LRU_COLS)
        for tb in range(T // PERM_T):
            nat = _dot(pmt_ref[...], hg_scr[MXU_DIM * tb:MXU_DIM * (tb + 1), cs]).astype(BF16)
            for i in range(N_STREAMS):
                hgn_scr[i * T + PERM_T * tb:i * T + PERM_T * (tb + 1), cs] = nat[i * PERM_T:(i + 1) * PERM_T]

    def out_project(c):
        cs = slice(c * LRU_COLS, (c + 1) * LRU_COLS)
        y_lru = _dot(hgn_scr[...], wout_ref[:, cs])
        gate = _sigmoid_of_twice(_dot(xb_scr[1 - slot], wmg_ref[:, cs]) + bmg_ref[:, cs])
        m = (gate * y_lru).astype(BF16)
        for b in range(nb):
            for sg in range(ns):
                i = b * ns + sg
                m_ref[b, sg, :, cs] = m[i * T:(i + 1) * T]

    gp_cost = _mxu_cycles(R, D_MODEL, LRU_COLS)
    out_items = [(functools.partial(gate_project, 0), gp_cost, 0)]
    for c in range(ncol):
        if c + 1 < ncol:
            out_items.append((functools.partial(gate_project, c + 1), gp_cost, 0))
        out_items.append((functools.partial(gate_apply, c), 0, _vpu_cycles(R, LRU_COLS, 14)))
        out_items.append((functools.partial(unpermute, c), (T // PERM_T) * _mxu_cycles(MXU_DIM, MXU_DIM, LRU_COLS),
                          _vpu_cycles(R, LRU_COLS, 1)))
    for c in range(D_MODEL // LRU_COLS):
        out_items.append((functools.partial(out_project, c),
                          _mxu_cycles(R, C, LRU_COLS) + _mxu_cycles(R, D_MODEL, LRU_COLS),
                          _vpu_cycles(R, LRU_COLS, 5)))

    in_items = _permute_items(x4_ref, pm_ref, xb_scr.at[slot], lhs_scr.at[slot], 0, nb=nb, ns=ns)
    in_items += _coeff_items(u_ref, u_ref, wg_ref, bg_ref, lam_ref, pre_scr)
    _interleave(out_items, in_items)


def _stream_split(batch):
    nb = math.gcd(batch, N_STREAMS)
    return nb, N_STREAMS // nb


def _halo_select(nb, ns):
    sel = np.zeros((HALO_ROWS, 2 * nb * SUBLANES), np.float32)
    for b in range(nb):
        first, last = b * ns, b * ns + ns - 1
        sel[0 * N_STREAMS + first, b * SUBLANES + SUBLANES - 2] = 1.0
        sel[1 * N_STREAMS + first, b * SUBLANES + SUBLANES - 1] = 1.0
        sel[2 * N_STREAMS + last, nb * SUBLANES + b * SUBLANES] = 1.0
    return sel


def _perm_matrix():
    pm = np.zeros((MXU_DIM, MXU_DIM), np.float32)
    for i in range(N_STREAMS):
        for t in range(PERM_T):
            pm[t * N_STREAMS + i, i * PERM_T + t] = 1.0
    return pm


def _lru_bwd_call(x, p):
    B, S, D = x.shape
    nb, ns = _stream_split(B)
    T = T_SEG
    R = N_STREAMS * T
    Q = S // T
    nj = Q // ns
    x4 = x.reshape(B, Q, T, D)
    x8 = x.reshape(B, S // SUBLANES, SUBLANES, D)
    hsel = jnp.asarray(_halo_select(nb, ns), BF16)
    C = D_RNN
    chunk_in = lambda bb, s: (bb, nj - 1 - jnp.minimum(s, nj - 1), 0, 0)
    chunk_scan = lambda bb, s: (bb, nj - 1 - jnp.clip(s - 2, 0, nj - 1), 0, 0)
    blocks_per_chunk = ns * T // SUBLANES
    before = lambda bb, s: (bb, jnp.maximum(chunk_in(bb, s)[1] * blocks_per_chunk - 1, 0), 0, 0)
    after = lambda bb, s: (bb, jnp.minimum((chunk_in(bb, s)[1] + 1) * blocks_per_chunk, S // SUBLANES - 1), 0, 0)
    in_specs = [
        pl.BlockSpec((nb, ns, T, D), chunk_in),
        pl.BlockSpec((nb, None, SUBLANES, D), before),
        pl.BlockSpec((nb, None, SUBLANES, D), after),
        _const_spec((HALO_ROWS, 2 * nb * SUBLANES)),
        _const_spec((MXU_DIM, MXU_DIM)),
        _const_spec((D, C)), _const_spec((1, C)),
        _const_spec((LRU_CONV_W, C)), _const_spec((1, C)),
        _const_spec((N_LRU_BLOCKS, MXU_DIM, 2 * MXU_DIM)), _const_spec((1, 2 * C)),
        _const_spec((1, C)),
        _const_spec((D, 2 * KV_DIM)), _const_spec((1, 2 * KV_DIM)),
    ]
    out_shape = [
        jax.ShapeDtypeStruct((B // nb, nj, R, C), BF16),
        jax.ShapeDtypeStruct((B // nb, nj, R, C), BF16),
        jax.ShapeDtypeStruct((B, Q, T, 2 * KV_DIM), BF16),
    ]
    out_specs = [
        pl.BlockSpec((None, None, R, C), chunk_scan),
        pl.BlockSpec((None, None, R, C), chunk_in),
        pl.BlockSpec((nb, ns, T, 2 * KV_DIM), chunk_in),
    ]
    scratch = [
        pltpu.VMEM((R, D), BF16),
        pltpu.VMEM((R + HALO_ROWS, D), BF16),
        pltpu.VMEM((R + HALO_ROWS, C), F32),
        pltpu.VMEM((2, R, C), F32),
        pltpu.VMEM((2, R, C), BF16),
        pltpu.VMEM((R, 2 * C), F32),
        pltpu.VMEM((R, C), F32),
        pltpu.VMEM((N_STREAMS, C), F32),
        pltpu.VMEM((R, C), F32),
    ]
    h_b, u, kv = pl.pallas_call(
        functools.partial(_lru_bwd_kernel, nb=nb, ns=ns, nj=nj),
        grid=(B // nb, nj + 2),
        in_specs=in_specs, out_specs=out_specs, out_shape=out_shape,
        scratch_shapes=scratch,
        compiler_params=pltpu.CompilerParams(
            dimension_semantics=("arbitrary", "arbitrary"), vmem_limit_bytes=VMEM_LIMIT),
        name="lru_bwd",
    )(x4, x8, x8, hsel, p["pm"], p["w_u"], p["b_u"], p["w_conv"], p["b_conv"], p["wg_b"], p["bg_b"], p["lam_b"],
      p["w_kv"], p["b_kv"])
    return h_b, u, kv.reshape(B, S, 2 * KV_DIM)


def _lru_fwd_call(x, u, h_b, p):
    B, S, D = x.shape
    nb, ns = _stream_split(B)
    T = T_SEG
    R = N_STREAMS * T
    Q = S // T
    nj = Q // ns
    x4 = x.reshape(B, Q, T, D)
    C = D_RNN
    chunk_in = lambda bb, s: (bb, jnp.minimum(s, nj - 1), 0, 0)
    chunk_out = lambda bb, s: (bb, jnp.clip(s - 1, 0, nj - 1), 0, 0)
    in_specs = [
        pl.BlockSpec((nb, ns, T, D), chunk_in),
        pl.BlockSpec((None, None, R, C), chunk_in),
        pl.BlockSpec((None, None, R, C), chunk_out),
        _const_spec((MXU_DIM, MXU_DIM)), _const_spec((MXU_DIM, MXU_DIM)),
        _const_spec((N_LRU_BLOCKS, MXU_DIM, 2 * MXU_DIM)), _const_spec((1, 2 * C)),
        _const_spec((1, C)),
        _const_spec((D, C)), _const_spec((1, C)),
        _const_spec((D, D)), _const_spec((1, D)),
        _const_spec((C, D)),
    ]
    scratch = [
        pltpu.VMEM((2, R, D), BF16),
        pltpu.VMEM((2, R, D), BF16),
        pltpu.VMEM((R, 2 * C), F32),
        pltpu.VMEM((R, C), F32),
        pltpu.VMEM((N_STREAMS, C), F32),
        pltpu.VMEM((R, C), F32),
        pltpu.VMEM((2, R, LRU_COLS), F32),
        pltpu.VMEM((R, C), BF16),
        pltpu.VMEM((R, C), BF16),
    ]
    m = pl.pallas_call(
        functools.partial(_lru_fwd_kernel, nb=nb, ns=ns, nj=nj),
        grid=(B // nb, nj + 1),
        in_specs=in_specs,
        out_specs=pl.BlockSpec((nb, ns, T, D), chunk_out),
        out_shape=jax.ShapeDtypeStruct((B, Q, T, D), BF16),
        scratch_shapes=scratch,
        compiler_params=pltpu.CompilerParams(
            dimension_semantics=("arbitrary", "arbitrary"), vmem_limit_bytes=VMEM_LIMIT),
        name="lru_fwd",
    )(x4, u, h_b, p["pm"], p["pmt"], p["wg_f"], p["bg_f"],
      p["lam_f"], p["w_gl"], p["b_gl"], p["w_mg_lru"], p["b_mg_lru"], p["w_lru_out"])
    return m.reshape(B, S, D)


def _attn_kernel(xa_ref, xb_ref, m_ref, kvp_ref, kvm_ref, kvn_ref, idx_ref, rel_ref, sink_ref,
                 wq_ref, bq_ref, wga_ref, bga_ref, wao_ref, wo_ref, bo_ref, g1_ref, b1_ref,
                 o_ref, xc_scr, q_scr, s_scr, attn_scr, merged_scr, y_scr, bias_scr, *, nr, alpha):
    R = ROWS_ATTN
    nblk = R // BLOCK
    i = pl.program_id(0)
    n_chunks = pl.num_programs(0) - 1
    r = lax.rem(jnp.minimum(i, n_chunks - 1), nr)
    slot = i % 2

    @pl.when(i == 0)
    def _():
        idx = idx_ref[...]
        for h in range(N_HEADS):
            def pick(bkt, acc):
                return jnp.where(idx == bkt, rel_ref[bkt, h], acc)
            tab = lax.fori_loop(0, N_BUCKETS, pick, jnp.full((BLOCK, 3 * BLOCK), NEG_INF, F32))
            kvh, g = divmod(h, GROUP)
            bias_scr[kvh, g * BLOCK:(g + 1) * BLOCK, :] = tab
        xc_scr[1] = jnp.zeros((R, D_MODEL), BF16)
        attn_scr[1] = jnp.zeros((R, Q_DIM), BF16)

    pen_prev = (r == 0).astype(F32) * NEG_INF
    pen_next = (r == nr - 1).astype(F32) * NEG_INF
    row_head = lax.broadcasted_iota(jnp.int32, (GROUP * BLOCK, 1), 0) // BLOCK

    def cast_x():
        xc_scr[slot] = xa_ref[...].astype(BF16)

    def q_project(kvh):
        cols = slice(kvh * GROUP * HEAD_DIM, (kvh + 1) * GROUP * HEAD_DIM)
        q_scr[:, cols] = (_dot(xc_scr[slot], wq_ref[:, cols]) + bq_ref[:, cols]).astype(BF16)

    def kv_window(col0, kvh, n):
        cols = slice(col0 + kvh * HEAD_DIM, col0 + (kvh + 1) * HEAD_DIM)
        prev = kvm_ref[(n - 1) * BLOCK:n * BLOCK, cols] if n > 0 else kvp_ref[:, cols]
        nxt = kvm_ref[(n + 1) * BLOCK:(n + 2) * BLOCK, cols] if n < nblk - 1 else kvn_ref[:, cols]
        return jnp.concatenate([prev, kvm_ref[n * BLOCK:(n + 1) * BLOCK, cols], nxt], axis=0)

    def scores(kvh, n):
        q4 = jnp.concatenate(
            [q_scr[n * BLOCK:(n + 1) * BLOCK, (kvh * GROUP + g) * HEAD_DIM:(kvh * GROUP + g + 1) * HEAD_DIM]
             for g in range(GROUP)], axis=0)
        k_ext = kv_window(0, kvh, n)
        s = lax.dot_general(q4, k_ext, (((1,), (1,)), ((), ())), preferred_element_type=F32)
        s = s + bias_scr[kvh]
        if n == 0:
            s = jnp.concatenate([s[:, 0:BLOCK] + pen_prev, s[:, BLOCK:]], axis=1)
        if n == nblk - 1:
            s = jnp.concatenate([s[:, 0:2 * BLOCK], s[:, 2 * BLOCK:] + pen_next], axis=1)
        s_scr[(kvh * nblk + n) % 2] = s

    def softmax_pv(kvh, n):
        sink = jnp.zeros((GROUP * BLOCK, 1), F32)
        for g in range(GROUP):
            sink = jnp.where(row_head == g, sink_ref[kvh * GROUP + g], sink)
        s = s_scr[(kvh * nblk + n) % 2]
        mx = jnp.maximum(jnp.max(s, axis=1, keepdims=True), sink)
        pr = jnp.exp(s - mx)
        denom = jnp.sum(pr, axis=1, keepdims=True) + jnp.exp(sink - mx)
        v_ext = kv_window(KV_DIM, kvh, n)
        o = _dot(pr.astype(BF16), v_ext) * (1.0 / denom)
        for g in range(GROUP):
            h = kvh * GROUP + g
            attn_scr[slot, n * BLOCK:(n + 1) * BLOCK, h * HEAD_DIM:(h + 1) * HEAD_DIM] = (
                o[g * BLOCK:(g + 1) * BLOCK].astype(BF16))

    rows4 = GROUP * BLOCK
    attn_items = [(cast_x, 0, _vpu_cycles(R, D_MODEL, 1))]
    for kvh in range(N_KV_HEADS):
        attn_items.append((functools.partial(q_project, kvh), _mxu_cycles(R, D_MODEL, GROUP * HEAD_DIM),
                           _vpu_cycles(R, GROUP * HEAD_DIM, 2)))
    blocks = [(kvh, n) for kvh in range(N_KV_HEADS) for n in range(nblk)]
    score_cost = (_mxu_cycles(rows4, HEAD_DIM, 3 * BLOCK), _vpu_cycles(rows4, 3 * BLOCK, 1))
    pv_cost = (2 * _mxu_cycles(rows4, 3 * BLOCK, HEAD_DIM), _vpu_cycles(rows4, 3 * BLOCK, 8))
    attn_items.append((functools.partial(scores, *blocks[0]),) + score_cost)
    for j, blk in enumerate(blocks):
        if j + 1 < len(blocks):
            attn_items.append((functools.partial(scores, *blocks[j + 1]),) + score_cost)
        attn_items.append((functools.partial(softmax_pv, *blk),) + pv_cost)

    def merge(c):
        cs = slice(c * ATTN_COLS, (c + 1) * ATTN_COLS)
        y_attn = _dot(attn_scr[1 - slot], wao_ref[:, cs])
        gate = _sigmoid_of_twice(_dot(xc_scr[1 - slot], wga_ref[:, cs]) + bga_ref[:, cs])
        merged_scr[:, cs] = (gate * y_attn + m_ref[:, cs].astype(F32)).astype(BF16)

    def out_project(c):
        cs = slice(c * ATTN_COLS, (c + 1) * ATTN_COLS)
        y_scr[:, cs] = alpha * xb_ref[:, cs] + (_dot(merged_scr[...], wo_ref[:, cs]) + bo_ref[:, cs])

    def norm():
        o_ref[...] = _layer_norm(y_scr[...], g1_ref[...], b1_ref[...])

    ncol = D_MODEL // ATTN_COLS
    merge_items = [(functools.partial(merge, c), _mxu_cycles(R, Q_DIM, ATTN_COLS) + _mxu_cycles(R, D_MODEL, ATTN_COLS),
                    _vpu_cycles(R, ATTN_COLS, 6)) for c in range(ncol)]
    merge_items += [(functools.partial(out_project, c), _mxu_cycles(R, D_MODEL, ATTN_COLS),
                     _vpu_cycles(R, ATTN_COLS, 3)) for c in range(ncol)]
    merge_items.append((norm, 0, _vpu_cycles(R, D_MODEL, 12)))
    _interleave(attn_items, merge_items)


def _attn_call(x, m_lru, kv, p, alpha):
    B, S, D = x.shape
    R = ROWS_ATTN
    nr = S // R
    bpr = R // BLOCK
    nkb = S // BLOCK
    n_chunks = B * nr
    cur = lambda i: jnp.minimum(i, n_chunks - 1)
    old = lambda i: jnp.maximum(i - 1, 0)
    main_cur = lambda i: (cur(i) // nr, cur(i) % nr, 0)
    main_old = lambda i: (old(i) // nr, old(i) % nr, 0)
    prev = lambda i: (cur(i) // nr, jnp.maximum((cur(i) % nr) * bpr - 1, 0), 0)
    nxt = lambda i: (cur(i) // nr, jnp.minimum((cur(i) % nr + 1) * bpr, nkb - 1), 0)
    in_specs = [
        pl.BlockSpec((None, R, D), main_cur),
        pl.BlockSpec((None, R, D), main_old),
        pl.BlockSpec((None, R, D), main_old),
        pl.BlockSpec((None, BLOCK, 2 * KV_DIM), prev),
        pl.BlockSpec((None, R, 2 * KV_DIM), main_cur),
        pl.BlockSpec((None, BLOCK, 2 * KV_DIM), nxt),
        _const_spec((BLOCK, 3 * BLOCK)),
        pl.BlockSpec(memory_space=pltpu.SMEM),
        pl.BlockSpec(memory_space=pltpu.SMEM),
        _const_spec((D, Q_DIM)), _const_spec((1, Q_DIM)),
        _const_spec((D, D)), _const_spec((1, D)),
        _const_spec((Q_DIM, D)),
        _const_spec((D, D)), _const_spec((1, D)),
        _const_spec((1, D)), _const_spec((1, D)),
    ]
    scratch = [
        pltpu.VMEM((2, R, D), BF16),
        pltpu.VMEM((R, Q_DIM), BF16),
        pltpu.VMEM((2, GROUP * BLOCK, 3 * BLOCK), F32),
        pltpu.VMEM((2, R, Q_DIM), BF16),
        pltpu.VMEM((R, D), BF16),
        pltpu.VMEM((R, D), F32),
        pltpu.VMEM((N_KV_HEADS, GROUP * BLOCK, 3 * BLOCK), F32),
    ]
    return pl.pallas_call(
        functools.partial(_attn_kernel, nr=nr, alpha=alpha),
        grid=(n_chunks + 1,),
        in_specs=in_specs,
        out_specs=pl.BlockSpec((None, R, D), main_old),
        out_shape=jax.ShapeDtypeStruct((B, S, D), F32),
        scratch_shapes=scratch,
        compiler_params=pltpu.CompilerParams(
            dimension_semantics=("arbitrary",), vmem_limit_bytes=VMEM_LIMIT),
        name="attn_merge",
    )(x, x, m_lru, kv, kv, kv, p["bias_idx"], p["rel_table"], p["sink"],
      p["w_q"], p["b_q"], p["w_mg_attn"], p["b_mg_attn"], p["w_attn_out"], p["w_o"], p["b_o"],
      p["ln1_g"], p["ln1_b"])


def _ffn_kernel(x_ref, xp_ref, xn_ref, wup_ref, bup_ref, wcv_ref, bcv_ref, wdn_ref, bdn_ref, g2_ref, b2_ref,
                o_ref, xe_scr, hs_scr, act_scr, *, nr, alpha):
    R = ROWS_FFN
    H = FFN_HALO
    nt = D_FF // FFN_COLS
    nslab = FFN_COLS // LANES
    r = pl.program_id(1)
    keep_prev = (r > 0).astype(F32)
    keep_next = (r < nr - 1).astype(F32)
    zeros = jnp.zeros((H - SUBLANES, D_MODEL), F32)
    xe_scr[...] = jnp.concatenate(
        [zeros, xp_ref[...] * keep_prev, x_ref[...], xn_ref[...] * keep_next, zeros], axis=0).astype(BF16)

    row = lax.broadcasted_iota(jnp.int32, (SUBLANES, FFN_COLS), 0)
    pad_prev = (row == SUBLANES - 1).astype(F32) * (1.0 - keep_prev)
    pad_next = (row == 0).astype(F32) * (1.0 - keep_next)

    def cols(j, half):
        return slice(half * D_FF + j * FFN_COLS, half * D_FF + (j + 1) * FFN_COLS)

    def up_proj(j):
        xe = xe_scr[...]
        for half in range(2):
            y = _dot(xe, wup_ref[:, cols(j, half)])
            b_up = bup_ref[:, cols(j, half)]
            y = jnp.concatenate(
                [y[0:H - SUBLANES], y[H - SUBLANES:H] - pad_prev * b_up, y[H:H + R],
                 y[H + R:H + R + SUBLANES] - pad_next * b_up, y[H + R + SUBLANES:]], axis=0)
            for c in range(nslab):
                hs_scr[j % 2, half, c] = y[:, c * LANES:(c + 1) * LANES]

    def conv(j, half, c):
        start = cols(j, half).start + c * LANES
        lanes = slice(start, start + LANES)
        w = wcv_ref[:, lanes]
        bias = bcv_ref[:, lanes] + bup_ref[:, lanes] * (w[0:1] + w[1:2] + w[2:3])
        return (w[0:1] * hs_scr[j % 2, half, c, H - 1:H - 1 + R, :]
                + w[1:2] * hs_scr[j % 2, half, c, H:H + R, :]
                + w[2:3] * hs_scr[j % 2, half, c, H + 1:H + 1 + R, :] + bias)

    def gate(j):
        for c in range(nslab):
            act = _gelu(conv(j, 1, c)) * conv(j, 0, c)
            act_scr[:, j * FFN_COLS + c * LANES:j * FFN_COLS + (c + 1) * LANES] = act.astype(BF16)

    per = nt // FFN_DOWN_SPLIT
    up_proj(0)
    for j in range(nt):
        if j + 1 < nt:
            up_proj(j + 1)
        gate(j)
        if (j + 1) % per == 0:
            part = (j + 1) // per - 1
            ks = slice(part * per * FFN_COLS, (part + 1) * per * FFN_COLS)
            contrib = _dot(act_scr[:, ks], wdn_ref[ks, :])
            if part == 0:
                o_ref[...] = contrib + (alpha * x_ref[...] + bdn_ref[...])
            elif part < FFN_DOWN_SPLIT - 1:
                o_ref[...] = o_ref[...] + contrib
            else:
                o_ref[...] = _layer_norm(o_ref[...] + contrib, g2_ref[...], b2_ref[...])


def _ffn_call(x, p, alpha):
    B, S, D = x.shape
    R = ROWS_FFN
    nr = S // R
    rpb = R // SUBLANES
    nsb = S // SUBLANES
    main = lambda b, r: (b, r, 0)
    prev = lambda b, r: (b, jnp.maximum(r * rpb - 1, 0), 0)
    nxt = lambda b, r: (b, jnp.minimum((r + 1) * rpb, nsb - 1), 0)
    in_specs = [
        pl.BlockSpec((None, R, D), main),
        pl.BlockSpec((None, SUBLANES, D), prev),
        pl.BlockSpec((None, SUBLANES, D), nxt),
        _const_spec((D, 2 * D_FF)), _const_spec((1, 2 * D_FF)),
        _const_spec((3, 2 * D_FF)), _const_spec((1, 2 * D_FF)),
        _const_spec((D_FF, D)), _const_spec((1, D)),
        _const_spec((1, D)), _const_spec((1, D)),
    ]
    scratch = [
        pltpu.VMEM((R + 2 * FFN_HALO, D), BF16),
        pltpu.VMEM((2, 2, FFN_COLS // LANES, R + 2 * FFN_HALO, LANES), F32),
        pltpu.VMEM((R, D_FF), BF16),
    ]
    return pl.pallas_call(
        functools.partial(_ffn_kernel, nr=nr, alpha=alpha),
        grid=(B, nr),
        in_specs=in_specs,
        out_specs=pl.BlockSpec((None, R, D), main),
        out_shape=jax.ShapeDtypeStruct((B, S, D), F32),
        scratch_shapes=scratch,
        compiler_params=pltpu.CompilerParams(
            dimension_semantics=("arbitrary", "arbitrary"), vmem_limit_bytes=VMEM_LIMIT_FFN),
        name="ffn",
    )(x, x, x, p["w_up"], p["b_up"], p["w_ffn_conv"], p["b_ffn_conv"], p["w_down"], p["b_down"],
      p["ln2_g"], p["ln2_b"])


def _t5_bucket(rel):
    nb = N_BUCKETS // 2
    ret = jnp.where(rel > 0, nb, 0)
    n = jnp.abs(rel)
    max_exact = nb // 2
    nf = jnp.maximum(n, 1).astype(jnp.float32)
    large = max_exact + (jnp.log(nf / max_exact) / math.log(MAX_DISTANCE / max_exact)
                         * (nb - max_exact)).astype(jnp.int32)
    large = jnp.minimum(large, nb - 1)
    return ret + jnp.where(n < max_exact, n, large)


def _pack_gate_weights(w_a, w_x):
    sel = np.zeros((N_LRU_BLOCKS, LRU_BLOCK, MXU_DIM), np.float32)
    for i in range(N_LRU_BLOCKS):
        lo = LRU_BLOCK * i - GATE_OFF[i]
        sel[i, np.arange(LRU_BLOCK), lo + np.arange(LRU_BLOCK)] = 1.0
    place = lambda w: jnp.einsum("nik,nij,njl->nkl", sel, w, sel, precision=lax.Precision.HIGHEST)
    return jnp.concatenate([place(w_a), place(w_x)], axis=-1).astype(BF16)


def _prepare(rel_table, w_in, b_in, w_lru_conv, b_lru_conv, w_rg_a, b_rg_a, w_rg_x, b_rg_x, lru_lambda,
             w_lru_out, attn_sink, w_attn_out, w_o, b_o, ln1_g, ln1_b, w_up, b_up, w_ffn_conv, b_ffn_conv,
             w_down, b_down, ln2_g, ln2_b):
    C = D_RNN
    o_g, o_q, o_k, o_gl, o_ga = C, 2 * C, 2 * C + Q_DIM, 2 * C + Q_DIM + 2 * KV_DIM, 2 * C + Q_DIM + 2 * KV_DIM + D_MODEL
    row = lambda a: a.reshape(1, -1).astype(F32)
    scale = HEAD_DIM ** -0.5
    pm = _perm_matrix()

    q_off = jnp.arange(BLOCK)[:, None]
    c_off = jnp.arange(3 * BLOCK)[None, :]
    rel = c_off - BLOCK - q_off
    bias_idx = jnp.where(jnp.abs(rel) <= WINDOW, _t5_bucket(rel), -1).astype(jnp.int32)

    return {
        "pm": jnp.asarray(pm, BF16), "pmt": jnp.asarray(pm.T, BF16),
        "w_u": w_in[:, 0:o_g].astype(BF16), "b_u": row(b_in[0:o_g]),
        "w_gl": w_in[:, o_g:o_q].astype(BF16), "b_gl": row(b_in[o_g:o_q]),
        "w_q": (w_in[:, o_q:o_k] * scale).astype(BF16), "b_q": row(b_in[o_q:o_k] * scale),
        "w_kv": w_in[:, o_k:o_gl].astype(BF16), "b_kv": row(b_in[o_k:o_gl]),
        "w_mg_lru": (0.5 * w_in[:, o_gl:o_ga]).astype(BF16), "b_mg_lru": row(0.5 * b_in[o_gl:o_ga]),
        "w_mg_attn": (0.5 * w_in[:, o_ga:]).astype(BF16), "b_mg_attn": row(0.5 * b_in[o_ga:]),
        "w_conv": w_lru_conv.astype(F32), "b_conv": row(b_lru_conv),
        "wg_f": _pack_gate_weights(0.5 * w_rg_a[0], 0.5 * w_rg_x[0]),
        "bg_f": row(0.5 * jnp.concatenate([b_rg_a[0].reshape(-1), b_rg_x[0].reshape(-1)])),
        "wg_b": _pack_gate_weights(0.5 * w_rg_a[1], 0.5 * w_rg_x[1]),
        "bg_b": row(0.5 * jnp.concatenate([b_rg_a[1].reshape(-1), b_rg_x[1].reshape(-1)])),
        "lam_f": row(lru_lambda[0]), "lam_b": row(lru_lambda[1]),
        "w_lru_out": w_lru_out.astype(BF16),
        "bias_idx": bias_idx, "rel_table": rel_table.astype(F32), "sink": attn_sink.astype(F32),
        "w_attn_out": w_attn_out.astype(BF16),
        "w_o": w_o.astype(BF16), "b_o": row(b_o),
        "ln1_g": row(ln1_g), "ln1_b": row(ln1_b),
        "w_up": w_up.astype(BF16), "b_up": row(b_up),
        "w_ffn_conv": w_ffn_conv.astype(F32), "b_ffn_conv": row(b_ffn_conv),
        "w_down": w_down.astype(BF16), "b_down": row(b_down),
        "ln2_g": row(ln2_g), "ln2_b": row(ln2_b),
    }


def _encoder_layer(x, p, alpha):
    h_b, u, kv = _lru_bwd_call(x, p)
    m_lru = _lru_fwd_call(x, u, h_b, p)
    x1 = _attn_call(x, m_lru, kv, p, alpha)
    return _ffn_call(x1, p, alpha)


def kernel(x_prompt, x_sample, rel_table, w_in, b_in, w_lru_conv, b_lru_conv, w_rg_a, b_rg_a, w_rg_x, b_rg_x, lru_lambda, w_lru_out, attn_sink, w_attn_out, w_o, b_o, ln1_g, ln1_b, w_up, b_up, w_ffn_conv, b_ffn_conv, w_down, b_down, ln2_g, ln2_b):
    depth = w_in.shape[0]
    alpha = (2.0 * depth) ** 0.25
    y_prompt, y_sample = x_prompt, x_sample
    for l in range(depth):
        p = _prepare(rel_table, w_in[l], b_in[l], w_lru_conv[l], b_lru_conv[l], w_rg_a[l], b_rg_a[l],
                     w_rg_x[l], b_rg_x[l], lru_lambda[l], w_lru_out[l], attn_sink[l], w_attn_out[l], w_o[l],
                     b_o[l], ln1_g[l], ln1_b[l], w_up[l], b_up[l], w_ffn_conv[l], b_ffn_conv[l], w_down[l],
                     b_down[l], ln2_g[l], ln2_b[l])
        y_prompt = _encoder_layer(y_prompt, p, alpha)
        y_sample = _encoder_layer(y_sample, p, alpha)
    return (y_prompt, y_sample)
```
